```python
import math
import jax
import jax.numpy as jnp
from jax import lax
import numpy as np

D_MODEL = 4096
BATCH = 4
SEQ = 2048
DEPTH = 1
DEC_BATCH = 32
DEC_SEQ = 4
PAST_LEN = 8192
PAGE_SIZE = 128

ATTN_WIDTH = D_MODEL // 2
SSM_WIDTH = D_MODEL - ATTN_WIDTH
HEAD_DIM = 128
N_HEADS = ATTN_WIDTH // HEAD_DIM
N_KV_HEADS = 4
GQA_REP = N_HEADS // N_KV_HEADS
KV_WIDTH = N_KV_HEADS * HEAD_DIM
ATTN_SCALE = HEAD_DIM ** -0.5
CMP_BLOCK = 32
CMP_STRIDE = 16
CMP_R = CMP_BLOCK // CMP_STRIDE
SLC_BLOCK = 64
N_SELECT = 16
FORCE_BONUS = 1.0e4
WINDOW = 512
SEL_Q_BLOCK = 32
WIN_Q_BLOCK = 128
NUM_BUCKETS = 32
MAX_DISTANCE = 128
SSM_GROUP = 16
N_SSM_GROUPS = SSM_WIDTH // SSM_GROUP
SSM_STATE = 64
DT_MIN = 1e-3
DT_MAX = 1e-1
PEER_HEADS = 8
PEER_KEYS = 128
PEER_EXPERTS = PEER_KEYS * PEER_KEYS
PEER_TOPK = 16
PEER_QDIM = 256
PEER_HALF = PEER_QDIM // 2
PEER_TOKEN_BLOCK = 64
EPS = 1e-6
NEG_INF = -1e30
OFF_KV = ATTN_WIDTH
OFF_GATE = OFF_KV + 6 * KV_WIDTH
OFF_SSM = OFF_GATE + 3 * N_HEADS
IN_COLS = OFF_SSM + SSM_WIDTH

kernel_name = 'hymba_nsa_s5_peer_step'


def rms_norm(x, g):
    xf = x.astype(jnp.float32)
    y = xf * lax.rsqrt(jnp.mean(xf * xf, axis=-1, keepdims=True) + EPS)
    return (y * g.astype(jnp.float32)).astype(x.dtype)


def t5_bucket(dist):
    n = jnp.maximum(dist, 0)
    max_exact = NUM_BUCKETS // 2
    nf = jnp.maximum(n, 1).astype(jnp.float32)
    large = max_exact + (jnp.log(nf / max_exact) / math.log(MAX_DISTANCE / max_exact)
                         * (NUM_BUCKETS - max_exact)).astype(jnp.int32)
    large = jnp.minimum(large, NUM_BUCKETS - 1)
    return jnp.where(n < max_exact, n, large)


def project(x_n, w_in, q_norm, k_norm):
    bsz, t, _ = x_n.shape
    p = x_n @ w_in
    q = rms_norm(p[..., :OFF_KV].reshape(bsz, t, N_HEADS, HEAD_DIM), q_norm)
    kv = p[..., OFF_KV:OFF_GATE].reshape(bsz, t, 3, 2, N_KV_HEADS, HEAD_DIM)
    gates = jax.nn.sigmoid(p[..., OFF_GATE:OFF_SSM].astype(jnp.float32)).reshape(bsz, t, 3, N_HEADS)
    u = p[..., OFF_SSM:]
    kv_cmp = kv[:, :, 0]
    kv_sel = jnp.stack([rms_norm(kv[:, :, 1, 0], k_norm[1]), kv[:, :, 1, 1]], axis=2)
    kv_win = jnp.stack([rms_norm(kv[:, :, 2, 0], k_norm[2]), kv[:, :, 2, 1]], axis=2)
    return q, kv_cmp, kv_sel, kv_win, gates, u


def compress_blocks(kv_all, w_phi, b_phi, k_norm_cmp):
    bsz, t = kv_all.shape[:2]
    n_chunks = t // CMP_STRIDE
    n_cmp = n_chunks - CMP_R + 1
    chunks = kv_all[:, :n_chunks * CMP_STRIDE].reshape(bsz, n_chunks, CMP_STRIDE, 2, N_KV_HEADS, HEAD_DIM)
    w = w_phi.reshape(2, CMP_R, CMP_STRIDE, HEAD_DIM, HEAD_DIM)
    out = b_phi[None, None, :, None, :]
    for r in range(CMP_R):
        out = out + jnp.einsum('bnscgd,csde->bncge', chunks[:, r:r + n_cmp], w[:, r])
    return rms_norm(out[:, :, 0], k_norm_cmp), out[:, :, 1]


def cmp_attention(q, k_c, v_c, q_pos, rel_table):
    bsz, t = q.shape[:2]
    n_cmp = k_c.shape[1]
    qg = q.reshape(bsz, t, N_KV_HEADS, GQA_REP, HEAD_DIM)
    logits = jnp.einsum('bqgrd,bngd->bgrqn', qg, k_c, preferred_element_type=jnp.float32) * ATTN_SCALE
    blk_end = jnp.arange(n_cmp, dtype=jnp.int32) * CMP_STRIDE + (CMP_BLOCK - 1)
    dist = q_pos[:, None] - blk_end[None, :]
    valid = dist >= 0
    bias = rel_table.astype(jnp.float32)[t5_bucket(dist)].reshape(t, n_cmp, N_KV_HEADS, GQA_REP).transpose(2, 3, 0, 1)
    logits = jnp.where(valid, logits + bias, NEG_INF)
    p = jax.nn.softmax(logits, axis=-1) * jnp.any(valid, axis=-1)[:, None].astype(jnp.float32)
    out = jnp.einsum('bgrqn,bngd->bqgrd', p.astype(v_c.dtype), v_c).reshape(bsz, t, N_HEADS, HEAD_DIM)
    return out, p


def select_blocks(p_cmp, q_pos, n_slc):
    n_cmp = p_cmp.shape[-1]
    c0 = np.arange(n_cmp)[:, None] * CMP_STRIDE
    s0 = np.arange(n_slc)[None, :] * SLC_BLOCK
    overlap = np.clip(np.minimum(c0 + CMP_BLOCK, s0 + SLC_BLOCK) - np.maximum(c0, s0), 0, None)
    m = jnp.asarray(overlap / CMP_BLOCK, dtype=jnp.float32)
    imp = jnp.einsum('bgrqn,ns->bgqs', p_cmp, m)
    j = jnp.arange(n_slc, dtype=jnp.int32)[None, :]
    cur = (q_pos // SLC_BLOCK)[:, None]
    valid = j * SLC_BLOCK <= q_pos[:, None]
    forced = (j == 0) | (j == cur) | (j == cur - 1)
    score = jnp.where(valid, imp + jnp.where(forced, FORCE_BONUS, 0.0), NEG_INF)
    top, idx = lax.top_k(score, min(N_SELECT, n_slc))
    return idx, top > 0.5 * NEG_INF


def sel_attention(q, k_s, v_s, idx, sel_valid, q_pos, rel_table):
    bsz, t = q.shape[:2]
    tk = k_s.shape[1]
    n_slc = -(-tk // SLC_BLOCK)
    pad = n_slc * SLC_BLOCK - tk
    n_sel = idx.shape[-1]

    def to_blocks(a):
        a = jnp.pad(a, ((0, 0), (0, pad), (0, 0), (0, 0)))
        return a.reshape(bsz, n_slc, SLC_BLOCK, N_KV_HEADS, HEAD_DIM).transpose(0, 3, 1, 2, 4)

    kb, vb = to_blocks(k_s), to_blocks(v_s)
    table_g = rel_table.astype(jnp.float32).reshape(NUM_BUCKETS, N_KV_HEADS, GQA_REP).transpose(1, 0, 2)
    b_ix = jnp.arange(bsz)[:, None, None, None]
    g_ix = jnp.arange(N_KV_HEADS)[None, :, None, None]
    offs = jnp.arange(SLC_BLOCK, dtype=jnp.int32)
    qb = math.gcd(t, SEL_Q_BLOCK)
    nblk = t // qb

    def one_block(args):
        q_blk, i_blk, v_blk, p_blk = args
        kg = kb[b_ix, g_ix, i_blk]
        vg = vb[b_ix, g_ix, i_blk]
        qg = q_blk.reshape(bsz, qb, N_KV_HEADS, GQA_REP, HEAD_DIM)
        logits = jnp.einsum('bqgrd,bgqkld->bgrqkl', qg, kg, preferred_element_type=jnp.float32) * ATTN_SCALE
        k_pos = i_blk[..., None] * SLC_BLOCK + offs
        dist = p_blk[None, None, :, None, None] - k_pos
        mask = (dist >= 0) & v_blk[..., None]
        bias = table_g[g_ix[..., None], t5_bucket(dist)].transpose(0, 1, 5, 2, 3, 4)
        logits = jnp.where(mask[:, :, None], logits + bias, NEG_INF)
        p = jax.nn.softmax(logits.reshape(bsz, N_KV_HEADS, GQA_REP, qb, n_sel * SLC_BLOCK), axis=-1)
        p = p.reshape(logits.shape).astype(vg.dtype)
        out = jnp.einsum('bgrqkl,bgqkld->bqgrd', p, vg)
        return out.reshape(bsz, qb, N_HEADS, HEAD_DIM)

    xs = (q.reshape(bsz, nblk, qb, N_HEADS, HEAD_DIM).transpose(1, 0, 2, 3, 4),
          idx.reshape(bsz, N_KV_HEADS, nblk, qb, n_sel).transpose(2, 0, 1, 3, 4),
          sel_valid.reshape(bsz, N_KV_HEADS, nblk, qb, n_sel).transpose(2, 0, 1, 3, 4),
          q_pos.reshape(nblk, qb))
    out = lax.map(one_block, xs)
    return out.transpose(1, 0, 2, 3, 4).reshape(bsz, t, N_HEADS, HEAD_DIM)


def banded_block(q_blk, k_blk, v_blk, q_pos, k_pos, rel_table):
    bsz, nq = q_blk.shape[:2]
    nk = k_blk.shape[1]
    qg = q_blk.reshape(bsz, nq, N_KV_HEADS, GQA_REP, HEAD_DIM)
    logits = jnp.einsum('bqgrd,bkgd->bgrqk', qg, k_blk, preferred_element_type=jnp.float32) * ATTN_SCALE
    dist = q_pos[:, None] - k_pos[None, :]
    mask = (dist >= 0) & (dist <= WINDOW) & (k_pos[None, :] >= 0)
    bias = rel_table.astype(jnp.float32)[t5_bucket(dist)].reshape(nq, nk, N_KV_HEADS, GQA_REP).transpose(2, 3, 0, 1)
    logits = jnp.where(mask, logits + bias, NEG_INF)
    p = jax.nn.softmax(logits, axis=-1).astype(v_blk.dtype)
    return jnp.einsum('bgrqk,bkgd->bqgrd', p, v_blk).reshape(bsz, nq, N_HEADS, HEAD_DIM)


def window_attention(q, win_all, pos0, rel_table):
    bsz, t = q.shape[:2]
    buf_len = win_all.shape[1] - t
    kv = jnp.pad(win_all, ((0, 0), (WINDOW - buf_len, 0), (0, 0), (0, 0), (0, 0)))
    qb = math.gcd(t, WIN_Q_BLOCK)
    nblk = t // qb
    q_blocks = q.reshape(bsz, nblk, qb, N_HEADS, HEAD_DIM).transpose(1, 0, 2, 3, 4)
    starts = jnp.arange(nblk, dtype=jnp.int32) * qb

    def one(args):
        q_blk, s = args
        kvb = lax.dynamic_slice_in_dim(kv, s, qb + WINDOW, axis=1)
        q_pos = pos0 + s + jnp.arange(qb, dtype=jnp.int32)
        k_pos = pos0 - WINDOW + s + jnp.arange(qb + WINDOW, dtype=jnp.int32)
        return banded_block(q_blk, kvb[:, :, 0], kvb[:, :, 1], q_pos, k_pos, rel_table)

    out = lax.map(one, (q_blocks, starts))
    return out.transpose(1, 0, 2, 3, 4).reshape(bsz, t, N_HEADS, HEAD_DIM)


def s5_mixer(u, h0_re, h0_im, lam_re, lam_im, log_dt, b_re, b_im, c_re, c_im, d_skip, w_glu, b_glu):
    f32 = jnp.float32
    bsz, t, _ = u.shape
    ug = u.astype(f32).reshape(bsz, t, N_SSM_GROUPS, SSM_GROUP)
    lam = lax.complex(lam_re.astype(f32), lam_im.astype(f32))
    dt = jnp.exp(log_dt.astype(f32))[:, None]
    lam_bar = jnp.exp(lam * dt)
    b_c = lax.complex(b_re.astype(f32), b_im.astype(f32))
    b_bar = ((lam_bar - 1.0) / lam)[..., None] * b_c
    bu = jnp.einsum('btgh,gph->btgp', ug.astype(jnp.complex64), b_bar)
    h0 = lax.complex(h0_re.astype(f32), h0_im.astype(f32))
    bu = bu.at[:, 0].add(lam_bar * h0)
    a = jnp.broadcast_to(lam_bar, bu.shape)

    def combine(e1, e2):
        a1, b1 = e1
        a2, b2 = e2
        return a1 * a2, a2 * b1 + b2

    _, h = lax.associative_scan(combine, (a, bu), axis=1)
    c_c = lax.complex(c_re.astype(f32), c_im.astype(f32))
    y = jnp.einsum('btgp,ghp->btgh', h, c_c).real + d_skip.astype(f32).reshape(N_SSM_GROUPS, SSM_GROUP) * ug
    g = jax.nn.gelu(y.reshape(bsz, t, SSM_WIDTH))
    out = g * jax.nn.sigmoid(g @ w_glu.astype(f32) + b_glu.astype(f32))
    h_last = h[:, -1]
    return out.astype(u.dtype), h_last.real, h_last.imag


def peer_ffn(x_n, w_query, sub_keys, peer_u, peer_v):
    bsz, t, d = x_n.shape
    n = bsz * t
    xf = x_n.reshape(n, d)
    q = (xf @ w_query).reshape(n, PEER_HEADS, 2, PEER_HALF)
    s = jnp.einsum('nhcd,hckd->nhck', q, sub_keys, preferred_element_type=jnp.float32)
    s1, i1 = lax.top_k(s[:, :, 0], PEER_TOPK)
    s2, i2 = lax.top_k(s[:, :, 1], PEER_TOPK)
    cand = (s1[..., :, None] + s2[..., None, :]).reshape(n, PEER_HEADS, PEER_TOPK * PEER_TOPK)
    cidx = (i1[..., :, None] * PEER_KEYS + i2[..., None, :]).reshape(n, PEER_HEADS, PEER_TOPK * PEER_TOPK)
    top, pos = lax.top_k(cand, PEER_TOPK)
    experts = jnp.take_along_axis(cidx, pos, axis=-1)
    gate = jax.nn.softmax(top, axis=-1)
    nb = -(-n // PEER_TOKEN_BLOCK)
    pad = nb * PEER_TOKEN_BLOCK - n
    xb = jnp.pad(xf, ((0, pad), (0, 0))).reshape(nb, PEER_TOKEN_BLOCK, d)
    eb = jnp.pad(experts, ((0, pad), (0, 0), (0, 0))).reshape(nb, PEER_TOKEN_BLOCK, PEER_HEADS, PEER_TOPK)
    gb = jnp.pad(gate, ((0, pad), (0, 0), (0, 0))).reshape(nb, PEER_TOKEN_BLOCK, PEER_HEADS, PEER_TOPK)

    def one(args):
        x_blk, e_blk, g_blk = args
        act = jax.nn.gelu(jnp.einsum('td,thkd->thk', x_blk, peer_u[e_blk], preferred_element_type=jnp.float32))
        w = (g_blk * act).astype(x_blk.dtype)
        return jnp.einsum('thk,thkd->td', w, peer_v[e_blk])

    out = lax.map(one, (xb, eb, gb)).reshape(nb * PEER_TOKEN_BLOCK, d)[:n]
    return out.reshape(bsz, t, d)


def layer_forward(x, pos0, past_cmp, past_sel, win_buf, h0_re, h0_im, rel_table, norm_mix, w_in, q_norm, k_norm,
                  w_phi, b_phi, lam_re, lam_im, log_dt, b_re, b_im, c_re, c_im, d_skip, w_glu, b_glu,
                  out_norm_attn, out_norm_ssm, w_out, norm_ffn, peer_w_query, peer_sub_keys, peer_u, peer_v):
    f32 = jnp.float32
    bsz, t, _ = x.shape
    q_pos = pos0 + jnp.arange(t, dtype=jnp.int32)
    x_n = rms_norm(x, norm_mix)
    q, kv_cmp, kv_sel, kv_win, gates, u = project(x_n, w_in, q_norm, k_norm)
    cmp_all = jnp.concatenate([past_cmp, kv_cmp], axis=1)
    sel_all = jnp.concatenate([past_sel, kv_sel], axis=1)
    win_all = jnp.concatenate([win_buf, kv_win], axis=1)
    k_c, v_c = compress_blocks(cmp_all, w_phi, b_phi, k_norm[0])
    o_cmp, p_cmp = cmp_attention(q, k_c, v_c, q_pos, rel_table)
    idx, sel_valid = select_blocks(p_cmp, q_pos, -(-sel_all.shape[1] // SLC_BLOCK))
    o_sel = sel_attention(q, sel_all[:, :, 0], sel_all[:, :, 1], idx, sel_valid, q_pos, rel_table)
    o_win = window_attention(q, win_all, pos0, rel_table)
    o_attn = (gates[:, :, 0, :, None] * o_cmp.astype(f32) + gates[:, :, 1, :, None] * o_sel.astype(f32)
              + gates[:, :, 2, :, None] * o_win.astype(f32)).astype(x.dtype).reshape(bsz, t, ATTN_WIDTH)
    y_ssm, h_re, h_im = s5_mixer(u, h0_re, h0_im, lam_re, lam_im, log_dt, b_re, b_im, c_re, c_im, d_skip, w_glu, b_glu)
    mixed = jnp.concatenate([rms_norm(o_attn, out_norm_attn), rms_norm(y_ssm, out_norm_ssm)], axis=-1)
    h = x + mixed @ w_out
    y = h + peer_ffn(rms_norm(h, norm_ffn), peer_w_query, peer_sub_keys, peer_u, peer_v)
    keep = min(WINDOW, pos0 + t)
    new_win = win_all[:, win_all.shape[1] - keep:]
    return y, kv_cmp, kv_sel, new_win, h_re, h_im


def setup_inputs(seed: int = 0) -> dict:
    key = jax.random.key(seed)
    ks = jax.random.split(key, 40)
    f32 = jnp.float32

    def nrm(k, shape, scale):
        return jax.random.normal(k, shape, f32) * scale

    def gain(k, shape):
        return 1.0 + 0.02 * jax.random.normal(k, shape, f32)

    n_pages = PAST_LEN // PAGE_SIZE
    n_phys = -(-5 * DEC_BATCH * n_pages // 4)
    win_buf = min(WINDOW, PAST_LEN)
    perm = jax.random.permutation(ks[0], n_phys)
    page_table = perm[:DEC_BATCH * n_pages].reshape(DEC_BATCH, n_pages).astype(jnp.int32)
    lam_im0 = jnp.pi * jnp.arange(SSM_STATE, dtype=f32)
    return {
        'x_prompt': nrm(ks[1], (BATCH, SEQ, D_MODEL), 1.0),
        'x_sample': nrm(ks[2], (DEC_BATCH, DEC_SEQ, D_MODEL), 1.0),
        'cache_kv_cmp': nrm(ks[3], (DEPTH, n_phys, PAGE_SIZE, 2, N_KV_HEADS, HEAD_DIM), 1.0),
        'cache_kv_sel': nrm(ks[4], (DEPTH, n_phys, PAGE_SIZE, 2, N_KV_HEADS, HEAD_DIM), 1.0),
        'cache_kv_win': nrm(ks[5], (DEPTH, DEC_BATCH, win_buf, 2, N_KV_HEADS, HEAD_DIM), 1.0),
        'state_ssm_re': nrm(ks[6], (DEPTH, DEC_BATCH, N_SSM_GROUPS, SSM_STATE), 0.5),
        'state_ssm_im': nrm(ks[7], (DEPTH, DEC_BATCH, N_SSM_GROUPS, SSM_STATE), 0.5),
        'page_table': page_table,
        'rel_table': nrm(ks[8], (NUM_BUCKETS, N_HEADS), 0.3),
        'norm_mix': gain(ks[9], (DEPTH, D_MODEL)),
        'w_in': nrm(ks[10], (DEPTH, D_MODEL, IN_COLS), D_MODEL ** -0.5),
        'q_norm': gain(ks[11], (DEPTH, HEAD_DIM)),
        'k_norm': gain(ks[12], (DEPTH, 3, HEAD_DIM)),
        'w_phi': nrm(ks[13], (DEPTH, 2, CMP_BLOCK, HEAD_DIM, HEAD_DIM), (CMP_BLOCK * HEAD_DIM) ** -0.5),
        'b_phi': nrm(ks[14], (DEPTH, 2, HEAD_DIM), 0.02),
        'ssm_lam_re': -0.5 + 0.01 * jax.random.normal(ks[15], (DEPTH, N_SSM_GROUPS, SSM_STATE), f32),
        'ssm_lam_im': lam_im0 + 0.01 * jax.random.normal(ks[16], (DEPTH, N_SSM_GROUPS, SSM_STATE), f32),
        'ssm_log_dt': jax.random.uniform(ks[17], (DEPTH, N_SSM_GROUPS), f32, math.log(DT_MIN), math.log(DT_MAX)),
        'ssm_b_re': nrm(ks[18], (DEPTH, N_SSM_GROUPS, SSM_STATE, SSM_GROUP), (2 * SSM_GROUP) ** -0.5),
        'ssm_b_im': nrm(ks[19], (DEPTH, N_SSM_GROUPS, SSM_STATE, SSM_GROUP), (2 * SSM_GROUP) ** -0.5),
        'ssm_c_re': nrm(ks[20], (DEPTH, N_SSM_GROUPS, SSM_GROUP, SSM_STATE), (2 * SSM_STATE) ** -0.5),
        'ssm_c_im': nrm(ks[21], (DEPTH, N_SSM_GROUPS, SSM_GROUP, SSM_STATE), (2 * SSM_STATE) ** -0.5),
        'ssm_d': nrm(ks[22], (DEPTH, SSM_WIDTH), 1.0),
        'w_glu': nrm(ks[23], (DEPTH, SSM_WIDTH, SSM_WIDTH), SSM_WIDTH ** -0.5),
        'b_glu': nrm(ks[24], (DEPTH, SSM_WIDTH), 0.02),
        'out_norm_attn': gain(ks[25], (DEPTH, ATTN_WIDTH)),
        'out_norm_ssm': gain(ks[26], (DEPTH, SSM_WIDTH)),
        'w_out': nrm(ks[27], (DEPTH, D_MODEL, D_MODEL), D_MODEL ** -0.5),
        'norm_ffn': gain(ks[28], (DEPTH, D_MODEL)),
        'peer_w_query': nrm(ks[29], (DEPTH, D_MODEL, PEER_HEADS * PEER_QDIM), D_MODEL ** -0.5),
        'peer_sub_keys': nrm(ks[30], (DEPTH, PEER_HEADS, 2, PEER_KEYS, PEER_HALF), PEER_HALF ** -0.5),
        'peer_u': nrm(ks[31], (DEPTH, PEER_EXPERTS, D_MODEL), D_MODEL ** -0.5),
        'peer_v': nrm(ks[32], (DEPTH, PEER_EXPERTS, D_MODEL), (PEER_HEADS * PEER_TOPK) ** -0.5),
    }


def reference(x_prompt, x_sample, cache_kv_cmp, cache_kv_sel, cache_kv_win, state_ssm_re, state_ssm_im, page_table,
              rel_table, norm_mix, w_in, q_norm, k_norm, w_phi, b_phi, ssm_lam_re, ssm_lam_im, ssm_log_dt,
              ssm_b_re, ssm_b_im, ssm_c_re, ssm_c_im, ssm_d, w_glu, b_glu, out_norm_attn, out_norm_ssm, w_out,
              norm_ffn, peer_w_query, peer_sub_keys, peer_u, peer_v):
    past_len = page_table.shape[1] * PAGE_SIZE
    bp, db = x_prompt.shape[0], x_sample.shape[0]
    yp, ys = x_prompt, x_sample
    new_p = [[] for _ in range(5)]
    new_s = [[] for _ in range(5)]
    for l in range(DEPTH):
        lw = (rel_table, norm_mix[l], w_in[l], q_norm[l], k_norm[l], w_phi[l], b_phi[l], ssm_lam_re[l], ssm_lam_im[l],
              ssm_log_dt[l], ssm_b_re[l], ssm_b_im[l], ssm_c_re[l], ssm_c_im[l], ssm_d[l], w_glu[l], b_glu[l],
              out_norm_attn[l], out_norm_ssm[l], w_out[l], norm_ffn[l], peer_w_query[l], peer_sub_keys[l],
              peer_u[l], peer_v[l])
        empty = jnp.zeros((bp, 0, 2, N_KV_HEADS, HEAD_DIM), x_prompt.dtype)
        h0 = jnp.zeros((bp, N_SSM_GROUPS, SSM_STATE), jnp.float32)
        outs = layer_forward(yp, 0, empty, empty, empty, h0, h0, *lw)
        yp = outs[0]
        for i in range(5):
            new_p[i].append(outs[i + 1])
        past_cmp = cache_kv_cmp[l][page_table].reshape(db, past_len, 2, N_KV_HEADS, HEAD_DIM)
        past_sel = cache_kv_sel[l][page_table].reshape(db, past_len, 2, N_KV_HEADS, HEAD_DIM)
        outs = layer_forward(ys, past_len, past_cmp, past_sel, cache_kv_win[l], state_ssm_re[l], state_ssm_im[l], *lw)
        ys = outs[0]
        for i in range(5):
            new_s[i].append(outs[i + 1])
    kv_cmp_p = jnp.stack(new_p[0])
    kv_sel_p = jnp.stack(new_p[1])
    kv_win_p = jnp.stack(new_p[2])
    ssm_re_p = jnp.stack(new_p[3])
    ssm_im_p = jnp.stack(new_p[4])
    kv_cmp_s = jnp.stack(new_s[0])
    kv_sel_s = jnp.stack(new_s[1])
    kv_win_s = jnp.stack(new_s[2])
    ssm_re_s = jnp.stack(new_s[3])
    ssm_im_s = jnp.stack(new_s[4])
    return (yp, ys, kv_cmp_p, kv_sel_p, kv_win_p, ssm_re_p, ssm_im_p, kv_cmp_s, kv_sel_s, kv_win_s, ssm_re_s, ssm_im_s)
```

```python
import functools
import math

import numpy as np
import jax
import jax.numpy as jnp
from jax import lax
from jax.experimental import pallas as pl
from jax.experimental.pallas import tpu as pltpu

F32 = jnp.float32
MXU_DTYPE = jnp.bfloat16

HEAD_DIM = 128
N_KV_HEADS = 4
GQA_REP = 4
N_HEADS = N_KV_HEADS * GQA_REP
ATTN_WIDTH = N_HEADS * HEAD_DIM
KV_WIDTH = N_KV_HEADS * HEAD_DIM
ATTN_SCALE = HEAD_DIM ** -0.5
CMP_BLOCK = 32
CMP_STRIDE = 16
CMP_R = CMP_BLOCK // CMP_STRIDE
SLC_BLOCK = 64
N_SELECT = 16
FORCE_BONUS = 1.0e4
WINDOW = 512
NUM_BUCKETS = 32
MAX_DISTANCE = 128
SSM_GROUP = 16
SSM_STATE = 64
PEER_HEADS = 8
PEER_KEYS = 128
PEER_TOPK = 16
PEER_HALF = 128
EPS = 1e-6
NEG_INF = -1e30
PAGE_SIZE = 128

LANES = 128
SUBLANES = 8
MXU_DIM = 256
VMEM_LIMIT_BYTES = 52 * 1024 * 1024

KEY_TILE = 128
WIN_TILES = WINDOW // KEY_TILE
SSM_CHUNK = 16
SAMPLE_ROWS = 8


def _cparams(sem):
    return pltpu.CompilerParams(dimension_semantics=sem, vmem_limit_bytes=VMEM_LIMIT_BYTES)


def _row_tile(n, pref):
    t = min(n, pref)
    while n % t:
        t -= SUBLANES
    return t


def _dot(a, b):
    return jnp.dot(a, b, preferred_element_type=F32)


def _dot_nt(a, b):
    return lax.dot_general(a, b, (((1,), (1,)), ((), ())), preferred_element_type=F32)


def _gelu(x):
    c = math.sqrt(2.0 / math.pi)
    return 0.5 * x * (1.0 + jnp.tanh(c * (x + 0.044715 * (x * x * x))))


def _sigmoid(x):
    return 1.0 / (1.0 + jnp.exp(-x))


def _t5_bucket(dist):
    n = jnp.maximum(dist, 0)
    max_exact = NUM_BUCKETS // 2
    nf = jnp.maximum(n, 1).astype(F32)
    large = max_exact + (jnp.log(nf / max_exact) / math.log(MAX_DISTANCE / max_exact)
                         * (NUM_BUCKETS - max_exact)).astype(jnp.int32)
    large = jnp.minimum(large, NUM_BUCKETS - 1)
    return jnp.where(n < max_exact, n, large)


def _rmsnorm_kernel(x_ref, g_ref, o_ref):
    x = x_ref[...]
    ms = jnp.mean(x * x, axis=-1, keepdims=True)
    o_ref[...] = (x * lax.rsqrt(ms + EPS) * g_ref[...]).astype(o_ref.dtype)


def _rmsnorm(x, g, out_dtype):
    n, d = x.shape
    tm = _row_tile(n, 256)
    return pl.pallas_call(
        _rmsnorm_kernel,
        grid=(n // tm,),
        in_specs=[pl.BlockSpec((tm, d), lambda i: (i, 0)), pl.BlockSpec((1, d), lambda i: (0, 0))],
        out_specs=pl.BlockSpec((tm, d), lambda i: (i, 0)),
        out_shape=jax.ShapeDtypeStruct((n, d), out_dtype),
        compiler_params=_cparams(("parallel",)),
    )(x, g.reshape(1, d))


def _matmul_kernel(*refs, epilogue):
    a_ref, b_ref = refs[0], refs[1]
    o_ref = refs[-1]
    acc = _dot(a_ref[...], b_ref[...])
    if epilogue == "none":
        o_ref[...] = acc
    elif epilogue == "sigmoid":
        o_ref[...] = _sigmoid(acc)
    elif epilogue == "residual":
        o_ref[...] = refs[2][...] + acc
    elif epilogue == "headnorm":
        mode_ref, gain_ref = refs[2], refs[3]
        for c in range(acc.shape[1] // HEAD_DIM):
            sl = slice(c * HEAD_DIM, (c + 1) * HEAD_DIM)
            blk = acc[:, sl]
            ms = jnp.mean(blk * blk, axis=-1, keepdims=True)
            nrm = blk * lax.rsqrt(ms + EPS) * gain_ref[:, sl]
            o_ref[:, sl] = jnp.where(mode_ref[:, sl] > 0.0, nrm, blk)
    else:
        raise ValueError(epilogue)


def _matmul(a, b, epilogue="none", extras=(), tm_pref=512, tn_pref=1024):
    m, k = a.shape
    _, n = b.shape
    tm = _row_tile(m, tm_pref)
    tn = min(n, tn_pref)
    while n % tn:
        tn -= LANES
    in_specs = [pl.BlockSpec((tm, k), lambda i, j: (i, 0)), pl.BlockSpec((k, tn), lambda i, j: (0, j))]
    for e in extras:
        if e.shape[0] == 1:
            in_specs.append(pl.BlockSpec((1, tn), lambda i, j: (0, j)))
        else:
            in_specs.append(pl.BlockSpec((tm, tn), lambda i, j: (i, j)))
    return pl.pallas_call(
        functools.partial(_matmul_kernel, epilogue=epilogue),
        grid=(m // tm, n // tn),
        in_specs=in_specs,
        out_specs=pl.BlockSpec((tm, tn), lambda i, j: (i, j)),
        out_shape=jax.ShapeDtypeStruct((m, n), F32),
        compiler_params=_cparams(("parallel", "parallel")),
    )(a, b, *extras)


CMP_PAGES_PER_STEP = 8


def _compress_kernel(pt_ref, *refs):
    del pt_ref
    pages = refs[:CMP_PAGES_PER_STEP]
    w_ref = refs[CMP_PAGES_PER_STEP]
    o_ref = refs[CMP_PAGES_PER_STEP + 1]
    x_scr = refs[CMP_PAGES_PER_STEP + 2]
    chunks_per_page = PAGE_SIZE // CMP_STRIDE
    rows_per_head = CMP_PAGES_PER_STEP * chunks_per_page
    heads_per_row = 2 * N_KV_HEADS
    for c in range(2):
        for g in range(N_KV_HEADS):
            for i in range(CMP_PAGES_PER_STEP):
                r0 = g * rows_per_head + i * chunks_per_page
                for s in range(CMP_STRIDE):
                    x_scr[r0:r0 + chunks_per_page, s * HEAD_DIM:(s + 1) * HEAD_DIM] = pages[i][
                        0, pl.ds(s * heads_per_row + c * N_KV_HEADS + g, chunks_per_page,
                                 stride=CMP_STRIDE * heads_per_row), :]
        res = _dot(x_scr[...].astype(MXU_DTYPE), w_ref[c])
        for g in range(N_KV_HEADS):
            o_ref[0, c * N_KV_HEADS + g] = res[g * rows_per_head:(g + 1) * rows_per_head]


def _compress(rows3, page_table, w_cmp):
    nb, n_pages = page_table.shape
    assert n_pages % CMP_PAGES_PER_STEP == 0
    n_steps = n_pages // CMP_PAGES_PER_STEP
    chunks_per_step = CMP_PAGES_PER_STEP * PAGE_SIZE // CMP_STRIDE
    n_chunks = n_pages * PAGE_SIZE // CMP_STRIDE

    def page_spec(i):
        return pl.BlockSpec((1, PAGE_SIZE * 2 * N_KV_HEADS, HEAD_DIM),
                            lambda b, j, pt: (pt[b, j * CMP_PAGES_PER_STEP + i], 0, 0))

    grid_spec = pltpu.PrefetchScalarGridSpec(
        num_scalar_prefetch=1,
        grid=(nb, n_steps),
        in_specs=[page_spec(i) for i in range(CMP_PAGES_PER_STEP)]
        + [pl.BlockSpec((2, CMP_STRIDE * HEAD_DIM, CMP_R * HEAD_DIM), lambda b, j, pt: (0, 0, 0))],
        out_specs=pl.BlockSpec((1, 2 * N_KV_HEADS, chunks_per_step, CMP_R * HEAD_DIM),
                               lambda b, j, pt: (b, 0, j, 0)),
        scratch_shapes=[pltpu.VMEM((N_KV_HEADS * chunks_per_step, CMP_STRIDE * HEAD_DIM), F32)],
    )
    return pl.pallas_call(
        _compress_kernel,
        grid_spec=grid_spec,
        out_shape=jax.ShapeDtypeStruct((nb, 2 * N_KV_HEADS, n_chunks, CMP_R * HEAD_DIM), F32),
        compiler_params=_cparams(("parallel", "arbitrary")),
    )(page_table, *([rows3] * CMP_PAGES_PER_STEP), w_cmp)


def _cmp_attn_kernel(q_ref, kp_ref, vp_ref, bphi_ref, kn_ref, bias_ref, m_ref, o_ref, sel_ref,
                     *, tq, pos0, n_cmp, n_slc):
    i = pl.program_id(2)
    n_chunks = kp_ref.shape[2]
    kp = kp_ref[0, 0]
    vp = vp_ref[0, 0]
    k_c = bphi_ref[0:1, :] + kp[:, :HEAD_DIM] + pltpu.roll(kp[:, HEAD_DIM:], n_chunks - 1, 0)
    v_c = bphi_ref[1:2, :] + vp[:, :HEAD_DIM] + pltpu.roll(vp[:, HEAD_DIM:], n_chunks - 1, 0)
    ms = jnp.mean(k_c * k_c, axis=-1, keepdims=True)
    k_c = k_c * lax.rsqrt(ms + EPS) * kn_ref[...]

    q = jnp.concatenate([q_ref[0, :, r * HEAD_DIM:(r + 1) * HEAD_DIM] for r in range(GQA_REP)], axis=0)
    logits = _dot_nt(q.astype(MXU_DTYPE), k_c.astype(MXU_DTYPE)) * ATTN_SCALE
    bias = jnp.concatenate([bias_ref[r] for r in range(GQA_REP)], axis=0)
    rows = GQA_REP * tq
    row = lax.broadcasted_iota(jnp.int32, (rows, n_chunks), 0) & (tq - 1)
    col = lax.broadcasted_iota(jnp.int32, (rows, n_chunks), 1)
    q_pos = pos0 + i * tq + row
    dist = q_pos - (col * CMP_STRIDE + (CMP_BLOCK - 1))
    valid = (dist >= 0) & (col < n_cmp)
    logits = jnp.where(valid, logits + bias, NEG_INF)
    mx = jnp.max(logits, axis=-1, keepdims=True)
    e = jnp.exp(logits - mx)
    any_valid = (jnp.max(jnp.where(valid, 1.0, 0.0), axis=-1, keepdims=True))
    p = e / jnp.sum(e, axis=-1, keepdims=True) * any_valid
    out = _dot(p.astype(MXU_DTYPE), v_c.astype(MXU_DTYPE))
    for r in range(GQA_REP):
        o_ref[0, :, r * HEAD_DIM:(r + 1) * HEAD_DIM] = out[r * tq:(r + 1) * tq]

    psum = p[0:tq]
    for r in range(1, GQA_REP):
        psum = psum + p[r * tq:(r + 1) * tq]
    hi = psum.astype(MXU_DTYPE)
    lo = (psum - hi.astype(F32)).astype(MXU_DTYPE)
    imp = _dot(hi, m_ref[...]) + _dot(lo, m_ref[...])
    n_pad = imp.shape[1]
    j = lax.broadcasted_iota(jnp.int32, (tq, n_pad), 1)
    qp = pos0 + i * tq + lax.broadcasted_iota(jnp.int32, (tq, n_pad), 0)
    cur = jnp.right_shift(qp, int(math.log2(SLC_BLOCK)))
    ok = j * SLC_BLOCK <= qp
    forced = (j == 0) | (j == cur) | (j == cur - 1)
    score = jnp.where(ok, imp + jnp.where(forced, FORCE_BONUS, 0.0), NEG_INF)
    rank = jnp.zeros((tq, n_pad), F32)
    for s in range(n_slc):
        cs = score[:, s:s + 1]
        beats = (cs > score) | ((cs == score) & (j > s))
        rank = rank + jnp.where(beats, 1.0, 0.0)
    selected = (rank < float(min(N_SELECT, n_slc))) & (score > 0.5 * NEG_INF)
    sel_ref[0, 0] = jnp.where(selected, 1.0, 0.0)


def _cmp_attention(p3, pk, bphi, k_norm0, bias_cmp, m_mat, *, pos0, n_cmp, n_slc, tq):
    nb, t, _ = p3.shape
    n_chunks = pk.shape[2]
    n_pad = m_mat.shape[1]
    nq = t // tq
    kern = functools.partial(_cmp_attn_kernel, tq=tq, pos0=pos0, n_cmp=n_cmp, n_slc=n_slc)
    return pl.pallas_call(
        kern,
        grid=(nb, N_KV_HEADS, nq),
        in_specs=[
            pl.BlockSpec((1, tq, GQA_REP * HEAD_DIM), lambda b, g, i: (b, i, g)),
            pl.BlockSpec((1, 1, n_chunks, CMP_R * HEAD_DIM), lambda b, g, i: (b, g, 0, 0)),
            pl.BlockSpec((1, 1, n_chunks, CMP_R * HEAD_DIM), lambda b, g, i: (b, N_KV_HEADS + g, 0, 0)),
            pl.BlockSpec((2, HEAD_DIM), lambda b, g, i: (0, 0)),
            pl.BlockSpec((1, HEAD_DIM), lambda b, g, i: (0, 0)),
            pl.BlockSpec((GQA_REP, tq, n_chunks), lambda b, g, i: (g, i, 0)),
            pl.BlockSpec((n_chunks, n_pad), lambda b, g, i: (0, 0)),
        ],
        out_specs=[
            pl.BlockSpec((1, tq, GQA_REP * HEAD_DIM), lambda b, g, i: (b, i, g)),
            pl.BlockSpec((1, 1, tq, n_pad), lambda b, g, i: (b, g, i, 0)),
        ],
        out_shape=[
            jax.ShapeDtypeStruct((nb, t, ATTN_WIDTH), F32),
            jax.ShapeDtypeStruct((nb, N_KV_HEADS, t, n_pad), F32),
        ],
        compiler_params=_cparams(("parallel", "parallel", "parallel")),
    )(p3, pk, pk, bphi, k_norm0, bias_cmp, m_mat)


def _flash_kernel(*refs, mode, tq, has_new, paged, qt_off):
    it = iter(refs)
    if paged:
        next(it)
    q_ref, k_ref, v_ref = next(it), next(it), next(it)
    if has_new:
        kn_ref, vn_ref = next(it), next(it)
    bias_ref = next(it)
    if mode == "sel":
        sel_ref, e_ref = next(it), next(it)
    o_ref, m_scr, l_scr, acc_scr = next(it), next(it), next(it), next(it)

    kk = pl.program_id(3)
    qt = pl.program_id(2) + qt_off
    if mode == "sel":
        kt = kk
        active = kt <= qt
    else:
        kt = qt - WIN_TILES + kk
        active = kt >= 0
    rows = GQA_REP * tq

    @pl.when(kk == 0)
    def _():
        m_scr[...] = jnp.full(m_scr.shape, NEG_INF, F32)
        l_scr[...] = jnp.zeros(l_scr.shape, F32)
        acc_scr[...] = jnp.zeros(acc_scr.shape, F32)

    def step(k, v):
        q = jnp.concatenate([q_ref[0, :, r * HEAD_DIM:(r + 1) * HEAD_DIM] for r in range(GQA_REP)], axis=0)
        s = _dot_nt(q.astype(MXU_DTYPE), k.astype(MXU_DTYPE)) * ATTN_SCALE
        bias = jnp.concatenate([bias_ref[0, r, 0:tq, :] for r in range(GQA_REP)], axis=0)
        row = lax.broadcasted_iota(jnp.int32, (rows, KEY_TILE), 0) & (tq - 1)
        col = lax.broadcasted_iota(jnp.int32, (rows, KEY_TILE), 1)
        dist = (qt - kt) * KEY_TILE + row - col
        if mode == "sel":
            chosen = _dot(sel_ref[0, 0].astype(MXU_DTYPE), e_ref[...])
            chosen = jnp.concatenate([chosen] * GQA_REP, axis=0)
            mask = (dist >= 0) & (chosen > 0.5)
        else:
            mask = (dist >= 0) & (dist <= WINDOW)
        s = jnp.where(mask, s + bias, NEG_INF)
        m_old = m_scr[...]
        m_new = jnp.maximum(m_old, jnp.max(s, axis=-1, keepdims=True))
        alpha = jnp.exp(m_old - m_new)
        p = jnp.where(mask, jnp.exp(s - m_new), 0.0)
        l_scr[...] = alpha * l_scr[...] + jnp.sum(p, axis=-1, keepdims=True)
        acc_scr[...] = alpha * acc_scr[...] + _dot(p.astype(MXU_DTYPE), v.astype(MXU_DTYPE))
        m_scr[...] = m_new

    if has_new:
        pad = jnp.zeros((KEY_TILE - tq, HEAD_DIM), F32)

        @pl.when(kt == qt)
        def _():
            step(jnp.concatenate([kn_ref[0], pad], axis=0), jnp.concatenate([vn_ref[0], pad], axis=0))

        @pl.when(active & (kt != qt))
        def _():
            step(k_ref[0], v_ref[0])
    else:
        @pl.when(active)
        def _():
            step(k_ref[0], v_ref[0])

    @pl.when(kk == pl.num_programs(3) - 1)
    def _():
        out = acc_scr[...] / l_scr[...]
        for r in range(GQA_REP):
            o_ref[0, :, r * HEAD_DIM:(r + 1) * HEAD_DIM] = out[r * tq:(r + 1) * tq]


def _flash_scratch(tq):
    rows = GQA_REP * tq
    return [pltpu.VMEM((rows, 1), F32), pltpu.VMEM((rows, 1), F32), pltpu.VMEM((rows, HEAD_DIM), F32)]


def _attn_prompt(p3, bias_tiles, mode, k_col, v_col, sel=None, e_mat=None):
    nb, t, _ = p3.shape
    tq = KEY_TILE
    nq = t // tq
    nk = nq if mode == "sel" else WIN_TILES + 1

    def kt_of(i, kk):
        if mode == "sel":
            return jnp.minimum(kk, i)
        return jnp.maximum(i - WIN_TILES + kk, 0)

    def tile_kind(i, kk):
        return jnp.clip(i - kt_of(i, kk), 0, 2)

    in_specs = [
        pl.BlockSpec((1, tq, GQA_REP * HEAD_DIM), lambda b, g, i, kk: (b, i, g)),
        pl.BlockSpec((1, KEY_TILE, HEAD_DIM), lambda b, g, i, kk: (b, kt_of(i, kk), k_col + g)),
        pl.BlockSpec((1, KEY_TILE, HEAD_DIM), lambda b, g, i, kk: (b, kt_of(i, kk), v_col + g)),
        pl.BlockSpec((1, GQA_REP, KEY_TILE, KEY_TILE), lambda b, g, i, kk: (tile_kind(i, kk), g, 0, 0)),
    ]
    args = [p3, p3, p3, bias_tiles]
    if mode == "sel":
        n_pad = sel.shape[-1]
        in_specs += [
            pl.BlockSpec((1, 1, tq, n_pad), lambda b, g, i, kk: (b, g, i, 0)),
            pl.BlockSpec((n_pad, KEY_TILE), lambda b, g, i, kk: (0, kt_of(i, kk))),
        ]
        args += [sel, e_mat]
    kern = functools.partial(_flash_kernel, mode=mode, tq=tq, has_new=False, paged=False, qt_off=0)
    return pl.pallas_call(
        kern,
        grid=(nb, N_KV_HEADS, nq, nk),
        in_specs=in_specs,
        out_specs=pl.BlockSpec((1, tq, GQA_REP * HEAD_DIM), lambda b, g, i, kk: (b, i, g)),
        out_shape=jax.ShapeDtypeStruct((nb, t, ATTN_WIDTH), F32),
        scratch_shapes=_flash_scratch(tq),
        compiler_params=_cparams(("parallel", "parallel", "parallel", "arbitrary")),
    )(*args)


def _attn_sample(p3, cache3, page_table, bias_tiles, mode, k_col, v_col, sel=None, e_mat=None):
    nb, tq, _ = p3.shape
    paged = page_table is not None
    n_tiles = page_table.shape[1] if paged else cache3.shape[1] // KEY_TILE
    if mode == "win":
        assert n_tiles == WIN_TILES
    nk = n_tiles + 1

    def cache_spec(col):
        if paged:
            return pl.BlockSpec((1, KEY_TILE, HEAD_DIM),
                                lambda b, g, i, kk, pt: (pt[b, jnp.minimum(kk, n_tiles - 1)], 0, col + g))
        return pl.BlockSpec((1, KEY_TILE, HEAD_DIM),
                            lambda b, g, i, kk, *_: (b, jnp.minimum(kk, n_tiles - 1), col + g))

    def fixed(shape, fn):
        return pl.BlockSpec(shape, lambda b, g, i, kk, *_: fn(b, g, kk))

    in_specs = [
        fixed((1, tq, GQA_REP * HEAD_DIM), lambda b, g, kk: (b, 0, g)),
        cache_spec(0),
        cache_spec(N_KV_HEADS),
        fixed((1, tq, HEAD_DIM), lambda b, g, kk: (b, 0, k_col + g)),
        fixed((1, tq, HEAD_DIM), lambda b, g, kk: (b, 0, v_col + g)),
        fixed((1, GQA_REP, KEY_TILE, KEY_TILE), lambda b, g, kk: (jnp.clip(n_tiles - kk, 0, 2), g, 0, 0)),
    ]
    args = [p3, cache3, cache3, p3, p3, bias_tiles]
    if mode == "sel":
        n_pad = sel.shape[-1]
        in_specs += [
            fixed((1, 1, tq, n_pad), lambda b, g, kk: (b, g, 0, 0)),
            fixed((n_pad, KEY_TILE), lambda b, g, kk: (0, kk)),
        ]
        args += [sel, e_mat]
    kern = functools.partial(_flash_kernel, mode=mode, tq=tq, has_new=True, paged=paged, qt_off=n_tiles)
    out_spec = fixed((1, tq, GQA_REP * HEAD_DIM), lambda b, g, kk: (b, 0, g))
    out_shape = jax.ShapeDtypeStruct((nb, tq, ATTN_WIDTH), F32)
    cp = _cparams(("parallel", "parallel", "parallel", "arbitrary"))
    if paged:
        grid_spec = pltpu.PrefetchScalarGridSpec(
            num_scalar_prefetch=1, grid=(nb, N_KV_HEADS, 1, nk), in_specs=in_specs, out_specs=out_spec,
            scratch_shapes=_flash_scratch(tq))
        return pl.pallas_call(kern, grid_spec=grid_spec, out_shape=out_shape, compiler_params=cp)(
            page_table, *args)
    return pl.pallas_call(kern, grid=(nb, N_KV_HEADS, 1, nk), in_specs=in_specs, out_specs=out_spec,
                          out_shape=out_shape, scratch_shapes=_flash_scratch(tq), compiler_params=cp)(*args)


def _combine_kernel(oc_ref, os_ref, ow_ref, gate_ref, g_ref, o_ref):
    gates = gate_ref[...]
    parts = []
    for h in range(N_HEADS):
        sl = slice(h * HEAD_DIM, (h + 1) * HEAD_DIM)
        parts.append(gates[:, h:h + 1] * oc_ref[:, sl]
                     + gates[:, N_HEADS + h:N_HEADS + h + 1] * os_ref[:, sl]
                     + gates[:, 2 * N_HEADS + h:2 * N_HEADS + h + 1] * ow_ref[:, sl])
    o = jnp.concatenate(parts, axis=1)
    ms = jnp.mean(o * o, axis=-1, keepdims=True)
    o_ref[...] = (o * lax.rsqrt(ms + EPS) * g_ref[...]).astype(o_ref.dtype)


def _combine(o_cmp, o_sel, o_win, gates, gain):
    n, w = o_cmp.shape
    tm = _row_tile(n, 256)
    row = pl.BlockSpec((tm, w), lambda i: (i, 0))
    return pl.pallas_call(
        _combine_kernel,
        grid=(n // tm,),
        in_specs=[row, row, row, pl.BlockSpec((tm, LANES), lambda i: (i, 0)),
                  pl.BlockSpec((1, w), lambda i: (0, 0))],
        out_specs=row,
        out_shape=jax.ShapeDtypeStruct((n, w), MXU_DTYPE),
        compiler_params=_cparams(("parallel",)),
    )(o_cmp, o_sel, o_win, gates, gain.reshape(1, w))


SSM_GROUPS_PER_STEP = 8


def _ssm_kernel(u_ref, tz_ref, bs_ref, cs_ref, d_ref, la_ref, lb_ref, h0_ref, y_ref, hout_ref,
                s_scr, hs_scr, *, n_seq, n_chunk):
    gb = SSM_GROUPS_PER_STEP
    for gi in range(gb):
        ub = u_ref[gi].astype(MXU_DTYPE)
        y_ref[gi] = _dot(ub, tz_ref[gi])
        s_scr[gi] = _dot(ub, bs_ref[gi])
    la = la_ref[...]
    lb = lb_ref[...]

    def body(c, h):
        r0 = c * n_seq
        hs_scr[:, pl.ds(r0, n_seq), :] = h
        swapped = jnp.concatenate([h[..., SSM_STATE:], h[..., :SSM_STATE]], axis=-1)
        return la * h + lb * swapped + s_scr[:, pl.ds(r0, n_seq), :]

    h = lax.fori_loop(0, n_chunk, body, h0_ref[...])
    hout_ref[...] = h
    for gi in range(gb):
        y_ref[gi] = y_ref[gi] + _dot(hs_scr[gi].astype(MXU_DTYPE), cs_ref[gi]) + d_ref[gi] * u_ref[gi]


def _ssm(u_g, tz, bs, cs, dvec, la, lb, h0, *, n_seq, n_chunk):
    g, r, w = u_g.shape
    gb = SSM_GROUPS_PER_STEP
    st = 2 * SSM_STATE

    def blk(shape):
        return pl.BlockSpec((gb,) + shape, lambda i: (i,) + (0,) * len(shape))

    return pl.pallas_call(
        functools.partial(_ssm_kernel, n_seq=n_seq, n_chunk=n_chunk),
        grid=(g // gb,),
        in_specs=[blk((r, w)), blk((w, w)), blk((w, st)), blk((st, w)), blk((1, w)), blk((1, st)),
                  blk((1, st)), blk((n_seq, st))],
        out_specs=[blk((r, w)), blk((n_seq, st))],
        out_shape=[jax.ShapeDtypeStruct((g, r, w), F32), jax.ShapeDtypeStruct((g, n_seq, st), F32)],
        scratch_shapes=[pltpu.VMEM((gb, r, st), F32), pltpu.VMEM((gb, r, st), F32)],
        compiler_params=_cparams(("parallel",)),
    )(u_g, tz, bs, cs, dvec, la, lb, h0)


def _ssm_matrices(lam_re, lam_im, log_dt, b_re, b_im, c_re, c_im, d_skip, l_eff):
    g = lam_re.shape[0]
    L = SSM_CHUNK
    lam = lax.complex(lam_re.astype(F32), lam_im.astype(F32))
    dt = jnp.exp(log_dt.astype(F32))[:, None]
    lam_bar = jnp.exp(lam * dt)
    b_bar = ((lam_bar - 1.0) / lam)[..., None] * lax.complex(b_re.astype(F32), b_im.astype(F32))
    c_c = lax.complex(c_re.astype(F32), c_im.astype(F32))
    pw = [jnp.ones_like(lam_bar)]
    for _ in range(L):
        pw.append(pw[-1] * lam_bar)
    pw = jnp.stack(pw)
    kern = jnp.einsum("gop,kgp,gpi->gkoi", c_c, pw[:L], b_bar).real
    s_idx = np.arange(L)[:, None]
    t_idx = np.arange(L)[None, :]
    tau = t_idx - s_idx
    tz = kern[:, np.clip(tau, 0, L - 1)]
    tz = jnp.where(jnp.asarray(tau >= 0)[None, :, :, None, None], tz, 0.0)
    tz = tz.transpose(0, 1, 4, 2, 3).reshape(g, L * SSM_GROUP, L * SSM_GROUP)
    exps = np.clip(l_eff - 1 - np.arange(L), 0, L)
    bx = pw[exps][..., None] * b_bar[None]
    bx = jnp.where(jnp.asarray(np.arange(L) < l_eff)[:, None, None, None], bx, 0.0)
    bs = jnp.concatenate([bx.real, bx.imag], axis=2)
    bs = bs.transpose(1, 0, 3, 2).reshape(g, L * SSM_GROUP, 2 * SSM_STATE)
    mt = c_c[None] * pw[1:L + 1][:, :, None, :]
    cs = jnp.concatenate([mt.real, -mt.imag], axis=3)
    cs = cs.transpose(1, 3, 0, 2).reshape(g, 2 * SSM_STATE, L * SSM_GROUP)
    lam_l = pw[l_eff]
    la = jnp.concatenate([lam_l.real, lam_l.real], axis=1)[:, None, :]
    lb = jnp.concatenate([-lam_l.imag, lam_l.imag], axis=1)[:, None, :]
    dvec = jnp.tile(d_skip.astype(F32).reshape(g, 1, SSM_GROUP), (1, 1, L))
    return tz.astype(MXU_DTYPE), bs.astype(MXU_DTYPE), cs.astype(MXU_DTYPE), dvec, la, lb


def _glu_kernel(y_ref, w_ref, b_ref, g_ref, o_ref):
    gl = _gelu(y_ref[...])
    z = _dot(gl.astype(MXU_DTYPE), w_ref[...]) + b_ref[...]
    o = gl * _sigmoid(z)
    ms = jnp.mean(o * o, axis=-1, keepdims=True)
    o_ref[...] = (o * lax.rsqrt(ms + EPS) * g_ref[...]).astype(o_ref.dtype)


def _glu(y, w, b, gain):
    n, d = y.shape
    tm = _row_tile(n, 256)
    vec = pl.BlockSpec((1, d), lambda i: (0, 0))
    return pl.pallas_call(
        _glu_kernel,
        grid=(n // tm,),
        in_specs=[pl.BlockSpec((tm, d), lambda i: (i, 0)), pl.BlockSpec((d, d), lambda i: (0, 0)), vec, vec],
        out_specs=pl.BlockSpec((tm, d), lambda i: (i, 0)),
        out_shape=jax.ShapeDtypeStruct((n, d), MXU_DTYPE),
        compiler_params=_cparams(("parallel",)),
    )(y, w, b.reshape(1, d), gain.reshape(1, d))


def _top_rows(x, k):
    n = x.shape[0]
    idx_iota = lax.broadcasted_iota(jnp.int32, x.shape, 0)
    vals, idxs = [], []
    cur = x
    for _ in range(k):
        m = jnp.max(cur, axis=0, keepdims=True)
        ix = jnp.min(jnp.where(cur == m, idx_iota, n), axis=0, keepdims=True)
        vals.append(m)
        idxs.append(ix)
        cur = jnp.where(idx_iota == ix, -jnp.inf, cur)
    return jnp.concatenate(vals, axis=0), jnp.concatenate(idxs, axis=0)


def _pick_rows(table, sel):
    out = jnp.zeros(sel.shape, table.dtype)
    for a in range(table.shape[0]):
        out = out + jnp.where(sel == a, table[a:a + 1, :], 0)
    return out


def _peer_topk_kernel(q_ref, keys_ref, i1_ref, i2_ref, gate_ref):
    q = q_ref[...].astype(MXU_DTYPE)
    s1 = _dot_nt(keys_ref[0, 0].astype(MXU_DTYPE), q[:, :PEER_HALF])
    s2 = _dot_nt(keys_ref[0, 1].astype(MXU_DTYPE), q[:, PEER_HALF:])
    v1, x1 = _top_rows(s1, PEER_TOPK)
    v2, x2 = _top_rows(s2, PEER_TOPK)
    t = v1.shape[1]
    cand = (v1[:, None, :] + v2[None, :, :]).reshape(PEER_TOPK * PEER_TOPK, t)
    top, pos = _top_rows(cand, PEER_TOPK)
    i1_ref[0] = _pick_rows(x1, jnp.right_shift(pos, int(math.log2(PEER_TOPK))))
    i2_ref[0] = _pick_rows(x2, pos & (PEER_TOPK - 1))
    e = jnp.exp(top - jnp.max(top, axis=0, keepdims=True))
    gate_ref[0] = e / jnp.sum(e, axis=0, keepdims=True)


def _peer_topk(q, sub_keys):
    n = q.shape[0]
    tt = _row_tile(n, 512)
    out = pl.BlockSpec((1, PEER_TOPK, tt), lambda i, h: (h, 0, i))
    shp = (PEER_HEADS, PEER_TOPK, n)
    return pl.pallas_call(
        _peer_topk_kernel,
        grid=(n // tt, PEER_HEADS),
        in_specs=[pl.BlockSpec((tt, 2 * PEER_HALF), lambda i, h: (i, h)),
                  pl.BlockSpec((1, 2, PEER_KEYS, PEER_HALF), lambda i, h: (h, 0, 0, 0))],
        out_specs=[out, out, out],
        out_shape=[jax.ShapeDtypeStruct(shp, jnp.int32), jax.ShapeDtypeStruct(shp, jnp.int32),
                   jax.ShapeDtypeStruct(shp, F32)],
        compiler_params=_cparams(("parallel", "parallel")),
    )(q, sub_keys)


PEER_EXPERT_BLOCK = 2 * PEER_KEYS


def _peer_kernel(x_ref, u_ref, v_ref, i1_ref, i2_ref, gate_ref, h_ref, o_ref, g_scr, *, tm):
    j = pl.program_id(1)
    n_entries = PEER_HEADS * PEER_TOPK

    @pl.when(j == 0)
    def _():
        o_ref[...] = h_ref[...]
        sub = lax.broadcasted_iota(jnp.int32, (PEER_KEYS, n_entries), 0)

        def build(n, carry):
            a = jnp.where(sub == i1_ref[pl.ds(n, 1), :], gate_ref[pl.ds(n, 1), :], 0.0)
            b = jnp.where(sub == i2_ref[pl.ds(n, 1), :], 1.0, 0.0)
            g_scr[pl.ds(pl.multiple_of(n * PEER_KEYS, PEER_KEYS), PEER_KEYS), :] = _dot_nt(
                a.astype(MXU_DTYPE), b.astype(MXU_DTYPE))
            return carry

        lax.fori_loop(0, tm, build, 0)

    act = _dot_nt(x_ref[...], u_ref[...])
    rows_per_step = PEER_EXPERT_BLOCK // PEER_KEYS
    g = jnp.concatenate(
        [g_scr[pl.ds(j * rows_per_step + r, tm, stride=PEER_KEYS), :] for r in range(rows_per_step)], axis=1)
    w = (g * _gelu(act)).astype(MXU_DTYPE)
    o_ref[...] += _dot(w, v_ref[...])


def _peer(xn, u, v, i1, i2, gate, h):
    n, d = xn.shape
    n_exp = u.shape[0]
    tm = _row_tile(n, 256)
    n_entries = PEER_HEADS * PEER_TOPK
    row = pl.BlockSpec((tm, d), lambda i, j: (i, 0))
    ent = pl.BlockSpec((tm, n_entries), lambda i, j: (i, 0))
    exp = pl.BlockSpec((PEER_EXPERT_BLOCK, d), lambda i, j: (j, 0))
    return pl.pallas_call(
        functools.partial(_peer_kernel, tm=tm),
        grid=(n // tm, n_exp // PEER_EXPERT_BLOCK),
        in_specs=[row, exp, exp, ent, ent, ent, row],
        out_specs=row,
        out_shape=jax.ShapeDtypeStruct((n, d), F32),
        scratch_shapes=[pltpu.VMEM((tm * PEER_KEYS, PEER_KEYS), F32)],
        compiler_params=_cparams(("parallel", "arbitrary")),
    )(xn, u, v, i1, i2, gate, h)


def _prepare_weights(rel_table, norm_mix, w_in, q_norm, k_norm, w_phi, b_phi, w_glu, w_out, peer_w_query,
                     peer_u, peer_v):
    d_model = w_in.shape[0]
    off_gate = ATTN_WIDTH + 6 * KV_WIDTH
    off_ssm = off_gate + 3 * N_HEADS
    w_main = jnp.concatenate([w_in[:, :off_gate], w_in[:, off_ssm:]], axis=1).astype(MXU_DTYPE)
    w_gate = jnp.pad(w_in[:, off_gate:off_ssm], ((0, 0), (0, LANES - 3 * N_HEADS))).astype(MXU_DTYPE)
    n_main = w_main.shape[1]
    ones = jnp.ones((KV_WIDTH,), F32)
    zeros = jnp.zeros((KV_WIDTH,), F32)
    ssm_w = n_main - off_gate
    gain = jnp.concatenate([
        jnp.tile(q_norm.astype(F32), N_HEADS), ones, ones,
        jnp.tile(k_norm[1].astype(F32), N_KV_HEADS), ones,
        jnp.tile(k_norm[2].astype(F32), N_KV_HEADS), ones, jnp.ones((ssm_w,), F32)]).reshape(1, n_main)
    mode = jnp.concatenate([
        jnp.ones((ATTN_WIDTH,), F32), zeros, zeros, ones, zeros, ones, zeros,
        jnp.zeros((ssm_w,), F32)]).reshape(1, n_main)
    w_cmp = w_phi.reshape(2, CMP_R, CMP_STRIDE, HEAD_DIM, HEAD_DIM).transpose(0, 2, 3, 1, 4)
    w_cmp = w_cmp.reshape(2, CMP_STRIDE * HEAD_DIM, CMP_R * HEAD_DIM).astype(MXU_DTYPE)
    ii = jnp.arange(KEY_TILE, dtype=jnp.int32)
    dist = (jnp.arange(3, dtype=jnp.int32)[:, None, None] * KEY_TILE + ii[None, :, None] - ii[None, None, :])
    bias_tiles = rel_table.astype(F32)[_t5_bucket(dist)].transpose(0, 3, 1, 2)
    return dict(
        d_model=d_model, w_main=w_main, w_gate=w_gate, gain=gain, mode=mode, w_cmp=w_cmp,
        bias_tiles=bias_tiles, w_glu=w_glu.astype(MXU_DTYPE), w_out=w_out.astype(MXU_DTYPE),
        w_query=peer_w_query.astype(MXU_DTYPE), peer_u=peer_u.astype(MXU_DTYPE),
        peer_v=peer_v.astype(MXU_DTYPE))


def _overlap_matrix(n_chunks, n_cmp, n_slc, n_pad):
    c0 = np.arange(n_chunks)[:, None] * CMP_STRIDE
    s0 = np.arange(n_pad)[None, :] * SLC_BLOCK
    overlap = np.clip(np.minimum(c0 + CMP_BLOCK, s0 + SLC_BLOCK) - np.maximum(c0, s0), 0, None) / CMP_BLOCK
    overlap = overlap * (np.arange(n_chunks)[:, None] < n_cmp) * (np.arange(n_pad)[None, :] < n_slc)
    return jnp.asarray(overlap, dtype=MXU_DTYPE)


def _expansion_matrix(n_pad, n_keys):
    e = (np.arange(n_keys)[None, :] // SLC_BLOCK) == np.arange(n_pad)[:, None]
    return jnp.asarray(e, dtype=MXU_DTYPE)


def _pad_lanes(n):
    return -(-n // LANES) * LANES


def _layer(x3, t_real, pos0, cache_cmp3, cache_sel3, cache_win3, page_table, h0, wts, rel_table, norm_mix,
           k_norm, b_phi, ssm, b_glu, out_norm_attn, out_norm_ssm, norm_ffn, sub_keys):
    nb, t, d_model = x3.shape
    n = nb * t
    sample = page_table is not None
    x2 = x3.reshape(n, d_model)
    xn = _rmsnorm(x2, norm_mix, MXU_DTYPE)
    p = _matmul(xn, wts["w_main"], "headnorm", (wts["mode"], wts["gain"]))
    gates = _matmul(xn, wts["w_gate"], "sigmoid")
    n_main = p.shape[1]
    p3 = p.reshape(nb, t, n_main)
    col = lambda off: off // HEAD_DIM
    off_cmp, off_sel, off_win = ATTN_WIDTH, ATTN_WIDTH + 2 * KV_WIDTH, ATTN_WIDTH + 4 * KV_WIDTH
    off_u = ATTN_WIDTH + 6 * KV_WIDTH

    if sample:
        assert t_real < CMP_STRIDE and pos0 % PAGE_SIZE == 0
        n_pages = page_table.shape[1]
        pk = _compress(cache_cmp3.reshape(-1, PAGE_SIZE * 2 * N_KV_HEADS, HEAD_DIM), page_table, wts["w_cmp"])
        total = pos0 + t_real
    else:
        n_pages = t // PAGE_SIZE
        pt = jnp.arange(nb * n_pages, dtype=jnp.int32).reshape(nb, n_pages)
        kv_cmp = p[:, off_cmp:off_cmp + 2 * KV_WIDTH]
        pk = _compress(kv_cmp.reshape(nb * n_pages, PAGE_SIZE * 2 * N_KV_HEADS, HEAD_DIM), pt, wts["w_cmp"])
        total = t
    n_chunks = pk.shape[2]
    n_cmp = total // CMP_STRIDE - CMP_R + 1
    n_slc = -(-total // SLC_BLOCK)
    n_pad = _pad_lanes(n_slc)
    q_pos = pos0 + jnp.arange(t, dtype=jnp.int32)
    blk_end = jnp.arange(n_chunks, dtype=jnp.int32) * CMP_STRIDE + (CMP_BLOCK - 1)
    bias_cmp = rel_table.astype(F32)[_t5_bucket(q_pos[:, None] - blk_end[None, :])].transpose(2, 0, 1)
    m_mat = _overlap_matrix(n_chunks, n_cmp, n_slc, n_pad)
    tq_cmp = _row_tile(t, 256)
    o_cmp, sel = _cmp_attention(p3, pk, b_phi.astype(F32), k_norm[0].astype(F32).reshape(1, HEAD_DIM),
                                bias_cmp, m_mat, pos0=pos0, n_cmp=n_cmp, n_slc=n_slc, tq=tq_cmp)

    if sample:
        n_keys = (n_pages + 1) * KEY_TILE
        e_mat = _expansion_matrix(n_pad, n_keys)
        o_sel = _attn_sample(p3, cache_sel3, page_table, wts["bias_tiles"], "sel", col(off_sel),
                             col(off_sel + KV_WIDTH), sel, e_mat)
        o_win = _attn_sample(p3, cache_win3, None, wts["bias_tiles"], "win", col(off_win),
                             col(off_win + KV_WIDTH))
    else:
        e_mat = _expansion_matrix(n_pad, t)
        o_sel = _attn_prompt(p3, wts["bias_tiles"], "sel", col(off_sel), col(off_sel + KV_WIDTH), sel, e_mat)
        o_win = _attn_prompt(p3, wts["bias_tiles"], "win", col(off_win), col(off_win + KV_WIDTH))
    mixed_attn = _combine(o_cmp.reshape(n, ATTN_WIDTH), o_sel.reshape(n, ATTN_WIDTH),
                          o_win.reshape(n, ATTN_WIDTH), gates, out_norm_attn)

    ssm_w = n_main - off_u
    n_groups = ssm_w // SSM_GROUP
    u = p3[:, :, off_u:]
    if sample:
        u = jnp.pad(u[:, :t_real], ((0, 0), (0, SSM_CHUNK - t_real), (0, 0)))
        l_eff = t_real
    else:
        l_eff = SSM_CHUNK
    t_ssm = u.shape[1]
    n_chunk = t_ssm // SSM_CHUNK
    u_g = u.reshape(nb, n_chunk, SSM_CHUNK, n_groups, SSM_GROUP).transpose(3, 1, 0, 2, 4)
    u_g = u_g.reshape(n_groups, n_chunk * nb, SSM_CHUNK * SSM_GROUP)
    mats = _ssm_matrices(*ssm, l_eff)
    y_g, h_last = _ssm(u_g, *mats, h0, n_seq=nb, n_chunk=n_chunk)
    y = y_g.reshape(n_groups, n_chunk, nb, SSM_CHUNK, SSM_GROUP).transpose(2, 1, 3, 0, 4)
    y = y.reshape(nb, t_ssm, ssm_w)[:, :t] if t_ssm >= t else jnp.pad(
        y.reshape(nb, t_ssm, ssm_w), ((0, 0), (0, t - t_ssm), (0, 0)))
    mixed_ssm = _glu(y.reshape(n, ssm_w), wts["w_glu"], b_glu, out_norm_ssm)

    mixed = jnp.concatenate([mixed_attn, mixed_ssm], axis=1)
    h = _matmul(mixed, wts["w_out"], "residual", (x2,))
    hn = _rmsnorm(h, norm_ffn, MXU_DTYPE)
    pq = _matmul(hn, wts["w_query"])
    i1, i2, gate = _peer_topk(pq, sub_keys)
    to_rows = lambda a: a.transpose(2, 0, 1).reshape(n, PEER_HEADS * PEER_TOPK)
    y_out = _peer(hn, wts["peer_u"], wts["peer_v"], to_rows(i1), to_rows(i2), to_rows(gate), h)
    return y_out.reshape(nb, t, d_model), p3, h_last


def kernel(x_prompt, x_sample, cache_kv_cmp, cache_kv_sel, cache_kv_win, state_ssm_re, state_ssm_im, page_table,
           rel_table, norm_mix, w_in, q_norm, k_norm, w_phi, b_phi, ssm_lam_re, ssm_lam_im, ssm_log_dt,
           ssm_b_re, ssm_b_im, ssm_c_re, ssm_c_im, ssm_d, w_glu, b_glu, out_norm_attn, out_norm_ssm, w_out,
           norm_ffn, peer_w_query, peer_sub_keys, peer_u, peer_v):
    depth = w_in.shape[0]
    bp, t_p, d_model = x_prompt.shape
    db, t_s, _ = x_sample.shape
    past_len = page_table.shape[1] * PAGE_SIZE
    kv_row = 2 * KV_WIDTH
    n_groups = ssm_lam_re.shape[1]
    assert cache_kv_win.shape[2] == WINDOW and t_p % KEY_TILE == 0 and t_s <= SAMPLE_ROWS

    yp, ys = x_prompt, jnp.pad(x_sample, ((0, 0), (0, SAMPLE_ROWS - t_s), (0, 0)))
    new_p = [[] for _ in range(5)]
    new_s = [[] for _ in range(5)]
    for l in range(depth):
        wts = _prepare_weights(rel_table, norm_mix[l], w_in[l], q_norm[l], k_norm[l], w_phi[l], b_phi[l],
                               w_glu[l], w_out[l], peer_w_query[l], peer_u[l], peer_v[l])
        ssm = (ssm_lam_re[l], ssm_lam_im[l], ssm_log_dt[l], ssm_b_re[l], ssm_b_im[l], ssm_c_re[l],
               ssm_c_im[l], ssm_d[l])
        shared = (wts, rel_table, norm_mix[l], k_norm[l], b_phi[l], ssm, b_glu[l], out_norm_attn[l],
                  out_norm_ssm[l], norm_ffn[l], peer_sub_keys[l])
        off_cmp, off_sel, off_win = ATTN_WIDTH, ATTN_WIDTH + kv_row, ATTN_WIDTH + 2 * kv_row

        h0 = jnp.zeros((n_groups, bp, 2 * SSM_STATE), F32)
        yp, p3, h_last = _layer(yp, t_p, 0, None, None, None, None, h0, *shared)
        kv = lambda a, off, rows: a[:, rows, off:off + kv_row].reshape(
            a.shape[0], -1, 2, N_KV_HEADS, HEAD_DIM)
        keep = min(WINDOW, t_p)
        new_p[0].append(kv(p3, off_cmp, slice(0, t_p)))
        new_p[1].append(kv(p3, off_sel, slice(0, t_p)))
        new_p[2].append(kv(p3, off_win, slice(t_p - keep, t_p)))
        new_p[3].append(h_last[:, :, :SSM_STATE].transpose(1, 0, 2))
        new_p[4].append(h_last[:, :, SSM_STATE:].transpose(1, 0, 2))

        n_phys = cache_kv_cmp.shape[1]
        h0 = jnp.concatenate([state_ssm_re[l], state_ssm_im[l]], axis=-1).astype(F32).transpose(1, 0, 2)
        ys, p3, h_last = _layer(ys, t_s, past_len, cache_kv_cmp[l].reshape(n_phys, PAGE_SIZE, kv_row),
                                cache_kv_sel[l].reshape(n_phys, PAGE_SIZE, kv_row),
                                cache_kv_win[l].reshape(db, WINDOW, kv_row), page_table, h0, *shared)
        new_s[0].append(kv(p3, off_cmp, slice(0, t_s)))
        new_s[1].append(kv(p3, off_sel, slice(0, t_s)))
        win_all = jnp.concatenate([cache_kv_win[l], kv(p3, off_win, slice(0, t_s))], axis=1)
        keep = min(WINDOW, past_len + t_s)
        new_s[2].append(win_all[:, win_all.shape[1] - keep:])
        new_s[3].append(h_last[:, :, :SSM_STATE].transpose(1, 0, 2))
        new_s[4].append(h_last[:, :, SSM_STATE:].transpose(1, 0, 2))
    outs_p = [jnp.stack(v) for v in new_p]
    outs_s = [jnp.stack(v) for v in new_s]
    return (yp, ys[:, :t_s], *outs_p, *outs_s)
```

```python
import functools
import math

import numpy as np
import jax
import jax.numpy as jnp
from jax import lax
from jax.experimental import pallas as pl
from jax.experimental.pallas import tpu as pltpu

F32 = jnp.float32
MXU_DTYPE = jnp.bfloat16

HEAD_DIM = 128
N_KV_HEADS = 4
GQA_REP = 4
N_HEADS = N_KV_HEADS * GQA_REP
ATTN_WIDTH = N_HEADS * HEAD_DIM
KV_WIDTH = N_KV_HEADS * HEAD_DIM
ATTN_SCALE = HEAD_DIM ** -0.5
CMP_BLOCK = 32
CMP_STRIDE = 16
CMP_R = CMP_BLOCK // CMP_STRIDE
SLC_BLOCK = 64
N_SELECT = 16
FORCE_BONUS = 1.0e4
WINDOW = 512
NUM_BUCKETS = 32
MAX_DISTANCE = 128
SSM_GROUP = 16
SSM_STATE = 64
PEER_HEADS = 8
PEER_KEYS = 128
PEER_TOPK = 16
PEER_HALF = 128
EPS = 1e-6
NEG_INF = -1e30
PAGE_SIZE = 128

LANES = 128
SUBLANES = 8
MXU_DIM = 256
VMEM_LIMIT_BYTES = 52 * 1024 * 1024

KEY_TILE = 128
WIN_TILES = WINDOW // KEY_TILE
SSM_CHUNK = 16
SAMPLE_ROWS = 8


def _cparams(sem):
    return pltpu.CompilerParams(dimension_semantics=sem, vmem_limit_bytes=VMEM_LIMIT_BYTES)


def _row_tile(n, pref):
    t = min(n, pref)
    while n % t:
        t -= SUBLANES
    return t


def _dot(a, b):
    return jnp.dot(a, b, preferred_element_type=F32)


def _dot_nt(a, b):
    return lax.dot_general(a, b, (((1,), (1,)), ((), ())), preferred_element_type=F32)


def _gelu(x):
    c = math.sqrt(2.0 / math.pi)
    return 0.5 * x * (1.0 + jnp.tanh(c * (x + 0.044715 * (x * x * x))))


def _sigmoid(x):
    return 1.0 / (1.0 + jnp.exp(-x))


def _bucket_lower_bounds():
    n = np.arange(2 * MAX_DISTANCE)
    max_exact = NUM_BUCKETS // 2
    nf = np.maximum(n, 1).astype(np.float32)
    large = max_exact + (np.log(nf / np.float32(max_exact)) / np.float32(math.log(MAX_DISTANCE / max_exact))
                         * np.float32(NUM_BUCKETS - max_exact)).astype(np.int32)
    bucket = np.where(n < max_exact, n, np.minimum(large, NUM_BUCKETS - 1))
    assert np.all(np.diff(bucket) >= 0) and bucket[-1] == NUM_BUCKETS - 1
    return [int(np.argmax(bucket >= b)) for b in range(NUM_BUCKETS)]


BUCKET_LO = _bucket_lower_bounds()


def _bias_from_dist(dist, table_entry):
    out = table_entry(0)
    for b in range(1, NUM_BUCKETS):
        out = jnp.where(dist >= BUCKET_LO[b], table_entry(b), out)
    return out


def _rmsnorm_kernel(x_ref, g_ref, o_ref):
    x = x_ref[...]
    ms = jnp.mean(x * x, axis=-1, keepdims=True)
    o_ref[...] = (x * lax.rsqrt(ms + EPS) * g_ref[...]).astype(o_ref.dtype)


def _rmsnorm(x, g, out_dtype):
    n, d = x.shape
    tm = _row_tile(n, 256)
    return pl.pallas_call(
        _rmsnorm_kernel,
        grid=(n // tm,),
        in_specs=[pl.BlockSpec((tm, d), lambda i: (i, 0)), pl.BlockSpec((1, d), lambda i: (0, 0))],
        out_specs=pl.BlockSpec((tm, d), lambda i: (i, 0)),
        out_shape=jax.ShapeDtypeStruct((n, d), out_dtype),
        compiler_params=_cparams(("parallel",)),
    )(x, g.reshape(1, d))


def _matmul_kernel(*refs, epilogue):
    a_ref, b_ref = refs[0], refs[1]
    o_ref = refs[-1]
    acc = _dot(a_ref[...], b_ref[...])
    if epilogue == "none":
        o_ref[...] = acc
    elif epilogue == "sigmoid":
        o_ref[...] = _sigmoid(acc)
    elif epilogue == "residual":
        o_ref[...] = refs[2][...] + acc
    elif epilogue == "headnorm":
        mode_ref, gain_ref = refs[2], refs[3]
        for c in range(acc.shape[1] // HEAD_DIM):
            sl = slice(c * HEAD_DIM, (c + 1) * HEAD_DIM)
            blk = acc[:, sl]
            ms = jnp.mean(blk * blk, axis=-1, keepdims=True)
            nrm = blk * lax.rsqrt(ms + EPS) * gain_ref[:, sl]
            o_ref[:, sl] = jnp.where(mode_ref[:, sl] > 0.0, nrm, blk)
    else:
        raise ValueError(epilogue)


def _matmul(a, b, epilogue="none", extras=(), tm_pref=512, tn_pref=1024):
    m, k = a.shape
    _, n = b.shape
    tm = _row_tile(m, tm_pref)
    tn = min(n, tn_pref)
    while n % tn:
        tn -= LANES
    in_specs = [pl.BlockSpec((tm, k), lambda i, j: (i, 0)), pl.BlockSpec((k, tn), lambda i, j: (0, j))]
    for e in extras:
        if e.shape[0] == 1:
            in_specs.append(pl.BlockSpec((1, tn), lambda i, j: (0, j)))
        else:
            in_specs.append(pl.BlockSpec((tm, tn), lambda i, j: (i, j)))
    return pl.pallas_call(
        functools.partial(_matmul_kernel, epilogue=epilogue),
        grid=(m // tm, n // tn),
        in_specs=in_specs,
        out_specs=pl.BlockSpec((tm, tn), lambda i, j: (i, j)),
        out_shape=jax.ShapeDtypeStruct((m, n), F32),
        compiler_params=_cparams(("parallel", "parallel")),
    )(a, b, *extras)


CMP_PAGES_PER_STEP = 8


def _compress_kernel(pt_ref, *refs):
    del pt_ref
    pages = refs[:CMP_PAGES_PER_STEP]
    w_ref = refs[CMP_PAGES_PER_STEP]
    o_ref = refs[CMP_PAGES_PER_STEP + 1]
    x_scr = refs[CMP_PAGES_PER_STEP + 2]
    chunks_per_page = PAGE_SIZE // CMP_STRIDE
    rows_per_head = CMP_PAGES_PER_STEP * chunks_per_page
    heads_per_row = 2 * N_KV_HEADS
    for c in range(2):
        for g in range(N_KV_HEADS):
            for i in range(CMP_PAGES_PER_STEP):
                r0 = g * rows_per_head + i * chunks_per_page
                for s in range(CMP_STRIDE):
                    x_scr[r0:r0 + chunks_per_page, s * HEAD_DIM:(s + 1) * HEAD_DIM] = pages[i][
                        0, pl.ds(s * heads_per_row + c * N_KV_HEADS + g, chunks_per_page,
                                 stride=CMP_STRIDE * heads_per_row), :]
        res = _dot(x_scr[...].astype(MXU_DTYPE), w_ref[c])
        for g in range(N_KV_HEADS):
            o_ref[0, c * N_KV_HEADS + g] = res[g * rows_per_head:(g + 1) * rows_per_head]


def _compress(rows3, page_table, w_cmp):
    nb, n_pages = page_table.shape
    assert n_pages % CMP_PAGES_PER_STEP == 0
    n_steps = n_pages // CMP_PAGES_PER_STEP
    chunks_per_step = CMP_PAGES_PER_STEP * PAGE_SIZE // CMP_STRIDE
    n_chunks = n_pages * PAGE_SIZE // CMP_STRIDE

    def page_spec(i):
        return pl.BlockSpec((1, PAGE_SIZE * 2 * N_KV_HEADS, HEAD_DIM),
                            lambda b, j, pt: (pt[b, j * CMP_PAGES_PER_STEP + i], 0, 0))

    grid_spec = pltpu.PrefetchScalarGridSpec(
        num_scalar_prefetch=1,
        grid=(nb, n_steps),
        in_specs=[page_spec(i) for i in range(CMP_PAGES_PER_STEP)]
        + [pl.BlockSpec((2, CMP_STRIDE * HEAD_DIM, CMP_R * HEAD_DIM), lambda b, j, pt: (0, 0, 0))],
        out_specs=pl.BlockSpec((1, 2 * N_KV_HEADS, chunks_per_step, CMP_R * HEAD_DIM),
                               lambda b, j, pt: (b, 0, j, 0)),
        scratch_shapes=[pltpu.VMEM((N_KV_HEADS * chunks_per_step, CMP_STRIDE * HEAD_DIM), F32)],
    )
    return pl.pallas_call(
        _compress_kernel,
        grid_spec=grid_spec,
        out_shape=jax.ShapeDtypeStruct((nb, 2 * N_KV_HEADS, n_chunks, CMP_R * HEAD_DIM), F32),
        compiler_params=_cparams(("parallel", "arbitrary")),
    )(page_table, *([rows3] * CMP_PAGES_PER_STEP), w_cmp)


def _cmp_attn_kernel(tab_ref, q_ref, kp_ref, vp_ref, bphi_ref, kn_ref, m_ref, o_ref, sel_ref,
                     *, tq, pos0, n_cmp, n_slc):
    g = pl.program_id(1)
    i = pl.program_id(2)
    n_chunks = kp_ref.shape[2]
    kp = kp_ref[0, 0]
    vp = vp_ref[0, 0]
    k_c = bphi_ref[0:1, :] + kp[:, :HEAD_DIM] + pltpu.roll(kp[:, HEAD_DIM:], n_chunks - 1, 0)
    v_c = bphi_ref[1:2, :] + vp[:, :HEAD_DIM] + pltpu.roll(vp[:, HEAD_DIM:], n_chunks - 1, 0)
    ms = jnp.mean(k_c * k_c, axis=-1, keepdims=True)
    k_c = k_c * lax.rsqrt(ms + EPS) * kn_ref[...]

    q = jnp.concatenate([q_ref[0, :, r * HEAD_DIM:(r + 1) * HEAD_DIM] for r in range(GQA_REP)], axis=0)
    logits = _dot_nt(q.astype(MXU_DTYPE), k_c.astype(MXU_DTYPE)) * ATTN_SCALE
    row = lax.broadcasted_iota(jnp.int32, (tq, n_chunks), 0)
    col = lax.broadcasted_iota(jnp.int32, (tq, n_chunks), 1)
    dist = pos0 + i * tq + row - (col * CMP_STRIDE + (CMP_BLOCK - 1))
    bias = jnp.concatenate(
        [_bias_from_dist(dist, lambda b, r=r: tab_ref[g * GQA_REP + r, b]) for r in range(GQA_REP)], axis=0)
    valid = (dist >= 0) & (col < n_cmp)
    valid = jnp.concatenate([valid] * GQA_REP, axis=0)
    logits = jnp.where(valid, logits + bias, NEG_INF)
    mx = jnp.max(logits, axis=-1, keepdims=True)
    e = jnp.exp(logits - mx)
    any_valid = (jnp.max(jnp.where(valid, 1.0, 0.0), axis=-1, keepdims=True))
    p = e / jnp.sum(e, axis=-1, keepdims=True) * any_valid
    out = _dot(p.astype(MXU_DTYPE), v_c.astype(MXU_DTYPE))
    for r in range(GQA_REP):
        o_ref[0, :, r * HEAD_DIM:(r + 1) * HEAD_DIM] = out[r * tq:(r + 1) * tq]

    psum = p[0:tq]
    for r in range(1, GQA_REP):
        psum = psum + p[r * tq:(r + 1) * tq]
    hi = psum.astype(MXU_DTYPE)
    lo = (psum - hi.astype(F32)).astype(MXU_DTYPE)
    imp = _dot(hi, m_ref[...]) + _dot(lo, m_ref[...])
    n_pad = imp.shape[1]
    j = lax.broadcasted_iota(jnp.int32, (tq, n_pad), 1)
    qp = pos0 + i * tq + lax.broadcasted_iota(jnp.int32, (tq, n_pad), 0)
    cur = jnp.right_shift(qp, int(math.log2(SLC_BLOCK)))
    ok = j * SLC_BLOCK <= qp
    forced = (j == 0) | (j == cur) | (j == cur - 1)
    score = jnp.where(ok, imp + jnp.where(forced, FORCE_BONUS, 0.0), NEG_INF)
    rank = jnp.zeros((tq, n_pad), F32)
    for s in range(n_slc):
        cs = score[:, s:s + 1]
        beats = (cs > score) | ((cs == score) & (j > s))
        rank = rank + jnp.where(beats, 1.0, 0.0)
    selected = (rank < float(min(N_SELECT, n_slc))) & (score > 0.5 * NEG_INF)
    sel_ref[0, 0] = jnp.where(selected, 1.0, 0.0)


def _cmp_attention(tab, p3, pk, bphi, k_norm0, m_mat, *, pos0, n_cmp, n_slc, tq):
    nb, t, _ = p3.shape
    n_chunks = pk.shape[2]
    n_pad = m_mat.shape[1]
    nq = t // tq
    kern = functools.partial(_cmp_attn_kernel, tq=tq, pos0=pos0, n_cmp=n_cmp, n_slc=n_slc)
    return pl.pallas_call(
        kern,
        grid=(nb, N_KV_HEADS, nq),
        in_specs=[
            pl.BlockSpec(memory_space=pltpu.SMEM),
            pl.BlockSpec((1, tq, GQA_REP * HEAD_DIM), lambda b, g, i: (b, i, g)),
            pl.BlockSpec((1, 1, n_chunks, CMP_R * HEAD_DIM), lambda b, g, i: (b, g, 0, 0)),
            pl.BlockSpec((1, 1, n_chunks, CMP_R * HEAD_DIM), lambda b, g, i: (b, N_KV_HEADS + g, 0, 0)),
            pl.BlockSpec((2, HEAD_DIM), lambda b, g, i: (0, 0)),
            pl.BlockSpec((1, HEAD_DIM), lambda b, g, i: (0, 0)),
            pl.BlockSpec((n_chunks, n_pad), lambda b, g, i: (0, 0)),
        ],
        out_specs=[
            pl.BlockSpec((1, tq, GQA_REP * HEAD_DIM), lambda b, g, i: (b, i, g)),
            pl.BlockSpec((1, 1, tq, n_pad), lambda b, g, i: (b, g, i, 0)),
        ],
        out_shape=[
            jax.ShapeDtypeStruct((nb, t, ATTN_WIDTH), F32),
            jax.ShapeDtypeStruct((nb, N_KV_HEADS, t, n_pad), F32),
        ],
        compiler_params=_cparams(("parallel", "parallel", "parallel")),
    )(tab, p3, pk, pk, bphi, k_norm0, m_mat)


def _online_softmax_step(s, v, m_scr, l_scr, acc_scr):
    m_old = m_scr[...]
    m_new = jnp.maximum(m_old, jnp.max(s, axis=-1, keepdims=True))
    alpha = jnp.exp(m_old - m_new)
    p = jnp.exp(s - m_new)
    l_scr[...] = alpha * l_scr[...] + jnp.sum(p, axis=-1, keepdims=True)
    acc_scr[...] = alpha * acc_scr[...] + _dot(p.astype(MXU_DTYPE), v)
    m_scr[...] = m_new


def _attn_prompt_kernel(*refs, mode):
    if mode == "sel":
        q_ref, k_ref, v_ref, mb_ref, sel_ref, e_ref, o_ref, m_scr, l_scr, acc_scr = refs
    else:
        q_ref, k_ref, v_ref, mb_ref, o_ref, m_scr, l_scr, acc_scr = refs
    tq = KEY_TILE
    qt = pl.program_id(2)
    q = jnp.concatenate([q_ref[0, :, r * HEAD_DIM:(r + 1) * HEAD_DIM] for r in range(GQA_REP)], axis=0)
    q = (q * ATTN_SCALE).astype(MXU_DTYPE)
    m_scr[...] = jnp.full(m_scr.shape, NEG_INF, F32)
    l_scr[...] = jnp.zeros(l_scr.shape, F32)
    acc_scr[...] = jnp.zeros(acc_scr.shape, F32)
    if mode == "sel":
        sel = sel_ref[0, 0].astype(MXU_DTYPE)
        first = 0
    else:
        first = jnp.maximum(qt - WIN_TILES, 0)

    def body(kt, carry):
        r0 = pl.multiple_of(kt * KEY_TILE, KEY_TILE)
        k = k_ref[0, pl.ds(r0, KEY_TILE), :].astype(MXU_DTYPE)
        v = v_ref[0, pl.ds(r0, KEY_TILE), :].astype(MXU_DTYPE)
        d = qt - kt
        if mode == "sel":
            kind = jnp.minimum(d, 2)
        else:
            kind = jnp.where(d == WIN_TILES, 3, jnp.minimum(d, 2))
        s = _dot_nt(q, k) + mb_ref[kind].reshape(GQA_REP * tq, KEY_TILE)
        if mode == "sel":
            chosen = _dot(sel, e_ref[kt])
            s = s + jnp.concatenate([(1.0 - chosen) * NEG_INF] * GQA_REP, axis=0)
        _online_softmax_step(s, v, m_scr, l_scr, acc_scr)
        return carry

    lax.fori_loop(first, qt + 1, body, 0)
    out = acc_scr[...] / l_scr[...]
    for r in range(GQA_REP):
        o_ref[0, :, r * HEAD_DIM:(r + 1) * HEAD_DIM] = out[r * tq:(r + 1) * tq]


def _prompt_bias_tiles(tab, mode):
    ii = jnp.arange(KEY_TILE, dtype=jnp.int32)
    entry = lambda b: tab[:, b][:, None, None]

    def tile(offset):
        dist = (offset * KEY_TILE + ii[:, None] - ii[None, :])[None]
        return dist, jnp.broadcast_to(_bias_from_dist(dist, entry), (N_HEADS, KEY_TILE, KEY_TILE))

    d0, b0 = tile(0)
    tiles = [jnp.where(d0 >= 0, b0, NEG_INF), tile(1)[1], tile(2)[1]]
    if mode == "win":
        d4, b4 = tile(WIN_TILES)
        tiles.append(jnp.where(d4 <= WINDOW, b4, NEG_INF))
    return jnp.stack(tiles)


def _attn_prompt(p3, tab, mode, k_col, v_col, sel=None, e3=None):
    nb, t, _ = p3.shape
    tq = KEY_TILE
    nq = t // tq
    mb = _prompt_bias_tiles(tab, mode)
    in_specs = [
        pl.BlockSpec((1, tq, GQA_REP * HEAD_DIM), lambda b, g, i: (b, i, g)),
        pl.BlockSpec((1, t, HEAD_DIM), lambda b, g, i: (b, 0, k_col + g)),
        pl.BlockSpec((1, t, HEAD_DIM), lambda b, g, i: (b, 0, v_col + g)),
        pl.BlockSpec((mb.shape[0], GQA_REP, KEY_TILE, KEY_TILE), lambda b, g, i: (0, g, 0, 0)),
    ]
    args = [p3, p3, p3, mb]
    if mode == "sel":
        n_pad = sel.shape[-1]
        in_specs += [
            pl.BlockSpec((1, 1, tq, n_pad), lambda b, g, i: (b, g, i, 0)),
            pl.BlockSpec((nq, n_pad, KEY_TILE), lambda b, g, i: (0, 0, 0)),
        ]
        args += [sel, e3]
    rows = GQA_REP * tq
    return pl.pallas_call(
        functools.partial(_attn_prompt_kernel, mode=mode),
        grid=(nb, N_KV_HEADS, nq),
        in_specs=in_specs,
        out_specs=pl.BlockSpec((1, tq, GQA_REP * HEAD_DIM), lambda b, g, i: (b, i, g)),
        out_shape=jax.ShapeDtypeStruct((nb, t, ATTN_WIDTH), F32),
        scratch_shapes=[pltpu.VMEM((rows, 1), F32), pltpu.VMEM((rows, 1), F32),
                        pltpu.VMEM((rows, HEAD_DIM), F32)],
        compiler_params=_cparams(("parallel", "parallel", "parallel")),
    )(*args)


def _attn_sample_kernel(*refs, mode, paged, pages, n_steps, tq, pos0, base_pos):
    it = iter(refs)
    if paged:
        next(it)
    q_ref = next(it)
    page_refs = [next(it) for _ in range(pages)]
    new_ref, rowtab_ref = next(it), next(it)
    if mode == "sel":
        sel_ref, e_ref = next(it), next(it)
    o_ref, m_scr, l_scr, acc_scr = next(it), next(it), next(it), next(it)
    kk = pl.program_id(1)
    rows = N_HEADS * tq
    heads_per_row = 2 * N_KV_HEADS

    @pl.when(kk == 0)
    def _():
        m_scr[...] = jnp.full(m_scr.shape, NEG_INF, F32)
        l_scr[...] = jnp.zeros(l_scr.shape, F32)
        acc_scr[...] = jnp.zeros(acc_scr.shape, F32)

    def queries():
        zero = jnp.zeros((tq, HEAD_DIM), F32)
        blocks = []
        for h in range(N_HEADS):
            qh = q_ref[0, :, h * HEAD_DIM:(h + 1) * HEAD_DIM] * ATTN_SCALE
            blocks.append(jnp.concatenate([qh if g == h // GQA_REP else zero for g in range(N_KV_HEADS)], axis=1))
        return jnp.concatenate(blocks, axis=0).astype(MXU_DTYPE)

    def step(k_all, v_all, key_pos0, chosen_cols, exact_bias):
        n_keys = k_all.shape[0]
        s = _dot_nt(queries(), k_all)
        row = lax.broadcasted_iota(jnp.int32, (rows, n_keys), 0) & (tq - 1)
        col = lax.broadcasted_iota(jnp.int32, (rows, n_keys), 1)
        dist = pos0 + row - (key_pos0 + col)
        if exact_bias:
            bias = _bias_from_dist(dist, lambda b: rowtab_ref[:, b:b + 1])
        else:
            bias = rowtab_ref[:, NUM_BUCKETS - 1:NUM_BUCKETS]
        if mode == "sel":
            sel = sel_ref[0].reshape(N_KV_HEADS * tq, sel_ref.shape[-1]).astype(MXU_DTYPE)
            chosen = _dot(sel, chosen_cols)
            chosen = jnp.concatenate(
                [chosen[g * tq:(g + 1) * tq] for g in range(N_KV_HEADS) for _ in range(GQA_REP)], axis=0)
            mask = (dist >= 0) & (chosen > 0.5)
        else:
            mask = (dist >= 0) & (dist <= WINDOW)
        s = jnp.where(mask, s + bias, NEG_INF)
        _online_softmax_step(s, v_all, m_scr, l_scr, acc_scr)

    def cached(kv):
        return jnp.concatenate([
            jnp.concatenate([pg[0, pl.ds(kv * N_KV_HEADS + g, PAGE_SIZE, stride=heads_per_row), :]
                             for g in range(N_KV_HEADS)], axis=1)
            for pg in page_refs], axis=0).astype(MXU_DTYPE)

    keys_per_step = pages * PAGE_SIZE
    key_pos0 = base_pos + kk * keys_per_step
    far = pos0 - (key_pos0 + keys_per_step - 1) >= MAX_DISTANCE
    is_cache = kk < n_steps
    chosen_cols = e_ref[...] if mode == "sel" else None

    @pl.when(is_cache & far)
    def _():
        step(cached(0), cached(1), key_pos0, chosen_cols, False)

    @pl.when(is_cache & jnp.logical_not(far))
    def _():
        step(cached(0), cached(1), key_pos0, chosen_cols, True)

    @pl.when(kk == n_steps)
    def _():
        pad = jnp.zeros((KEY_TILE - tq, N_KV_HEADS * HEAD_DIM), F32)
        k_new = jnp.concatenate([new_ref[0, :, :KV_WIDTH], pad], axis=0).astype(MXU_DTYPE)
        v_new = jnp.concatenate([new_ref[0, :, KV_WIDTH:], pad], axis=0).astype(MXU_DTYPE)
        cols = e_ref[:, :KEY_TILE] if mode == "sel" else None
        step(k_new, v_new, pos0, cols, True)
        out = acc_scr[...] / l_scr[...]
        for h in range(N_HEADS):
            g = h // GQA_REP
            o_ref[0, :, h * HEAD_DIM:(h + 1) * HEAD_DIM] = out[h * tq:(h + 1) * tq, g * HEAD_DIM:(g + 1) * HEAD_DIM]


def _attn_sample(p3, cache_rows, page_table, tab, mode, new_col, pos0, base_pos, sel=None, e_mat=None):
    nb, tq, _ = p3.shape
    paged = page_table is not None
    pages = 8 if paged else WIN_TILES
    n_tiles = page_table.shape[1] if paged else WIN_TILES
    assert n_tiles % pages == 0
    n_steps = n_tiles // pages
    rows = N_HEADS * tq
    rowtab = jnp.repeat(tab, tq, axis=0)

    def page_spec(i):
        def index(b, kk, *pt):
            page = jnp.minimum(kk, n_steps - 1) * pages + i
            return (pt[0][b, page] if paged else b * n_tiles + page, 0, 0)
        return pl.BlockSpec((1, PAGE_SIZE * 2 * N_KV_HEADS, HEAD_DIM), index)

    def fixed(shape, fn):
        return pl.BlockSpec(shape, lambda b, kk, *_: fn(b, kk))

    in_specs = [fixed((1, tq, ATTN_WIDTH), lambda b, kk: (b, 0, 0))]
    in_specs += [page_spec(i) for i in range(pages)]
    in_specs += [fixed((1, tq, 2 * KV_WIDTH), lambda b, kk: (b, 0, new_col)),
                 fixed((rows, NUM_BUCKETS), lambda b, kk: (0, 0))]
    args = [p3] + [cache_rows] * pages + [p3, rowtab]
    if mode == "sel":
        n_pad = sel.shape[-1]
        in_specs += [fixed((1, N_KV_HEADS, tq, n_pad), lambda b, kk: (b, 0, 0, 0)),
                     fixed((n_pad, pages * PAGE_SIZE), lambda b, kk: (0, kk))]
        args += [sel, e_mat]
    kern = functools.partial(_attn_sample_kernel, mode=mode, paged=paged, pages=pages, n_steps=n_steps, tq=tq,
                             pos0=pos0, base_pos=base_pos)
    out_spec = fixed((1, tq, ATTN_WIDTH), lambda b, kk: (b, 0, 0))
    out_shape = jax.ShapeDtypeStruct((nb, tq, ATTN_WIDTH), F32)
    scratch = [pltpu.VMEM((rows, 1), F32), pltpu.VMEM((rows, 1), F32),
               pltpu.VMEM((rows, N_KV_HEADS * HEAD_DIM), F32)]
    cp = _cparams(("parallel", "arbitrary"))
    grid = (nb, n_steps + 1)
    if paged:
        grid_spec = pltpu.PrefetchScalarGridSpec(num_scalar_prefetch=1, grid=grid, in_specs=in_specs,
                                                 out_specs=out_spec, scratch_shapes=scratch)
        return pl.pallas_call(kern, grid_spec=grid_spec, out_shape=out_shape, compiler_params=cp)(
            page_table, *args)
    return pl.pallas_call(kern, grid=grid, in_specs=in_specs, out_specs=out_spec, out_shape=out_shape,
                          scratch_shapes=scratch, compiler_params=cp)(*args)


def _combine_kernel(oc_ref, os_ref, ow_ref, gate_ref, g_ref, o_ref):
    gates = gate_ref[...]
    parts = []
    for h in range(N_HEADS):
        sl = slice(h * HEAD_DIM, (h + 1) * HEAD_DIM)
        parts.append(gates[:, h:h + 1] * oc_ref[:, sl]
                     + gates[:, N_HEADS + h:N_HEADS + h + 1] * os_ref[:, sl]
                     + gates[:, 2 * N_HEADS + h:2 * N_HEADS + h + 1] * ow_ref[:, sl])
    o = jnp.concatenate(parts, axis=1)
    ms = jnp.mean(o * o, axis=-1, keepdims=True)
    o_ref[...] = (o * lax.rsqrt(ms + EPS) * g_ref[...]).astype(o_ref.dtype)


def _combine(o_cmp, o_sel, o_win, gates, gain):
    n, w = o_cmp.shape
    tm = _row_tile(n, 256)
    row = pl.BlockSpec((tm, w), lambda i: (i, 0))
    return pl.pallas_call(
        _combine_kernel,
        grid=(n // tm,),
        in_specs=[row, row, row, pl.BlockSpec((tm, LANES), lambda i: (i, 0)),
                  pl.BlockSpec((1, w), lambda i: (0, 0))],
        out_specs=row,
        out_shape=jax.ShapeDtypeStruct((n, w), MXU_DTYPE),
        compiler_params=_cparams(("parallel",)),
    )(o_cmp, o_sel, o_win, gates, gain.reshape(1, w))


SSM_GROUPS_PER_STEP = 8


def _ssm_kernel(u_ref, tz_ref, bs_ref, cs_ref, d_ref, la_ref, lb_ref, h0_ref, y_ref, hout_ref,
                s_scr, hs_scr, *, n_seq, n_chunk):
    gb = SSM_GROUPS_PER_STEP
    for gi in range(gb):
        ub = u_ref[gi].astype(MXU_DTYPE)
        y_ref[gi] = _dot(ub, tz_ref[gi])
        s_scr[gi] = _dot(ub, bs_ref[gi])
    la = la_ref[...]
    lb = lb_ref[...]

    def body(c, h):
        r0 = c * n_seq
        hs_scr[:, pl.ds(r0, n_seq), :] = h
        swapped = jnp.concatenate([h[..., SSM_STATE:], h[..., :SSM_STATE]], axis=-1)
        return la * h + lb * swapped + s_scr[:, pl.ds(r0, n_seq), :]

    h = lax.fori_loop(0, n_chunk, body, h0_ref[...])
    hout_ref[...] = h
    for gi in range(gb):
        y_ref[gi] = y_ref[gi] + _dot(hs_scr[gi].astype(MXU_DTYPE), cs_ref[gi]) + d_ref[gi] * u_ref[gi]


def _ssm(u_g, tz, bs, cs, dvec, la, lb, h0, *, n_seq, n_chunk):
    g, r, w = u_g.shape
    gb = SSM_GROUPS_PER_STEP
    st = 2 * SSM_STATE

    def blk(shape):
        return pl.BlockSpec((gb,) + shape, lambda i: (i,) + (0,) * len(shape))

    return pl.pallas_call(
        functools.partial(_ssm_kernel, n_seq=n_seq, n_chunk=n_chunk),
        grid=(g // gb,),
        in_specs=[blk((r, w)), blk((w, w)), blk((w, st)), blk((st, w)), blk((1, w)), blk((1, st)),
                  blk((1, st)), blk((n_seq, st))],
        out_specs=[blk((r, w)), blk((n_seq, st))],
        out_shape=[jax.ShapeDtypeStruct((g, r, w), F32), jax.ShapeDtypeStruct((g, n_seq, st), F32)],
        scratch_shapes=[pltpu.VMEM((gb, r, st), F32), pltpu.VMEM((gb, r, st), F32)],
        compiler_params=_cparams(("parallel",)),
    )(u_g, tz, bs, cs, dvec, la, lb, h0)


def _ssm_matrices(lam_re, lam_im, log_dt, b_re, b_im, c_re, c_im, d_skip, l_eff):
    g = lam_re.shape[0]
    L = SSM_CHUNK
    lam = lax.complex(lam_re.astype(F32), lam_im.astype(F32))
    dt = jnp.exp(log_dt.astype(F32))[:, None]
    lam_bar = jnp.exp(lam * dt)
    b_bar = ((lam_bar - 1.0) / lam)[..., None] * lax.complex(b_re.astype(F32), b_im.astype(F32))
    c_c = lax.complex(c_re.astype(F32), c_im.astype(F32))
    pw = [jnp.ones_like(lam_bar)]
    for _ in range(L):
        pw.append(pw[-1] * lam_bar)
    pw = jnp.stack(pw)
    kern = jnp.einsum("gop,kgp,gpi->gkoi", c_c, pw[:L], b_bar).real
    s_idx = np.arange(L)[:, None]
    t_idx = np.arange(L)[None, :]
    tau = t_idx - s_idx
    tz = kern[:, np.clip(tau, 0, L - 1)]
    tz = jnp.where(jnp.asarray(tau >= 0)[None, :, :, None, None], tz, 0.0)
    tz = tz.transpose(0, 1, 4, 2, 3).reshape(g, L * SSM_GROUP, L * SSM_GROUP)
    exps = np.clip(l_eff - 1 - np.arange(L), 0, L)
    bx = pw[exps][..., None] * b_bar[None]
    bx = jnp.where(jnp.asarray(np.arange(L) < l_eff)[:, None, None, None], bx, 0.0)
    bs = jnp.concatenate([bx.real, bx.imag], axis=2)
    bs = bs.transpose(1, 0, 3, 2).reshape(g, L * SSM_GROUP, 2 * SSM_STATE)
    mt = c_c[None] * pw[1:L + 1][:, :, None, :]
    cs = jnp.concatenate([mt.real, -mt.imag], axis=3)
    cs = cs.transpose(1, 3, 0, 2).reshape(g, 2 * SSM_STATE, L * SSM_GROUP)
    lam_l = pw[l_eff]
    la = jnp.concatenate([lam_l.real, lam_l.real], axis=1)[:, None, :]
    lb = jnp.concatenate([-lam_l.imag, lam_l.imag], axis=1)[:, None, :]
    dvec = jnp.tile(d_skip.astype(F32).reshape(g, 1, SSM_GROUP), (1, 1, L))
    return tz.astype(MXU_DTYPE), bs.astype(MXU_DTYPE), cs.astype(MXU_DTYPE), dvec, la, lb


def _glu_kernel(y_ref, w_ref, b_ref, g_ref, o_ref):
    gl = _gelu(y_ref[...])
    z = _dot(gl.astype(MXU_DTYPE), w_ref[...]) + b_ref[...]
    o = gl * _sigmoid(z)
    ms = jnp.mean(o * o, axis=-1, keepdims=True)
    o_ref[...] = (o * lax.rsqrt(ms + EPS) * g_ref[...]).astype(o_ref.dtype)


def _glu(y, w, b, gain):
    n, d = y.shape
    tm = _row_tile(n, 256)
    vec = pl.BlockSpec((1, d), lambda i: (0, 0))
    return pl.pallas_call(
        _glu_kernel,
        grid=(n // tm,),
        in_specs=[pl.BlockSpec((tm, d), lambda i: (i, 0)), pl.BlockSpec((d, d), lambda i: (0, 0)), vec, vec],
        out_specs=pl.BlockSpec((tm, d), lambda i: (i, 0)),
        out_shape=jax.ShapeDtypeStruct((n, d), MXU_DTYPE),
        compiler_params=_cparams(("parallel",)),
    )(y, w, b.reshape(1, d), gain.reshape(1, d))


def _top_rows(x, k):
    n = x.shape[0]
    idx_iota = lax.broadcasted_iota(jnp.int32, x.shape, 0)
    vals, idxs = [], []
    cur = x
    for _ in range(k):
        m = jnp.max(cur, axis=0, keepdims=True)
        ix = jnp.min(jnp.where(cur == m, idx_iota, n), axis=0, keepdims=True)
        vals.append(m)
        idxs.append(ix)
        cur = jnp.where(idx_iota == ix, -jnp.inf, cur)
    return jnp.concatenate(vals, axis=0), jnp.concatenate(idxs, axis=0)


def _pick_rows(table, sel):
    out = jnp.zeros(sel.shape, table.dtype)
    for a in range(table.shape[0]):
        out = out + jnp.where(sel == a, table[a:a + 1, :], 0)
    return out


def _peer_topk_kernel(q_ref, keys_ref, i1_ref, i2_ref, gate_ref):
    q = q_ref[...].astype(MXU_DTYPE)
    s1 = _dot_nt(keys_ref[0, 0].astype(MXU_DTYPE), q[:, :PEER_HALF])
    s2 = _dot_nt(keys_ref[0, 1].astype(MXU_DTYPE), q[:, PEER_HALF:])
    v1, x1 = _top_rows(s1, PEER_TOPK)
    v2, x2 = _top_rows(s2, PEER_TOPK)
    t = v1.shape[1]
    cand = (v1[:, None, :] + v2[None, :, :]).reshape(PEER_TOPK * PEER_TOPK, t)
    top, pos = _top_rows(cand, PEER_TOPK)
    i1_ref[0] = _pick_rows(x1, jnp.right_shift(pos, int(math.log2(PEER_TOPK))))
    i2_ref[0] = _pick_rows(x2, pos & (PEER_TOPK - 1))
    e = jnp.exp(top - jnp.max(top, axis=0, keepdims=True))
    gate_ref[0] = e / jnp.sum(e, axis=0, keepdims=True)


def _peer_topk(q, sub_keys):
    n = q.shape[0]
    tt = _row_tile(n, 512)
    out = pl.BlockSpec((1, PEER_TOPK, tt), lambda i, h: (h, 0, i))
    shp = (PEER_HEADS, PEER_TOPK, n)
    return pl.pallas_call(
        _peer_topk_kernel,
        grid=(n // tt, PEER_HEADS),
        in_specs=[pl.BlockSpec((tt, 2 * PEER_HALF), lambda i, h: (i, h)),
                  pl.BlockSpec((1, 2, PEER_KEYS, PEER_HALF), lambda i, h: (h, 0, 0, 0))],
        out_specs=[out, out, out],
        out_shape=[jax.ShapeDtypeStruct(shp, jnp.int32), jax.ShapeDtypeStruct(shp, jnp.int32),
                   jax.ShapeDtypeStruct(shp, F32)],
        compiler_params=_cparams(("parallel", "parallel")),
    )(q, sub_keys)


PEER_EXPERT_BLOCK = 2 * PEER_KEYS


PEER_GRID_HALF = PEER_KEYS // 2
HIGH_HALF = 0xFFFF0000


def _peer_kernel(x_ref, u_ref, v_ref, i1_ref, i2_ref, gate_ref, h_ref, o_ref, g_scr, w_scr, *, tm, n_blocks):
    j = pl.program_id(1)
    n_entries = PEER_HEADS * PEER_TOPK
    words = PEER_GRID_HALF // 2
    blocks_per_half = n_blocks // 2

    def build(base):
        r = lax.broadcasted_iota(jnp.int32, (PEER_GRID_HALF, n_entries), 0)
        i1_of_row = base + jnp.where(r < words, 2 * r, 2 * (r - words) + 1)
        i2_of_row = lax.broadcasted_iota(jnp.int32, (PEER_KEYS, n_entries), 0)

        def body(n8, carry):
            for k in range(SUBLANES):
                n = n8 * SUBLANES + k
                a = jnp.where(i1_of_row == i1_ref[pl.ds(n, 1), :], gate_ref[pl.ds(n, 1), :], 0.0)
                b = jnp.where(i2_of_row == i2_ref[pl.ds(n, 1), :], 1.0, 0.0)
                grid = _dot_nt(a.astype(MXU_DTYPE), b.astype(MXU_DTYPE))
                bits = lax.bitcast_convert_type(grid.astype(jnp.bfloat16).astype(F32), jnp.uint32)
                g_scr[pl.ds(pl.multiple_of(n * words, words), words), :] = (
                    jnp.right_shift(bits[:words], jnp.uint32(16)) | (bits[words:] & jnp.uint32(HIGH_HALF)))
            return carry

        lax.fori_loop(0, tm // SUBLANES, body, 0)

    def weights():
        act = _dot_nt(x_ref[...], u_ref[...])
        m = j - jnp.where(j >= blocks_per_half, blocks_per_half, 0)
        word = g_scr[pl.ds(m, tm, stride=words), :]
        even = lax.bitcast_convert_type(jnp.left_shift(word, jnp.uint32(16)), F32)
        odd = lax.bitcast_convert_type(word & jnp.uint32(HIGH_HALF), F32)
        g = jnp.concatenate([even, odd], axis=1)
        w_scr[j % 2] = (g * _gelu(act)).astype(MXU_DTYPE)

    def values():
        o_ref[...] += _dot(w_scr[(j + 1) % 2], v_ref[...])

    @pl.when(j == 0)
    def _():
        o_ref[...] = h_ref[...]
        build(0)
        weights()

    @pl.when(j == blocks_per_half)
    def _():
        build(PEER_GRID_HALF)

    @pl.when((j > 0) & (j < n_blocks))
    def _():
        values()
        weights()

    @pl.when(j == n_blocks)
    def _():
        values()


def _peer(xn, u, v, i1, i2, gate, h):
    n, d = xn.shape
    n_blocks = u.shape[0] // PEER_EXPERT_BLOCK
    assert PEER_EXPERT_BLOCK == 2 * PEER_KEYS and n_blocks == PEER_KEYS // 2
    tm = _row_tile(n, 512)
    n_entries = PEER_HEADS * PEER_TOPK
    once = pl.Buffered(1)
    row_in = pl.BlockSpec((tm, d), lambda i, j: (i, 0), pipeline_mode=once)
    ent = pl.BlockSpec((tm, n_entries), lambda i, j: (i, 0))
    return pl.pallas_call(
        functools.partial(_peer_kernel, tm=tm, n_blocks=n_blocks),
        grid=(n // tm, n_blocks + 1),
        in_specs=[row_in,
                  pl.BlockSpec((PEER_EXPERT_BLOCK, d), lambda i, j: (jnp.minimum(j, n_blocks - 1), 0)),
                  pl.BlockSpec((PEER_EXPERT_BLOCK, d), lambda i, j: (jnp.maximum(j - 1, 0), 0)),
                  ent, ent, ent, row_in],
        out_specs=pl.BlockSpec((tm, d), lambda i, j: (i, 0)),
        out_shape=jax.ShapeDtypeStruct((n, d), F32),
        scratch_shapes=[pltpu.VMEM((tm * PEER_GRID_HALF // 2, PEER_KEYS), jnp.uint32),
                        pltpu.VMEM((2, tm, PEER_EXPERT_BLOCK), MXU_DTYPE)],
        compiler_params=_cparams(("parallel", "arbitrary")),
    )(xn, u, v, i1, i2, gate, h)


def _prepare_weights(rel_table, norm_mix, w_in, q_norm, k_norm, w_phi, b_phi, w_glu, w_out, peer_w_query,
                     peer_u, peer_v):
    d_model = w_in.shape[0]
    off_gate = ATTN_WIDTH + 6 * KV_WIDTH
    off_ssm = off_gate + 3 * N_HEADS
    w_main = jnp.concatenate([w_in[:, :off_gate], w_in[:, off_ssm:]], axis=1).astype(MXU_DTYPE)
    w_gate = jnp.pad(w_in[:, off_gate:off_ssm], ((0, 0), (0, LANES - 3 * N_HEADS))).astype(MXU_DTYPE)
    n_main = w_main.shape[1]
    ones = jnp.ones((KV_WIDTH,), F32)
    zeros = jnp.zeros((KV_WIDTH,), F32)
    ssm_w = n_main - off_gate
    gain = jnp.concatenate([
        jnp.tile(q_norm.astype(F32), N_HEADS), ones, ones,
        jnp.tile(k_norm[1].astype(F32), N_KV_HEADS), ones,
        jnp.tile(k_norm[2].astype(F32), N_KV_HEADS), ones, jnp.ones((ssm_w,), F32)]).reshape(1, n_main)
    mode = jnp.concatenate([
        jnp.ones((ATTN_WIDTH,), F32), zeros, zeros, ones, zeros, ones, zeros,
        jnp.zeros((ssm_w,), F32)]).reshape(1, n_main)
    w_cmp = w_phi.reshape(2, CMP_R, CMP_STRIDE, HEAD_DIM, HEAD_DIM).transpose(0, 2, 3, 1, 4)
    w_cmp = w_cmp.reshape(2, CMP_STRIDE * HEAD_DIM, CMP_R * HEAD_DIM).astype(MXU_DTYPE)
    return dict(
        d_model=d_model, w_main=w_main, w_gate=w_gate, gain=gain, mode=mode, w_cmp=w_cmp,
        tab=rel_table.astype(F32).T, w_glu=w_glu.astype(MXU_DTYPE), w_out=w_out.astype(MXU_DTYPE),
        w_query=peer_w_query.astype(MXU_DTYPE), peer_u=peer_u.astype(MXU_DTYPE),
        peer_v=peer_v.astype(MXU_DTYPE))


def _overlap_matrix(n_chunks, n_cmp, n_slc, n_pad):
    c0 = np.arange(n_chunks)[:, None] * CMP_STRIDE
    s0 = np.arange(n_pad)[None, :] * SLC_BLOCK
    overlap = np.clip(np.minimum(c0 + CMP_BLOCK, s0 + SLC_BLOCK) - np.maximum(c0, s0), 0, None) / CMP_BLOCK
    overlap = overlap * (np.arange(n_chunks)[:, None] < n_cmp) * (np.arange(n_pad)[None, :] < n_slc)
    return jnp.asarray(overlap, dtype=MXU_DTYPE)


def _expansion_matrix(n_pad, n_keys, n_cols):
    key = np.arange(n_cols)[None, :]
    e = ((key // SLC_BLOCK) == np.arange(n_pad)[:, None]) & (key < n_keys)
    return jnp.asarray(e, dtype=MXU_DTYPE)


def _pad_lanes(n):
    return -(-n // LANES) * LANES


def _layer(x3, t_real, pos0, cache_cmp3, cache_sel3, cache_win3, page_table, h0, wts, rel_table, norm_mix,
           k_norm, b_phi, ssm, b_glu, out_norm_attn, out_norm_ssm, norm_ffn, sub_keys):
    nb, t, d_model = x3.shape
    n = nb * t
    sample = page_table is not None
    x2 = x3.reshape(n, d_model)
    xn = _rmsnorm(x2, norm_mix, MXU_DTYPE)
    p = _matmul(xn, wts["w_main"], "headnorm", (wts["mode"], wts["gain"]))
    gates = _matmul(xn, wts["w_gate"], "sigmoid")
    n_main = p.shape[1]
    p3 = p.reshape(nb, t, n_main)
    col = lambda off: off // HEAD_DIM
    off_cmp, off_sel, off_win = ATTN_WIDTH, ATTN_WIDTH + 2 * KV_WIDTH, ATTN_WIDTH + 4 * KV_WIDTH
    off_u = ATTN_WIDTH + 6 * KV_WIDTH

    if sample:
        assert t_real < CMP_STRIDE and pos0 % PAGE_SIZE == 0
        n_pages = page_table.shape[1]
        pk = _compress(cache_cmp3, page_table, wts["w_cmp"])
        total = pos0 + t_real
    else:
        n_pages = t // PAGE_SIZE
        pt = jnp.arange(nb * n_pages, dtype=jnp.int32).reshape(nb, n_pages)
        kv_cmp = p[:, off_cmp:off_cmp + 2 * KV_WIDTH]
        pk = _compress(kv_cmp.reshape(nb * n_pages, PAGE_SIZE * 2 * N_KV_HEADS, HEAD_DIM), pt, wts["w_cmp"])
        total = t
    n_chunks = pk.shape[2]
    n_cmp = total // CMP_STRIDE - CMP_R + 1
    n_slc = -(-total // SLC_BLOCK)
    n_pad = _pad_lanes(n_slc)
    m_mat = _overlap_matrix(n_chunks, n_cmp, n_slc, n_pad)
    tq_cmp = _row_tile(t, 256)
    tab = wts["tab"]
    o_cmp, sel = _cmp_attention(tab, p3, pk, b_phi.astype(F32), k_norm[0].astype(F32).reshape(1, HEAD_DIM),
                                m_mat, pos0=pos0, n_cmp=n_cmp, n_slc=n_slc, tq=tq_cmp)

    if sample:
        kv_block = 2 * KV_WIDTH
        n_cols = (n_pages + 8) * KEY_TILE
        e_mat = _expansion_matrix(n_pad, (n_pages + 1) * KEY_TILE, n_cols)
        o_sel = _attn_sample(p3, cache_sel3, page_table, tab, "sel", off_sel // kv_block, pos0, 0, sel, e_mat)
        o_win = _attn_sample(p3, cache_win3, None, tab, "win", off_win // kv_block, pos0, pos0 - WINDOW)
    else:
        e3 = _expansion_matrix(n_pad, t, t).reshape(n_pad, t // KEY_TILE, KEY_TILE).transpose(1, 0, 2)
        o_sel = _attn_prompt(p3, tab, "sel", col(off_sel), col(off_sel + KV_WIDTH), sel, e3)
        o_win = _attn_prompt(p3, tab, "win", col(off_win), col(off_win + KV_WIDTH))
    mixed_attn = _combine(o_cmp.reshape(n, ATTN_WIDTH), o_sel.reshape(n, ATTN_WIDTH),
                          o_win.reshape(n, ATTN_WIDTH), gates, out_norm_attn)

    ssm_w = n_main - off_u
    n_groups = ssm_w // SSM_GROUP
    u = p3[:, :, off_u:]
    if sample:
        u = jnp.pad(u[:, :t_real], ((0, 0), (0, SSM_CHUNK - t_real), (0, 0)))
        l_eff = t_real
    else:
        l_eff = SSM_CHUNK
    t_ssm = u.shape[1]
    n_chunk = t_ssm // SSM_CHUNK
    u_g = u.reshape(nb, n_chunk, SSM_CHUNK, n_groups, SSM_GROUP).transpose(3, 1, 0, 2, 4)
    u_g = u_g.reshape(n_groups, n_chunk * nb, SSM_CHUNK * SSM_GROUP)
    mats = _ssm_matrices(*ssm, l_eff)
    y_g, h_last = _ssm(u_g, *mats, h0, n_seq=nb, n_chunk=n_chunk)
    y = y_g.reshape(n_groups, n_chunk, nb, SSM_CHUNK, SSM_GROUP).transpose(2, 1, 3, 0, 4)
    y = y.reshape(nb, t_ssm, ssm_w)[:, :t] if t_ssm >= t else jnp.pad(
        y.reshape(nb, t_ssm, ssm_w), ((0, 0), (0, t - t_ssm), (0, 0)))
    mixed_ssm = _glu(y.reshape(n, ssm_w), wts["w_glu"], b_glu, out_norm_ssm)

    mixed = jnp.concatenate([mixed_attn, mixed_ssm], axis=1)
    h = _matmul(mixed, wts["w_out"], "residual", (x2,))
    hn = _rmsnorm(h, norm_ffn, MXU_DTYPE)
    pq = _matmul(hn, wts["w_query"])
    i1, i2, gate = _peer_topk(pq, sub_keys)
    to_rows = lambda a: a.transpose(2, 0, 1).reshape(n, PEER_HEADS * PEER_TOPK)
    y_out = _peer(hn, wts["peer_u"], wts["peer_v"], to_rows(i1), to_rows(i2), to_rows(gate), h)
    return y_out.reshape(nb, t, d_model), p3, h_last


def kernel(x_prompt, x_sample, cache_kv_cmp, cache_kv_sel, cache_kv_win, state_ssm_re, state_ssm_im, page_table,
           rel_table, norm_mix, w_in, q_norm, k_norm, w_phi, b_phi, ssm_lam_re, ssm_lam_im, ssm_log_dt,
           ssm_b_re, ssm_b_im, ssm_c_re, ssm_c_im, ssm_d, w_glu, b_glu, out_norm_attn, out_norm_ssm, w_out,
           norm_ffn, peer_w_query, peer_sub_keys, peer_u, peer_v):
    depth = w_in.shape[0]
    bp, t_p, d_model = x_prompt.shape
    db, t_s, _ = x_sample.shape
    past_len = page_table.shape[1] * PAGE_SIZE
    kv_row = 2 * KV_WIDTH
    n_groups = ssm_lam_re.shape[1]
    assert cache_kv_win.shape[2] == WINDOW and t_p % KEY_TILE == 0 and t_s <= SAMPLE_ROWS

    yp, ys = x_prompt, jnp.pad(x_sample, ((0, 0), (0, SAMPLE_ROWS - t_s), (0, 0)))
    new_p = [[] for _ in range(5)]
    new_s = [[] for _ in range(5)]
    for l in range(depth):
        wts = _prepare_weights(rel_table, norm_mix[l], w_in[l], q_norm[l], k_norm[l], w_phi[l], b_phi[l],
                               w_glu[l], w_out[l], peer_w_query[l], peer_u[l], peer_v[l])
        ssm = (ssm_lam_re[l], ssm_lam_im[l], ssm_log_dt[l], ssm_b_re[l], ssm_b_im[l], ssm_c_re[l],
               ssm_c_im[l], ssm_d[l])
        shared = (wts, rel_table, norm_mix[l], k_norm[l], b_phi[l], ssm, b_glu[l], out_norm_attn[l],
                  out_norm_ssm[l], norm_ffn[l], peer_sub_keys[l])
        off_cmp, off_sel, off_win = ATTN_WIDTH, ATTN_WIDTH + kv_row, ATTN_WIDTH + 2 * kv_row

        h0 = jnp.zeros((n_groups, bp, 2 * SSM_STATE), F32)
        yp, p3, h_last = _layer(yp, t_p, 0, None, None, None, None, h0, *shared)
        kv = lambda a, off, rows: a[:, rows, off:off + kv_row].reshape(
            a.shape[0], -1, 2, N_KV_HEADS, HEAD_DIM)
        keep = min(WINDOW, t_p)
        new_p[0].append(kv(p3, off_cmp, slice(0, t_p)))
        new_p[1].append(kv(p3, off_sel, slice(0, t_p)))
        new_p[2].append(kv(p3, off_win, slice(t_p - keep, t_p)))
        new_p[3].append(h_last[:, :, :SSM_STATE].transpose(1, 0, 2))
        new_p[4].append(h_last[:, :, SSM_STATE:].transpose(1, 0, 2))

        n_phys = cache_kv_cmp.shape[1]
        h0 = jnp.concatenate([state_ssm_re[l], state_ssm_im[l]], axis=-1).astype(F32).transpose(1, 0, 2)
        page_rows = PAGE_SIZE * 2 * N_KV_HEADS
        ys, p3, h_last = _layer(ys, t_s, past_len, cache_kv_cmp[l].reshape(n_phys, page_rows, HEAD_DIM),
                                cache_kv_sel[l].reshape(n_phys, page_rows, HEAD_DIM),
                                cache_kv_win[l].reshape(db * WIN_TILES, page_rows, HEAD_DIM), page_table, h0,
                                *shared)
        new_s[0].append(kv(p3, off_cmp, slice(0, t_s)))
        new_s[1].append(kv(p3, off_sel, slice(0, t_s)))
        win_all = jnp.concatenate([cache_kv_win[l], kv(p3, off_win, slice(0, t_s))], axis=1)
        keep = min(WINDOW, past_len + t_s)
        new_s[2].append(win_all[:, win_all.shape[1] - keep:])
        new_s[3].append(h_last[:, :, :SSM_STATE].transpose(1, 0, 2))
        new_s[4].append(h_last[:, :, SSM_STATE:].transpose(1, 0, 2))
    outs_p = [jnp.stack(v) for v in new_p]
    outs_s = [jnp.stack(v) for v in new_s]
    return (yp, ys[:, :t_s], *outs_p, *outs_s)
```

```python
import functools
import math

import numpy as np
import jax
import jax.numpy as jnp
from jax import lax
from jax.experimental import pallas as pl
from jax.experimental.pallas import tpu as pltpu

F32 = jnp.float32
MXU_DTYPE = jnp.bfloat16

HEAD_DIM = 128
N_KV_HEADS = 4
GQA_REP = 4
N_HEADS = N_KV_HEADS * GQA_REP
ATTN_WIDTH = N_HEADS * HEAD_DIM
KV_WIDTH = N_KV_HEADS * HEAD_DIM
ATTN_SCALE = HEAD_DIM ** -0.5
CMP_BLOCK = 32
CMP_STRIDE = 16
CMP_R = CMP_BLOCK // CMP_STRIDE
SLC_BLOCK = 64
N_SELECT = 16
FORCE_BONUS = 1.0e4
WINDOW = 512
NUM_BUCKETS = 32
MAX_DISTANCE = 128
SSM_GROUP = 16
SSM_STATE = 64
PEER_HEADS = 8
PEER_KEYS = 128
PEER_TOPK = 16
PEER_HALF = 128
EPS = 1e-6
NEG_INF = -1e30
PAGE_SIZE = 128

LANES = 128
SUBLANES = 8
MXU_DIM = 256
VMEM_LIMIT_BYTES = 52 * 1024 * 1024

KEY_TILE = 128
WIN_TILES = WINDOW // KEY_TILE
SSM_CHUNK = 16
SAMPLE_ROWS = 8


def _cparams(sem):
    return pltpu.CompilerParams(dimension_semantics=sem, vmem_limit_bytes=VMEM_LIMIT_BYTES)


def _row_tile(n, pref):
    t = min(n, pref)
    while n % t:
        t -= SUBLANES
    return t


def _dot(a, b):
    return jnp.dot(a, b, preferred_element_type=F32)


def _dot_nt(a, b):
    return lax.dot_general(a, b, (((1,), (1,)), ((), ())), preferred_element_type=F32)


def _gelu(x):
    c = math.sqrt(2.0 / math.pi)
    return 0.5 * x * (1.0 + jnp.tanh(c * (x + 0.044715 * (x * x * x))))


def _sigmoid(x):
    return 1.0 / (1.0 + jnp.exp(-x))


def _bucket_lower_bounds():
    n = np.arange(2 * MAX_DISTANCE)
    max_exact = NUM_BUCKETS // 2
    nf = np.maximum(n, 1).astype(np.float32)
    large = max_exact + (np.log(nf / np.float32(max_exact)) / np.float32(math.log(MAX_DISTANCE / max_exact))
                         * np.float32(NUM_BUCKETS - max_exact)).astype(np.int32)
    bucket = np.where(n < max_exact, n, np.minimum(large, NUM_BUCKETS - 1))
    assert np.all(np.diff(bucket) >= 0) and bucket[-1] == NUM_BUCKETS - 1
    return [int(np.argmax(bucket >= b)) for b in range(NUM_BUCKETS)]


BUCKET_LO = _bucket_lower_bounds()


def _bias_from_dist(dist, table_entry):
    out = table_entry(0)
    for b in range(1, NUM_BUCKETS):
        out = jnp.where(dist >= BUCKET_LO[b], table_entry(b), out)
    return out


def _rmsnorm_kernel(x_ref, g_ref, o_ref):
    x = x_ref[...]
    ms = jnp.mean(x * x, axis=-1, keepdims=True)
    o_ref[...] = (x * lax.rsqrt(ms + EPS) * g_ref[...]).astype(o_ref.dtype)


def _rmsnorm(x, g, out_dtype):
    n, d = x.shape
    tm = _row_tile(n, 256)
    return pl.pallas_call(
        _rmsnorm_kernel,
        grid=(n // tm,),
        in_specs=[pl.BlockSpec((tm, d), lambda i: (i, 0)), pl.BlockSpec((1, d), lambda i: (0, 0))],
        out_specs=pl.BlockSpec((tm, d), lambda i: (i, 0)),
        out_shape=jax.ShapeDtypeStruct((n, d), out_dtype),
        compiler_params=_cparams(("parallel",)),
    )(x, g.reshape(1, d))


def _matmul_kernel(*refs, epilogue):
    a_ref, b_ref = refs[0], refs[1]
    o_ref = refs[-1]
    acc = _dot(a_ref[...], b_ref[...])
    if epilogue == "none":
        o_ref[...] = acc
    elif epilogue == "sigmoid":
        o_ref[...] = _sigmoid(acc)
    elif epilogue == "headnorm":
        mode_ref, gain_ref = refs[2], refs[3]
        for c in range(acc.shape[1] // HEAD_DIM):
            sl = slice(c * HEAD_DIM, (c + 1) * HEAD_DIM)
            blk = acc[:, sl]
            ms = jnp.mean(blk * blk, axis=-1, keepdims=True)
            nrm = blk * lax.rsqrt(ms + EPS) * gain_ref[:, sl]
            o_ref[:, sl] = jnp.where(mode_ref[:, sl] > 0.0, nrm, blk)
    else:
        raise ValueError(epilogue)


def _matmul(a, b, epilogue="none", extras=(), tm_pref=512, tn_pref=1024):
    m, k = a.shape
    _, n = b.shape
    tm = _row_tile(m, tm_pref)
    tn = min(n, tn_pref)
    while n % tn:
        tn -= LANES
    in_specs = [pl.BlockSpec((tm, k), lambda i, j: (i, 0)), pl.BlockSpec((k, tn), lambda i, j: (0, j))]
    for e in extras:
        if e.shape[0] == 1:
            in_specs.append(pl.BlockSpec((1, tn), lambda i, j: (0, j)))
        else:
            in_specs.append(pl.BlockSpec((tm, tn), lambda i, j: (i, j)))
    return pl.pallas_call(
        functools.partial(_matmul_kernel, epilogue=epilogue),
        grid=(m // tm, n // tn),
        in_specs=in_specs,
        out_specs=pl.BlockSpec((tm, tn), lambda i, j: (i, j)),
        out_shape=jax.ShapeDtypeStruct((m, n), F32),
        compiler_params=_cparams(("parallel", "parallel")),
    )(a, b, *extras)


def _out_proj_kernel(a1_ref, a2_ref, b1_ref, b2_ref, res_ref, o_ref):
    o_ref[...] = res_ref[...] + _dot(a1_ref[...], b1_ref[...]) + _dot(a2_ref[...], b2_ref[...])


def _out_proj(a1, a2, w, res, tm_pref=512, tn_pref=1024):
    m, k1 = a1.shape
    k2 = a2.shape[1]
    assert k1 == k2 and w.shape[0] == k1 + k2
    n = w.shape[1]
    tm = _row_tile(m, tm_pref)
    tn = min(n, tn_pref)
    return pl.pallas_call(
        _out_proj_kernel,
        grid=(m // tm, n // tn),
        in_specs=[pl.BlockSpec((tm, k1), lambda i, j: (i, 0)), pl.BlockSpec((tm, k2), lambda i, j: (i, 0)),
                  pl.BlockSpec((k1, tn), lambda i, j: (0, j)), pl.BlockSpec((k2, tn), lambda i, j: (1, j)),
                  pl.BlockSpec((tm, tn), lambda i, j: (i, j))],
        out_specs=pl.BlockSpec((tm, tn), lambda i, j: (i, j)),
        out_shape=jax.ShapeDtypeStruct((m, n), F32),
        compiler_params=_cparams(("parallel", "parallel")),
    )(a1, a2, w, w, res)


CMP_PAGES_PER_STEP = 16


def _compress_kernel(pt_ref, *refs):
    del pt_ref
    pages = refs[:CMP_PAGES_PER_STEP]
    w_ref = refs[CMP_PAGES_PER_STEP]
    o_ref = refs[CMP_PAGES_PER_STEP + 1]
    x_scr = refs[CMP_PAGES_PER_STEP + 2]
    chunks_per_page = PAGE_SIZE // CMP_STRIDE
    rows_per_head = CMP_PAGES_PER_STEP * chunks_per_page
    heads_per_row = 2 * N_KV_HEADS
    for c in range(2):
        for g in range(N_KV_HEADS):
            for i in range(CMP_PAGES_PER_STEP):
                r0 = g * rows_per_head + i * chunks_per_page
                for s in range(CMP_STRIDE):
                    x_scr[r0:r0 + chunks_per_page, s * HEAD_DIM:(s + 1) * HEAD_DIM] = pages[i][
                        0, pl.ds(s * heads_per_row + c * N_KV_HEADS + g, chunks_per_page,
                                 stride=CMP_STRIDE * heads_per_row), :]
        res = _dot(x_scr[...].astype(MXU_DTYPE), w_ref[c])
        for g in range(N_KV_HEADS):
            o_ref[0, c * N_KV_HEADS + g] = res[g * rows_per_head:(g + 1) * rows_per_head]


def _compress(rows3, page_table, w_cmp):
    nb, n_pages = page_table.shape
    assert n_pages % CMP_PAGES_PER_STEP == 0
    n_steps = n_pages // CMP_PAGES_PER_STEP
    chunks_per_step = CMP_PAGES_PER_STEP * PAGE_SIZE // CMP_STRIDE
    n_chunks = n_pages * PAGE_SIZE // CMP_STRIDE

    def page_spec(i):
        return pl.BlockSpec((1, PAGE_SIZE * 2 * N_KV_HEADS, HEAD_DIM),
                            lambda b, j, pt: (pt[b, j * CMP_PAGES_PER_STEP + i], 0, 0))

    grid_spec = pltpu.PrefetchScalarGridSpec(
        num_scalar_prefetch=1,
        grid=(nb, n_steps),
        in_specs=[page_spec(i) for i in range(CMP_PAGES_PER_STEP)]
        + [pl.BlockSpec((2, CMP_STRIDE * HEAD_DIM, CMP_R * HEAD_DIM), lambda b, j, pt: (0, 0, 0))],
        out_specs=pl.BlockSpec((1, 2 * N_KV_HEADS, chunks_per_step, CMP_R * HEAD_DIM),
                               lambda b, j, pt: (b, 0, j, 0)),
        scratch_shapes=[pltpu.VMEM((N_KV_HEADS * chunks_per_step, CMP_STRIDE * HEAD_DIM), F32)],
    )
    return pl.pallas_call(
        _compress_kernel,
        grid_spec=grid_spec,
        out_shape=jax.ShapeDtypeStruct((nb, 2 * N_KV_HEADS, n_chunks, CMP_R * HEAD_DIM), F32),
        compiler_params=_cparams(("parallel", "arbitrary")),
    )(page_table, *([rows3] * CMP_PAGES_PER_STEP), w_cmp)


def _cmp_attn_kernel(tab_ref, q_ref, kp_ref, vp_ref, bphi_ref, kn_ref, m_ref, o_ref, sel_ref,
                     *, tq, pos0, n_cmp, n_slc):
    g = pl.program_id(1)
    i = pl.program_id(2)
    n_chunks = kp_ref.shape[2]
    kp = kp_ref[0, 0]
    vp = vp_ref[0, 0]
    k_c = bphi_ref[0:1, :] + kp[:, :HEAD_DIM] + pltpu.roll(kp[:, HEAD_DIM:], n_chunks - 1, 0)
    v_c = bphi_ref[1:2, :] + vp[:, :HEAD_DIM] + pltpu.roll(vp[:, HEAD_DIM:], n_chunks - 1, 0)
    ms = jnp.mean(k_c * k_c, axis=-1, keepdims=True)
    k_c = k_c * lax.rsqrt(ms + EPS) * kn_ref[...]

    q = jnp.concatenate([q_ref[0, :, r * HEAD_DIM:(r + 1) * HEAD_DIM] for r in range(GQA_REP)], axis=0)
    logits = _dot_nt(q.astype(MXU_DTYPE), k_c.astype(MXU_DTYPE)) * ATTN_SCALE
    row = lax.broadcasted_iota(jnp.int32, (tq, n_chunks), 0)
    col = lax.broadcasted_iota(jnp.int32, (tq, n_chunks), 1)
    dist = pos0 + i * tq + row - (col * CMP_STRIDE + (CMP_BLOCK - 1))
    bias = jnp.concatenate(
        [_bias_from_dist(dist, lambda b, r=r: tab_ref[g * GQA_REP + r, b]) for r in range(GQA_REP)], axis=0)
    valid = (dist >= 0) & (col < n_cmp)
    valid = jnp.concatenate([valid] * GQA_REP, axis=0)
    logits = jnp.where(valid, logits + bias, NEG_INF)
    mx = jnp.max(logits, axis=-1, keepdims=True)
    e = jnp.exp(logits - mx)
    any_valid = (jnp.max(jnp.where(valid, 1.0, 0.0), axis=-1, keepdims=True))
    p = e / jnp.sum(e, axis=-1, keepdims=True) * any_valid
    out = _dot(p.astype(MXU_DTYPE), v_c.astype(MXU_DTYPE))
    for r in range(GQA_REP):
        o_ref[0, :, r * HEAD_DIM:(r + 1) * HEAD_DIM] = out[r * tq:(r + 1) * tq]

    psum = p[0:tq]
    for r in range(1, GQA_REP):
        psum = psum + p[r * tq:(r + 1) * tq]
    hi = psum.astype(MXU_DTYPE)
    lo = (psum - hi.astype(F32)).astype(MXU_DTYPE)
    imp = _dot(hi, m_ref[...]) + _dot(lo, m_ref[...])
    n_pad = imp.shape[1]
    j = lax.broadcasted_iota(jnp.int32, (tq, n_pad), 1)
    qp = pos0 + i * tq + lax.broadcasted_iota(jnp.int32, (tq, n_pad), 0)
    cur = jnp.right_shift(qp, int(math.log2(SLC_BLOCK)))
    ok = j * SLC_BLOCK <= qp
    forced = (j == 0) | (j == cur) | (j == cur - 1)
    score = jnp.where(ok, imp + jnp.where(forced, FORCE_BONUS, 0.0), NEG_INF)
    rank = jnp.zeros((tq, n_pad), F32)
    for s in range(n_slc):
        cs = score[:, s:s + 1]
        beats = (cs > score) | ((cs == score) & (j > s))
        rank = rank + jnp.where(beats, 1.0, 0.0)
    selected = (rank < float(min(N_SELECT, n_slc))) & (score > 0.5 * NEG_INF)
    sel_ref[0, 0] = jnp.where(selected, 1.0, 0.0)


def _cmp_attention(tab, p3, pk, bphi, k_norm0, m_mat, *, pos0, n_cmp, n_slc, tq):
    nb, t, _ = p3.shape
    n_chunks = pk.shape[2]
    n_pad = m_mat.shape[1]
    nq = t // tq
    kern = functools.partial(_cmp_attn_kernel, tq=tq, pos0=pos0, n_cmp=n_cmp, n_slc=n_slc)
    return pl.pallas_call(
        kern,
        grid=(nb, N_KV_HEADS, nq),
        in_specs=[
            pl.BlockSpec(memory_space=pltpu.SMEM),
            pl.BlockSpec((1, tq, GQA_REP * HEAD_DIM), lambda b, g, i: (b, i, g)),
            pl.BlockSpec((1, 1, n_chunks, CMP_R * HEAD_DIM), lambda b, g, i: (b, g, 0, 0)),
            pl.BlockSpec((1, 1, n_chunks, CMP_R * HEAD_DIM), lambda b, g, i: (b, N_KV_HEADS + g, 0, 0)),
            pl.BlockSpec((2, HEAD_DIM), lambda b, g, i: (0, 0)),
            pl.BlockSpec((1, HEAD_DIM), lambda b, g, i: (0, 0)),
            pl.BlockSpec((n_chunks, n_pad), lambda b, g, i: (0, 0)),
        ],
        out_specs=[
            pl.BlockSpec((1, tq, GQA_REP * HEAD_DIM), lambda b, g, i: (b, i, g)),
            pl.BlockSpec((1, 1, tq, n_pad), lambda b, g, i: (b, g, i, 0)),
        ],
        out_shape=[
            jax.ShapeDtypeStruct((nb, t, ATTN_WIDTH), F32),
            jax.ShapeDtypeStruct((nb, N_KV_HEADS, t, n_pad), F32),
        ],
        compiler_params=_cparams(("parallel", "parallel", "parallel")),
    )(tab, p3, pk, pk, bphi, k_norm0, m_mat)


def _online_softmax_step(s, v, m_scr, l_scr, acc_scr):
    m_old = m_scr[...]
    m_new = jnp.maximum(m_old, jnp.max(s, axis=-1, keepdims=True))
    alpha = jnp.exp(m_old - m_new)
    p = jnp.exp(s - m_new)
    l_scr[...] = alpha * l_scr[...] + jnp.sum(p, axis=-1, keepdims=True)
    acc_scr[...] = alpha * acc_scr[...] + _dot(p.astype(MXU_DTYPE), v)
    m_scr[...] = m_new


ATTN_TILES_PER_ITER = 4


def _attn_prompt_kernel(*refs, mode, n_kt):
    if mode == "sel":
        q_ref, k_ref, v_ref, mb_ref, sel_ref, e_ref, o_ref, kb_scr, vt_scr, m_scr, l_scr, acc_scr = refs
    else:
        q_ref, k_ref, v_ref, mb_ref, o_ref, kb_scr, vt_scr, m_scr, l_scr, acc_scr = refs
    tq = KEY_TILE
    qt = pl.program_id(2)

    @pl.when(qt == 0)
    def _():
        for kt in range(n_kt):
            rows = slice(kt * KEY_TILE, (kt + 1) * KEY_TILE)
            kb_scr[kt] = k_ref[0, rows, :].astype(MXU_DTYPE)
            vt_scr[kt] = v_ref[0, rows, :].T.astype(MXU_DTYPE)

    q = jnp.concatenate([q_ref[0, :, r * HEAD_DIM:(r + 1) * HEAD_DIM] for r in range(GQA_REP)], axis=0)
    q = (q * ATTN_SCALE).astype(MXU_DTYPE)
    m_scr[...] = jnp.full(m_scr.shape, NEG_INF, F32)
    l_scr[...] = jnp.zeros(l_scr.shape, F32)
    acc_scr[...] = jnp.zeros(acc_scr.shape, F32)
    if mode == "sel":
        sel = sel_ref[0, 0].astype(MXU_DTYPE)

    n_kinds = mb_ref.shape[0]

    def logits(kt):
        d = qt - kt
        if mode == "sel":
            kind = jnp.where(d < 0, n_kinds - 1, jnp.minimum(d, 2))
        else:
            kind = jnp.where(kt < 0, n_kinds - 1, jnp.where(d == WIN_TILES, 3, jnp.minimum(d, 2)))
        kt = jnp.clip(kt, 0, n_kt - 1)
        s = _dot_nt(kb_scr[kt], q) + mb_ref[kind, 0]
        if mode == "sel":
            chosen = _dot_nt(e_ref[kt], sel)
            s = s + jnp.concatenate([(1.0 - chosen) * NEG_INF] * GQA_REP, axis=1)
        return kt, s

    def update(key_tiles):
        tiles = [logits(kt) for kt in key_tiles]
        m_old = m_scr[...]
        m_new = m_old
        for _, s in tiles:
            m_new = jnp.maximum(m_new, jnp.max(s, axis=0, keepdims=True))
        alpha = jnp.exp(m_old - m_new)
        l_new = alpha * l_scr[...]
        acc = alpha * acc_scr[...]
        for kt, s in tiles:
            p = jnp.exp(s - m_new)
            l_new = l_new + jnp.sum(p, axis=0, keepdims=True)
            acc = acc + _dot(vt_scr[kt], p.astype(MXU_DTYPE))
        l_scr[...] = l_new
        acc_scr[...] = acc
        m_scr[...] = m_new

    if mode == "sel":
        def body(it, carry):
            update([it * ATTN_TILES_PER_ITER + u for u in range(ATTN_TILES_PER_ITER)])
            return carry

        lax.fori_loop(0, (qt + ATTN_TILES_PER_ITER) // ATTN_TILES_PER_ITER, body, 0)
    else:
        update([qt - WIN_TILES + u for u in range(WIN_TILES + 1)])
    out = acc_scr[...] / l_scr[...]
    for r in range(GQA_REP):
        o_ref[0, :, r * HEAD_DIM:(r + 1) * HEAD_DIM] = out[:, r * tq:(r + 1) * tq].T


def _prompt_bias_tiles(tab, mode):
    ii = jnp.arange(KEY_TILE, dtype=jnp.int32)
    entry = lambda b: tab[:, b][:, None, None]

    def tile(offset):
        dist = (offset * KEY_TILE + ii[:, None] - ii[None, :])[None]
        return dist, jnp.broadcast_to(_bias_from_dist(dist, entry), (N_HEADS, KEY_TILE, KEY_TILE))

    d0, b0 = tile(0)
    tiles = [jnp.where(d0 >= 0, b0, NEG_INF), tile(1)[1], tile(2)[1]]
    if mode == "win":
        d4, b4 = tile(WIN_TILES)
        tiles.append(jnp.where(d4 <= WINDOW, b4, NEG_INF))
    tiles.append(jnp.full_like(b0, NEG_INF))
    mb = jnp.stack(tiles).reshape(len(tiles), N_KV_HEADS, GQA_REP, KEY_TILE, KEY_TILE)
    return mb.transpose(0, 1, 4, 2, 3).reshape(len(tiles), N_KV_HEADS, KEY_TILE, GQA_REP * KEY_TILE)


def _attn_prompt(p3, tab, mode, k_col, v_col, sel=None, e3=None):
    nb, t, _ = p3.shape
    tq = KEY_TILE
    nq = t // tq
    mb = _prompt_bias_tiles(tab, mode)
    cols = GQA_REP * tq
    in_specs = [
        pl.BlockSpec((1, tq, GQA_REP * HEAD_DIM), lambda b, g, i: (b, i, g)),
        pl.BlockSpec((1, t, HEAD_DIM), lambda b, g, i: (b, 0, k_col + g)),
        pl.BlockSpec((1, t, HEAD_DIM), lambda b, g, i: (b, 0, v_col + g)),
        pl.BlockSpec((mb.shape[0], 1, KEY_TILE, cols), lambda b, g, i: (0, g, 0, 0)),
    ]
    args = [p3, p3, p3, mb]
    if mode == "sel":
        n_pad = sel.shape[-1]
        in_specs += [
            pl.BlockSpec((1, 1, tq, n_pad), lambda b, g, i: (b, g, i, 0)),
            pl.BlockSpec((nq, KEY_TILE, n_pad), lambda b, g, i: (0, 0, 0)),
        ]
        args += [sel, e3]
    return pl.pallas_call(
        functools.partial(_attn_prompt_kernel, mode=mode, n_kt=nq),
        grid=(nb, N_KV_HEADS, nq),
        in_specs=in_specs,
        out_specs=pl.BlockSpec((1, tq, GQA_REP * HEAD_DIM), lambda b, g, i: (b, i, g)),
        out_shape=jax.ShapeDtypeStruct((nb, t, ATTN_WIDTH), F32),
        scratch_shapes=[pltpu.VMEM((nq, KEY_TILE, HEAD_DIM), MXU_DTYPE),
                        pltpu.VMEM((nq, HEAD_DIM, KEY_TILE), MXU_DTYPE),
                        pltpu.VMEM((1, cols), F32), pltpu.VMEM((1, cols), F32),
                        pltpu.VMEM((HEAD_DIM, cols), F32)],
        compiler_params=_cparams(("parallel", "parallel", "arbitrary")),
    )(*args)


def _attn_sample_kernel(*refs, mode, paged, pages, n_steps, tq, pos0, base_pos):
    it = iter(refs)
    if paged:
        next(it)
    q_ref = next(it)
    page_refs = [next(it) for _ in range(pages)]
    new_ref, rowtab_ref = next(it), next(it)
    if mode == "sel":
        sel_ref, e_ref = next(it), next(it)
    o_ref, m_scr, l_scr, acc_scr = next(it), next(it), next(it), next(it)
    kk = pl.program_id(1)
    rows = N_HEADS * tq
    heads_per_row = 2 * N_KV_HEADS

    @pl.when(kk == 0)
    def _():
        m_scr[...] = jnp.full(m_scr.shape, NEG_INF, F32)
        l_scr[...] = jnp.zeros(l_scr.shape, F32)
        acc_scr[...] = jnp.zeros(acc_scr.shape, F32)

    def queries():
        zero = jnp.zeros((tq, HEAD_DIM), F32)
        blocks = []
        for h in range(N_HEADS):
            qh = q_ref[0, :, h * HEAD_DIM:(h + 1) * HEAD_DIM] * ATTN_SCALE
            blocks.append(jnp.concatenate([qh if g == h // GQA_REP else zero for g in range(N_KV_HEADS)], axis=1))
        return jnp.concatenate(blocks, axis=0).astype(MXU_DTYPE)

    def step(k_all, v_all, key_pos0, chosen_cols, exact_bias):
        n_keys = k_all.shape[0]
        s = _dot_nt(queries(), k_all)
        row = lax.broadcasted_iota(jnp.int32, (rows, n_keys), 0) & (tq - 1)
        col = lax.broadcasted_iota(jnp.int32, (rows, n_keys), 1)
        dist = pos0 + row - (key_pos0 + col)
        if exact_bias:
            bias = _bias_from_dist(dist, lambda b: rowtab_ref[:, b:b + 1])
        else:
            bias = rowtab_ref[:, NUM_BUCKETS - 1:NUM_BUCKETS]
        if mode == "sel":
            sel = sel_ref[0].reshape(N_KV_HEADS * tq, sel_ref.shape[-1]).astype(MXU_DTYPE)
            chosen = _dot(sel, chosen_cols)
            chosen = jnp.concatenate(
                [chosen[g * tq:(g + 1) * tq] for g in range(N_KV_HEADS) for _ in range(GQA_REP)], axis=0)
            mask = (dist >= 0) & (chosen > 0.5)
        else:
            mask = (dist >= 0) & (dist <= WINDOW)
        s = jnp.where(mask, s + bias, NEG_INF)
        _online_softmax_step(s, v_all, m_scr, l_scr, acc_scr)

    def cached(kv):
        return jnp.concatenate([
            jnp.concatenate([pg[0, pl.ds(kv * N_KV_HEADS + g, PAGE_SIZE, stride=heads_per_row), :]
                             for g in range(N_KV_HEADS)], axis=1)
            for pg in page_refs], axis=0).astype(MXU_DTYPE)

    keys_per_step = pages * PAGE_SIZE
    key_pos0 = base_pos + kk * keys_per_step
    far = pos0 - (key_pos0 + keys_per_step - 1) >= MAX_DISTANCE
    is_cache = kk < n_steps
    chosen_cols = e_ref[...] if mode == "sel" else None

    @pl.when(is_cache & far)
    def _():
        step(cached(0), cached(1), key_pos0, chosen_cols, False)

    @pl.when(is_cache & jnp.logical_not(far))
    def _():
        step(cached(0), cached(1), key_pos0, chosen_cols, True)

    @pl.when(kk == n_steps)
    def _():
        pad = jnp.zeros((KEY_TILE - tq, N_KV_HEADS * HEAD_DIM), F32)
        k_new = jnp.concatenate([new_ref[0, :, :KV_WIDTH], pad], axis=0).astype(MXU_DTYPE)
        v_new = jnp.concatenate([new_ref[0, :, KV_WIDTH:], pad], axis=0).astype(MXU_DTYPE)
        cols = e_ref[:, :KEY_TILE] if mode == "sel" else None
        step(k_new, v_new, pos0, cols, True)
        out = acc_scr[...] / l_scr[...]
        for h in range(N_HEADS):
            g = h // GQA_REP
            o_ref[0, :, h * HEAD_DIM:(h + 1) * HEAD_DIM] = out[h * tq:(h + 1) * tq, g * HEAD_DIM:(g + 1) * HEAD_DIM]


def _attn_sample(p3, cache_rows, page_table, tab, mode, new_col, pos0, base_pos, sel=None, e_mat=None):
    nb, tq, _ = p3.shape
    paged = page_table is not None
    pages = 8 if paged else WIN_TILES
    n_tiles = page_table.shape[1] if paged else WIN_TILES
    assert n_tiles % pages == 0
    n_steps = n_tiles // pages
    rows = N_HEADS * tq
    rowtab = jnp.repeat(tab, tq, axis=0)

    def page_spec(i):
        def index(b, kk, *pt):
            page = jnp.minimum(kk, n_steps - 1) * pages + i
            return (pt[0][b, page] if paged else b * n_tiles + page, 0, 0)
        return pl.BlockSpec((1, PAGE_SIZE * 2 * N_KV_HEADS, HEAD_DIM), index)

    def fixed(shape, fn):
        return pl.BlockSpec(shape, lambda b, kk, *_: fn(b, kk))

    in_specs = [fixed((1, tq, ATTN_WIDTH), lambda b, kk: (b, 0, 0))]
    in_specs += [page_spec(i) for i in range(pages)]
    in_specs += [fixed((1, tq, 2 * KV_WIDTH), lambda b, kk: (b, 0, new_col)),
                 fixed((rows, NUM_BUCKETS), lambda b, kk: (0, 0))]
    args = [p3] + [cache_rows] * pages + [p3, rowtab]
    if mode == "sel":
        n_pad = sel.shape[-1]
        in_specs += [fixed((1, N_KV_HEADS, tq, n_pad), lambda b, kk: (b, 0, 0, 0)),
                     fixed((n_pad, pages * PAGE_SIZE), lambda b, kk: (0, kk))]
        args += [sel, e_mat]
    kern = functools.partial(_attn_sample_kernel, mode=mode, paged=paged, pages=pages, n_steps=n_steps, tq=tq,
                             pos0=pos0, base_pos=base_pos)
    out_spec = fixed((1, tq, ATTN_WIDTH), lambda b, kk: (b, 0, 0))
    out_shape = jax.ShapeDtypeStruct((nb, tq, ATTN_WIDTH), F32)
    scratch = [pltpu.VMEM((rows, 1), F32), pltpu.VMEM((rows, 1), F32),
               pltpu.VMEM((rows, N_KV_HEADS * HEAD_DIM), F32)]
    cp = _cparams(("parallel", "arbitrary"))
    grid = (nb, n_steps + 1)
    if paged:
        grid_spec = pltpu.PrefetchScalarGridSpec(num_scalar_prefetch=1, grid=grid, in_specs=in_specs,
                                                 out_specs=out_spec, scratch_shapes=scratch)
        return pl.pallas_call(kern, grid_spec=grid_spec, out_shape=out_shape, compiler_params=cp)(
            page_table, *args)
    return pl.pallas_call(kern, grid=grid, in_specs=in_specs, out_specs=out_spec, out_shape=out_shape,
                          scratch_shapes=scratch, compiler_params=cp)(*args)


def _combine_kernel(oc_ref, os_ref, ow_ref, gate_ref, g_ref, o_ref):
    gates = gate_ref[...]
    parts = []
    for h in range(N_HEADS):
        sl = slice(h * HEAD_DIM, (h + 1) * HEAD_DIM)
        parts.append(gates[:, h:h + 1] * oc_ref[:, sl]
                     + gates[:, N_HEADS + h:N_HEADS + h + 1] * os_ref[:, sl]
                     + gates[:, 2 * N_HEADS + h:2 * N_HEADS + h + 1] * ow_ref[:, sl])
    o = jnp.concatenate(parts, axis=1)
    ms = jnp.mean(o * o, axis=-1, keepdims=True)
    o_ref[...] = (o * lax.rsqrt(ms + EPS) * g_ref[...]).astype(o_ref.dtype)


def _combine(o_cmp, o_sel, o_win, gates, gain):
    n, w = o_cmp.shape
    tm = _row_tile(n, 256)
    row = pl.BlockSpec((tm, w), lambda i: (i, 0))
    return pl.pallas_call(
        _combine_kernel,
        grid=(n // tm,),
        in_specs=[row, row, row, pl.BlockSpec((tm, LANES), lambda i: (i, 0)),
                  pl.BlockSpec((1, w), lambda i: (0, 0))],
        out_specs=row,
        out_shape=jax.ShapeDtypeStruct((n, w), MXU_DTYPE),
        compiler_params=_cparams(("parallel",)),
    )(o_cmp, o_sel, o_win, gates, gain.reshape(1, w))


SSM_GROUPS_PER_STEP = 8


def _ssm_kernel(u_ref, tz_ref, bs_ref, cs_ref, d_ref, la_ref, lb_ref, h0_ref, y_ref, hout_ref,
                s_scr, hs_scr, *, n_seq, n_chunk):
    gb = SSM_GROUPS_PER_STEP
    for gi in range(gb):
        ub = u_ref[gi].astype(MXU_DTYPE)
        y_ref[gi] = _dot(ub, tz_ref[gi])
        s_scr[gi] = _dot(ub, bs_ref[gi])
    la = la_ref[...]
    lb = lb_ref[...]

    def body(c, h):
        r0 = c * n_seq
        hs_scr[:, pl.ds(r0, n_seq), :] = h
        swapped = jnp.concatenate([h[..., SSM_STATE:], h[..., :SSM_STATE]], axis=-1)
        return la * h + lb * swapped + s_scr[:, pl.ds(r0, n_seq), :]

    h = lax.fori_loop(0, n_chunk, body, h0_ref[...])
    hout_ref[...] = h
    for gi in range(gb):
        y_ref[gi] = y_ref[gi] + _dot(hs_scr[gi].astype(MXU_DTYPE), cs_ref[gi]) + d_ref[gi] * u_ref[gi]


def _ssm(u_g, tz, bs, cs, dvec, la, lb, h0, *, n_seq, n_chunk):
    g, r, w = u_g.shape
    gb = SSM_GROUPS_PER_STEP
    st = 2 * SSM_STATE

    def blk(shape):
        return pl.BlockSpec((gb,) + shape, lambda i: (i,) + (0,) * len(shape))

    return pl.pallas_call(
        functools.partial(_ssm_kernel, n_seq=n_seq, n_chunk=n_chunk),
        grid=(g // gb,),
        in_specs=[blk((r, w)), blk((w, w)), blk((w, st)), blk((st, w)), blk((1, w)), blk((1, st)),
                  blk((1, st)), blk((n_seq, st))],
        out_specs=[blk((r, w)), blk((n_seq, st))],
        out_shape=[jax.ShapeDtypeStruct((g, r, w), F32), jax.ShapeDtypeStruct((g, n_seq, st), F32)],
        scratch_shapes=[pltpu.VMEM((gb, r, st), F32), pltpu.VMEM((gb, r, st), F32)],
        compiler_params=_cparams(("parallel",)),
    )(u_g, tz, bs, cs, dvec, la, lb, h0)


def _ssm_matrices(lam_re, lam_im, log_dt, b_re, b_im, c_re, c_im, d_skip, l_eff):
    g = lam_re.shape[0]
    L = SSM_CHUNK
    lam = lax.complex(lam_re.astype(F32), lam_im.astype(F32))
    dt = jnp.exp(log_dt.astype(F32))[:, None]
    lam_bar = jnp.exp(lam * dt)
    b_bar = ((lam_bar - 1.0) / lam)[..., None] * lax.complex(b_re.astype(F32), b_im.astype(F32))
    c_c = lax.complex(c_re.astype(F32), c_im.astype(F32))
    pw = [jnp.ones_like(lam_bar)]
    for _ in range(L):
        pw.append(pw[-1] * lam_bar)
    pw = jnp.stack(pw)
    kern = jnp.einsum("gop,kgp,gpi->gkoi", c_c, pw[:L], b_bar).real
    s_idx = np.arange(L)[:, None]
    t_idx = np.arange(L)[None, :]
    tau = t_idx - s_idx
    tz = kern[:, np.clip(tau, 0, L - 1)]
    tz = jnp.where(jnp.asarray(tau >= 0)[None, :, :, None, None], tz, 0.0)
    tz = tz.transpose(0, 1, 4, 2, 3).reshape(g, L * SSM_GROUP, L * SSM_GROUP)
    exps = np.clip(l_eff - 1 - np.arange(L), 0, L)
    bx = pw[exps][..., None] * b_bar[None]
    bx = jnp.where(jnp.asarray(np.arange(L) < l_eff)[:, None, None, None], bx, 0.0)
    bs = jnp.concatenate([bx.real, bx.imag], axis=2)
    bs = bs.transpose(1, 0, 3, 2).reshape(g, L * SSM_GROUP, 2 * SSM_STATE)
    mt = c_c[None] * pw[1:L + 1][:, :, None, :]
    cs = jnp.concatenate([mt.real, -mt.imag], axis=3)
    cs = cs.transpose(1, 3, 0, 2).reshape(g, 2 * SSM_STATE, L * SSM_GROUP)
    lam_l = pw[l_eff]
    la = jnp.concatenate([lam_l.real, lam_l.real], axis=1)[:, None, :]
    lb = jnp.concatenate([-lam_l.imag, lam_l.imag], axis=1)[:, None, :]
    dvec = jnp.tile(d_skip.astype(F32).reshape(g, 1, SSM_GROUP), (1, 1, L))
    return tz.astype(MXU_DTYPE), bs.astype(MXU_DTYPE), cs.astype(MXU_DTYPE), dvec, la, lb


def _glu_kernel(y_ref, w_ref, b_ref, g_ref, o_ref):
    gl = _gelu(y_ref[...])
    z = _dot(gl.astype(MXU_DTYPE), w_ref[...]) + b_ref[...]
    o = gl * _sigmoid(z)
    ms = jnp.mean(o * o, axis=-1, keepdims=True)
    o_ref[...] = (o * lax.rsqrt(ms + EPS) * g_ref[...]).astype(o_ref.dtype)


def _glu(y, w, b, gain):
    n, d = y.shape
    tm = _row_tile(n, 256)
    vec = pl.BlockSpec((1, d), lambda i: (0, 0))
    return pl.pallas_call(
        _glu_kernel,
        grid=(n // tm,),
        in_specs=[pl.BlockSpec((tm, d), lambda i: (i, 0)), pl.BlockSpec((d, d), lambda i: (0, 0)), vec, vec],
        out_specs=pl.BlockSpec((tm, d), lambda i: (i, 0)),
        out_shape=jax.ShapeDtypeStruct((n, d), MXU_DTYPE),
        compiler_params=_cparams(("parallel",)),
    )(y, w, b.reshape(1, d), gain.reshape(1, d))


def _top_rows(x, k):
    n = x.shape[0]
    idx_iota = lax.broadcasted_iota(jnp.int32, x.shape, 0)
    vals, idxs = [], []
    cur = x
    for _ in range(k):
        m = jnp.max(cur, axis=0, keepdims=True)
        ix = jnp.min(jnp.where(cur == m, idx_iota, n), axis=0, keepdims=True)
        vals.append(m)
        idxs.append(ix)
        cur = jnp.where(idx_iota == ix, -jnp.inf, cur)
    return jnp.concatenate(vals, axis=0), jnp.concatenate(idxs, axis=0)


def _pick_rows(table, sel):
    out = jnp.zeros(sel.shape, table.dtype)
    for a in range(table.shape[0]):
        out = out + jnp.where(sel == a, table[a:a + 1, :], 0)
    return out


def _peer_topk_kernel(q_ref, keys_ref, i1_ref, i2_ref, gate_ref):
    q = q_ref[...].astype(MXU_DTYPE)
    s1 = _dot_nt(keys_ref[0, 0].astype(MXU_DTYPE), q[:, :PEER_HALF])
    s2 = _dot_nt(keys_ref[0, 1].astype(MXU_DTYPE), q[:, PEER_HALF:])
    v1, x1 = _top_rows(s1, PEER_TOPK)
    v2, x2 = _top_rows(s2, PEER_TOPK)
    t = v1.shape[1]
    pairs = [(a, b) for a in range(PEER_TOPK) for b in range(PEER_TOPK) if (a + 1) * (b + 1) <= PEER_TOPK]
    n_pad = -len(pairs) % SUBLANES
    cand = jnp.concatenate([v1[a:a + 1] + v2[b:b + 1] for a, b in pairs]
                           + [jnp.full((n_pad, t), -jnp.inf, F32)], axis=0)
    top, pos = _top_rows(cand, PEER_TOPK)
    i1_ref[0] = _pick_rows(jnp.concatenate([x1[a:a + 1] for a, _ in pairs], axis=0), pos)
    i2_ref[0] = _pick_rows(jnp.concatenate([x2[b:b + 1] for _, b in pairs], axis=0), pos)
    e = jnp.exp(top - jnp.max(top, axis=0, keepdims=True))
    gate_ref[0] = e / jnp.sum(e, axis=0, keepdims=True)


def _peer_topk(q, sub_keys):
    n = q.shape[0]
    tt = _row_tile(n, 512)
    out = pl.BlockSpec((1, PEER_TOPK, tt), lambda i, h: (h, 0, i))
    shp = (PEER_HEADS, PEER_TOPK, n)
    return pl.pallas_call(
        _peer_topk_kernel,
        grid=(n // tt, PEER_HEADS),
        in_specs=[pl.BlockSpec((tt, 2 * PEER_HALF), lambda i, h: (i, h)),
                  pl.BlockSpec((1, 2, PEER_KEYS, PEER_HALF), lambda i, h: (h, 0, 0, 0))],
        out_specs=[out, out, out],
        out_shape=[jax.ShapeDtypeStruct(shp, jnp.int32), jax.ShapeDtypeStruct(shp, jnp.int32),
                   jax.ShapeDtypeStruct(shp, F32)],
        compiler_params=_cparams(("parallel", "parallel")),
    )(q, sub_keys)


PEER_EXPERT_BLOCK = 2 * PEER_KEYS


PEER_GRID_HALF = PEER_KEYS // 2
HIGH_HALF = 0xFFFF0000


def _peer_kernel(x_ref, u_ref, v_ref, i1_ref, i2_ref, gate_ref, h_ref, o_ref, g_scr, w_scr, *, tm, n_blocks):
    j = pl.program_id(1)
    n_entries = PEER_HEADS * PEER_TOPK
    words = PEER_GRID_HALF // 2
    blocks_per_half = n_blocks // 2

    def build(base):
        r = lax.broadcasted_iota(jnp.int32, (PEER_GRID_HALF, n_entries), 0)
        i1_of_row = base + jnp.where(r < words, 2 * r, 2 * (r - words) + 1)
        i2_of_row = lax.broadcasted_iota(jnp.int32, (PEER_KEYS, n_entries), 0)

        def body(n8, carry):
            for k in range(SUBLANES):
                n = n8 * SUBLANES + k
                a = jnp.where(i1_of_row == i1_ref[pl.ds(n, 1), :], gate_ref[pl.ds(n, 1), :], 0.0)
                b = jnp.where(i2_of_row == i2_ref[pl.ds(n, 1), :], 1.0, 0.0)
                grid = _dot_nt(a.astype(MXU_DTYPE), b.astype(MXU_DTYPE))
                bits = lax.bitcast_convert_type(grid.astype(jnp.bfloat16).astype(F32), jnp.uint32)
                g_scr[pl.ds(pl.multiple_of(n * words, words), words), :] = (
                    jnp.right_shift(bits[:words], jnp.uint32(16)) | (bits[words:] & jnp.uint32(HIGH_HALF)))
            return carry

        lax.fori_loop(0, tm // SUBLANES, body, 0)

    def weights():
        act = _dot_nt(x_ref[...], u_ref[...])
        m = j - jnp.where(j >= blocks_per_half, blocks_per_half, 0)
        word = g_scr[pl.ds(m, tm, stride=words), :]
        even = lax.bitcast_convert_type(jnp.left_shift(word, jnp.uint32(16)), F32)
        odd = lax.bitcast_convert_type(word & jnp.uint32(HIGH_HALF), F32)
        g = jnp.concatenate([even, odd], axis=1)
        w_scr[j % 2] = (g * _gelu(act)).astype(MXU_DTYPE)

    def values():
        o_ref[...] += _dot(w_scr[(j + 1) % 2], v_ref[...])

    @pl.when(j == 0)
    def _():
        o_ref[...] = h_ref[...]
        build(0)
        weights()

    @pl.when(j == blocks_per_half)
    def _():
        build(PEER_GRID_HALF)

    @pl.when((j > 0) & (j < n_blocks))
    def _():
        values()
        weights()

    @pl.when(j == n_blocks)
    def _():
        values()


def _peer(xn, u, v, i1, i2, gate, h):
    n, d = xn.shape
    n_blocks = u.shape[0] // PEER_EXPERT_BLOCK
    assert PEER_EXPERT_BLOCK == 2 * PEER_KEYS and n_blocks == PEER_KEYS // 2
    tm = _row_tile(n, 512)
    n_entries = PEER_HEADS * PEER_TOPK
    once = pl.Buffered(1)
    row_in = pl.BlockSpec((tm, d), lambda i, j: (i, 0), pipeline_mode=once)
    ent = pl.BlockSpec((tm, n_entries), lambda i, j: (i, 0))
    return pl.pallas_call(
        functools.partial(_peer_kernel, tm=tm, n_blocks=n_blocks),
        grid=(n // tm, n_blocks + 1),
        in_specs=[row_in,
                  pl.BlockSpec((PEER_EXPERT_BLOCK, d), lambda i, j: (jnp.minimum(j, n_blocks - 1), 0)),
                  pl.BlockSpec((PEER_EXPERT_BLOCK, d), lambda i, j: (jnp.maximum(j - 1, 0), 0)),
                  ent, ent, ent, row_in],
        out_specs=pl.BlockSpec((tm, d), lambda i, j: (i, 0)),
        out_shape=jax.ShapeDtypeStruct((n, d), F32),
        scratch_shapes=[pltpu.VMEM((tm * PEER_GRID_HALF // 2, PEER_KEYS), jnp.uint32),
                        pltpu.VMEM((2, tm, PEER_EXPERT_BLOCK), MXU_DTYPE)],
        compiler_params=_cparams(("parallel", "arbitrary")),
    )(xn, u, v, i1, i2, gate, h)


def _prepare_weights(rel_table, norm_mix, w_in, q_norm, k_norm, w_phi, b_phi, w_glu, w_out, peer_w_query,
                     peer_u, peer_v):
    d_model = w_in.shape[0]
    off_gate = ATTN_WIDTH + 6 * KV_WIDTH
    off_ssm = off_gate + 3 * N_HEADS
    w_main = jnp.concatenate([w_in[:, :off_gate], w_in[:, off_ssm:]], axis=1).astype(MXU_DTYPE)
    w_gate = jnp.pad(w_in[:, off_gate:off_ssm], ((0, 0), (0, LANES - 3 * N_HEADS))).astype(MXU_DTYPE)
    n_main = w_main.shape[1]
    ones = jnp.ones((KV_WIDTH,), F32)
    zeros = jnp.zeros((KV_WIDTH,), F32)
    ssm_w = n_main - off_gate
    gain = jnp.concatenate([
        jnp.tile(q_norm.astype(F32), N_HEADS), ones, ones,
        jnp.tile(k_norm[1].astype(F32), N_KV_HEADS), ones,
        jnp.tile(k_norm[2].astype(F32), N_KV_HEADS), ones, jnp.ones((ssm_w,), F32)]).reshape(1, n_main)
    mode = jnp.concatenate([
        jnp.ones((ATTN_WIDTH,), F32), zeros, zeros, ones, zeros, ones, zeros,
        jnp.zeros((ssm_w,), F32)]).reshape(1, n_main)
    w_cmp = w_phi.reshape(2, CMP_R, CMP_STRIDE, HEAD_DIM, HEAD_DIM).transpose(0, 2, 3, 1, 4)
    w_cmp = w_cmp.reshape(2, CMP_STRIDE * HEAD_DIM, CMP_R * HEAD_DIM).astype(MXU_DTYPE)
    return dict(
        d_model=d_model, w_main=w_main, w_gate=w_gate, gain=gain, mode=mode, w_cmp=w_cmp,
        tab=rel_table.astype(F32).T, w_glu=w_glu.astype(MXU_DTYPE), w_out=w_out.astype(MXU_DTYPE),
        w_query=peer_w_query.astype(MXU_DTYPE), peer_u=peer_u.astype(MXU_DTYPE),
        peer_v=peer_v.astype(MXU_DTYPE))


def _overlap_matrix(n_chunks, n_cmp, n_slc, n_pad):
    c0 = np.arange(n_chunks)[:, None] * CMP_STRIDE
    s0 = np.arange(n_pad)[None, :] * SLC_BLOCK
    overlap = np.clip(np.minimum(c0 + CMP_BLOCK, s0 + SLC_BLOCK) - np.maximum(c0, s0), 0, None) / CMP_BLOCK
    overlap = overlap * (np.arange(n_chunks)[:, None] < n_cmp) * (np.arange(n_pad)[None, :] < n_slc)
    return jnp.asarray(overlap, dtype=MXU_DTYPE)


def _expansion_matrix(n_pad, n_keys, n_cols):
    key = np.arange(n_cols)[None, :]
    e = ((key // SLC_BLOCK) == np.arange(n_pad)[:, None]) & (key < n_keys)
    return jnp.asarray(e, dtype=MXU_DTYPE)


def _pad_lanes(n):
    return -(-n // LANES) * LANES


def _layer(x3, t_real, pos0, cache_cmp3, cache_sel3, cache_win3, page_table, h0, wts, rel_table, norm_mix,
           k_norm, b_phi, ssm, b_glu, out_norm_attn, out_norm_ssm, norm_ffn, sub_keys):
    nb, t, d_model = x3.shape
    n = nb * t
    sample = page_table is not None
    x2 = x3.reshape(n, d_model)
    xn = _rmsnorm(x2, norm_mix, MXU_DTYPE)
    p = _matmul(xn, wts["w_main"], "headnorm", (wts["mode"], wts["gain"]))
    gates = _matmul(xn, wts["w_gate"], "sigmoid")
    n_main = p.shape[1]
    p3 = p.reshape(nb, t, n_main)
    col = lambda off: off // HEAD_DIM
    off_cmp, off_sel, off_win = ATTN_WIDTH, ATTN_WIDTH + 2 * KV_WIDTH, ATTN_WIDTH + 4 * KV_WIDTH
    off_u = ATTN_WIDTH + 6 * KV_WIDTH

    if sample:
        assert t_real < CMP_STRIDE and pos0 % PAGE_SIZE == 0
        n_pages = page_table.shape[1]
        pk = _compress(cache_cmp3, page_table, wts["w_cmp"])
        total = pos0 + t_real
    else:
        n_pages = t // PAGE_SIZE
        pt = jnp.arange(nb * n_pages, dtype=jnp.int32).reshape(nb, n_pages)
        kv_cmp = p[:, off_cmp:off_cmp + 2 * KV_WIDTH]
        pk = _compress(kv_cmp.reshape(nb * n_pages, PAGE_SIZE * 2 * N_KV_HEADS, HEAD_DIM), pt, wts["w_cmp"])
        total = t
    n_chunks = pk.shape[2]
    n_cmp = total // CMP_STRIDE - CMP_R + 1
    n_slc = -(-total // SLC_BLOCK)
    n_pad = _pad_lanes(n_slc)
    m_mat = _overlap_matrix(n_chunks, n_cmp, n_slc, n_pad)
    tq_cmp = _row_tile(t, 256)
    tab = wts["tab"]
    o_cmp, sel = _cmp_attention(tab, p3, pk, b_phi.astype(F32), k_norm[0].astype(F32).reshape(1, HEAD_DIM),
                                m_mat, pos0=pos0, n_cmp=n_cmp, n_slc=n_slc, tq=tq_cmp)

    if sample:
        kv_block = 2 * KV_WIDTH
        n_cols = (n_pages + 8) * KEY_TILE
        e_mat = _expansion_matrix(n_pad, (n_pages + 1) * KEY_TILE, n_cols)
        o_sel = _attn_sample(p3, cache_sel3, page_table, tab, "sel", off_sel // kv_block, pos0, 0, sel, e_mat)
        o_win = _attn_sample(p3, cache_win3, None, tab, "win", off_win // kv_block, pos0, pos0 - WINDOW)
    else:
        e3 = _expansion_matrix(n_pad, t, t).T.reshape(t // KEY_TILE, KEY_TILE, n_pad)
        o_sel = _attn_prompt(p3, tab, "sel", col(off_sel), col(off_sel + KV_WIDTH), sel, e3)
        o_win = _attn_prompt(p3, tab, "win", col(off_win), col(off_win + KV_WIDTH))
    mixed_attn = _combine(o_cmp.reshape(n, ATTN_WIDTH), o_sel.reshape(n, ATTN_WIDTH),
                          o_win.reshape(n, ATTN_WIDTH), gates, out_norm_attn)

    ssm_w = n_main - off_u
    n_groups = ssm_w // SSM_GROUP
    u = p3[:, :, off_u:]
    if sample:
        u = jnp.pad(u[:, :t_real], ((0, 0), (0, SSM_CHUNK - t_real), (0, 0)))
        l_eff = t_real
    else:
        l_eff = SSM_CHUNK
    t_ssm = u.shape[1]
    n_chunk = t_ssm // SSM_CHUNK
    u_g = u.reshape(nb, n_chunk, SSM_CHUNK, n_groups, SSM_GROUP).transpose(3, 1, 0, 2, 4)
    u_g = u_g.reshape(n_groups, n_chunk * nb, SSM_CHUNK * SSM_GROUP)
    mats = _ssm_matrices(*ssm, l_eff)
    y_g, h_last = _ssm(u_g, *mats, h0, n_seq=nb, n_chunk=n_chunk)
    y = y_g.reshape(n_groups, n_chunk, nb, SSM_CHUNK, SSM_GROUP).transpose(2, 1, 3, 0, 4)
    y = y.reshape(nb, t_ssm, ssm_w)[:, :t] if t_ssm >= t else jnp.pad(
        y.reshape(nb, t_ssm, ssm_w), ((0, 0), (0, t - t_ssm), (0, 0)))
    mixed_ssm = _glu(y.reshape(n, ssm_w), wts["w_glu"], b_glu, out_norm_ssm)

    h = _out_proj(mixed_attn, mixed_ssm, wts["w_out"], x2)
    hn = _rmsnorm(h, norm_ffn, MXU_DTYPE)
    pq = _matmul(hn, wts["w_query"])
    i1, i2, gate = _peer_topk(pq, sub_keys)
    to_rows = lambda a: a.transpose(2, 0, 1).reshape(n, PEER_HEADS * PEER_TOPK)
    y_out = _peer(hn, wts["peer_u"], wts["peer_v"], to_rows(i1), to_rows(i2), to_rows(gate), h)
    return y_out.reshape(nb, t, d_model), p3, h_last


def kernel(x_prompt, x_sample, cache_kv_cmp, cache_kv_sel, cache_kv_win, state_ssm_re, state_ssm_im, page_table,
           rel_table, norm_mix, w_in, q_norm, k_norm, w_phi, b_phi, ssm_lam_re, ssm_lam_im, ssm_log_dt,
           ssm_b_re, ssm_b_im, ssm_c_re, ssm_c_im, ssm_d, w_glu, b_glu, out_norm_attn, out_norm_ssm, w_out,
           norm_ffn, peer_w_query, peer_sub_keys, peer_u, peer_v):
    depth = w_in.shape[0]
    bp, t_p, d_model = x_prompt.shape
    db, t_s, _ = x_sample.shape
    past_len = page_table.shape[1] * PAGE_SIZE
    kv_row = 2 * KV_WIDTH
    n_groups = ssm_lam_re.shape[1]
    assert cache_kv_win.shape[2] == WINDOW and t_p % KEY_TILE == 0 and t_s <= SAMPLE_ROWS

    yp, ys = x_prompt, jnp.pad(x_sample, ((0, 0), (0, SAMPLE_ROWS - t_s), (0, 0)))
    new_p = [[] for _ in range(5)]
    new_s = [[] for _ in range(5)]
    for l in range(depth):
        wts = _prepare_weights(rel_table, norm_mix[l], w_in[l], q_norm[l], k_norm[l], w_phi[l], b_phi[l],
                               w_glu[l], w_out[l], peer_w_query[l], peer_u[l], peer_v[l])
        ssm = (ssm_lam_re[l], ssm_lam_im[l], ssm_log_dt[l], ssm_b_re[l], ssm_b_im[l], ssm_c_re[l],
               ssm_c_im[l], ssm_d[l])
        shared = (wts, rel_table, norm_mix[l], k_norm[l], b_phi[l], ssm, b_glu[l], out_norm_attn[l],
                  out_norm_ssm[l], norm_ffn[l], peer_sub_keys[l])
        off_cmp, off_sel, off_win = ATTN_WIDTH, ATTN_WIDTH + kv_row, ATTN_WIDTH + 2 * kv_row

        h0 = jnp.zeros((n_groups, bp, 2 * SSM_STATE), F32)
        yp, p3, h_last = _layer(yp, t_p, 0, None, None, None, None, h0, *shared)
        kv = lambda a, off, rows: a[:, rows, off:off + kv_row].reshape(
            a.shape[0], -1, 2, N_KV_HEADS, HEAD_DIM)
        keep = min(WINDOW, t_p)
        new_p[0].append(kv(p3, off_cmp, slice(0, t_p)))
        new_p[1].append(kv(p3, off_sel, slice(0, t_p)))
        new_p[2].append(kv(p3, off_win, slice(t_p - keep, t_p)))
        new_p[3].append(h_last[:, :, :SSM_STATE].transpose(1, 0, 2))
        new_p[4].append(h_last[:, :, SSM_STATE:].transpose(1, 0, 2))

        n_phys = cache_kv_cmp.shape[1]
        h0 = jnp.concatenate([state_ssm_re[l], state_ssm_im[l]], axis=-1).astype(F32).transpose(1, 0, 2)
        page_rows = PAGE_SIZE * 2 * N_KV_HEADS
        ys, p3, h_last = _layer(ys, t_s, past_len, cache_kv_cmp[l].reshape(n_phys, page_rows, HEAD_DIM),
                                cache_kv_sel[l].reshape(n_phys, page_rows, HEAD_DIM),
                                cache_kv_win[l].reshape(db * WIN_TILES, page_rows, HEAD_DIM), page_table, h0,
                                *shared)
        new_s[0].append(kv(p3, off_cmp, slice(0, t_s)))
        new_s[1].append(kv(p3, off_sel, slice(0, t_s)))
        win_all = jnp.concatenate([cache_kv_win[l], kv(p3, off_win, slice(0, t_s))], axis=1)
        keep = min(WINDOW, past_len + t_s)
        new_s[2].append(win_all[:, win_all.shape[1] - keep:])
        new_s[3].append(h_last[:, :, :SSM_STATE].transpose(1, 0, 2))
        new_s[4].append(h_last[:, :, SSM_STATE:].transpose(1, 0, 2))
    outs_p = [jnp.stack(v) for v in new_p]
    outs_s = [jnp.stack(v) for v in new_s]
    return (yp, ys[:, :t_s], *outs_p, *outs_s)
```

```python
import functools
import math

import numpy as np
import jax
import jax.numpy as jnp
from jax import lax
from jax.experimental import pallas as pl
from jax.experimental.pallas import tpu as pltpu

F32 = jnp.float32
MXU_DTYPE = jnp.bfloat16

HEAD_DIM = 128
N_KV_HEADS = 4
GQA_REP = 4
N_HEADS = N_KV_HEADS * GQA_REP
ATTN_WIDTH = N_HEADS * HEAD_DIM
KV_WIDTH = N_KV_HEADS * HEAD_DIM
ATTN_SCALE = HEAD_DIM ** -0.5
CMP_BLOCK = 32
CMP_STRIDE = 16
CMP_R = CMP_BLOCK // CMP_STRIDE
SLC_BLOCK = 64
N_SELECT = 16
FORCE_BONUS = 1.0e4
WINDOW = 512
NUM_BUCKETS = 32
MAX_DISTANCE = 128
SSM_GROUP = 16
SSM_STATE = 64
PEER_HEADS = 8
PEER_KEYS = 128
PEER_TOPK = 16
PEER_HALF = 128
EPS = 1e-6
NEG_INF = -1e30
PAGE_SIZE = 128

LANES = 128
SUBLANES = 8
MXU_DIM = 256
VMEM_LIMIT_BYTES = 52 * 1024 * 1024

KEY_TILE = 128
WIN_TILES = WINDOW // KEY_TILE
SSM_CHUNK = 16
SAMPLE_ROWS = 8


def _cparams(sem):
    return pltpu.CompilerParams(dimension_semantics=sem, vmem_limit_bytes=VMEM_LIMIT_BYTES)


def _row_tile(n, pref):
    t = min(n, pref)
    while n % t:
        t -= SUBLANES
    return t


def _dot(a, b):
    return jnp.dot(a, b, preferred_element_type=F32)


def _dot_nt(a, b):
    return lax.dot_general(a, b, (((1,), (1,)), ((), ())), preferred_element_type=F32)


def _gelu(x):
    c = math.sqrt(2.0 / math.pi)
    return 0.5 * x * (1.0 + jnp.tanh(c * (x + 0.044715 * (x * x * x))))


def _sigmoid(x):
    return 1.0 / (1.0 + jnp.exp(-x))


def _bucket_lower_bounds():
    n = np.arange(2 * MAX_DISTANCE)
    max_exact = NUM_BUCKETS // 2
    nf = np.maximum(n, 1).astype(np.float32)
    large = max_exact + (np.log(nf / np.float32(max_exact)) / np.float32(math.log(MAX_DISTANCE / max_exact))
                         * np.float32(NUM_BUCKETS - max_exact)).astype(np.int32)
    bucket = np.where(n < max_exact, n, np.minimum(large, NUM_BUCKETS - 1))
    assert np.all(np.diff(bucket) >= 0) and bucket[-1] == NUM_BUCKETS - 1
    return [int(np.argmax(bucket >= b)) for b in range(NUM_BUCKETS)]


BUCKET_LO = _bucket_lower_bounds()


def _bias_from_dist(dist, table_entry):
    out = table_entry(0)
    for b in range(1, NUM_BUCKETS):
        out = jnp.where(dist >= BUCKET_LO[b], table_entry(b), out)
    return out


def _rmsnorm_kernel(x_ref, g_ref, o_ref):
    x = x_ref[...]
    ms = jnp.mean(x * x, axis=-1, keepdims=True)
    o_ref[...] = (x * lax.rsqrt(ms + EPS) * g_ref[...]).astype(o_ref.dtype)


def _rmsnorm(x, g, out_dtype):
    n, d = x.shape
    tm = _row_tile(n, 256)
    return pl.pallas_call(
        _rmsnorm_kernel,
        grid=(n // tm,),
        in_specs=[pl.BlockSpec((tm, d), lambda i: (i, 0)), pl.BlockSpec((1, d), lambda i: (0, 0))],
        out_specs=pl.BlockSpec((tm, d), lambda i: (i, 0)),
        out_shape=jax.ShapeDtypeStruct((n, d), out_dtype),
        compiler_params=_cparams(("parallel",)),
    )(x, g.reshape(1, d))


def _matmul_kernel(*refs, epilogue):
    a_ref, b_ref = refs[0], refs[1]
    o_ref = refs[-1]
    acc = _dot(a_ref[...], b_ref[...])
    if epilogue == "none":
        o_ref[...] = acc
    elif epilogue == "sigmoid":
        o_ref[...] = _sigmoid(acc)
    elif epilogue == "laneblocks":
        for c in range(acc.shape[1] // LANES):
            o_ref[c] = acc[:, c * LANES:(c + 1) * LANES]
    elif epilogue == "headnorm":
        mode_ref, gain_ref = refs[2], refs[3]
        for c in range(acc.shape[1] // HEAD_DIM):
            sl = slice(c * HEAD_DIM, (c + 1) * HEAD_DIM)
            blk = acc[:, sl]
            ms = jnp.mean(blk * blk, axis=-1, keepdims=True)
            nrm = blk * lax.rsqrt(ms + EPS) * gain_ref[:, sl]
            o_ref[:, sl] = jnp.where(mode_ref[:, sl] > 0.0, nrm, blk)
    else:
        raise ValueError(epilogue)


def _matmul(a, b, epilogue="none", extras=(), tm_pref=512, tn_pref=1024):
    m, k = a.shape
    _, n = b.shape
    tm = _row_tile(m, tm_pref)
    tn = min(n, tn_pref)
    while n % tn:
        tn -= LANES
    in_specs = [pl.BlockSpec((tm, k), lambda i, j: (i, 0)), pl.BlockSpec((k, tn), lambda i, j: (0, j))]
    for e in extras:
        if e.shape[0] == 1:
            in_specs.append(pl.BlockSpec((1, tn), lambda i, j: (0, j)))
        else:
            in_specs.append(pl.BlockSpec((tm, tn), lambda i, j: (i, j)))
    if epilogue == "laneblocks":
        out_spec = pl.BlockSpec((tn // LANES, tm, LANES), lambda i, j: (j, i, 0))
        out_shape = jax.ShapeDtypeStruct((n // LANES, m, LANES), F32)
    else:
        out_spec = pl.BlockSpec((tm, tn), lambda i, j: (i, j))
        out_shape = jax.ShapeDtypeStruct((m, n), F32)
    return pl.pallas_call(
        functools.partial(_matmul_kernel, epilogue=epilogue),
        grid=(m // tm, n // tn),
        in_specs=in_specs,
        out_specs=out_spec,
        out_shape=out_shape,
        compiler_params=_cparams(("parallel", "parallel")),
    )(a, b, *extras)


def _out_proj_kernel(a1_ref, a2_ref, b1_ref, b2_ref, res_ref, o_ref):
    o_ref[...] = res_ref[...] + _dot(a1_ref[...], b1_ref[...]) + _dot(a2_ref[...], b2_ref[...])


def _out_proj(a1, a2, w, res, tm_pref=512, tn_pref=1024):
    m, k1 = a1.shape
    k2 = a2.shape[1]
    assert k1 == k2 and w.shape[0] == k1 + k2
    n = w.shape[1]
    tm = _row_tile(m, tm_pref)
    tn = min(n, tn_pref)
    return pl.pallas_call(
        _out_proj_kernel,
        grid=(m // tm, n // tn),
        in_specs=[pl.BlockSpec((tm, k1), lambda i, j: (i, 0)), pl.BlockSpec((tm, k2), lambda i, j: (i, 0)),
                  pl.BlockSpec((k1, tn), lambda i, j: (0, j)), pl.BlockSpec((k2, tn), lambda i, j: (1, j)),
                  pl.BlockSpec((tm, tn), lambda i, j: (i, j))],
        out_specs=pl.BlockSpec((tm, tn), lambda i, j: (i, j)),
        out_shape=jax.ShapeDtypeStruct((m, n), F32),
        compiler_params=_cparams(("parallel", "parallel")),
    )(a1, a2, w, w, res)


CMP_PAGES_PER_STEP = 16


def _compress_kernel(pt_ref, *refs):
    del pt_ref
    pages = refs[:CMP_PAGES_PER_STEP]
    w_ref = refs[CMP_PAGES_PER_STEP]
    o_ref = refs[CMP_PAGES_PER_STEP + 1]
    x_scr = refs[CMP_PAGES_PER_STEP + 2]
    chunks_per_page = PAGE_SIZE // CMP_STRIDE
    rows_per_head = CMP_PAGES_PER_STEP * chunks_per_page
    heads_per_row = 2 * N_KV_HEADS
    for c in range(2):
        for g in range(N_KV_HEADS):
            for i in range(CMP_PAGES_PER_STEP):
                r0 = g * rows_per_head + i * chunks_per_page
                for s in range(CMP_STRIDE):
                    x_scr[r0:r0 + chunks_per_page, s * HEAD_DIM:(s + 1) * HEAD_DIM] = pages[i][
                        0, pl.ds(s * heads_per_row + c * N_KV_HEADS + g, chunks_per_page,
                                 stride=CMP_STRIDE * heads_per_row), :]
        res = _dot(x_scr[...].astype(MXU_DTYPE), w_ref[c])
        for g in range(N_KV_HEADS):
            o_ref[0, c * N_KV_HEADS + g] = res[g * rows_per_head:(g + 1) * rows_per_head]


def _compress(rows3, page_table, w_cmp):
    nb, n_pages = page_table.shape
    assert n_pages % CMP_PAGES_PER_STEP == 0
    n_steps = n_pages // CMP_PAGES_PER_STEP
    chunks_per_step = CMP_PAGES_PER_STEP * PAGE_SIZE // CMP_STRIDE
    n_chunks = n_pages * PAGE_SIZE // CMP_STRIDE

    def page_spec(i):
        return pl.BlockSpec((1, PAGE_SIZE * 2 * N_KV_HEADS, HEAD_DIM),
                            lambda b, j, pt: (pt[b, j * CMP_PAGES_PER_STEP + i], 0, 0))

    grid_spec = pltpu.PrefetchScalarGridSpec(
        num_scalar_prefetch=1,
        grid=(nb, n_steps),
        in_specs=[page_spec(i) for i in range(CMP_PAGES_PER_STEP)]
        + [pl.BlockSpec((2, CMP_STRIDE * HEAD_DIM, CMP_R * HEAD_DIM), lambda b, j, pt: (0, 0, 0))],
        out_specs=pl.BlockSpec((1, 2 * N_KV_HEADS, chunks_per_step, CMP_R * HEAD_DIM),
                               lambda b, j, pt: (b, 0, j, 0)),
        scratch_shapes=[pltpu.VMEM((N_KV_HEADS * chunks_per_step, CMP_STRIDE * HEAD_DIM), F32)],
    )
    return pl.pallas_call(
        _compress_kernel,
        grid_spec=grid_spec,
        out_shape=jax.ShapeDtypeStruct((nb, 2 * N_KV_HEADS, n_chunks, CMP_R * HEAD_DIM), F32),
        compiler_params=_cparams(("parallel", "arbitrary")),
    )(page_table, *([rows3] * CMP_PAGES_PER_STEP), w_cmp)


def _cmp_attn_kernel(tab_ref, q_ref, kp_ref, vp_ref, bphi_ref, kn_ref, m_ref, o_ref, sel_ref,
                     *, tq, pos0, n_cmp, n_slc):
    g = pl.program_id(1)
    i = pl.program_id(2)
    n_chunks = kp_ref.shape[2]
    kp = kp_ref[0, 0]
    vp = vp_ref[0, 0]
    k_c = bphi_ref[0:1, :] + kp[:, :HEAD_DIM] + pltpu.roll(kp[:, HEAD_DIM:], n_chunks - 1, 0)
    v_c = bphi_ref[1:2, :] + vp[:, :HEAD_DIM] + pltpu.roll(vp[:, HEAD_DIM:], n_chunks - 1, 0)
    ms = jnp.mean(k_c * k_c, axis=-1, keepdims=True)
    k_c = k_c * lax.rsqrt(ms + EPS) * kn_ref[...]

    q = jnp.concatenate([q_ref[0, :, r * HEAD_DIM:(r + 1) * HEAD_DIM] for r in range(GQA_REP)], axis=0)
    logits = _dot_nt(q.astype(MXU_DTYPE), k_c.astype(MXU_DTYPE)) * ATTN_SCALE
    row = lax.broadcasted_iota(jnp.int32, (tq, n_chunks), 0)
    col = lax.broadcasted_iota(jnp.int32, (tq, n_chunks), 1)
    dist = pos0 + i * tq + row - (col * CMP_STRIDE + (CMP_BLOCK - 1))
    bias = jnp.concatenate(
        [_bias_from_dist(dist, lambda b, r=r: tab_ref[g * GQA_REP + r, b]) for r in range(GQA_REP)], axis=0)
    valid = (dist >= 0) & (col < n_cmp)
    valid = jnp.concatenate([valid] * GQA_REP, axis=0)
    logits = jnp.where(valid, logits + bias, NEG_INF)
    mx = jnp.max(logits, axis=-1, keepdims=True)
    e = jnp.exp(logits - mx)
    any_valid = (jnp.max(jnp.where(valid, 1.0, 0.0), axis=-1, keepdims=True))
    p = e / jnp.sum(e, axis=-1, keepdims=True) * any_valid
    out = _dot(p.astype(MXU_DTYPE), v_c.astype(MXU_DTYPE))
    for r in range(GQA_REP):
        o_ref[0, :, r * HEAD_DIM:(r + 1) * HEAD_DIM] = out[r * tq:(r + 1) * tq]

    psum = p[0:tq]
    for r in range(1, GQA_REP):
        psum = psum + p[r * tq:(r + 1) * tq]
    hi = psum.astype(MXU_DTYPE)
    lo = (psum - hi.astype(F32)).astype(MXU_DTYPE)
    imp = _dot(hi, m_ref[...]) + _dot(lo, m_ref[...])
    n_pad = imp.shape[1]
    j = lax.broadcasted_iota(jnp.int32, (tq, n_pad), 1)
    qp = pos0 + i * tq + lax.broadcasted_iota(jnp.int32, (tq, n_pad), 0)
    cur = jnp.right_shift(qp, int(math.log2(SLC_BLOCK)))
    ok = j * SLC_BLOCK <= qp
    forced = (j == 0) | (j == cur) | (j == cur - 1)
    score = jnp.where(ok, imp + jnp.where(forced, FORCE_BONUS, 0.0), NEG_INF)
    rank = jnp.zeros((tq, n_pad), F32)
    for s in range(n_slc):
        cs = score[:, s:s + 1]
        beats = (cs > score) | ((cs == score) & (j > s))
        rank = rank + jnp.where(beats, 1.0, 0.0)
    selected = (rank < float(min(N_SELECT, n_slc))) & (score > 0.5 * NEG_INF)
    sel_ref[0, 0] = jnp.where(selected, 1.0, 0.0)


def _cmp_attention(tab, p3, pk, bphi, k_norm0, m_mat, *, pos0, n_cmp, n_slc, tq):
    nb, t, _ = p3.shape
    n_chunks = pk.shape[2]
    n_pad = m_mat.shape[1]
    nq = t // tq
    kern = functools.partial(_cmp_attn_kernel, tq=tq, pos0=pos0, n_cmp=n_cmp, n_slc=n_slc)
    return pl.pallas_call(
        kern,
        grid=(nb, N_KV_HEADS, nq),
        in_specs=[
            pl.BlockSpec(memory_space=pltpu.SMEM),
            pl.BlockSpec((1, tq, GQA_REP * HEAD_DIM), lambda b, g, i: (b, i, g)),
            pl.BlockSpec((1, 1, n_chunks, CMP_R * HEAD_DIM), lambda b, g, i: (b, g, 0, 0)),
            pl.BlockSpec((1, 1, n_chunks, CMP_R * HEAD_DIM), lambda b, g, i: (b, N_KV_HEADS + g, 0, 0)),
            pl.BlockSpec((2, HEAD_DIM), lambda b, g, i: (0, 0)),
            pl.BlockSpec((1, HEAD_DIM), lambda b, g, i: (0, 0)),
            pl.BlockSpec((n_chunks, n_pad), lambda b, g, i: (0, 0)),
        ],
        out_specs=[
            pl.BlockSpec((1, tq, GQA_REP * HEAD_DIM), lambda b, g, i: (b, i, g)),
            pl.BlockSpec((1, 1, tq, n_pad), lambda b, g, i: (b, g, i, 0)),
        ],
        out_shape=[
            jax.ShapeDtypeStruct((nb, t, ATTN_WIDTH), F32),
            jax.ShapeDtypeStruct((nb, N_KV_HEADS, t, n_pad), F32),
        ],
        compiler_params=_cparams(("parallel", "parallel", "parallel")),
    )(tab, p3, pk, pk, bphi, k_norm0, m_mat)


def _online_softmax_step(s, v, m_scr, l_scr, acc_scr):
    m_old = m_scr[...]
    m_new = jnp.maximum(m_old, jnp.max(s, axis=-1, keepdims=True))
    alpha = jnp.exp(m_old - m_new)
    p = jnp.exp(s - m_new)
    l_scr[...] = alpha * l_scr[...] + jnp.sum(p, axis=-1, keepdims=True)
    acc_scr[...] = alpha * acc_scr[...] + _dot(p.astype(MXU_DTYPE), v)
    m_scr[...] = m_new


ATTN_TILES_PER_ITER = 4


def _attn_prompt_kernel(*refs, mode, n_kt):
    if mode == "sel":
        q_ref, k_ref, v_ref, mb_ref, sel_ref, e_ref, o_ref, kb_scr, vt_scr, m_scr, l_scr, acc_scr = refs
    else:
        q_ref, k_ref, v_ref, mb_ref, o_ref, kb_scr, vt_scr, m_scr, l_scr, acc_scr = refs
    tq = KEY_TILE
    qt = pl.program_id(2)

    @pl.when(qt == 0)
    def _():
        for kt in range(n_kt):
            rows = slice(kt * KEY_TILE, (kt + 1) * KEY_TILE)
            kb_scr[kt] = k_ref[0, rows, :].astype(MXU_DTYPE)
            vt_scr[kt] = v_ref[0, rows, :].T.astype(MXU_DTYPE)

    q = jnp.concatenate([q_ref[0, :, r * HEAD_DIM:(r + 1) * HEAD_DIM] for r in range(GQA_REP)], axis=0)
    q = (q * ATTN_SCALE).astype(MXU_DTYPE)
    m_scr[...] = jnp.full(m_scr.shape, NEG_INF, F32)
    l_scr[...] = jnp.zeros(l_scr.shape, F32)
    acc_scr[...] = jnp.zeros(acc_scr.shape, F32)
    if mode == "sel":
        sel = sel_ref[0, 0].astype(MXU_DTYPE)

    n_kinds = mb_ref.shape[0]

    def logits(kt):
        d = qt - kt
        if mode == "sel":
            kind = jnp.where(d < 0, n_kinds - 1, jnp.minimum(d, 2))
        else:
            kind = jnp.where(kt < 0, n_kinds - 1, jnp.where(d == WIN_TILES, 3, jnp.minimum(d, 2)))
        kt = jnp.clip(kt, 0, n_kt - 1)
        s = _dot_nt(kb_scr[kt], q) + mb_ref[kind, 0]
        if mode == "sel":
            chosen = _dot_nt(e_ref[kt], sel)
            s = s + jnp.concatenate([(1.0 - chosen) * NEG_INF] * GQA_REP, axis=1)
        return kt, s

    def update(key_tiles):
        tiles = [logits(kt) for kt in key_tiles]
        m_old = m_scr[...]
        m_new = m_old
        for _, s in tiles:
            m_new = jnp.maximum(m_new, jnp.max(s, axis=0, keepdims=True))
        alpha = jnp.exp(m_old - m_new)
        l_new = alpha * l_scr[...]
        acc = alpha * acc_scr[...]
        for kt, s in tiles:
            p = jnp.exp(s - m_new)
            l_new = l_new + jnp.sum(p, axis=0, keepdims=True)
            acc = acc + _dot(vt_scr[kt], p.astype(MXU_DTYPE))
        l_scr[...] = l_new
        acc_scr[...] = acc
        m_scr[...] = m_new

    if mode == "sel":
        def body(it, carry):
            update([it * ATTN_TILES_PER_ITER + u for u in range(ATTN_TILES_PER_ITER)])
            return carry

        lax.fori_loop(0, (qt + ATTN_TILES_PER_ITER) // ATTN_TILES_PER_ITER, body, 0)
    else:
        update([qt - WIN_TILES + u for u in range(WIN_TILES + 1)])
    out = acc_scr[...] / l_scr[...]
    for r in range(GQA_REP):
        o_ref[0, :, r * HEAD_DIM:(r + 1) * HEAD_DIM] = out[:, r * tq:(r + 1) * tq].T


def _prompt_bias_tiles(tab, mode):
    ii = jnp.arange(KEY_TILE, dtype=jnp.int32)
    entry = lambda b: tab[:, b][:, None, None]

    def tile(offset):
        dist = (offset * KEY_TILE + ii[:, None] - ii[None, :])[None]
        return dist, jnp.broadcast_to(_bias_from_dist(dist, entry), (N_HEADS, KEY_TILE, KEY_TILE))

    d0, b0 = tile(0)
    tiles = [jnp.where(d0 >= 0, b0, NEG_INF), tile(1)[1], tile(2)[1]]
    if mode == "win":
        d4, b4 = tile(WIN_TILES)
        tiles.append(jnp.where(d4 <= WINDOW, b4, NEG_INF))
    tiles.append(jnp.full_like(b0, NEG_INF))
    mb = jnp.stack(tiles).reshape(len(tiles), N_KV_HEADS, GQA_REP, KEY_TILE, KEY_TILE)
    return mb.transpose(0, 1, 4, 2, 3).reshape(len(tiles), N_KV_HEADS, KEY_TILE, GQA_REP * KEY_TILE)


def _attn_prompt(p3, tab, mode, k_col, v_col, sel=None, e3=None):
    nb, t, _ = p3.shape
    tq = KEY_TILE
    nq = t // tq
    mb = _prompt_bias_tiles(tab, mode)
    cols = GQA_REP * tq
    in_specs = [
        pl.BlockSpec((1, tq, GQA_REP * HEAD_DIM), lambda b, g, i: (b, i, g)),
        pl.BlockSpec((1, t, HEAD_DIM), lambda b, g, i: (b, 0, k_col + g)),
        pl.BlockSpec((1, t, HEAD_DIM), lambda b, g, i: (b, 0, v_col + g)),
        pl.BlockSpec((mb.shape[0], 1, KEY_TILE, cols), lambda b, g, i: (0, g, 0, 0)),
    ]
    args = [p3, p3, p3, mb]
    if mode == "sel":
        n_pad = sel.shape[-1]
        in_specs += [
            pl.BlockSpec((1, 1, tq, n_pad), lambda b, g, i: (b, g, i, 0)),
            pl.BlockSpec((nq, KEY_TILE, n_pad), lambda b, g, i: (0, 0, 0)),
        ]
        args += [sel, e3]
    return pl.pallas_call(
        functools.partial(_attn_prompt_kernel, mode=mode, n_kt=nq),
        grid=(nb, N_KV_HEADS, nq),
        in_specs=in_specs,
        out_specs=pl.BlockSpec((1, tq, GQA_REP * HEAD_DIM), lambda b, g, i: (b, i, g)),
        out_shape=jax.ShapeDtypeStruct((nb, t, ATTN_WIDTH), F32),
        scratch_shapes=[pltpu.VMEM((nq, KEY_TILE, HEAD_DIM), MXU_DTYPE),
                        pltpu.VMEM((nq, HEAD_DIM, KEY_TILE), MXU_DTYPE),
                        pltpu.VMEM((1, cols), F32), pltpu.VMEM((1, cols), F32),
                        pltpu.VMEM((HEAD_DIM, cols), F32)],
        compiler_params=_cparams(("parallel", "parallel", "arbitrary")),
    )(*args)


def _attn_sample_kernel(*refs, mode, paged, pages, n_steps, tq, pos0, base_pos):
    it = iter(refs)
    if paged:
        next(it)
    q_ref = next(it)
    page_refs = [next(it) for _ in range(pages)]
    new_ref, rowtab_ref = next(it), next(it)
    if mode == "sel":
        sel_ref, e_ref = next(it), next(it)
    o_ref, m_scr, l_scr, acc_scr = next(it), next(it), next(it), next(it)
    kk = pl.program_id(1)
    rows = N_HEADS * tq
    heads_per_row = 2 * N_KV_HEADS

    @pl.when(kk == 0)
    def _():
        m_scr[...] = jnp.full(m_scr.shape, NEG_INF, F32)
        l_scr[...] = jnp.zeros(l_scr.shape, F32)
        acc_scr[...] = jnp.zeros(acc_scr.shape, F32)

    def queries():
        zero = jnp.zeros((tq, HEAD_DIM), F32)
        blocks = []
        for h in range(N_HEADS):
            qh = q_ref[0, :, h * HEAD_DIM:(h + 1) * HEAD_DIM] * ATTN_SCALE
            blocks.append(jnp.concatenate([qh if g == h // GQA_REP else zero for g in range(N_KV_HEADS)], axis=1))
        return jnp.concatenate(blocks, axis=0).astype(MXU_DTYPE)

    def step(k_all, v_all, key_pos0, chosen_cols, exact_bias):
        n_keys = k_all.shape[0]
        s = _dot_nt(queries(), k_all)
        row = lax.broadcasted_iota(jnp.int32, (rows, n_keys), 0) & (tq - 1)
        col = lax.broadcasted_iota(jnp.int32, (rows, n_keys), 1)
        dist = pos0 + row - (key_pos0 + col)
        if exact_bias:
            bias = _bias_from_dist(dist, lambda b: rowtab_ref[:, b:b + 1])
        else:
            bias = rowtab_ref[:, NUM_BUCKETS - 1:NUM_BUCKETS]
        if mode == "sel":
            sel = sel_ref[0].reshape(N_KV_HEADS * tq, sel_ref.shape[-1]).astype(MXU_DTYPE)
            chosen = _dot(sel, chosen_cols)
            chosen = jnp.concatenate(
                [chosen[g * tq:(g + 1) * tq] for g in range(N_KV_HEADS) for _ in range(GQA_REP)], axis=0)
            mask = (dist >= 0) & (chosen > 0.5)
        else:
            mask = (dist >= 0) & (dist <= WINDOW)
        s = jnp.where(mask, s + bias, NEG_INF)
        _online_softmax_step(s, v_all, m_scr, l_scr, acc_scr)

    def cached(kv):
        return jnp.concatenate([
            jnp.concatenate([pg[0, pl.ds(kv * N_KV_HEADS + g, PAGE_SIZE, stride=heads_per_row), :]
                             for g in range(N_KV_HEADS)], axis=1)
            for pg in page_refs], axis=0).astype(MXU_DTYPE)

    keys_per_step = pages * PAGE_SIZE
    key_pos0 = base_pos + kk * keys_per_step
    far = pos0 - (key_pos0 + keys_per_step - 1) >= MAX_DISTANCE
    is_cache = kk < n_steps
    chosen_cols = e_ref[...] if mode == "sel" else None

    @pl.when(is_cache & far)
    def _():
        step(cached(0), cached(1), key_pos0, chosen_cols, False)

    @pl.when(is_cache & jnp.logical_not(far))
    def _():
        step(cached(0), cached(1), key_pos0, chosen_cols, True)

    @pl.when(kk == n_steps)
    def _():
        pad = jnp.zeros((KEY_TILE - tq, N_KV_HEADS * HEAD_DIM), F32)
        k_new = jnp.concatenate([new_ref[0, :, :KV_WIDTH], pad], axis=0).astype(MXU_DTYPE)
        v_new = jnp.concatenate([new_ref[0, :, KV_WIDTH:], pad], axis=0).astype(MXU_DTYPE)
        cols = e_ref[:, :KEY_TILE] if mode == "sel" else None
        step(k_new, v_new, pos0, cols, True)
        out = acc_scr[...] / l_scr[...]
        for h in range(N_HEADS):
            g = h // GQA_REP
            o_ref[0, :, h * HEAD_DIM:(h + 1) * HEAD_DIM] = out[h * tq:(h + 1) * tq, g * HEAD_DIM:(g + 1) * HEAD_DIM]


def _attn_sample(p3, cache_rows, page_table, tab, mode, new_col, pos0, base_pos, sel=None, e_mat=None):
    nb, tq, _ = p3.shape
    paged = page_table is not None
    pages = 8 if paged else WIN_TILES
    n_tiles = page_table.shape[1] if paged else WIN_TILES
    assert n_tiles % pages == 0
    n_steps = n_tiles // pages
    rows = N_HEADS * tq
    rowtab = jnp.repeat(tab, tq, axis=0)

    def page_spec(i):
        def index(b, kk, *pt):
            page = jnp.minimum(kk, n_steps - 1) * pages + i
            return (pt[0][b, page] if paged else b * n_tiles + page, 0, 0)
        return pl.BlockSpec((1, PAGE_SIZE * 2 * N_KV_HEADS, HEAD_DIM), index)

    def fixed(shape, fn):
        return pl.BlockSpec(shape, lambda b, kk, *_: fn(b, kk))

    in_specs = [fixed((1, tq, ATTN_WIDTH), lambda b, kk: (b, 0, 0))]
    in_specs += [page_spec(i) for i in range(pages)]
    in_specs += [fixed((1, tq, 2 * KV_WIDTH), lambda b, kk: (b, 0, new_col)),
                 fixed((rows, NUM_BUCKETS), lambda b, kk: (0, 0))]
    args = [p3] + [cache_rows] * pages + [p3, rowtab]
    if mode == "sel":
        n_pad = sel.shape[-1]
        in_specs += [fixed((1, N_KV_HEADS, tq, n_pad), lambda b, kk: (b, 0, 0, 0)),
                     fixed((n_pad, pages * PAGE_SIZE), lambda b, kk: (0, kk))]
        args += [sel, e_mat]
    kern = functools.partial(_attn_sample_kernel, mode=mode, paged=paged, pages=pages, n_steps=n_steps, tq=tq,
                             pos0=pos0, base_pos=base_pos)
    out_spec = fixed((1, tq, ATTN_WIDTH), lambda b, kk: (b, 0, 0))
    out_shape = jax.ShapeDtypeStruct((nb, tq, ATTN_WIDTH), F32)
    scratch = [pltpu.VMEM((rows, 1), F32), pltpu.VMEM((rows, 1), F32),
               pltpu.VMEM((rows, N_KV_HEADS * HEAD_DIM), F32)]
    cp = _cparams(("parallel", "arbitrary"))
    grid = (nb, n_steps + 1)
    if paged:
        grid_spec = pltpu.PrefetchScalarGridSpec(num_scalar_prefetch=1, grid=grid, in_specs=in_specs,
                                                 out_specs=out_spec, scratch_shapes=scratch)
        return pl.pallas_call(kern, grid_spec=grid_spec, out_shape=out_shape, compiler_params=cp)(
            page_table, *args)
    return pl.pallas_call(kern, grid=grid, in_specs=in_specs, out_specs=out_spec, out_shape=out_shape,
                          scratch_shapes=scratch, compiler_params=cp)(*args)


def _combine_kernel(oc_ref, os_ref, ow_ref, gate_ref, g_ref, o_ref):
    gates = gate_ref[...]
    parts = []
    for h in range(N_HEADS):
        sl = slice(h * HEAD_DIM, (h + 1) * HEAD_DIM)
        parts.append(gates[:, h:h + 1] * oc_ref[:, sl]
                     + gates[:, N_HEADS + h:N_HEADS + h + 1] * os_ref[:, sl]
                     + gates[:, 2 * N_HEADS + h:2 * N_HEADS + h + 1] * ow_ref[:, sl])
    o = jnp.concatenate(parts, axis=1)
    ms = jnp.mean(o * o, axis=-1, keepdims=True)
    o_ref[...] = (o * lax.rsqrt(ms + EPS) * g_ref[...]).astype(o_ref.dtype)


def _combine(o_cmp, o_sel, o_win, gates, gain):
    n, w = o_cmp.shape
    tm = _row_tile(n, 256)
    row = pl.BlockSpec((tm, w), lambda i: (i, 0))
    return pl.pallas_call(
        _combine_kernel,
        grid=(n // tm,),
        in_specs=[row, row, row, pl.BlockSpec((tm, LANES), lambda i: (i, 0)),
                  pl.BlockSpec((1, w), lambda i: (0, 0))],
        out_specs=row,
        out_shape=jax.ShapeDtypeStruct((n, w), MXU_DTYPE),
        compiler_params=_cparams(("parallel",)),
    )(o_cmp, o_sel, o_win, gates, gain.reshape(1, w))


SSM_LANE_GROUPS = LANES // SSM_GROUP
SSM_BLOCK_STATE = SSM_LANE_GROUPS * 2 * SSM_STATE


def _ssm_kernel(u_ref, kt_ref, bs_ref, cs_ref, d_ref, la_ref, lb_ref, h0_ref, y_ref, hout_ref,
                us_scr, s_scr, hs_scr, *, n_seq, n_chunk):
    L = SSM_CHUNK
    rows = n_seq * n_chunk
    st = SSM_BLOCK_STATE
    group_shift = int(math.log2(SSM_GROUP))
    state_shift = int(math.log2(2 * SSM_STATE))
    in_mask = (jnp.right_shift(lax.broadcasted_iota(jnp.int32, (LANES, st), 0), group_shift)
               == jnp.right_shift(lax.broadcasted_iota(jnp.int32, (LANES, st), 1), state_shift))
    out_mask = (jnp.right_shift(lax.broadcasted_iota(jnp.int32, (st, LANES), 0), state_shift)
                == jnp.right_shift(lax.broadcasted_iota(jnp.int32, (st, LANES), 1), group_shift))
    zero = jnp.zeros((), MXU_DTYPE)

    s_acc = jnp.zeros((rows, st), F32)
    for s in range(L):
        us = u_ref[0, pl.ds(s, rows, stride=L), :].astype(MXU_DTYPE)
        us_scr[s] = us
        b_full = jnp.where(in_mask, jnp.concatenate([bs_ref[0, s]] * SSM_LANE_GROUPS, axis=1), zero)
        s_acc = s_acc + _dot(us, b_full)
    s_scr[...] = s_acc

    la = la_ref[0]
    lb = lb_ref[0]
    is_re = (lax.broadcasted_iota(jnp.int32, (1, st), 1) & SSM_STATE) == 0

    def swap(h):
        return jnp.where(is_re, pltpu.roll(h, st - SSM_STATE, 1), pltpu.roll(h, SSM_STATE, 1))

    def body(c, hs):
        new = []
        for b in range(n_seq):
            row = b * n_chunk + c
            hs_scr[pl.ds(row, 1), :] = hs[b]
            new.append(la * hs[b] + lb * swap(hs[b]) + s_scr[pl.ds(row, 1), :])
        return tuple(new)

    hs = lax.fori_loop(0, n_chunk, body, tuple(h0_ref[0, b:b + 1, :] for b in range(n_seq)))
    for b in range(n_seq):
        hout_ref[0, b:b + 1, :] = hs[b]

    h_start = hs_scr[...].astype(MXU_DTYPE)
    for t in range(L):
        c_full = jnp.where(out_mask, jnp.concatenate([cs_ref[0, t]] * SSM_LANE_GROUPS, axis=0), zero)
        acc = _dot(h_start, c_full)
        for s in range(t + 1):
            acc = acc + _dot(us_scr[s], kt_ref[0, t - s])
        acc = acc + d_ref[0] * u_ref[0, pl.ds(t, rows, stride=L), :]
        y_ref[0, pl.ds(t, rows, stride=L), :] = acc


def _ssm(u_lb, kt, bs, cs, dvec, la, lb, h0, *, n_seq, n_chunk):
    nblk, n, w = u_lb.shape
    L = SSM_CHUNK
    rows = n_seq * n_chunk
    assert n == rows * L and w == LANES
    st = SSM_BLOCK_STATE

    def blk(shape):
        return pl.BlockSpec((1,) + shape, lambda i: (i,) + (0,) * len(shape))

    return pl.pallas_call(
        functools.partial(_ssm_kernel, n_seq=n_seq, n_chunk=n_chunk),
        grid=(nblk,),
        in_specs=[blk((n, w)), blk((L, w, w)), blk((L, w, w)), blk((L, w, w)), blk((1, w)), blk((1, st)),
                  blk((1, st)), blk((n_seq, st))],
        out_specs=[blk((n, w)), blk((n_seq, st))],
        out_shape=[jax.ShapeDtypeStruct((nblk, n, w), F32), jax.ShapeDtypeStruct((nblk, n_seq, st), F32)],
        scratch_shapes=[pltpu.VMEM((L, rows, w), MXU_DTYPE), pltpu.VMEM((rows, st), F32),
                        pltpu.VMEM((rows, st), F32)],
        compiler_params=_cparams(("parallel",)),
    )(u_lb, kt, bs, cs, dvec, la, lb, h0)


def _ssm_matrices(lam_re, lam_im, log_dt, b_re, b_im, c_re, c_im, d_skip, l_eff):
    g = lam_re.shape[0]
    L = SSM_CHUNK
    lg = SSM_LANE_GROUPS
    nblk = g // lg
    lam = lax.complex(lam_re.astype(F32), lam_im.astype(F32))
    dt = jnp.exp(log_dt.astype(F32))[:, None]
    lam_bar = jnp.exp(lam * dt)
    b_bar = ((lam_bar - 1.0) / lam)[..., None] * lax.complex(b_re.astype(F32), b_im.astype(F32))
    c_c = lax.complex(c_re.astype(F32), c_im.astype(F32))
    pw = [jnp.ones_like(lam_bar)]
    for _ in range(L):
        pw.append(pw[-1] * lam_bar)
    pw = jnp.stack(pw)
    kern = jnp.einsum("gop,kgp,gpi->gkoi", c_c, pw[:L], b_bar).real
    kern = kern.reshape(nblk, lg, L, SSM_GROUP, SSM_GROUP)
    kt = jnp.einsum("lgtoi,gk->ltgiko", kern, jnp.eye(lg, dtype=kern.dtype)).reshape(nblk, L, LANES, LANES)
    exps = np.clip(l_eff - 1 - np.arange(L), 0, L)
    bx = pw[exps][..., None] * b_bar[None]
    bx = jnp.where(jnp.asarray(np.arange(L) < l_eff)[:, None, None, None], bx, 0.0)
    bs = jnp.stack([bx.real, bx.imag]).reshape(2, L, nblk, lg, SSM_STATE, SSM_GROUP)
    bs = bs.transpose(2, 1, 3, 5, 0, 4).reshape(nblk, L, LANES, 2 * SSM_STATE)
    mt = c_c[None] * pw[1:L + 1][:, :, None, :]
    cs = jnp.stack([mt.real, -mt.imag]).reshape(2, L, nblk, lg, SSM_GROUP, SSM_STATE)
    cs = cs.transpose(2, 1, 0, 5, 3, 4).reshape(nblk, L, 2 * SSM_STATE, LANES)
    lam_l = pw[l_eff].reshape(nblk, lg, SSM_STATE)
    la = jnp.stack([lam_l.real, lam_l.real], axis=2).reshape(nblk, 1, SSM_BLOCK_STATE)
    lb = jnp.stack([-lam_l.imag, lam_l.imag], axis=2).reshape(nblk, 1, SSM_BLOCK_STATE)
    dvec = d_skip.astype(F32).reshape(nblk, 1, LANES)
    return kt.astype(MXU_DTYPE), bs.astype(MXU_DTYPE), cs.astype(MXU_DTYPE), dvec, la, lb


def _state_to_blocks(re, im):
    n_seq, g, p = re.shape
    h = jnp.stack([re, im], axis=2).astype(F32).reshape(n_seq, g // SSM_LANE_GROUPS, SSM_BLOCK_STATE)
    return h.transpose(1, 0, 2)


def _state_from_blocks(h):
    nblk, n_seq, _ = h.shape
    h = h.transpose(1, 0, 2).reshape(n_seq, nblk * SSM_LANE_GROUPS, 2, SSM_STATE)
    return h[:, :, 0], h[:, :, 1]


def _glu_kernel(y_ref, w_ref, b_ref, g_ref, o_ref):
    gl = _gelu(jnp.concatenate([y_ref[k] for k in range(y_ref.shape[0])], axis=1))
    z = _dot(gl.astype(MXU_DTYPE), w_ref[...]) + b_ref[...]
    o = gl * _sigmoid(z)
    ms = jnp.mean(o * o, axis=-1, keepdims=True)
    o_ref[...] = (o * lax.rsqrt(ms + EPS) * g_ref[...]).astype(o_ref.dtype)


def _glu(y_lb, w, b, gain):
    nblk, n, _ = y_lb.shape
    d = nblk * LANES
    tm = _row_tile(n, 256)
    vec = pl.BlockSpec((1, d), lambda i: (0, 0))
    return pl.pallas_call(
        _glu_kernel,
        grid=(n // tm,),
        in_specs=[pl.BlockSpec((nblk, tm, LANES), lambda i: (0, i, 0)), pl.BlockSpec((d, d), lambda i: (0, 0)),
                  vec, vec],
        out_specs=pl.BlockSpec((tm, d), lambda i: (i, 0)),
        out_shape=jax.ShapeDtypeStruct((n, d), MXU_DTYPE),
        compiler_params=_cparams(("parallel",)),
    )(y_lb, w, b.reshape(1, d), gain.reshape(1, d))


def _top_rows(x, k):
    n = x.shape[0]
    idx_iota = lax.broadcasted_iota(jnp.int32, x.shape, 0)
    vals, idxs = [], []
    cur = x
    for _ in range(k):
        m = jnp.max(cur, axis=0, keepdims=True)
        ix = jnp.min(jnp.where(cur == m, idx_iota, n), axis=0, keepdims=True)
        vals.append(m)
        idxs.append(ix)
        cur = jnp.where(idx_iota == ix, -jnp.inf, cur)
    return jnp.concatenate(vals, axis=0), jnp.concatenate(idxs, axis=0)


def _pick_rows(table, sel):
    out = jnp.zeros(sel.shape, table.dtype)
    for a in range(table.shape[0]):
        out = out + jnp.where(sel == a, table[a:a + 1, :], 0)
    return out


def _peer_topk_kernel(q_ref, keys_ref, i1_ref, i2_ref, gate_ref):
    q = q_ref[...].astype(MXU_DTYPE)
    s1 = _dot_nt(keys_ref[0, 0].astype(MXU_DTYPE), q[:, :PEER_HALF])
    s2 = _dot_nt(keys_ref[0, 1].astype(MXU_DTYPE), q[:, PEER_HALF:])
    v1, x1 = _top_rows(s1, PEER_TOPK)
    v2, x2 = _top_rows(s2, PEER_TOPK)
    t = v1.shape[1]
    pairs = [(a, b) for a in range(PEER_TOPK) for b in range(PEER_TOPK) if (a + 1) * (b + 1) <= PEER_TOPK]
    n_pad = -len(pairs) % SUBLANES
    cand = jnp.concatenate([v1[a:a + 1] + v2[b:b + 1] for a, b in pairs]
                           + [jnp.full((n_pad, t), -jnp.inf, F32)], axis=0)
    top, pos = _top_rows(cand, PEER_TOPK)
    i1_ref[0] = _pick_rows(jnp.concatenate([x1[a:a + 1] for a, _ in pairs], axis=0), pos)
    i2_ref[0] = _pick_rows(jnp.concatenate([x2[b:b + 1] for _, b in pairs], axis=0), pos)
    e = jnp.exp(top - jnp.max(top, axis=0, keepdims=True))
    gate_ref[0] = e / jnp.sum(e, axis=0, keepdims=True)


def _peer_topk(q, sub_keys):
    n = q.shape[0]
    tt = _row_tile(n, 512)
    out = pl.BlockSpec((1, PEER_TOPK, tt), lambda i, h: (h, 0, i))
    shp = (PEER_HEADS, PEER_TOPK, n)
    return pl.pallas_call(
        _peer_topk_kernel,
        grid=(n // tt, PEER_HEADS),
        in_specs=[pl.BlockSpec((tt, 2 * PEER_HALF), lambda i, h: (i, h)),
                  pl.BlockSpec((1, 2, PEER_KEYS, PEER_HALF), lambda i, h: (h, 0, 0, 0))],
        out_specs=[out, out, out],
        out_shape=[jax.ShapeDtypeStruct(shp, jnp.int32), jax.ShapeDtypeStruct(shp, jnp.int32),
                   jax.ShapeDtypeStruct(shp, F32)],
        compiler_params=_cparams(("parallel", "parallel")),
    )(q, sub_keys)


PEER_EXPERT_BLOCK = 2 * PEER_KEYS


PEER_GRID_HALF = PEER_KEYS // 2
HIGH_HALF = 0xFFFF0000
PEER_BUILD_UNROLL = 32


def _peer_kernel(x_ref, u_ref, v_ref, i1_ref, i2_ref, gate_ref, h_ref, o_ref, g_scr, w_scr, *, tm, n_blocks):
    j = pl.program_id(1)
    n_entries = PEER_HEADS * PEER_TOPK
    words = PEER_GRID_HALF // 2
    blocks_per_half = n_blocks // 2
    unroll = math.gcd(tm, PEER_BUILD_UNROLL)

    def build(base):
        r = lax.broadcasted_iota(jnp.int32, (PEER_GRID_HALF, n_entries), 0)
        i1_of_row = base + jnp.where(r < words, 2 * r, 2 * (r - words) + 1)
        i2_of_row = lax.broadcasted_iota(jnp.int32, (PEER_KEYS, n_entries), 0)

        def body(nb, carry):
            for k in range(unroll):
                n = nb * unroll + k
                a = jnp.where(i1_of_row == i1_ref[pl.ds(n, 1), :], gate_ref[pl.ds(n, 1), :], 0.0)
                b = jnp.where(i2_of_row == i2_ref[pl.ds(n, 1), :], 1.0, 0.0)
                grid = _dot_nt(a.astype(MXU_DTYPE), b.astype(MXU_DTYPE))
                bits = lax.bitcast_convert_type(grid.astype(jnp.bfloat16).astype(F32), jnp.uint32)
                g_scr[pl.ds(pl.multiple_of(n * words, words), words), :] = (
                    jnp.right_shift(bits[:words], jnp.uint32(16)) | (bits[words:] & jnp.uint32(HIGH_HALF)))
            return carry

        lax.fori_loop(0, tm // unroll, body, 0)

    def weights():
        act = _dot_nt(x_ref[...], u_ref[...])
        m = j - jnp.where(j >= blocks_per_half, blocks_per_half, 0)
        word = g_scr[pl.ds(m, tm, stride=words), :]
        even = lax.bitcast_convert_type(jnp.left_shift(word, jnp.uint32(16)), F32)
        odd = lax.bitcast_convert_type(word & jnp.uint32(HIGH_HALF), F32)
        g = jnp.concatenate([even, odd], axis=1)
        w_scr[j % 2] = (g * _gelu(act)).astype(MXU_DTYPE)

    def values():
        o_ref[...] += _dot(w_scr[(j + 1) % 2], v_ref[...])

    @pl.when(j == 0)
    def _():
        o_ref[...] = h_ref[...]
        build(0)
        weights()

    @pl.when(j == blocks_per_half)
    def _():
        build(PEER_GRID_HALF)

    @pl.when((j > 0) & (j < n_blocks))
    def _():
        values()
        weights()

    @pl.when(j == n_blocks)
    def _():
        values()


def _peer(xn, u, v, i1, i2, gate, h):
    n, d = xn.shape
    n_blocks = u.shape[0] // PEER_EXPERT_BLOCK
    assert PEER_EXPERT_BLOCK == 2 * PEER_KEYS and n_blocks == PEER_KEYS // 2
    tm = _row_tile(n, 512)
    n_entries = PEER_HEADS * PEER_TOPK
    once = pl.Buffered(1)
    row_in = pl.BlockSpec((tm, d), lambda i, j: (i, 0), pipeline_mode=once)
    ent = pl.BlockSpec((tm, n_entries), lambda i, j: (i, 0))
    return pl.pallas_call(
        functools.partial(_peer_kernel, tm=tm, n_blocks=n_blocks),
        grid=(n // tm, n_blocks + 1),
        in_specs=[row_in,
                  pl.BlockSpec((PEER_EXPERT_BLOCK, d), lambda i, j: (jnp.minimum(j, n_blocks - 1), 0)),
                  pl.BlockSpec((PEER_EXPERT_BLOCK, d), lambda i, j: (jnp.maximum(j - 1, 0), 0)),
                  ent, ent, ent, row_in],
        out_specs=pl.BlockSpec((tm, d), lambda i, j: (i, 0)),
        out_shape=jax.ShapeDtypeStruct((n, d), F32),
        scratch_shapes=[pltpu.VMEM((tm * PEER_GRID_HALF // 2, PEER_KEYS), jnp.uint32),
                        pltpu.VMEM((2, tm, PEER_EXPERT_BLOCK), MXU_DTYPE)],
        compiler_params=_cparams(("parallel", "arbitrary")),
    )(xn, u, v, i1, i2, gate, h)


def _prepare_weights(rel_table, norm_mix, w_in, q_norm, k_norm, w_phi, b_phi, w_glu, w_out, peer_w_query,
                     peer_u, peer_v):
    d_model = w_in.shape[0]
    off_gate = ATTN_WIDTH + 6 * KV_WIDTH
    off_ssm = off_gate + 3 * N_HEADS
    w_main = w_in[:, :off_gate].astype(MXU_DTYPE)
    w_ssm = w_in[:, off_ssm:].astype(MXU_DTYPE)
    w_gate = jnp.pad(w_in[:, off_gate:off_ssm], ((0, 0), (0, LANES - 3 * N_HEADS))).astype(MXU_DTYPE)
    n_main = w_main.shape[1]
    ones = jnp.ones((KV_WIDTH,), F32)
    zeros = jnp.zeros((KV_WIDTH,), F32)
    gain = jnp.concatenate([
        jnp.tile(q_norm.astype(F32), N_HEADS), ones, ones,
        jnp.tile(k_norm[1].astype(F32), N_KV_HEADS), ones,
        jnp.tile(k_norm[2].astype(F32), N_KV_HEADS), ones]).reshape(1, n_main)
    mode = jnp.concatenate([
        jnp.ones((ATTN_WIDTH,), F32), zeros, zeros, ones, zeros, ones, zeros]).reshape(1, n_main)
    w_cmp = w_phi.reshape(2, CMP_R, CMP_STRIDE, HEAD_DIM, HEAD_DIM).transpose(0, 2, 3, 1, 4)
    w_cmp = w_cmp.reshape(2, CMP_STRIDE * HEAD_DIM, CMP_R * HEAD_DIM).astype(MXU_DTYPE)
    return dict(
        d_model=d_model, w_main=w_main, w_ssm=w_ssm, w_gate=w_gate, gain=gain, mode=mode, w_cmp=w_cmp,
        tab=rel_table.astype(F32).T, w_glu=w_glu.astype(MXU_DTYPE), w_out=w_out.astype(MXU_DTYPE),
        w_query=peer_w_query.astype(MXU_DTYPE), peer_u=peer_u.astype(MXU_DTYPE),
        peer_v=peer_v.astype(MXU_DTYPE))


def _overlap_matrix(n_chunks, n_cmp, n_slc, n_pad):
    c0 = np.arange(n_chunks)[:, None] * CMP_STRIDE
    s0 = np.arange(n_pad)[None, :] * SLC_BLOCK
    overlap = np.clip(np.minimum(c0 + CMP_BLOCK, s0 + SLC_BLOCK) - np.maximum(c0, s0), 0, None) / CMP_BLOCK
    overlap = overlap * (np.arange(n_chunks)[:, None] < n_cmp) * (np.arange(n_pad)[None, :] < n_slc)
    return jnp.asarray(overlap, dtype=MXU_DTYPE)


def _expansion_matrix(n_pad, n_keys, n_cols):
    key = np.arange(n_cols)[None, :]
    e = ((key // SLC_BLOCK) == np.arange(n_pad)[:, None]) & (key < n_keys)
    return jnp.asarray(e, dtype=MXU_DTYPE)


def _pad_lanes(n):
    return -(-n // LANES) * LANES


def _layer(x3, t_real, pos0, cache_cmp3, cache_sel3, cache_win3, page_table, h0, wts, rel_table, norm_mix,
           k_norm, b_phi, ssm, b_glu, out_norm_attn, out_norm_ssm, norm_ffn, sub_keys):
    nb, t, d_model = x3.shape
    n = nb * t
    sample = page_table is not None
    x2 = x3.reshape(n, d_model)
    xn = _rmsnorm(x2, norm_mix, MXU_DTYPE)
    p = _matmul(xn, wts["w_main"], "headnorm", (wts["mode"], wts["gain"]))
    gates = _matmul(xn, wts["w_gate"], "sigmoid")
    n_main = p.shape[1]
    p3 = p.reshape(nb, t, n_main)
    col = lambda off: off // HEAD_DIM
    off_cmp, off_sel, off_win = ATTN_WIDTH, ATTN_WIDTH + 2 * KV_WIDTH, ATTN_WIDTH + 4 * KV_WIDTH

    if sample:
        assert t_real < CMP_STRIDE and pos0 % PAGE_SIZE == 0
        n_pages = page_table.shape[1]
        pk = _compress(cache_cmp3, page_table, wts["w_cmp"])
        total = pos0 + t_real
    else:
        n_pages = t // PAGE_SIZE
        pt = jnp.arange(nb * n_pages, dtype=jnp.int32).reshape(nb, n_pages)
        kv_cmp = p[:, off_cmp:off_cmp + 2 * KV_WIDTH]
        pk = _compress(kv_cmp.reshape(nb * n_pages, PAGE_SIZE * 2 * N_KV_HEADS, HEAD_DIM), pt, wts["w_cmp"])
        total = t
    n_chunks = pk.shape[2]
    n_cmp = total // CMP_STRIDE - CMP_R + 1
    n_slc = -(-total // SLC_BLOCK)
    n_pad = _pad_lanes(n_slc)
    m_mat = _overlap_matrix(n_chunks, n_cmp, n_slc, n_pad)
    tq_cmp = _row_tile(t, 256)
    tab = wts["tab"]
    o_cmp, sel = _cmp_attention(tab, p3, pk, b_phi.astype(F32), k_norm[0].astype(F32).reshape(1, HEAD_DIM),
                                m_mat, pos0=pos0, n_cmp=n_cmp, n_slc=n_slc, tq=tq_cmp)

    if sample:
        kv_block = 2 * KV_WIDTH
        n_cols = (n_pages + 8) * KEY_TILE
        e_mat = _expansion_matrix(n_pad, (n_pages + 1) * KEY_TILE, n_cols)
        o_sel = _attn_sample(p3, cache_sel3, page_table, tab, "sel", off_sel // kv_block, pos0, 0, sel, e_mat)
        o_win = _attn_sample(p3, cache_win3, None, tab, "win", off_win // kv_block, pos0, pos0 - WINDOW)
    else:
        e3 = _expansion_matrix(n_pad, t, t).T.reshape(t // KEY_TILE, KEY_TILE, n_pad)
        o_sel = _attn_prompt(p3, tab, "sel", col(off_sel), col(off_sel + KV_WIDTH), sel, e3)
        o_win = _attn_prompt(p3, tab, "win", col(off_win), col(off_win + KV_WIDTH))
    mixed_attn = _combine(o_cmp.reshape(n, ATTN_WIDTH), o_sel.reshape(n, ATTN_WIDTH),
                          o_win.reshape(n, ATTN_WIDTH), gates, out_norm_attn)

    u_lb = _matmul(xn, wts["w_ssm"], "laneblocks")
    nblk = u_lb.shape[0]
    if t % SSM_CHUNK:
        assert t < SSM_CHUNK and t_real <= t
        u_lb = jnp.pad(u_lb.reshape(nblk, nb, t, LANES), ((0, 0), (0, 0), (0, SSM_CHUNK - t), (0, 0)))
        u_lb = u_lb.reshape(nblk, nb * SSM_CHUNK, LANES)
        l_eff, n_chunk = t_real, 1
    else:
        l_eff, n_chunk = SSM_CHUNK, t // SSM_CHUNK
    mats = _ssm_matrices(*ssm, l_eff)
    y_lb, h_last = _ssm(u_lb, *mats, _state_to_blocks(*h0), n_seq=nb, n_chunk=n_chunk)
    if t % SSM_CHUNK:
        y_lb = y_lb.reshape(nblk, nb, SSM_CHUNK, LANES)[:, :, :t].reshape(nblk, n, LANES)
    mixed_ssm = _glu(y_lb, wts["w_glu"], b_glu, out_norm_ssm)

    h = _out_proj(mixed_attn, mixed_ssm, wts["w_out"], x2)
    hn = _rmsnorm(h, norm_ffn, MXU_DTYPE)
    pq = _matmul(hn, wts["w_query"])
    i1, i2, gate = _peer_topk(pq, sub_keys)
    to_rows = lambda a: a.transpose(2, 0, 1).reshape(n, PEER_HEADS * PEER_TOPK)
    y_out = _peer(hn, wts["peer_u"], wts["peer_v"], to_rows(i1), to_rows(i2), to_rows(gate), h)
    return y_out.reshape(nb, t, d_model), p3, _state_from_blocks(h_last)


def kernel(x_prompt, x_sample, cache_kv_cmp, cache_kv_sel, cache_kv_win, state_ssm_re, state_ssm_im, page_table,
           rel_table, norm_mix, w_in, q_norm, k_norm, w_phi, b_phi, ssm_lam_re, ssm_lam_im, ssm_log_dt,
           ssm_b_re, ssm_b_im, ssm_c_re, ssm_c_im, ssm_d, w_glu, b_glu, out_norm_attn, out_norm_ssm, w_out,
           norm_ffn, peer_w_query, peer_sub_keys, peer_u, peer_v):
    depth = w_in.shape[0]
    bp, t_p, d_model = x_prompt.shape
    db, t_s, _ = x_sample.shape
    past_len = page_table.shape[1] * PAGE_SIZE
    kv_row = 2 * KV_WIDTH
    n_groups = ssm_lam_re.shape[1]
    assert cache_kv_win.shape[2] == WINDOW and t_p % KEY_TILE == 0 and t_s <= SAMPLE_ROWS

    yp, ys = x_prompt, jnp.pad(x_sample, ((0, 0), (0, SAMPLE_ROWS - t_s), (0, 0)))
    new_p = [[] for _ in range(5)]
    new_s = [[] for _ in range(5)]
    for l in range(depth):
        wts = _prepare_weights(rel_table, norm_mix[l], w_in[l], q_norm[l], k_norm[l], w_phi[l], b_phi[l],
                               w_glu[l], w_out[l], peer_w_query[l], peer_u[l], peer_v[l])
        ssm = (ssm_lam_re[l], ssm_lam_im[l], ssm_log_dt[l], ssm_b_re[l], ssm_b_im[l], ssm_c_re[l],
               ssm_c_im[l], ssm_d[l])
        shared = (wts, rel_table, norm_mix[l], k_norm[l], b_phi[l], ssm, b_glu[l], out_norm_attn[l],
                  out_norm_ssm[l], norm_ffn[l], peer_sub_keys[l])
        off_cmp, off_sel, off_win = ATTN_WIDTH, ATTN_WIDTH + kv_row, ATTN_WIDTH + 2 * kv_row

        zero_state = jnp.zeros((bp, n_groups, SSM_STATE), F32)
        yp, p3, h_last = _layer(yp, t_p, 0, None, None, None, None, (zero_state, zero_state), *shared)
        kv = lambda a, off, rows: a[:, rows, off:off + kv_row].reshape(
            a.shape[0], -1, 2, N_KV_HEADS, HEAD_DIM)
        keep = min(WINDOW, t_p)
        new_p[0].append(kv(p3, off_cmp, slice(0, t_p)))
        new_p[1].append(kv(p3, off_sel, slice(0, t_p)))
        new_p[2].append(kv(p3, off_win, slice(t_p - keep, t_p)))
        new_p[3].append(h_last[0])
        new_p[4].append(h_last[1])

        n_phys = cache_kv_cmp.shape[1]
        h0 = (state_ssm_re[l], state_ssm_im[l])
        page_rows = PAGE_SIZE * 2 * N_KV_HEADS
        ys, p3, h_last = _layer(ys, t_s, past_len, cache_kv_cmp[l].reshape(n_phys, page_rows, HEAD_DIM),
                                cache_kv_sel[l].reshape(n_phys, page_rows, HEAD_DIM),
                                cache_kv_win[l].reshape(db * WIN_TILES, page_rows, HEAD_DIM), page_table, h0,
                                *shared)
        new_s[0].append(kv(p3, off_cmp, slice(0, t_s)))
        new_s[1].append(kv(p3, off_sel, slice(0, t_s)))
        win_all = jnp.concatenate([cache_kv_win[l], kv(p3, off_win, slice(0, t_s))], axis=1)
        keep = min(WINDOW, past_len + t_s)
        new_s[2].append(win_all[:, win_all.shape[1] - keep:])
        new_s[3].append(h_last[0])
        new_s[4].append(h_last[1])
    outs_p = [jnp.stack(v) for v in new_p]
    outs_s = [jnp.stack(v) for v in new_s]
    return (yp, ys[:, :t_s], *outs_p, *outs_s)
```

```python
import functools
import math

import numpy as np
import jax
import jax.numpy as jnp
from jax import lax
from jax.experimental import pallas as pl
from jax.experimental.pallas import tpu as pltpu

F32 = jnp.float32
MXU_DTYPE = jnp.bfloat16

HEAD_DIM = 128
N_KV_HEADS = 4
GQA_REP = 4
N_HEADS = N_KV_HEADS * GQA_REP
ATTN_WIDTH = N_HEADS * HEAD_DIM
KV_WIDTH = N_KV_HEADS * HEAD_DIM
ATTN_SCALE = HEAD_DIM ** -0.5
CMP_BLOCK = 32
CMP_STRIDE = 16
CMP_R = CMP_BLOCK // CMP_STRIDE
SLC_BLOCK = 64
N_SELECT = 16
FORCE_BONUS = 1.0e4
WINDOW = 512
NUM_BUCKETS = 32
MAX_DISTANCE = 128
SSM_GROUP = 16
SSM_STATE = 64
PEER_HEADS = 8
PEER_KEYS = 128
PEER_TOPK = 16
PEER_HALF = 128
EPS = 1e-6
NEG_INF = -1e30
PAGE_SIZE = 128

LANES = 128
SUBLANES = 8
MXU_DIM = 256
VMEM_LIMIT_BYTES = 52 * 1024 * 1024

KEY_TILE = 128
WIN_TILES = WINDOW // KEY_TILE
SSM_CHUNK = 16
SAMPLE_ROWS = 8


def _cparams(sem):
    return pltpu.CompilerParams(dimension_semantics=sem, vmem_limit_bytes=VMEM_LIMIT_BYTES)


def _row_tile(n, pref):
    t = min(n, pref)
    while n % t:
        t -= SUBLANES
    return t


def _dot(a, b):
    return jnp.dot(a, b, preferred_element_type=F32)


def _dot_nt(a, b):
    return lax.dot_general(a, b, (((1,), (1,)), ((), ())), preferred_element_type=F32)


def _gelu(x):
    c = math.sqrt(2.0 / math.pi)
    return 0.5 * x * (1.0 + jnp.tanh(c * (x + 0.044715 * (x * x * x))))


def _sigmoid(x):
    return 1.0 / (1.0 + jnp.exp(-x))


def _bucket_lower_bounds():
    n = np.arange(2 * MAX_DISTANCE)
    max_exact = NUM_BUCKETS // 2
    nf = np.maximum(n, 1).astype(np.float32)
    large = max_exact + (np.log(nf / np.float32(max_exact)) / np.float32(math.log(MAX_DISTANCE / max_exact))
                         * np.float32(NUM_BUCKETS - max_exact)).astype(np.int32)
    bucket = np.where(n < max_exact, n, np.minimum(large, NUM_BUCKETS - 1))
    assert np.all(np.diff(bucket) >= 0) and bucket[-1] == NUM_BUCKETS - 1
    return [int(np.argmax(bucket >= b)) for b in range(NUM_BUCKETS)]


BUCKET_LO = _bucket_lower_bounds()


def _bias_from_dist(dist, table_entry):
    out = table_entry(0)
    for b in range(1, NUM_BUCKETS):
        out = jnp.where(dist >= BUCKET_LO[b], table_entry(b), out)
    return out


def _rmsnorm_kernel(x_ref, g_ref, o_ref):
    x = x_ref[...]
    ms = jnp.mean(x * x, axis=-1, keepdims=True)
    o_ref[...] = (x * lax.rsqrt(ms + EPS) * g_ref[...]).astype(o_ref.dtype)


def _rmsnorm(x, g, out_dtype):
    n, d = x.shape
    tm = _row_tile(n, 256)
    return pl.pallas_call(
        _rmsnorm_kernel,
        grid=(n // tm,),
        in_specs=[pl.BlockSpec((tm, d), lambda i: (i, 0)), pl.BlockSpec((1, d), lambda i: (0, 0))],
        out_specs=pl.BlockSpec((tm, d), lambda i: (i, 0)),
        out_shape=jax.ShapeDtypeStruct((n, d), out_dtype),
        compiler_params=_cparams(("parallel",)),
    )(x, g.reshape(1, d))


def _matmul_kernel(*refs, epilogue):
    a_ref, b_ref = refs[0], refs[1]
    o_ref = refs[-1]
    acc = _dot(a_ref[...], b_ref[...])
    if epilogue == "none":
        o_ref[...] = acc
    elif epilogue == "sigmoid":
        o_ref[...] = _sigmoid(acc)
    elif epilogue == "laneblocks":
        for c in range(acc.shape[1] // LANES):
            o_ref[c] = acc[:, c * LANES:(c + 1) * LANES]
    else:
        raise ValueError(epilogue)


def _matmul(a, b, epilogue="none", extras=(), tm_pref=512, tn_pref=1024):
    m, k = a.shape
    _, n = b.shape
    tm = _row_tile(m, tm_pref)
    tn = min(n, tn_pref)
    while n % tn:
        tn -= LANES
    in_specs = [pl.BlockSpec((tm, k), lambda i, j: (i, 0)), pl.BlockSpec((k, tn), lambda i, j: (0, j))]
    for e in extras:
        if e.shape[0] == 1:
            in_specs.append(pl.BlockSpec((1, tn), lambda i, j: (0, j)))
        else:
            in_specs.append(pl.BlockSpec((tm, tn), lambda i, j: (i, j)))
    if epilogue == "laneblocks":
        out_spec = pl.BlockSpec((tn // LANES, tm, LANES), lambda i, j: (j, i, 0))
        out_shape = jax.ShapeDtypeStruct((n // LANES, m, LANES), F32)
    else:
        out_spec = pl.BlockSpec((tm, tn), lambda i, j: (i, j))
        out_shape = jax.ShapeDtypeStruct((m, n), F32)
    return pl.pallas_call(
        functools.partial(_matmul_kernel, epilogue=epilogue),
        grid=(m // tm, n // tn),
        in_specs=in_specs,
        out_specs=out_spec,
        out_shape=out_shape,
        compiler_params=_cparams(("parallel", "parallel")),
    )(a, b, *extras)


def _in_proj_kernel(a_ref, b_ref, mode_ref, gain_ref, o_ref, *kv_refs, first_kv_tile):
    j = pl.program_id(1)
    acc = _dot(a_ref[...], b_ref[...])
    tm = acc.shape[0]
    blocks = []
    for c in range(acc.shape[1] // HEAD_DIM):
        sl = slice(c * HEAD_DIM, (c + 1) * HEAD_DIM)
        blk = acc[:, sl]
        ms = jnp.mean(blk * blk, axis=-1, keepdims=True)
        nrm = blk * lax.rsqrt(ms + EPS) * gain_ref[:, sl]
        blocks.append(jnp.where(mode_ref[:, sl] > 0.0, nrm, blk))
        o_ref[:, sl] = blocks[-1]
    for b, kv_ref in enumerate(kv_refs):
        @pl.when(j == first_kv_tile + b)
        def _(kv_ref=kv_ref):
            for c, blk in enumerate(blocks):
                kv_ref[pl.ds(c, tm, stride=len(blocks)), :] = blk


def _in_proj(a, b, mode, gain, tm_pref=512):
    m, k = a.shape
    n = b.shape[1]
    tn = 2 * KV_WIDTH
    assert ATTN_WIDTH % tn == 0 and n == ATTN_WIDTH + 3 * tn
    tm = _row_tile(m, tm_pref)
    heads = tn // HEAD_DIM
    vec = pl.BlockSpec((1, tn), lambda i, j: (0, j))
    kv_spec = pl.BlockSpec((tm * heads, HEAD_DIM), lambda i, j: (i, 0))
    kv_shape = jax.ShapeDtypeStruct((m * heads, HEAD_DIM), F32)
    return pl.pallas_call(
        functools.partial(_in_proj_kernel, first_kv_tile=ATTN_WIDTH // tn),
        grid=(m // tm, n // tn),
        in_specs=[pl.BlockSpec((tm, k), lambda i, j: (i, 0)), pl.BlockSpec((k, tn), lambda i, j: (0, j)), vec, vec],
        out_specs=[pl.BlockSpec((tm, tn), lambda i, j: (i, j)), kv_spec, kv_spec, kv_spec],
        out_shape=[jax.ShapeDtypeStruct((m, n), F32), kv_shape, kv_shape, kv_shape],
        compiler_params=_cparams(("parallel", "arbitrary")),
    )(a, b, mode, gain)


def _out_proj_kernel(a1_ref, a2_ref, b1_ref, b2_ref, res_ref, o_ref):
    o_ref[...] = res_ref[...] + _dot(a1_ref[...], b1_ref[...]) + _dot(a2_ref[...], b2_ref[...])


def _out_proj(a1, a2, w, res, tm_pref=512, tn_pref=1024):
    m, k1 = a1.shape
    k2 = a2.shape[1]
    assert k1 == k2 and w.shape[0] == k1 + k2
    n = w.shape[1]
    tm = _row_tile(m, tm_pref)
    tn = min(n, tn_pref)
    return pl.pallas_call(
        _out_proj_kernel,
        grid=(m // tm, n // tn),
        in_specs=[pl.BlockSpec((tm, k1), lambda i, j: (i, 0)), pl.BlockSpec((tm, k2), lambda i, j: (i, 0)),
                  pl.BlockSpec((k1, tn), lambda i, j: (0, j)), pl.BlockSpec((k2, tn), lambda i, j: (1, j)),
                  pl.BlockSpec((tm, tn), lambda i, j: (i, j))],
        out_specs=pl.BlockSpec((tm, tn), lambda i, j: (i, j)),
        out_shape=jax.ShapeDtypeStruct((m, n), F32),
        compiler_params=_cparams(("parallel", "parallel")),
    )(a1, a2, w, w, res)


CMP_PAGES_PER_STEP = 16


def _compress_kernel(pt_ref, *refs):
    del pt_ref
    pages = refs[:CMP_PAGES_PER_STEP]
    w_ref = refs[CMP_PAGES_PER_STEP]
    o_ref = refs[CMP_PAGES_PER_STEP + 1]
    x_scr = refs[CMP_PAGES_PER_STEP + 2]
    chunks_per_page = PAGE_SIZE // CMP_STRIDE
    rows_per_head = CMP_PAGES_PER_STEP * chunks_per_page
    heads_per_row = 2 * N_KV_HEADS
    for c in range(2):
        for g in range(N_KV_HEADS):
            for i in range(CMP_PAGES_PER_STEP):
                r0 = g * rows_per_head + i * chunks_per_page
                for s in range(CMP_STRIDE):
                    x_scr[r0:r0 + chunks_per_page, s * HEAD_DIM:(s + 1) * HEAD_DIM] = pages[i][
                        0, pl.ds(s * heads_per_row + c * N_KV_HEADS + g, chunks_per_page,
                                 stride=CMP_STRIDE * heads_per_row), :]
        res = _dot(x_scr[...].astype(MXU_DTYPE), w_ref[c])
        for g in range(N_KV_HEADS):
            o_ref[0, c * N_KV_HEADS + g] = res[g * rows_per_head:(g + 1) * rows_per_head]


def _compress(rows3, page_table, w_cmp):
    nb, n_pages = page_table.shape
    assert n_pages % CMP_PAGES_PER_STEP == 0
    n_steps = n_pages // CMP_PAGES_PER_STEP
    chunks_per_step = CMP_PAGES_PER_STEP * PAGE_SIZE // CMP_STRIDE
    n_chunks = n_pages * PAGE_SIZE // CMP_STRIDE

    def page_spec(i):
        return pl.BlockSpec((1, PAGE_SIZE * 2 * N_KV_HEADS, HEAD_DIM),
                            lambda b, j, pt: (pt[b, j * CMP_PAGES_PER_STEP + i], 0, 0))

    grid_spec = pltpu.PrefetchScalarGridSpec(
        num_scalar_prefetch=1,
        grid=(nb, n_steps),
        in_specs=[page_spec(i) for i in range(CMP_PAGES_PER_STEP)]
        + [pl.BlockSpec((2, CMP_STRIDE * HEAD_DIM, CMP_R * HEAD_DIM), lambda b, j, pt: (0, 0, 0))],
        out_specs=pl.BlockSpec((1, 2 * N_KV_HEADS, chunks_per_step, CMP_R * HEAD_DIM),
                               lambda b, j, pt: (b, 0, j, 0)),
        scratch_shapes=[pltpu.VMEM((N_KV_HEADS * chunks_per_step, CMP_STRIDE * HEAD_DIM), F32)],
    )
    return pl.pallas_call(
        _compress_kernel,
        grid_spec=grid_spec,
        out_shape=jax.ShapeDtypeStruct((nb, 2 * N_KV_HEADS, n_chunks, CMP_R * HEAD_DIM), F32),
        compiler_params=_cparams(("parallel", "arbitrary")),
    )(page_table, *([rows3] * CMP_PAGES_PER_STEP), w_cmp)


def _cmp_attn_kernel(tab_ref, q_ref, kp_ref, vp_ref, bphi_ref, kn_ref, m_ref, o_ref, sel_ref,
                     *, tq, pos0, n_cmp, n_slc):
    g = pl.program_id(1)
    i = pl.program_id(2)
    n_chunks = kp_ref.shape[2]
    kp = kp_ref[0, 0]
    vp = vp_ref[0, 0]
    k_c = bphi_ref[0:1, :] + kp[:, :HEAD_DIM] + pltpu.roll(kp[:, HEAD_DIM:], n_chunks - 1, 0)
    v_c = bphi_ref[1:2, :] + vp[:, :HEAD_DIM] + pltpu.roll(vp[:, HEAD_DIM:], n_chunks - 1, 0)
    ms = jnp.mean(k_c * k_c, axis=-1, keepdims=True)
    k_c = k_c * lax.rsqrt(ms + EPS) * kn_ref[...]

    q = jnp.concatenate([q_ref[0, :, r * HEAD_DIM:(r + 1) * HEAD_DIM] for r in range(GQA_REP)], axis=0)
    logits = _dot_nt(q.astype(MXU_DTYPE), k_c.astype(MXU_DTYPE)) * ATTN_SCALE
    row = lax.broadcasted_iota(jnp.int32, (tq, n_chunks), 0)
    col = lax.broadcasted_iota(jnp.int32, (tq, n_chunks), 1)
    dist = pos0 + i * tq + row - (col * CMP_STRIDE + (CMP_BLOCK - 1))
    bias = jnp.concatenate(
        [_bias_from_dist(dist, lambda b, r=r: tab_ref[g * GQA_REP + r, b]) for r in range(GQA_REP)], axis=0)
    valid = (dist >= 0) & (col < n_cmp)
    valid = jnp.concatenate([valid] * GQA_REP, axis=0)
    logits = jnp.where(valid, logits + bias, NEG_INF)
    mx = jnp.max(logits, axis=-1, keepdims=True)
    e = jnp.exp(logits - mx)
    any_valid = (jnp.max(jnp.where(valid, 1.0, 0.0), axis=-1, keepdims=True))
    p = e / jnp.sum(e, axis=-1, keepdims=True) * any_valid
    out = _dot(p.astype(MXU_DTYPE), v_c.astype(MXU_DTYPE))
    for r in range(GQA_REP):
        o_ref[0, :, r * HEAD_DIM:(r + 1) * HEAD_DIM] = out[r * tq:(r + 1) * tq]

    psum = p[0:tq]
    for r in range(1, GQA_REP):
        psum = psum + p[r * tq:(r + 1) * tq]
    hi = psum.astype(MXU_DTYPE)
    lo = (psum - hi.astype(F32)).astype(MXU_DTYPE)
    imp = _dot(hi, m_ref[...]) + _dot(lo, m_ref[...])
    n_pad = imp.shape[1]
    j = lax.broadcasted_iota(jnp.int32, (tq, n_pad), 1)
    qp = pos0 + i * tq + lax.broadcasted_iota(jnp.int32, (tq, n_pad), 0)
    cur = jnp.right_shift(qp, int(math.log2(SLC_BLOCK)))
    ok = j * SLC_BLOCK <= qp
    forced = (j == 0) | (j == cur) | (j == cur - 1)
    score = jnp.where(ok, imp + jnp.where(forced, FORCE_BONUS, 0.0), NEG_INF)
    rank = jnp.zeros((tq, n_pad), F32)
    for s in range(n_slc):
        cs = score[:, s:s + 1]
        beats = (cs > score) | ((cs == score) & (j > s))
        rank = rank + jnp.where(beats, 1.0, 0.0)
    selected = (rank < float(min(N_SELECT, n_slc))) & (score > 0.5 * NEG_INF)
    sel_ref[0, 0] = jnp.where(selected, 1.0, 0.0)


def _cmp_attention(tab, p3, pk, bphi, k_norm0, m_mat, *, pos0, n_cmp, n_slc, tq):
    nb, t, _ = p3.shape
    n_chunks = pk.shape[2]
    n_pad = m_mat.shape[1]
    nq = t // tq
    kern = functools.partial(_cmp_attn_kernel, tq=tq, pos0=pos0, n_cmp=n_cmp, n_slc=n_slc)
    return pl.pallas_call(
        kern,
        grid=(nb, N_KV_HEADS, nq),
        in_specs=[
            pl.BlockSpec(memory_space=pltpu.SMEM),
            pl.BlockSpec((1, tq, GQA_REP * HEAD_DIM), lambda b, g, i: (b, i, g)),
            pl.BlockSpec((1, 1, n_chunks, CMP_R * HEAD_DIM), lambda b, g, i: (b, g, 0, 0)),
            pl.BlockSpec((1, 1, n_chunks, CMP_R * HEAD_DIM), lambda b, g, i: (b, N_KV_HEADS + g, 0, 0)),
            pl.BlockSpec((2, HEAD_DIM), lambda b, g, i: (0, 0)),
            pl.BlockSpec((1, HEAD_DIM), lambda b, g, i: (0, 0)),
            pl.BlockSpec((n_chunks, n_pad), lambda b, g, i: (0, 0)),
        ],
        out_specs=[
            pl.BlockSpec((1, tq, GQA_REP * HEAD_DIM), lambda b, g, i: (b, i, g)),
            pl.BlockSpec((1, 1, tq, n_pad), lambda b, g, i: (b, g, i, 0)),
        ],
        out_shape=[
            jax.ShapeDtypeStruct((nb, t, ATTN_WIDTH), F32),
            jax.ShapeDtypeStruct((nb, N_KV_HEADS, t, n_pad), F32),
        ],
        compiler_params=_cparams(("parallel", "parallel", "parallel")),
    )(tab, p3, pk, pk, bphi, k_norm0, m_mat)


def _online_softmax_step(s, v, m_scr, l_scr, acc_scr):
    m_old = m_scr[...]
    m_new = jnp.maximum(m_old, jnp.max(s, axis=-1, keepdims=True))
    alpha = jnp.exp(m_old - m_new)
    p = jnp.exp(s - m_new)
    l_scr[...] = alpha * l_scr[...] + jnp.sum(p, axis=-1, keepdims=True)
    acc_scr[...] = alpha * acc_scr[...] + _dot(p.astype(MXU_DTYPE), v)
    m_scr[...] = m_new


ATTN_TILES_PER_ITER = 4


def _attn_prompt_kernel(*refs, mode, n_kt):
    if mode == "sel":
        q_ref, k_ref, v_ref, mb_ref, sel_ref, e_ref, o_ref, kb_scr, vt_scr, m_scr, l_scr, acc_scr = refs
    else:
        q_ref, k_ref, v_ref, mb_ref, o_ref, kb_scr, vt_scr, m_scr, l_scr, acc_scr = refs
    tq = KEY_TILE
    qt = pl.program_id(2)

    @pl.when(qt == 0)
    def _():
        for kt in range(n_kt):
            rows = slice(kt * KEY_TILE, (kt + 1) * KEY_TILE)
            kb_scr[kt] = k_ref[0, rows, :].astype(MXU_DTYPE)
            vt_scr[kt] = v_ref[0, rows, :].T.astype(MXU_DTYPE)

    q = jnp.concatenate([q_ref[0, :, r * HEAD_DIM:(r + 1) * HEAD_DIM] for r in range(GQA_REP)], axis=0)
    q = (q * ATTN_SCALE).astype(MXU_DTYPE)
    m_scr[...] = jnp.full(m_scr.shape, NEG_INF, F32)
    l_scr[...] = jnp.zeros(l_scr.shape, F32)
    acc_scr[...] = jnp.zeros(acc_scr.shape, F32)
    if mode == "sel":
        sel = sel_ref[0, 0].astype(MXU_DTYPE)

    n_kinds = mb_ref.shape[0]

    def logits(kt):
        d = qt - kt
        if mode == "sel":
            kind = jnp.where(d < 0, n_kinds - 1, jnp.minimum(d, 2))
        else:
            kind = jnp.where(kt < 0, n_kinds - 1, jnp.where(d == WIN_TILES, 3, jnp.minimum(d, 2)))
        kt = jnp.clip(kt, 0, n_kt - 1)
        s = _dot_nt(kb_scr[kt], q) + mb_ref[kind, 0]
        if mode == "sel":
            chosen = _dot_nt(e_ref[kt], sel)
            s = s + jnp.concatenate([(1.0 - chosen) * NEG_INF] * GQA_REP, axis=1)
        return kt, s

    def update(key_tiles):
        tiles = [logits(kt) for kt in key_tiles]
        m_old = m_scr[...]
        m_new = m_old
        for _, s in tiles:
            m_new = jnp.maximum(m_new, jnp.max(s, axis=0, keepdims=True))
        alpha = jnp.exp(m_old - m_new)
        l_new = alpha * l_scr[...]
        acc = alpha * acc_scr[...]
        for kt, s in tiles:
            p = jnp.exp(s - m_new)
            l_new = l_new + jnp.sum(p, axis=0, keepdims=True)
            acc = acc + _dot(vt_scr[kt], p.astype(MXU_DTYPE))
        l_scr[...] = l_new
        acc_scr[...] = acc
        m_scr[...] = m_new

    if mode == "sel":
        def body(it, carry):
            update([it * ATTN_TILES_PER_ITER + u for u in range(ATTN_TILES_PER_ITER)])
            return carry

        lax.fori_loop(0, (qt + ATTN_TILES_PER_ITER) // ATTN_TILES_PER_ITER, body, 0)
    else:
        update([qt - WIN_TILES + u for u in range(WIN_TILES + 1)])
    out = acc_scr[...] / l_scr[...]
    for r in range(GQA_REP):
        o_ref[0, :, r * HEAD_DIM:(r + 1) * HEAD_DIM] = out[:, r * tq:(r + 1) * tq].T


def _prompt_bias_tiles(tab, mode):
    ii = jnp.arange(KEY_TILE, dtype=jnp.int32)
    entry = lambda b: tab[:, b][:, None, None]

    def tile(offset):
        dist = (offset * KEY_TILE + ii[:, None] - ii[None, :])[None]
        return dist, jnp.broadcast_to(_bias_from_dist(dist, entry), (N_HEADS, KEY_TILE, KEY_TILE))

    d0, b0 = tile(0)
    tiles = [jnp.where(d0 >= 0, b0, NEG_INF), tile(1)[1], tile(2)[1]]
    if mode == "win":
        d4, b4 = tile(WIN_TILES)
        tiles.append(jnp.where(d4 <= WINDOW, b4, NEG_INF))
    tiles.append(jnp.full_like(b0, NEG_INF))
    mb = jnp.stack(tiles).reshape(len(tiles), N_KV_HEADS, GQA_REP, KEY_TILE, KEY_TILE)
    return mb.transpose(0, 1, 4, 2, 3).reshape(len(tiles), N_KV_HEADS, KEY_TILE, GQA_REP * KEY_TILE)


def _attn_prompt(p3, tab, mode, k_col, v_col, sel=None, e3=None):
    nb, t, _ = p3.shape
    tq = KEY_TILE
    nq = t // tq
    mb = _prompt_bias_tiles(tab, mode)
    cols = GQA_REP * tq
    in_specs = [
        pl.BlockSpec((1, tq, GQA_REP * HEAD_DIM), lambda b, g, i: (b, i, g)),
        pl.BlockSpec((1, t, HEAD_DIM), lambda b, g, i: (b, 0, k_col + g)),
        pl.BlockSpec((1, t, HEAD_DIM), lambda b, g, i: (b, 0, v_col + g)),
        pl.BlockSpec((mb.shape[0], 1, KEY_TILE, cols), lambda b, g, i: (0, g, 0, 0)),
    ]
    args = [p3, p3, p3, mb]
    if mode == "sel":
        n_pad = sel.shape[-1]
        in_specs += [
            pl.BlockSpec((1, 1, tq, n_pad), lambda b, g, i: (b, g, i, 0)),
            pl.BlockSpec((nq, KEY_TILE, n_pad), lambda b, g, i: (0, 0, 0)),
        ]
        args += [sel, e3]
    return pl.pallas_call(
        functools.partial(_attn_prompt_kernel, mode=mode, n_kt=nq),
        grid=(nb, N_KV_HEADS, nq),
        in_specs=in_specs,
        out_specs=pl.BlockSpec((1, tq, GQA_REP * HEAD_DIM), lambda b, g, i: (b, i, g)),
        out_shape=jax.ShapeDtypeStruct((nb, t, ATTN_WIDTH), F32),
        scratch_shapes=[pltpu.VMEM((nq, KEY_TILE, HEAD_DIM), MXU_DTYPE),
                        pltpu.VMEM((nq, HEAD_DIM, KEY_TILE), MXU_DTYPE),
                        pltpu.VMEM((1, cols), F32), pltpu.VMEM((1, cols), F32),
                        pltpu.VMEM((HEAD_DIM, cols), F32)],
        compiler_params=_cparams(("parallel", "parallel", "arbitrary")),
    )(*args)


def _attn_sample_kernel(*refs, mode, paged, pages, n_steps, tq, pos0, base_pos):
    it = iter(refs)
    if paged:
        next(it)
    q_ref = next(it)
    page_refs = [next(it) for _ in range(pages)]
    new_ref, rowtab_ref = next(it), next(it)
    if mode == "sel":
        sel_ref, e_ref = next(it), next(it)
    o_ref, m_scr, l_scr, acc_scr = next(it), next(it), next(it), next(it)
    kk = pl.program_id(1)
    rows = N_HEADS * tq
    heads_per_row = 2 * N_KV_HEADS

    @pl.when(kk == 0)
    def _():
        m_scr[...] = jnp.full(m_scr.shape, NEG_INF, F32)
        l_scr[...] = jnp.zeros(l_scr.shape, F32)
        acc_scr[...] = jnp.zeros(acc_scr.shape, F32)

    def queries():
        zero = jnp.zeros((tq, HEAD_DIM), F32)
        blocks = []
        for h in range(N_HEADS):
            qh = q_ref[0, :, h * HEAD_DIM:(h + 1) * HEAD_DIM] * ATTN_SCALE
            blocks.append(jnp.concatenate([qh if g == h // GQA_REP else zero for g in range(N_KV_HEADS)], axis=1))
        return jnp.concatenate(blocks, axis=0).astype(MXU_DTYPE)

    def step(k_all, v_all, key_pos0, chosen_cols, exact_bias):
        n_keys = k_all.shape[0]
        s = _dot_nt(queries(), k_all)
        row = lax.broadcasted_iota(jnp.int32, (rows, n_keys), 0) & (tq - 1)
        col = lax.broadcasted_iota(jnp.int32, (rows, n_keys), 1)
        dist = pos0 + row - (key_pos0 + col)
        if exact_bias:
            bias = _bias_from_dist(dist, lambda b: rowtab_ref[:, b:b + 1])
        else:
            bias = rowtab_ref[:, NUM_BUCKETS - 1:NUM_BUCKETS]
        if mode == "sel":
            sel = sel_ref[0].reshape(N_KV_HEADS * tq, sel_ref.shape[-1]).astype(MXU_DTYPE)
            chosen = _dot(sel, chosen_cols)
            chosen = jnp.concatenate(
                [chosen[g * tq:(g + 1) * tq] for g in range(N_KV_HEADS) for _ in range(GQA_REP)], axis=0)
            mask = (dist >= 0) & (chosen > 0.5)
        else:
            mask = (dist >= 0) & (dist <= WINDOW)
        s = jnp.where(mask, s + bias, NEG_INF)
        _online_softmax_step(s, v_all, m_scr, l_scr, acc_scr)

    def cached(kv):
        return jnp.concatenate([
            jnp.concatenate([pg[0, pl.ds(kv * N_KV_HEADS + g, PAGE_SIZE, stride=heads_per_row), :]
                             for g in range(N_KV_HEADS)], axis=1)
            for pg in page_refs], axis=0).astype(MXU_DTYPE)

    keys_per_step = pages * PAGE_SIZE
    key_pos0 = base_pos + kk * keys_per_step
    far = pos0 - (key_pos0 + keys_per_step - 1) >= MAX_DISTANCE
    is_cache = kk < n_steps
    chosen_cols = e_ref[...] if mode == "sel" else None

    @pl.when(is_cache & far)
    def _():
        step(cached(0), cached(1), key_pos0, chosen_cols, False)

    @pl.when(is_cache & jnp.logical_not(far))
    def _():
        step(cached(0), cached(1), key_pos0, chosen_cols, True)

    @pl.when(kk == n_steps)
    def _():
        pad = jnp.zeros((KEY_TILE - tq, N_KV_HEADS * HEAD_DIM), F32)
        k_new = jnp.concatenate([new_ref[0, :, :KV_WIDTH], pad], axis=0).astype(MXU_DTYPE)
        v_new = jnp.concatenate([new_ref[0, :, KV_WIDTH:], pad], axis=0).astype(MXU_DTYPE)
        cols = e_ref[:, :KEY_TILE] if mode == "sel" else None
        step(k_new, v_new, pos0, cols, True)
        out = acc_scr[...] / l_scr[...]
        for h in range(N_HEADS):
            g = h // GQA_REP
            o_ref[0, :, h * HEAD_DIM:(h + 1) * HEAD_DIM] = out[h * tq:(h + 1) * tq, g * HEAD_DIM:(g + 1) * HEAD_DIM]


def _attn_sample(p3, cache_rows, page_table, tab, mode, new_col, pos0, base_pos, sel=None, e_mat=None):
    nb, tq, _ = p3.shape
    paged = page_table is not None
    pages = 8 if paged else WIN_TILES
    n_tiles = page_table.shape[1] if paged else WIN_TILES
    assert n_tiles % pages == 0
    n_steps = n_tiles // pages
    rows = N_HEADS * tq
    rowtab = jnp.repeat(tab, tq, axis=0)

    def page_spec(i):
        def index(b, kk, *pt):
            page = jnp.minimum(kk, n_steps - 1) * pages + i
            return (pt[0][b, page] if paged else b * n_tiles + page, 0, 0)
        return pl.BlockSpec((1, PAGE_SIZE * 2 * N_KV_HEADS, HEAD_DIM), index)

    def fixed(shape, fn):
        return pl.BlockSpec(shape, lambda b, kk, *_: fn(b, kk))

    in_specs = [fixed((1, tq, ATTN_WIDTH), lambda b, kk: (b, 0, 0))]
    in_specs += [page_spec(i) for i in range(pages)]
    in_specs += [fixed((1, tq, 2 * KV_WIDTH), lambda b, kk: (b, 0, new_col)),
                 fixed((rows, NUM_BUCKETS), lambda b, kk: (0, 0))]
    args = [p3] + [cache_rows] * pages + [p3, rowtab]
    if mode == "sel":
        n_pad = sel.shape[-1]
        in_specs += [fixed((1, N_KV_HEADS, tq, n_pad), lambda b, kk: (b, 0, 0, 0)),
                     fixed((n_pad, pages * PAGE_SIZE), lambda b, kk: (0, kk))]
        args += [sel, e_mat]
    kern = functools.partial(_attn_sample_kernel, mode=mode, paged=paged, pages=pages, n_steps=n_steps, tq=tq,
                             pos0=pos0, base_pos=base_pos)
    out_spec = fixed((1, tq, ATTN_WIDTH), lambda b, kk: (b, 0, 0))
    out_shape = jax.ShapeDtypeStruct((nb, tq, ATTN_WIDTH), F32)
    scratch = [pltpu.VMEM((rows, 1), F32), pltpu.VMEM((rows, 1), F32),
               pltpu.VMEM((rows, N_KV_HEADS * HEAD_DIM), F32)]
    cp = _cparams(("parallel", "arbitrary"))
    grid = (nb, n_steps + 1)
    if paged:
        grid_spec = pltpu.PrefetchScalarGridSpec(num_scalar_prefetch=1, grid=grid, in_specs=in_specs,
                                                 out_specs=out_spec, scratch_shapes=scratch)
        return pl.pallas_call(kern, grid_spec=grid_spec, out_shape=out_shape, compiler_params=cp)(
            page_table, *args)
    return pl.pallas_call(kern, grid=grid, in_specs=in_specs, out_specs=out_spec, out_shape=out_shape,
                          scratch_shapes=scratch, compiler_params=cp)(*args)


def _combine_kernel(oc_ref, os_ref, ow_ref, gate_ref, g_ref, o_ref):
    gates = gate_ref[...]
    parts = []
    for h in range(N_HEADS):
        sl = slice(h * HEAD_DIM, (h + 1) * HEAD_DIM)
        parts.append(gates[:, h:h + 1] * oc_ref[:, sl]
                     + gates[:, N_HEADS + h:N_HEADS + h + 1] * os_ref[:, sl]
                     + gates[:, 2 * N_HEADS + h:2 * N_HEADS + h + 1] * ow_ref[:, sl])
    o = jnp.concatenate(parts, axis=1)
    ms = jnp.mean(o * o, axis=-1, keepdims=True)
    o_ref[...] = (o * lax.rsqrt(ms + EPS) * g_ref[...]).astype(o_ref.dtype)


def _combine(o_cmp, o_sel, o_win, gates, gain):
    n, w = o_cmp.shape
    tm = _row_tile(n, 256)
    row = pl.BlockSpec((tm, w), lambda i: (i, 0))
    return pl.pallas_call(
        _combine_kernel,
        grid=(n // tm,),
        in_specs=[row, row, row, pl.BlockSpec((tm, LANES), lambda i: (i, 0)),
                  pl.BlockSpec((1, w), lambda i: (0, 0))],
        out_specs=row,
        out_shape=jax.ShapeDtypeStruct((n, w), MXU_DTYPE),
        compiler_params=_cparams(("parallel",)),
    )(o_cmp, o_sel, o_win, gates, gain.reshape(1, w))


SSM_LANE_GROUPS = LANES // SSM_GROUP
SSM_BLOCK_STATE = SSM_LANE_GROUPS * 2 * SSM_STATE


def _ssm_kernel(u_ref, kt_ref, bs_ref, cs_ref, d_ref, la_ref, lb_ref, h0_ref, y_ref, hout_ref,
                us_scr, s_scr, hs_scr, *, n_seq, n_chunk):
    L = SSM_CHUNK
    rows = n_seq * n_chunk
    st = SSM_BLOCK_STATE
    group_shift = int(math.log2(SSM_GROUP))
    state_shift = int(math.log2(2 * SSM_STATE))
    in_mask = (jnp.right_shift(lax.broadcasted_iota(jnp.int32, (LANES, st), 0), group_shift)
               == jnp.right_shift(lax.broadcasted_iota(jnp.int32, (LANES, st), 1), state_shift))
    out_mask = (jnp.right_shift(lax.broadcasted_iota(jnp.int32, (st, LANES), 0), state_shift)
                == jnp.right_shift(lax.broadcasted_iota(jnp.int32, (st, LANES), 1), group_shift))
    zero = jnp.zeros((), MXU_DTYPE)

    s_acc = jnp.zeros((rows, st), F32)
    for s in range(L):
        us = u_ref[0, pl.ds(s, rows, stride=L), :].astype(MXU_DTYPE)
        us_scr[s] = us
        b_full = jnp.where(in_mask, jnp.concatenate([bs_ref[0, s]] * SSM_LANE_GROUPS, axis=1), zero)
        s_acc = s_acc + _dot(us, b_full)
    s_scr[...] = s_acc

    la = la_ref[0]
    lb = lb_ref[0]
    is_re = (lax.broadcasted_iota(jnp.int32, (1, st), 1) & SSM_STATE) == 0

    def swap(h):
        return jnp.where(is_re, pltpu.roll(h, st - SSM_STATE, 1), pltpu.roll(h, SSM_STATE, 1))

    def body(c, hs):
        new = []
        for b in range(n_seq):
            row = b * n_chunk + c
            hs_scr[pl.ds(row, 1), :] = hs[b]
            new.append(la * hs[b] + lb * swap(hs[b]) + s_scr[pl.ds(row, 1), :])
        return tuple(new)

    hs = lax.fori_loop(0, n_chunk, body, tuple(h0_ref[0, b:b + 1, :] for b in range(n_seq)))
    for b in range(n_seq):
        hout_ref[0, b:b + 1, :] = hs[b]

    h_start = hs_scr[...].astype(MXU_DTYPE)
    for t in range(L):
        c_full = jnp.where(out_mask, jnp.concatenate([cs_ref[0, t]] * SSM_LANE_GROUPS, axis=0), zero)
        acc = _dot(h_start, c_full)
        for s in range(t + 1):
            acc = acc + _dot(us_scr[s], kt_ref[0, t - s])
        acc = acc + d_ref[0] * u_ref[0, pl.ds(t, rows, stride=L), :]
        y_ref[0, pl.ds(t, rows, stride=L), :] = acc


def _ssm(u_lb, kt, bs, cs, dvec, la, lb, h0, *, n_seq, n_chunk):
    nblk, n, w = u_lb.shape
    L = SSM_CHUNK
    rows = n_seq * n_chunk
    assert n == rows * L and w == LANES
    st = SSM_BLOCK_STATE

    def blk(shape):
        return pl.BlockSpec((1,) + shape, lambda i: (i,) + (0,) * len(shape))

    return pl.pallas_call(
        functools.partial(_ssm_kernel, n_seq=n_seq, n_chunk=n_chunk),
        grid=(nblk,),
        in_specs=[blk((n, w)), blk((L, w, w)), blk((L, w, w)), blk((L, w, w)), blk((1, w)), blk((1, st)),
                  blk((1, st)), blk((n_seq, st))],
        out_specs=[blk((n, w)), blk((n_seq, st))],
        out_shape=[jax.ShapeDtypeStruct((nblk, n, w), F32), jax.ShapeDtypeStruct((nblk, n_seq, st), F32)],
        scratch_shapes=[pltpu.VMEM((L, rows, w), MXU_DTYPE), pltpu.VMEM((rows, st), F32),
                        pltpu.VMEM((rows, st), F32)],
        compiler_params=_cparams(("parallel",)),
    )(u_lb, kt, bs, cs, dvec, la, lb, h0)


def _ssm_matrices(lam_re, lam_im, log_dt, b_re, b_im, c_re, c_im, d_skip, l_eff):
    g = lam_re.shape[0]
    L = SSM_CHUNK
    lg = SSM_LANE_GROUPS
    nblk = g // lg
    lam = lax.complex(lam_re.astype(F32), lam_im.astype(F32))
    dt = jnp.exp(log_dt.astype(F32))[:, None]
    lam_bar = jnp.exp(lam * dt)
    b_bar = ((lam_bar - 1.0) / lam)[..., None] * lax.complex(b_re.astype(F32), b_im.astype(F32))
    c_c = lax.complex(c_re.astype(F32), c_im.astype(F32))
    pw = [jnp.ones_like(lam_bar)]
    for _ in range(L):
        pw.append(pw[-1] * lam_bar)
    pw = jnp.stack(pw)
    pw_b = pw.reshape(L + 1, nblk, lg, SSM_STATE)
    bb = b_bar.reshape(nblk, lg, SSM_STATE, SSM_GROUP).transpose(0, 1, 3, 2)
    cc = c_c.reshape(nblk, lg, SSM_GROUP, SSM_STATE).transpose(0, 3, 1, 2)
    kern = jnp.einsum("gop,kgp,gpi->gkio", c_c, pw[:L], b_bar).real
    kern = kern.reshape(nblk, lg, L, SSM_GROUP, SSM_GROUP).transpose(0, 2, 1, 3, 4)
    same_group = jnp.eye(lg, dtype=kern.dtype)[None, None, :, None, :, None]
    kt = (kern[:, :, :, :, None, :] * same_group).reshape(nblk, L, LANES, LANES)
    exps = np.clip(l_eff - 1 - np.arange(L), 0, L)
    bx = pw_b[exps].transpose(1, 0, 2, 3)[:, :, :, None, :] * bb[:, None]
    bx = jnp.where(jnp.asarray(np.arange(L) < l_eff)[None, :, None, None, None], bx, 0.0)
    bs = jnp.stack([bx.real, bx.imag], axis=4).reshape(nblk, L, LANES, 2 * SSM_STATE)
    mt = cc[:, None] * pw_b[1:L + 1].transpose(1, 0, 3, 2)[..., None]
    cs = jnp.stack([mt.real, -mt.imag], axis=2).reshape(nblk, L, 2 * SSM_STATE, LANES)
    lam_l = pw[l_eff].reshape(nblk, lg, SSM_STATE)
    la = jnp.stack([lam_l.real, lam_l.real], axis=2).reshape(nblk, 1, SSM_BLOCK_STATE)
    lb = jnp.stack([-lam_l.imag, lam_l.imag], axis=2).reshape(nblk, 1, SSM_BLOCK_STATE)
    dvec = d_skip.astype(F32).reshape(nblk, 1, LANES)
    return kt.astype(MXU_DTYPE), bs.astype(MXU_DTYPE), cs.astype(MXU_DTYPE), dvec, la, lb


def _state_to_blocks(re, im):
    n_seq, g, p = re.shape
    h = jnp.stack([re, im], axis=2).astype(F32).reshape(n_seq, g // SSM_LANE_GROUPS, SSM_BLOCK_STATE)
    return h.transpose(1, 0, 2)


def _state_from_blocks(h):
    nblk, n_seq, _ = h.shape
    h = h.transpose(1, 0, 2).reshape(n_seq, nblk * SSM_LANE_GROUPS, 2, SSM_STATE)
    return h[:, :, 0], h[:, :, 1]


def _glu_kernel(y_ref, w_ref, b_ref, g_ref, o_ref):
    gl = _gelu(jnp.concatenate([y_ref[k] for k in range(y_ref.shape[0])], axis=1))
    z = _dot(gl.astype(MXU_DTYPE), w_ref[...]) + b_ref[...]
    o = gl * _sigmoid(z)
    ms = jnp.mean(o * o, axis=-1, keepdims=True)
    o_ref[...] = (o * lax.rsqrt(ms + EPS) * g_ref[...]).astype(o_ref.dtype)


def _glu(y_lb, w, b, gain):
    nblk, n, _ = y_lb.shape
    d = nblk * LANES
    tm = _row_tile(n, 256)
    vec = pl.BlockSpec((1, d), lambda i: (0, 0))
    return pl.pallas_call(
        _glu_kernel,
        grid=(n // tm,),
        in_specs=[pl.BlockSpec((nblk, tm, LANES), lambda i: (0, i, 0)), pl.BlockSpec((d, d), lambda i: (0, 0)),
                  vec, vec],
        out_specs=pl.BlockSpec((tm, d), lambda i: (i, 0)),
        out_shape=jax.ShapeDtypeStruct((n, d), MXU_DTYPE),
        compiler_params=_cparams(("parallel",)),
    )(y_lb, w, b.reshape(1, d), gain.reshape(1, d))


def _top_rows(x, k):
    n = x.shape[0]
    idx_iota = lax.broadcasted_iota(jnp.int32, x.shape, 0)
    vals, idxs = [], []
    cur = x
    for _ in range(k):
        m = jnp.max(cur, axis=0, keepdims=True)
        ix = jnp.min(jnp.where(cur == m, idx_iota, n), axis=0, keepdims=True)
        vals.append(m)
        idxs.append(ix)
        cur = jnp.where(idx_iota == ix, -jnp.inf, cur)
    return jnp.concatenate(vals, axis=0), jnp.concatenate(idxs, axis=0)


def _pick_rows(table, sel):
    out = jnp.zeros(sel.shape, table.dtype)
    for a in range(table.shape[0]):
        out = out + jnp.where(sel == a, table[a:a + 1, :], 0)
    return out


def _peer_topk_kernel(q_ref, keys_ref, i1_ref, i2_ref, gate_ref):
    q = q_ref[...].astype(MXU_DTYPE)
    s1 = _dot_nt(keys_ref[0, 0].astype(MXU_DTYPE), q[:, :PEER_HALF])
    s2 = _dot_nt(keys_ref[0, 1].astype(MXU_DTYPE), q[:, PEER_HALF:])
    v1, x1 = _top_rows(s1, PEER_TOPK)
    v2, x2 = _top_rows(s2, PEER_TOPK)
    t = v1.shape[1]
    pairs = [(a, b) for a in range(PEER_TOPK) for b in range(PEER_TOPK) if (a + 1) * (b + 1) <= PEER_TOPK]
    n_pad = -len(pairs) % SUBLANES
    cand = jnp.concatenate([v1[a:a + 1] + v2[b:b + 1] for a, b in pairs]
                           + [jnp.full((n_pad, t), -jnp.inf, F32)], axis=0)
    top, pos = _top_rows(cand, PEER_TOPK)
    i1_ref[0] = _pick_rows(jnp.concatenate([x1[a:a + 1] for a, _ in pairs], axis=0), pos)
    i2_ref[0] = _pick_rows(jnp.concatenate([x2[b:b + 1] for _, b in pairs], axis=0), pos)
    e = jnp.exp(top - jnp.max(top, axis=0, keepdims=True))
    gate_ref[0] = e / jnp.sum(e, axis=0, keepdims=True)


def _peer_topk(q, sub_keys):
    n = q.shape[0]
    tt = _row_tile(n, 512)
    out = pl.BlockSpec((1, PEER_TOPK, tt), lambda i, h: (h, 0, i))
    shp = (PEER_HEADS, PEER_TOPK, n)
    return pl.pallas_call(
        _peer_topk_kernel,
        grid=(n // tt, PEER_HEADS),
        in_specs=[pl.BlockSpec((tt, 2 * PEER_HALF), lambda i, h: (i, h)),
                  pl.BlockSpec((1, 2, PEER_KEYS, PEER_HALF), lambda i, h: (h, 0, 0, 0))],
        out_specs=[out, out, out],
        out_shape=[jax.ShapeDtypeStruct(shp, jnp.int32), jax.ShapeDtypeStruct(shp, jnp.int32),
                   jax.ShapeDtypeStruct(shp, F32)],
        compiler_params=_cparams(("parallel", "parallel")),
    )(q, sub_keys)


PEER_EXPERT_BLOCK = 4 * PEER_KEYS


PEER_GRID_HALF = PEER_KEYS // 2
HIGH_HALF = 0xFFFF0000
PEER_BUILD_UNROLL = 32


def _peer_kernel(x_ref, u_ref, v_ref, i1_ref, i2_ref, gate_ref, h_ref, o_ref, g_scr, w_scr, *, tm, n_blocks):
    j = pl.program_id(1)
    n_entries = PEER_HEADS * PEER_TOPK
    words = PEER_GRID_HALF // 2
    blocks_per_half = n_blocks // 2
    unroll = math.gcd(tm, PEER_BUILD_UNROLL)

    def build(base):
        r = lax.broadcasted_iota(jnp.int32, (PEER_GRID_HALF, n_entries), 0)
        i1_of_row = base + jnp.where(r < words, 2 * r, 2 * (r - words) + 1)
        i2_of_row = lax.broadcasted_iota(jnp.int32, (PEER_KEYS, n_entries), 0)

        def body(nb, carry):
            for k in range(unroll):
                n = nb * unroll + k
                a = jnp.where(i1_of_row == i1_ref[pl.ds(n, 1), :], gate_ref[pl.ds(n, 1), :], 0.0)
                b = jnp.where(i2_of_row == i2_ref[pl.ds(n, 1), :], 1.0, 0.0)
                grid = _dot_nt(a.astype(MXU_DTYPE), b.astype(MXU_DTYPE))
                bits = lax.bitcast_convert_type(grid.astype(jnp.bfloat16).astype(F32), jnp.uint32)
                g_scr[pl.ds(pl.multiple_of(n * words, words), words), :] = (
                    jnp.right_shift(bits[:words], jnp.uint32(16)) | (bits[words:] & jnp.uint32(HIGH_HALF)))
            return carry

        lax.fori_loop(0, tm // unroll, body, 0)

    def weights():
        act = _dot_nt(x_ref[...], u_ref[...])
        pairs = PEER_EXPERT_BLOCK // (2 * PEER_KEYS)
        m0 = (j - jnp.where(j >= blocks_per_half, blocks_per_half, 0)) * pairs
        pieces = []
        for q in range(pairs):
            word = g_scr[pl.ds(m0 + q, tm, stride=words), :]
            pieces.append(lax.bitcast_convert_type(jnp.left_shift(word, jnp.uint32(16)), F32))
            pieces.append(lax.bitcast_convert_type(word & jnp.uint32(HIGH_HALF), F32))
        g = jnp.concatenate(pieces, axis=1)
        w_scr[j % 2] = (g * _gelu(act)).astype(MXU_DTYPE)

    def values():
        o_ref[...] += _dot(w_scr[(j + 1) % 2], v_ref[...])

    @pl.when(j == 0)
    def _():
        o_ref[...] = h_ref[...]
        build(0)
        weights()

    @pl.when(j == blocks_per_half)
    def _():
        build(PEER_GRID_HALF)

    @pl.when((j > 0) & (j < n_blocks))
    def _():
        values()
        weights()

    @pl.when(j == n_blocks)
    def _():
        values()


def _peer(xn, u, v, i1, i2, gate, h):
    n, d = xn.shape
    n_blocks = u.shape[0] // PEER_EXPERT_BLOCK
    assert n_blocks * PEER_EXPERT_BLOCK == PEER_KEYS * PEER_KEYS and n_blocks % 2 == 0
    tm = _row_tile(n, 512)
    n_entries = PEER_HEADS * PEER_TOPK
    once = pl.Buffered(1)
    row_in = pl.BlockSpec((tm, d), lambda i, j: (i, 0), pipeline_mode=once)
    ent = pl.BlockSpec((tm, n_entries), lambda i, j: (i, 0))
    return pl.pallas_call(
        functools.partial(_peer_kernel, tm=tm, n_blocks=n_blocks),
        grid=(n // tm, n_blocks + 1),
        in_specs=[row_in,
                  pl.BlockSpec((PEER_EXPERT_BLOCK, d), lambda i, j: (jnp.minimum(j, n_blocks - 1), 0)),
                  pl.BlockSpec((PEER_EXPERT_BLOCK, d), lambda i, j: (jnp.maximum(j - 1, 0), 0)),
                  ent, ent, ent, row_in],
        out_specs=pl.BlockSpec((tm, d), lambda i, j: (i, 0), pipeline_mode=once),
        out_shape=jax.ShapeDtypeStruct((n, d), F32),
        scratch_shapes=[pltpu.VMEM((tm * PEER_GRID_HALF // 2, PEER_KEYS), jnp.uint32),
                        pltpu.VMEM((2, tm, PEER_EXPERT_BLOCK), MXU_DTYPE)],
        compiler_params=_cparams(("parallel", "arbitrary")),
    )(xn, u, v, i1, i2, gate, h)


def _prepare_weights(rel_table, norm_mix, w_in, q_norm, k_norm, w_phi, b_phi, w_glu, w_out, peer_w_query,
                     peer_u, peer_v):
    d_model = w_in.shape[0]
    off_gate = ATTN_WIDTH + 6 * KV_WIDTH
    off_ssm = off_gate + 3 * N_HEADS
    w_main = w_in[:, :off_gate].astype(MXU_DTYPE)
    w_ssm = w_in[:, off_ssm:].astype(MXU_DTYPE)
    w_gate = jnp.pad(w_in[:, off_gate:off_ssm], ((0, 0), (0, LANES - 3 * N_HEADS))).astype(MXU_DTYPE)
    n_main = w_main.shape[1]
    ones = jnp.ones((KV_WIDTH,), F32)
    zeros = jnp.zeros((KV_WIDTH,), F32)
    gain = jnp.concatenate([
        jnp.tile(q_norm.astype(F32), N_HEADS), ones, ones,
        jnp.tile(k_norm[1].astype(F32), N_KV_HEADS), ones,
        jnp.tile(k_norm[2].astype(F32), N_KV_HEADS), ones]).reshape(1, n_main)
    mode = jnp.concatenate([
        jnp.ones((ATTN_WIDTH,), F32), zeros, zeros, ones, zeros, ones, zeros]).reshape(1, n_main)
    w_cmp = w_phi.reshape(2, CMP_R, CMP_STRIDE, HEAD_DIM, HEAD_DIM).transpose(0, 2, 3, 1, 4)
    w_cmp = w_cmp.reshape(2, CMP_STRIDE * HEAD_DIM, CMP_R * HEAD_DIM).astype(MXU_DTYPE)
    return dict(
        d_model=d_model, w_main=w_main, w_ssm=w_ssm, w_gate=w_gate, gain=gain, mode=mode, w_cmp=w_cmp,
        tab=rel_table.astype(F32).T, w_glu=w_glu.astype(MXU_DTYPE), w_out=w_out.astype(MXU_DTYPE),
        w_query=peer_w_query.astype(MXU_DTYPE), peer_u=peer_u.astype(MXU_DTYPE),
        peer_v=peer_v.astype(MXU_DTYPE))


def _overlap_matrix(n_chunks, n_cmp, n_slc, n_pad):
    c0 = np.arange(n_chunks)[:, None] * CMP_STRIDE
    s0 = np.arange(n_pad)[None, :] * SLC_BLOCK
    overlap = np.clip(np.minimum(c0 + CMP_BLOCK, s0 + SLC_BLOCK) - np.maximum(c0, s0), 0, None) / CMP_BLOCK
    overlap = overlap * (np.arange(n_chunks)[:, None] < n_cmp) * (np.arange(n_pad)[None, :] < n_slc)
    return jnp.asarray(overlap, dtype=MXU_DTYPE)


def _expansion_matrix(n_pad, n_keys, n_cols):
    key = np.arange(n_cols)[None, :]
    e = ((key // SLC_BLOCK) == np.arange(n_pad)[:, None]) & (key < n_keys)
    return jnp.asarray(e, dtype=MXU_DTYPE)


def _pad_lanes(n):
    return -(-n // LANES) * LANES


def _layer(x3, t_real, pos0, cache_cmp3, cache_sel3, cache_win3, page_table, h0, wts, rel_table, norm_mix,
           k_norm, b_phi, ssm, b_glu, out_norm_attn, out_norm_ssm, norm_ffn, sub_keys):
    nb, t, d_model = x3.shape
    n = nb * t
    sample = page_table is not None
    x2 = x3.reshape(n, d_model)
    xn = _rmsnorm(x2, norm_mix, MXU_DTYPE)
    p, kv_cmp, kv_sel, kv_win = _in_proj(xn, wts["w_main"], wts["mode"], wts["gain"])
    gates = _matmul(xn, wts["w_gate"], "sigmoid")
    n_main = p.shape[1]
    p3 = p.reshape(nb, t, n_main)
    col = lambda off: off // HEAD_DIM
    off_cmp, off_sel, off_win = ATTN_WIDTH, ATTN_WIDTH + 2 * KV_WIDTH, ATTN_WIDTH + 4 * KV_WIDTH

    if sample:
        assert t_real < CMP_STRIDE and pos0 % PAGE_SIZE == 0
        n_pages = page_table.shape[1]
        pk = _compress(cache_cmp3, page_table, wts["w_cmp"])
        total = pos0 + t_real
    else:
        n_pages = t // PAGE_SIZE
        pt = jnp.arange(nb * n_pages, dtype=jnp.int32).reshape(nb, n_pages)
        pk = _compress(kv_cmp.reshape(nb * n_pages, PAGE_SIZE * 2 * N_KV_HEADS, HEAD_DIM), pt, wts["w_cmp"])
        total = t
    n_chunks = pk.shape[2]
    n_cmp = total // CMP_STRIDE - CMP_R + 1
    n_slc = -(-total // SLC_BLOCK)
    n_pad = _pad_lanes(n_slc)
    m_mat = _overlap_matrix(n_chunks, n_cmp, n_slc, n_pad)
    tq_cmp = _row_tile(t, 256)
    tab = wts["tab"]
    o_cmp, sel = _cmp_attention(tab, p3, pk, b_phi.astype(F32), k_norm[0].astype(F32).reshape(1, HEAD_DIM),
                                m_mat, pos0=pos0, n_cmp=n_cmp, n_slc=n_slc, tq=tq_cmp)

    if sample:
        kv_block = 2 * KV_WIDTH
        n_cols = (n_pages + 8) * KEY_TILE
        e_mat = _expansion_matrix(n_pad, (n_pages + 1) * KEY_TILE, n_cols)
        o_sel = _attn_sample(p3, cache_sel3, page_table, tab, "sel", off_sel // kv_block, pos0, 0, sel, e_mat)
        o_win = _attn_sample(p3, cache_win3, None, tab, "win", off_win // kv_block, pos0, pos0 - WINDOW)
    else:
        e3 = _expansion_matrix(n_pad, t, t).T.reshape(t // KEY_TILE, KEY_TILE, n_pad)
        o_sel = _attn_prompt(p3, tab, "sel", col(off_sel), col(off_sel + KV_WIDTH), sel, e3)
        o_win = _attn_prompt(p3, tab, "win", col(off_win), col(off_win + KV_WIDTH))
    mixed_attn = _combine(o_cmp.reshape(n, ATTN_WIDTH), o_sel.reshape(n, ATTN_WIDTH),
                          o_win.reshape(n, ATTN_WIDTH), gates, out_norm_attn)

    u_lb = _matmul(xn, wts["w_ssm"], "laneblocks")
    nblk = u_lb.shape[0]
    if t % SSM_CHUNK:
        assert t < SSM_CHUNK and t_real <= t
        u_lb = jnp.pad(u_lb.reshape(nblk, nb, t, LANES), ((0, 0), (0, 0), (0, SSM_CHUNK - t), (0, 0)))
        u_lb = u_lb.reshape(nblk, nb * SSM_CHUNK, LANES)
        l_eff, n_chunk = t_real, 1
    else:
        l_eff, n_chunk = SSM_CHUNK, t // SSM_CHUNK
    mats = _ssm_matrices(*ssm, l_eff)
    y_lb, h_last = _ssm(u_lb, *mats, _state_to_blocks(*h0), n_seq=nb, n_chunk=n_chunk)
    if t % SSM_CHUNK:
        y_lb = y_lb.reshape(nblk, nb, SSM_CHUNK, LANES)[:, :, :t].reshape(nblk, n, LANES)
    mixed_ssm = _glu(y_lb, wts["w_glu"], b_glu, out_norm_ssm)

    h = _out_proj(mixed_attn, mixed_ssm, wts["w_out"], x2)
    hn = _rmsnorm(h, norm_ffn, MXU_DTYPE)
    pq = _matmul(hn, wts["w_query"])
    i1, i2, gate = _peer_topk(pq, sub_keys)
    to_rows = lambda a: a.transpose(2, 0, 1).reshape(n, PEER_HEADS * PEER_TOPK)
    y_out = _peer(hn, wts["peer_u"], wts["peer_v"], to_rows(i1), to_rows(i2), to_rows(gate), h)
    new_kv = [a.reshape(nb, t, 2, N_KV_HEADS, HEAD_DIM) for a in (kv_cmp, kv_sel, kv_win)]
    return y_out.reshape(nb, t, d_model), new_kv, _state_from_blocks(h_last)


def kernel(x_prompt, x_sample, cache_kv_cmp, cache_kv_sel, cache_kv_win, state_ssm_re, state_ssm_im, page_table,
           rel_table, norm_mix, w_in, q_norm, k_norm, w_phi, b_phi, ssm_lam_re, ssm_lam_im, ssm_log_dt,
           ssm_b_re, ssm_b_im, ssm_c_re, ssm_c_im, ssm_d, w_glu, b_glu, out_norm_attn, out_norm_ssm, w_out,
           norm_ffn, peer_w_query, peer_sub_keys, peer_u, peer_v):
    depth = w_in.shape[0]
    bp, t_p, d_model = x_prompt.shape
    db, t_s, _ = x_sample.shape
    past_len = page_table.shape[1] * PAGE_SIZE
    kv_row = 2 * KV_WIDTH
    n_groups = ssm_lam_re.shape[1]
    assert cache_kv_win.shape[2] == WINDOW and t_p % KEY_TILE == 0 and t_s <= SAMPLE_ROWS

    yp, ys = x_prompt, jnp.pad(x_sample, ((0, 0), (0, SAMPLE_ROWS - t_s), (0, 0)))
    new_p = [[] for _ in range(5)]
    new_s = [[] for _ in range(5)]
    for l in range(depth):
        wts = _prepare_weights(rel_table, norm_mix[l], w_in[l], q_norm[l], k_norm[l], w_phi[l], b_phi[l],
                               w_glu[l], w_out[l], peer_w_query[l], peer_u[l], peer_v[l])
        ssm = (ssm_lam_re[l], ssm_lam_im[l], ssm_log_dt[l], ssm_b_re[l], ssm_b_im[l], ssm_c_re[l],
               ssm_c_im[l], ssm_d[l])
        shared = (wts, rel_table, norm_mix[l], k_norm[l], b_phi[l], ssm, b_glu[l], out_norm_attn[l],
                  out_norm_ssm[l], norm_ffn[l], peer_sub_keys[l])

        zero_state = jnp.zeros((bp, n_groups, SSM_STATE), F32)
        yp, new_kv, h_last = _layer(yp, t_p, 0, None, None, None, None, (zero_state, zero_state), *shared)
        keep = min(WINDOW, t_p)
        new_p[0].append(new_kv[0])
        new_p[1].append(new_kv[1])
        new_p[2].append(new_kv[2][:, t_p - keep:])
        new_p[3].append(h_last[0])
        new_p[4].append(h_last[1])

        n_phys = cache_kv_cmp.shape[1]
        h0 = (state_ssm_re[l], state_ssm_im[l])
        page_rows = PAGE_SIZE * 2 * N_KV_HEADS
        ys, new_kv, h_last = _layer(ys, t_s, past_len, cache_kv_cmp[l].reshape(n_phys, page_rows, HEAD_DIM),
                                cache_kv_sel[l].reshape(n_phys, page_rows, HEAD_DIM),
                                cache_kv_win[l].reshape(db * WIN_TILES, page_rows, HEAD_DIM), page_table, h0,
                                *shared)
        new_s[0].append(new_kv[0][:, :t_s])
        new_s[1].append(new_kv[1][:, :t_s])
        win_all = jnp.concatenate([cache_kv_win[l], new_kv[2][:, :t_s]], axis=1)
        keep = min(WINDOW, past_len + t_s)
        new_s[2].append(win_all[:, win_all.shape[1] - keep:])
        new_s[3].append(h_last[0])
        new_s[4].append(h_last[1])
    outs_p = [jnp.stack(v) for v in new_p]
    outs_s = [jnp.stack(v) for v in new_s]
    return (yp, ys[:, :t_s], *outs_p, *outs_s)
```

```python
import functools
import math

import numpy as np
import jax
import jax.numpy as jnp
from jax import lax
from jax.experimental import pallas as pl
from jax.experimental.pallas import tpu as pltpu

F32 = jnp.float32
MXU_DTYPE = jnp.bfloat16

HEAD_DIM = 128
N_KV_HEADS = 4
GQA_REP = 4
N_HEADS = N_KV_HEADS * GQA_REP
ATTN_WIDTH = N_HEADS * HEAD_DIM
KV_WIDTH = N_KV_HEADS * HEAD_DIM
ATTN_SCALE = HEAD_DIM ** -0.5
CMP_BLOCK = 32
CMP_STRIDE = 16
CMP_R = CMP_BLOCK // CMP_STRIDE
SLC_BLOCK = 64
N_SELECT = 16
FORCE_BONUS = 1.0e4
WINDOW = 512
NUM_BUCKETS = 32
MAX_DISTANCE = 128
SSM_GROUP = 16
SSM_STATE = 64
PEER_HEADS = 8
PEER_KEYS = 128
PEER_TOPK = 16
PEER_HALF = 128
EPS = 1e-6
NEG_INF = -1e30
PAGE_SIZE = 128

LANES = 128
SUBLANES = 8
MXU_DIM = 256
VMEM_LIMIT_BYTES = 52 * 1024 * 1024

KEY_TILE = 128
WIN_TILES = WINDOW // KEY_TILE
SSM_CHUNK = 16
SAMPLE_ROWS = 8


def _cparams(sem):
    return pltpu.CompilerParams(dimension_semantics=sem, vmem_limit_bytes=VMEM_LIMIT_BYTES)


def _row_tile(n, pref):
    t = min(n, pref)
    while n % t:
        t -= SUBLANES
    return t


def _dot(a, b):
    return jnp.dot(a, b, preferred_element_type=F32)


def _dot_nt(a, b):
    return lax.dot_general(a, b, (((1,), (1,)), ((), ())), preferred_element_type=F32)


def _gelu(x):
    c = math.sqrt(2.0 / math.pi)
    return 0.5 * x * (1.0 + jnp.tanh(c * (x + 0.044715 * (x * x * x))))


def _sigmoid(x):
    return 1.0 / (1.0 + jnp.exp(-x))


def _bucket_lower_bounds():
    n = np.arange(2 * MAX_DISTANCE)
    max_exact = NUM_BUCKETS // 2
    nf = np.maximum(n, 1).astype(np.float32)
    large = max_exact + (np.log(nf / np.float32(max_exact)) / np.float32(math.log(MAX_DISTANCE / max_exact))
                         * np.float32(NUM_BUCKETS - max_exact)).astype(np.int32)
    bucket = np.where(n < max_exact, n, np.minimum(large, NUM_BUCKETS - 1))
    assert np.all(np.diff(bucket) >= 0) and bucket[-1] == NUM_BUCKETS - 1
    return [int(np.argmax(bucket >= b)) for b in range(NUM_BUCKETS)]


BUCKET_LO = _bucket_lower_bounds()


def _bias_from_dist(dist, table_entry):
    out = table_entry(0)
    for b in range(1, NUM_BUCKETS):
        out = jnp.where(dist >= BUCKET_LO[b], table_entry(b), out)
    return out


def _rmsnorm_kernel(x_ref, g_ref, o_ref):
    x = x_ref[...]
    ms = jnp.mean(x * x, axis=-1, keepdims=True)
    o_ref[...] = (x * lax.rsqrt(ms + EPS) * g_ref[...]).astype(o_ref.dtype)


def _rmsnorm(x, g, out_dtype):
    n, d = x.shape
    tm = _row_tile(n, 256)
    return pl.pallas_call(
        _rmsnorm_kernel,
        grid=(n // tm,),
        in_specs=[pl.BlockSpec((tm, d), lambda i: (i, 0)), pl.BlockSpec((1, d), lambda i: (0, 0))],
        out_specs=pl.BlockSpec((tm, d), lambda i: (i, 0)),
        out_shape=jax.ShapeDtypeStruct((n, d), out_dtype),
        compiler_params=_cparams(("parallel",)),
    )(x, g.reshape(1, d))


def _matmul_kernel(*refs, epilogue):
    a_ref, b_ref = refs[0], refs[1]
    o_ref = refs[-1]
    acc = _dot(a_ref[...], b_ref[...])
    if epilogue == "none":
        o_ref[...] = acc
    elif epilogue == "sigmoid":
        o_ref[...] = _sigmoid(acc)
    elif epilogue == "laneblocks":
        for c in range(acc.shape[1] // LANES):
            o_ref[c] = acc[:, c * LANES:(c + 1) * LANES]
    else:
        raise ValueError(epilogue)


def _matmul(a, b, epilogue="none", extras=(), tm_pref=512, tn_pref=1024):
    m, k = a.shape
    _, n = b.shape
    tm = _row_tile(m, tm_pref)
    tn = min(n, tn_pref)
    while n % tn:
        tn -= LANES
    in_specs = [pl.BlockSpec((tm, k), lambda i, j: (i, 0)), pl.BlockSpec((k, tn), lambda i, j: (0, j))]
    for e in extras:
        if e.shape[0] == 1:
            in_specs.append(pl.BlockSpec((1, tn), lambda i, j: (0, j)))
        else:
            in_specs.append(pl.BlockSpec((tm, tn), lambda i, j: (i, j)))
    if epilogue == "laneblocks":
        out_spec = pl.BlockSpec((tn // LANES, tm, LANES), lambda i, j: (j, i, 0))
        out_shape = jax.ShapeDtypeStruct((n // LANES, m, LANES), F32)
    else:
        out_spec = pl.BlockSpec((tm, tn), lambda i, j: (i, j))
        out_shape = jax.ShapeDtypeStruct((m, n), F32)
    return pl.pallas_call(
        functools.partial(_matmul_kernel, epilogue=epilogue),
        grid=(m // tm, n // tn),
        in_specs=in_specs,
        out_specs=out_spec,
        out_shape=out_shape,
        compiler_params=_cparams(("parallel", "parallel")),
    )(a, b, *extras)


def _in_proj_kernel(a_ref, b_ref, mode_ref, gain_ref, o_ref, *kv_refs, first_kv_tile):
    j = pl.program_id(1)
    acc = _dot(a_ref[...], b_ref[...])
    tm = acc.shape[0]
    blocks = []
    for c in range(acc.shape[1] // HEAD_DIM):
        sl = slice(c * HEAD_DIM, (c + 1) * HEAD_DIM)
        blk = acc[:, sl]
        ms = jnp.mean(blk * blk, axis=-1, keepdims=True)
        nrm = blk * lax.rsqrt(ms + EPS) * gain_ref[:, sl]
        blocks.append(jnp.where(mode_ref[:, sl] > 0.0, nrm, blk))
        o_ref[:, sl] = blocks[-1]
    for b, kv_ref in enumerate(kv_refs):
        @pl.when(j == first_kv_tile + b)
        def _(kv_ref=kv_ref):
            for c, blk in enumerate(blocks):
                kv_ref[pl.ds(c, tm, stride=len(blocks)), :] = blk


def _in_proj(a, b, mode, gain, tm_pref=512):
    m, k = a.shape
    n = b.shape[1]
    tn = 2 * KV_WIDTH
    assert ATTN_WIDTH % tn == 0 and n == ATTN_WIDTH + 3 * tn
    tm = _row_tile(m, tm_pref)
    heads = tn // HEAD_DIM
    vec = pl.BlockSpec((1, tn), lambda i, j: (0, j))
    kv_spec = pl.BlockSpec((tm * heads, HEAD_DIM), lambda i, j: (i, 0))
    kv_shape = jax.ShapeDtypeStruct((m * heads, HEAD_DIM), F32)
    return pl.pallas_call(
        functools.partial(_in_proj_kernel, first_kv_tile=ATTN_WIDTH // tn),
        grid=(m // tm, n // tn),
        in_specs=[pl.BlockSpec((tm, k), lambda i, j: (i, 0)), pl.BlockSpec((k, tn), lambda i, j: (0, j)), vec, vec],
        out_specs=[pl.BlockSpec((tm, tn), lambda i, j: (i, j)), kv_spec, kv_spec, kv_spec],
        out_shape=[jax.ShapeDtypeStruct((m, n), F32), kv_shape, kv_shape, kv_shape],
        compiler_params=_cparams(("parallel", "arbitrary")),
    )(a, b, mode, gain)


def _out_proj_kernel(a1_ref, a2_ref, b1_ref, b2_ref, res_ref, o_ref):
    o_ref[...] = res_ref[...] + _dot(a1_ref[...], b1_ref[...]) + _dot(a2_ref[...], b2_ref[...])


def _out_proj(a1, a2, w, res, tm_pref=512, tn_pref=1024):
    m, k1 = a1.shape
    k2 = a2.shape[1]
    assert k1 == k2 and w.shape[0] == k1 + k2
    n = w.shape[1]
    tm = _row_tile(m, tm_pref)
    tn = min(n, tn_pref)
    return pl.pallas_call(
        _out_proj_kernel,
        grid=(m // tm, n // tn),
        in_specs=[pl.BlockSpec((tm, k1), lambda i, j: (i, 0)), pl.BlockSpec((tm, k2), lambda i, j: (i, 0)),
                  pl.BlockSpec((k1, tn), lambda i, j: (0, j)), pl.BlockSpec((k2, tn), lambda i, j: (1, j)),
                  pl.BlockSpec((tm, tn), lambda i, j: (i, j))],
        out_specs=pl.BlockSpec((tm, tn), lambda i, j: (i, j)),
        out_shape=jax.ShapeDtypeStruct((m, n), F32),
        compiler_params=_cparams(("parallel", "parallel")),
    )(a1, a2, w, w, res)


CMP_PAGES_PER_STEP = 16


def _compress_kernel(pt_ref, *refs):
    del pt_ref
    pages = refs[:CMP_PAGES_PER_STEP]
    w_ref = refs[CMP_PAGES_PER_STEP]
    o_ref = refs[CMP_PAGES_PER_STEP + 1]
    x_scr = refs[CMP_PAGES_PER_STEP + 2]
    chunks_per_page = PAGE_SIZE // CMP_STRIDE
    rows_per_head = CMP_PAGES_PER_STEP * chunks_per_page
    heads_per_row = 2 * N_KV_HEADS
    for c in range(2):
        for g in range(N_KV_HEADS):
            for i in range(CMP_PAGES_PER_STEP):
                r0 = g * rows_per_head + i * chunks_per_page
                for s in range(CMP_STRIDE):
                    x_scr[r0:r0 + chunks_per_page, s * HEAD_DIM:(s + 1) * HEAD_DIM] = pages[i][
                        0, pl.ds(s * heads_per_row + c * N_KV_HEADS + g, chunks_per_page,
                                 stride=CMP_STRIDE * heads_per_row), :]
        res = _dot(x_scr[...].astype(MXU_DTYPE), w_ref[c])
        for g in range(N_KV_HEADS):
            o_ref[0, c * N_KV_HEADS + g] = res[g * rows_per_head:(g + 1) * rows_per_head]


def _compress(rows3, page_table, w_cmp):
    nb, n_pages = page_table.shape
    assert n_pages % CMP_PAGES_PER_STEP == 0
    n_steps = n_pages // CMP_PAGES_PER_STEP
    chunks_per_step = CMP_PAGES_PER_STEP * PAGE_SIZE // CMP_STRIDE
    n_chunks = n_pages * PAGE_SIZE // CMP_STRIDE

    def page_spec(i):
        return pl.BlockSpec((1, PAGE_SIZE * 2 * N_KV_HEADS, HEAD_DIM),
                            lambda b, j, pt: (pt[b, j * CMP_PAGES_PER_STEP + i], 0, 0))

    grid_spec = pltpu.PrefetchScalarGridSpec(
        num_scalar_prefetch=1,
        grid=(nb, n_steps),
        in_specs=[page_spec(i) for i in range(CMP_PAGES_PER_STEP)]
        + [pl.BlockSpec((2, CMP_STRIDE * HEAD_DIM, CMP_R * HEAD_DIM), lambda b, j, pt: (0, 0, 0))],
        out_specs=pl.BlockSpec((1, 2 * N_KV_HEADS, chunks_per_step, CMP_R * HEAD_DIM),
                               lambda b, j, pt: (b, 0, j, 0)),
        scratch_shapes=[pltpu.VMEM((N_KV_HEADS * chunks_per_step, CMP_STRIDE * HEAD_DIM), F32)],
    )
    return pl.pallas_call(
        _compress_kernel,
        grid_spec=grid_spec,
        out_shape=jax.ShapeDtypeStruct((nb, 2 * N_KV_HEADS, n_chunks, CMP_R * HEAD_DIM), F32),
        compiler_params=_cparams(("parallel", "arbitrary")),
    )(page_table, *([rows3] * CMP_PAGES_PER_STEP), w_cmp)


def _cmp_attn_kernel(tab_ref, q_ref, kp_ref, vp_ref, bphi_ref, kn_ref, m_ref, o_ref, sel_ref, bias_scr,
                     *, tq, pos0, n_cmp, n_slc):
    g = pl.program_id(0)
    i = pl.program_id(1)
    n_chunks = kp_ref.shape[2]
    kp = kp_ref[0, 0]
    vp = vp_ref[0, 0]
    k_c = bphi_ref[0:1, :] + kp[:, :HEAD_DIM] + pltpu.roll(kp[:, HEAD_DIM:], n_chunks - 1, 0)
    v_c = bphi_ref[1:2, :] + vp[:, :HEAD_DIM] + pltpu.roll(vp[:, HEAD_DIM:], n_chunks - 1, 0)
    ms = jnp.mean(k_c * k_c, axis=-1, keepdims=True)
    k_c = k_c * lax.rsqrt(ms + EPS) * kn_ref[...]

    q = jnp.concatenate([q_ref[0, :, r * HEAD_DIM:(r + 1) * HEAD_DIM] for r in range(GQA_REP)], axis=0)
    logits = _dot_nt(q.astype(MXU_DTYPE), k_c.astype(MXU_DTYPE)) * ATTN_SCALE
    row = lax.broadcasted_iota(jnp.int32, (tq, n_chunks), 0)
    col = lax.broadcasted_iota(jnp.int32, (tq, n_chunks), 1)
    dist = pos0 + i * tq + row - (col * CMP_STRIDE + (CMP_BLOCK - 1))

    @pl.when(pl.program_id(2) == 0)
    def _():
        bias_scr[...] = jnp.concatenate(
            [_bias_from_dist(dist, lambda b, r=r: tab_ref[g * GQA_REP + r, b]) for r in range(GQA_REP)], axis=0)

    bias = bias_scr[...]
    valid = (dist >= 0) & (col < n_cmp)
    valid = jnp.concatenate([valid] * GQA_REP, axis=0)
    logits = jnp.where(valid, logits + bias, NEG_INF)
    mx = jnp.max(logits, axis=-1, keepdims=True)
    e = jnp.exp(logits - mx)
    any_valid = (jnp.max(jnp.where(valid, 1.0, 0.0), axis=-1, keepdims=True))
    p = e / jnp.sum(e, axis=-1, keepdims=True) * any_valid
    out = _dot(p.astype(MXU_DTYPE), v_c.astype(MXU_DTYPE))
    for r in range(GQA_REP):
        o_ref[0, :, r * HEAD_DIM:(r + 1) * HEAD_DIM] = out[r * tq:(r + 1) * tq]

    psum = p[0:tq]
    for r in range(1, GQA_REP):
        psum = psum + p[r * tq:(r + 1) * tq]
    hi = psum.astype(MXU_DTYPE)
    lo = (psum - hi.astype(F32)).astype(MXU_DTYPE)
    imp = _dot(hi, m_ref[...]) + _dot(lo, m_ref[...])
    n_pad = imp.shape[1]
    j = lax.broadcasted_iota(jnp.int32, (tq, n_pad), 1)
    qp = pos0 + i * tq + lax.broadcasted_iota(jnp.int32, (tq, n_pad), 0)
    cur = jnp.right_shift(qp, int(math.log2(SLC_BLOCK)))
    ok = j * SLC_BLOCK <= qp
    forced = (j == 0) | (j == cur) | (j == cur - 1)
    score = jnp.where(ok, imp + jnp.where(forced, FORCE_BONUS, 0.0), NEG_INF)
    rank = jnp.zeros((tq, n_pad), F32)
    for s in range(n_slc):
        cs = score[:, s:s + 1]
        beats = (cs > score) | ((cs == score) & (j > s))
        rank = rank + jnp.where(beats, 1.0, 0.0)
    selected = (rank < float(min(N_SELECT, n_slc))) & (score > 0.5 * NEG_INF)
    sel_ref[0, 0] = jnp.where(selected, 1.0, 0.0)


def _cmp_attention(tab, p3, pk, bphi, k_norm0, m_mat, *, pos0, n_cmp, n_slc, tq):
    nb, t, _ = p3.shape
    n_chunks = pk.shape[2]
    n_pad = m_mat.shape[1]
    nq = t // tq
    kern = functools.partial(_cmp_attn_kernel, tq=tq, pos0=pos0, n_cmp=n_cmp, n_slc=n_slc)
    return pl.pallas_call(
        kern,
        grid=(N_KV_HEADS, nq, nb),
        in_specs=[
            pl.BlockSpec(memory_space=pltpu.SMEM),
            pl.BlockSpec((1, tq, GQA_REP * HEAD_DIM), lambda g, i, b: (b, i, g)),
            pl.BlockSpec((1, 1, n_chunks, CMP_R * HEAD_DIM), lambda g, i, b: (b, g, 0, 0)),
            pl.BlockSpec((1, 1, n_chunks, CMP_R * HEAD_DIM), lambda g, i, b: (b, N_KV_HEADS + g, 0, 0)),
            pl.BlockSpec((2, HEAD_DIM), lambda g, i, b: (0, 0)),
            pl.BlockSpec((1, HEAD_DIM), lambda g, i, b: (0, 0)),
            pl.BlockSpec((n_chunks, n_pad), lambda g, i, b: (0, 0)),
        ],
        out_specs=[
            pl.BlockSpec((1, tq, GQA_REP * HEAD_DIM), lambda g, i, b: (b, i, g)),
            pl.BlockSpec((1, 1, tq, n_pad), lambda g, i, b: (b, g, i, 0)),
        ],
        out_shape=[
            jax.ShapeDtypeStruct((nb, t, ATTN_WIDTH), F32),
            jax.ShapeDtypeStruct((nb, N_KV_HEADS, t, n_pad), F32),
        ],
        scratch_shapes=[pltpu.VMEM((GQA_REP * tq, n_chunks), F32)],
        compiler_params=_cparams(("parallel", "parallel", "arbitrary")),
    )(tab, p3, pk, pk, bphi, k_norm0, m_mat)


def _online_softmax_step(s, v, m_scr, l_scr, acc_scr):
    m_old = m_scr[...]
    m_new = jnp.maximum(m_old, jnp.max(s, axis=-1, keepdims=True))
    alpha = jnp.exp(m_old - m_new)
    p = jnp.exp(s - m_new)
    l_scr[...] = alpha * l_scr[...] + jnp.sum(p, axis=-1, keepdims=True)
    acc_scr[...] = alpha * acc_scr[...] + _dot(p.astype(MXU_DTYPE), v)
    m_scr[...] = m_new


ATTN_TILES_PER_ITER = 4


def _attn_prompt_kernel(*refs, mode, n_kt):
    if mode == "sel":
        q_ref, k_ref, v_ref, mb_ref, sel_ref, e_ref, o_ref, kb_scr, vt_scr, m_scr, l_scr, acc_scr = refs
    else:
        q_ref, k_ref, v_ref, mb_ref, o_ref, kb_scr, vt_scr, m_scr, l_scr, acc_scr = refs
    tq = KEY_TILE
    qt = pl.program_id(2)

    @pl.when(qt == 0)
    def _():
        for kt in range(n_kt):
            rows = slice(kt * KEY_TILE, (kt + 1) * KEY_TILE)
            kb_scr[kt] = k_ref[0, rows, :].astype(MXU_DTYPE)
            vt_scr[kt] = v_ref[0, rows, :].T.astype(MXU_DTYPE)

    q = jnp.concatenate([q_ref[0, :, r * HEAD_DIM:(r + 1) * HEAD_DIM] for r in range(GQA_REP)], axis=0)
    q = (q * ATTN_SCALE).astype(MXU_DTYPE)
    m_scr[...] = jnp.full(m_scr.shape, NEG_INF, F32)
    l_scr[...] = jnp.zeros(l_scr.shape, F32)
    acc_scr[...] = jnp.zeros(acc_scr.shape, F32)
    if mode == "sel":
        sel = sel_ref[0, 0].astype(MXU_DTYPE)

    n_kinds = mb_ref.shape[0]

    def logits(kt):
        d = qt - kt
        if mode == "sel":
            kind = jnp.where(d < 0, n_kinds - 1, jnp.minimum(d, 2))
        else:
            kind = jnp.where(kt < 0, n_kinds - 1, jnp.where(d == WIN_TILES, 3, jnp.minimum(d, 2)))
        kt = jnp.clip(kt, 0, n_kt - 1)
        s = _dot_nt(kb_scr[kt], q) + mb_ref[kind, 0]
        if mode == "sel":
            chosen = _dot_nt(e_ref[kt], sel)
            s = s + jnp.concatenate([(1.0 - chosen) * NEG_INF] * GQA_REP, axis=1)
        return kt, s

    def update(key_tiles):
        tiles = [logits(kt) for kt in key_tiles]
        m_old = m_scr[...]
        m_new = m_old
        for _, s in tiles:
            m_new = jnp.maximum(m_new, jnp.max(s, axis=0, keepdims=True))
        alpha = jnp.exp(m_old - m_new)
        l_new = alpha * l_scr[...]
        acc = alpha * acc_scr[...]
        for kt, s in tiles:
            p = jnp.exp(s - m_new)
            l_new = l_new + jnp.sum(p, axis=0, keepdims=True)
            acc = acc + _dot(vt_scr[kt], p.astype(MXU_DTYPE))
        l_scr[...] = l_new
        acc_scr[...] = acc
        m_scr[...] = m_new

    if mode == "sel":
        def body(it, carry):
            update([it * ATTN_TILES_PER_ITER + u for u in range(ATTN_TILES_PER_ITER)])
            return carry

        lax.fori_loop(0, (qt + ATTN_TILES_PER_ITER) // ATTN_TILES_PER_ITER, body, 0)
    else:
        update([qt - WIN_TILES + u for u in range(WIN_TILES + 1)])
    out = acc_scr[...] / l_scr[...]
    for r in range(GQA_REP):
        o_ref[0, :, r * HEAD_DIM:(r + 1) * HEAD_DIM] = out[:, r * tq:(r + 1) * tq].T


def _prompt_bias_tiles(tab, mode):
    ii = jnp.arange(KEY_TILE, dtype=jnp.int32)
    entry = lambda b: tab[:, b][:, None, None]

    def tile(offset):
        dist = (offset * KEY_TILE + ii[:, None] - ii[None, :])[None]
        return dist, jnp.broadcast_to(_bias_from_dist(dist, entry), (N_HEADS, KEY_TILE, KEY_TILE))

    d0, b0 = tile(0)
    tiles = [jnp.where(d0 >= 0, b0, NEG_INF), tile(1)[1], tile(2)[1]]
    if mode == "win":
        d4, b4 = tile(WIN_TILES)
        tiles.append(jnp.where(d4 <= WINDOW, b4, NEG_INF))
    tiles.append(jnp.full_like(b0, NEG_INF))
    mb = jnp.stack(tiles).reshape(len(tiles), N_KV_HEADS, GQA_REP, KEY_TILE, KEY_TILE)
    return mb.transpose(0, 1, 4, 2, 3).reshape(len(tiles), N_KV_HEADS, KEY_TILE, GQA_REP * KEY_TILE)


def _attn_prompt(p3, tab, mode, k_col, v_col, sel=None, e3=None):
    nb, t, _ = p3.shape
    tq = KEY_TILE
    nq = t // tq
    mb = _prompt_bias_tiles(tab, mode)
    cols = GQA_REP * tq
    in_specs = [
        pl.BlockSpec((1, tq, GQA_REP * HEAD_DIM), lambda b, g, i: (b, i, g)),
        pl.BlockSpec((1, t, HEAD_DIM), lambda b, g, i: (b, 0, k_col + g)),
        pl.BlockSpec((1, t, HEAD_DIM), lambda b, g, i: (b, 0, v_col + g)),
        pl.BlockSpec((mb.shape[0], 1, KEY_TILE, cols), lambda b, g, i: (0, g, 0, 0)),
    ]
    args = [p3, p3, p3, mb]
    if mode == "sel":
        n_pad = sel.shape[-1]
        in_specs += [
            pl.BlockSpec((1, 1, tq, n_pad), lambda b, g, i: (b, g, i, 0)),
            pl.BlockSpec((nq, KEY_TILE, n_pad), lambda b, g, i: (0, 0, 0)),
        ]
        args += [sel, e3]
    return pl.pallas_call(
        functools.partial(_attn_prompt_kernel, mode=mode, n_kt=nq),
        grid=(nb, N_KV_HEADS, nq),
        in_specs=in_specs,
        out_specs=pl.BlockSpec((1, tq, GQA_REP * HEAD_DIM), lambda b, g, i: (b, i, g)),
        out_shape=jax.ShapeDtypeStruct((nb, t, ATTN_WIDTH), F32),
        scratch_shapes=[pltpu.VMEM((nq, KEY_TILE, HEAD_DIM), MXU_DTYPE),
                        pltpu.VMEM((nq, HEAD_DIM, KEY_TILE), MXU_DTYPE),
                        pltpu.VMEM((1, cols), F32), pltpu.VMEM((1, cols), F32),
                        pltpu.VMEM((HEAD_DIM, cols), F32)],
        compiler_params=_cparams(("parallel", "parallel", "arbitrary")),
    )(*args)


def _attn_sample_kernel(*refs, mode, paged, pages, n_steps, tq, pos0, base_pos):
    it = iter(refs)
    if paged:
        next(it)
    q_ref = next(it)
    page_refs = [next(it) for _ in range(pages)]
    new_ref, rowtab_ref = next(it), next(it)
    if mode == "sel":
        sel_ref, e_ref = next(it), next(it)
    o_ref, m_scr, l_scr, acc_scr = next(it), next(it), next(it), next(it)
    kk = pl.program_id(1)
    rows = N_HEADS * tq
    heads_per_row = 2 * N_KV_HEADS

    @pl.when(kk == 0)
    def _():
        m_scr[...] = jnp.full(m_scr.shape, NEG_INF, F32)
        l_scr[...] = jnp.zeros(l_scr.shape, F32)
        acc_scr[...] = jnp.zeros(acc_scr.shape, F32)

    def queries():
        zero = jnp.zeros((tq, HEAD_DIM), F32)
        blocks = []
        for h in range(N_HEADS):
            qh = q_ref[0, :, h * HEAD_DIM:(h + 1) * HEAD_DIM] * ATTN_SCALE
            blocks.append(jnp.concatenate([qh if g == h // GQA_REP else zero for g in range(N_KV_HEADS)], axis=1))
        return jnp.concatenate(blocks, axis=0).astype(MXU_DTYPE)

    def step(k_all, v_all, key_pos0, chosen_cols, exact_bias):
        n_keys = k_all.shape[0]
        s = _dot_nt(queries(), k_all)
        row = lax.broadcasted_iota(jnp.int32, (rows, n_keys), 0) & (tq - 1)
        col = lax.broadcasted_iota(jnp.int32, (rows, n_keys), 1)
        dist = pos0 + row - (key_pos0 + col)
        if exact_bias:
            bias = _bias_from_dist(dist, lambda b: rowtab_ref[:, b:b + 1])
        else:
            bias = rowtab_ref[:, NUM_BUCKETS - 1:NUM_BUCKETS]
        if mode == "sel":
            sel = sel_ref[0].reshape(N_KV_HEADS * tq, sel_ref.shape[-1]).astype(MXU_DTYPE)
            chosen = _dot(sel, chosen_cols)
            chosen = jnp.concatenate(
                [chosen[g * tq:(g + 1) * tq] for g in range(N_KV_HEADS) for _ in range(GQA_REP)], axis=0)
            mask = (dist >= 0) & (chosen > 0.5)
        else:
            mask = (dist >= 0) & (dist <= WINDOW)
        s = jnp.where(mask, s + bias, NEG_INF)
        _online_softmax_step(s, v_all, m_scr, l_scr, acc_scr)

    def cached(kv):
        return jnp.concatenate([
            jnp.concatenate([pg[0, pl.ds(kv * N_KV_HEADS + g, PAGE_SIZE, stride=heads_per_row), :]
                             for g in range(N_KV_HEADS)], axis=1)
            for pg in page_refs], axis=0).astype(MXU_DTYPE)

    keys_per_step = pages * PAGE_SIZE
    key_pos0 = base_pos + kk * keys_per_step
    far = pos0 - (key_pos0 + keys_per_step - 1) >= MAX_DISTANCE
    is_cache = kk < n_steps
    chosen_cols = e_ref[...] if mode == "sel" else None

    @pl.when(is_cache & far)
    def _():
        step(cached(0), cached(1), key_pos0, chosen_cols, False)

    @pl.when(is_cache & jnp.logical_not(far))
    def _():
        step(cached(0), cached(1), key_pos0, chosen_cols, True)

    @pl.when(kk == n_steps)
    def _():
        pad = jnp.zeros((KEY_TILE - tq, N_KV_HEADS * HEAD_DIM), F32)
        k_new = jnp.concatenate([new_ref[0, :, :KV_WIDTH], pad], axis=0).astype(MXU_DTYPE)
        v_new = jnp.concatenate([new_ref[0, :, KV_WIDTH:], pad], axis=0).astype(MXU_DTYPE)
        cols = e_ref[:, :KEY_TILE] if mode == "sel" else None
        step(k_new, v_new, pos0, cols, True)
        out = acc_scr[...] / l_scr[...]
        for h in range(N_HEADS):
            g = h // GQA_REP
            o_ref[0, :, h * HEAD_DIM:(h + 1) * HEAD_DIM] = out[h * tq:(h + 1) * tq, g * HEAD_DIM:(g + 1) * HEAD_DIM]


def _attn_sample(p3, cache_rows, page_table, tab, mode, new_col, pos0, base_pos, sel=None, e_mat=None):
    nb, tq, _ = p3.shape
    paged = page_table is not None
    pages = 8 if paged else WIN_TILES
    n_tiles = page_table.shape[1] if paged else WIN_TILES
    assert n_tiles % pages == 0
    n_steps = n_tiles // pages
    rows = N_HEADS * tq
    rowtab = jnp.repeat(tab, tq, axis=0)

    def page_spec(i):
        def index(b, kk, *pt):
            page = jnp.minimum(kk, n_steps - 1) * pages + i
            return (pt[0][b, page] if paged else b * n_tiles + page, 0, 0)
        return pl.BlockSpec((1, PAGE_SIZE * 2 * N_KV_HEADS, HEAD_DIM), index)

    def fixed(shape, fn):
        return pl.BlockSpec(shape, lambda b, kk, *_: fn(b, kk))

    in_specs = [fixed((1, tq, ATTN_WIDTH), lambda b, kk: (b, 0, 0))]
    in_specs += [page_spec(i) for i in range(pages)]
    in_specs += [fixed((1, tq, 2 * KV_WIDTH), lambda b, kk: (b, 0, new_col)),
                 fixed((rows, NUM_BUCKETS), lambda b, kk: (0, 0))]
    args = [p3] + [cache_rows] * pages + [p3, rowtab]
    if mode == "sel":
        n_pad = sel.shape[-1]
        in_specs += [fixed((1, N_KV_HEADS, tq, n_pad), lambda b, kk: (b, 0, 0, 0)),
                     fixed((n_pad, pages * PAGE_SIZE), lambda b, kk: (0, kk))]
        args += [sel, e_mat]
    kern = functools.partial(_attn_sample_kernel, mode=mode, paged=paged, pages=pages, n_steps=n_steps, tq=tq,
                             pos0=pos0, base_pos=base_pos)
    out_spec = fixed((1, tq, ATTN_WIDTH), lambda b, kk: (b, 0, 0))
    out_shape = jax.ShapeDtypeStruct((nb, tq, ATTN_WIDTH), F32)
    scratch = [pltpu.VMEM((rows, 1), F32), pltpu.VMEM((rows, 1), F32),
               pltpu.VMEM((rows, N_KV_HEADS * HEAD_DIM), F32)]
    cp = _cparams(("parallel", "arbitrary"))
    grid = (nb, n_steps + 1)
    if paged:
        grid_spec = pltpu.PrefetchScalarGridSpec(num_scalar_prefetch=1, grid=grid, in_specs=in_specs,
                                                 out_specs=out_spec, scratch_shapes=scratch)
        return pl.pallas_call(kern, grid_spec=grid_spec, out_shape=out_shape, compiler_params=cp)(
            page_table, *args)
    return pl.pallas_call(kern, grid=grid, in_specs=in_specs, out_specs=out_spec, out_shape=out_shape,
                          scratch_shapes=scratch, compiler_params=cp)(*args)


def _combine_kernel(oc_ref, os_ref, ow_ref, gate_ref, g_ref, o_ref):
    gates = gate_ref[...]
    parts = []
    for h in range(N_HEADS):
        sl = slice(h * HEAD_DIM, (h + 1) * HEAD_DIM)
        parts.append(gates[:, h:h + 1] * oc_ref[:, sl]
                     + gates[:, N_HEADS + h:N_HEADS + h + 1] * os_ref[:, sl]
                     + gates[:, 2 * N_HEADS + h:2 * N_HEADS + h + 1] * ow_ref[:, sl])
    o = jnp.concatenate(parts, axis=1)
    ms = jnp.mean(o * o, axis=-1, keepdims=True)
    o_ref[...] = (o * lax.rsqrt(ms + EPS) * g_ref[...]).astype(o_ref.dtype)


def _combine(o_cmp, o_sel, o_win, gates, gain):
    n, w = o_cmp.shape
    tm = _row_tile(n, 256)
    row = pl.BlockSpec((tm, w), lambda i: (i, 0))
    return pl.pallas_call(
        _combine_kernel,
        grid=(n // tm,),
        in_specs=[row, row, row, pl.BlockSpec((tm, LANES), lambda i: (i, 0)),
                  pl.BlockSpec((1, w), lambda i: (0, 0))],
        out_specs=row,
        out_shape=jax.ShapeDtypeStruct((n, w), MXU_DTYPE),
        compiler_params=_cparams(("parallel",)),
    )(o_cmp, o_sel, o_win, gates, gain.reshape(1, w))


SSM_LANE_GROUPS = LANES // SSM_GROUP
SSM_BLOCK_STATE = SSM_LANE_GROUPS * 2 * SSM_STATE


def _ssm_kernel(u_ref, kt_ref, bs_ref, cs_ref, d_ref, la_ref, lb_ref, h0_ref, y_ref, hout_ref,
                us_scr, s_scr, hs_scr, *, n_seq, n_chunk):
    L = SSM_CHUNK
    rows = n_seq * n_chunk
    st = SSM_BLOCK_STATE
    group_shift = int(math.log2(SSM_GROUP))
    state_shift = int(math.log2(2 * SSM_STATE))
    in_mask = (jnp.right_shift(lax.broadcasted_iota(jnp.int32, (LANES, st), 0), group_shift)
               == jnp.right_shift(lax.broadcasted_iota(jnp.int32, (LANES, st), 1), state_shift))
    out_mask = (jnp.right_shift(lax.broadcasted_iota(jnp.int32, (st, LANES), 0), state_shift)
                == jnp.right_shift(lax.broadcasted_iota(jnp.int32, (st, LANES), 1), group_shift))
    zero = jnp.zeros((), MXU_DTYPE)

    s_acc = jnp.zeros((rows, st), F32)
    for s in range(L):
        us = u_ref[0, pl.ds(s, rows, stride=L), :].astype(MXU_DTYPE)
        us_scr[s] = us
        b_full = jnp.where(in_mask, jnp.concatenate([bs_ref[0, s]] * SSM_LANE_GROUPS, axis=1), zero)
        s_acc = s_acc + _dot(us, b_full)
    s_scr[...] = s_acc

    la = la_ref[0]
    lb = lb_ref[0]
    is_re = (lax.broadcasted_iota(jnp.int32, (1, st), 1) & SSM_STATE) == 0

    def swap(h):
        return jnp.where(is_re, pltpu.roll(h, st - SSM_STATE, 1), pltpu.roll(h, SSM_STATE, 1))

    def body(c, hs):
        new = []
        for b in range(n_seq):
            row = b * n_chunk + c
            hs_scr[pl.ds(row, 1), :] = hs[b]
            new.append(la * hs[b] + lb * swap(hs[b]) + s_scr[pl.ds(row, 1), :])
        return tuple(new)

    hs = lax.fori_loop(0, n_chunk, body, tuple(h0_ref[0, b:b + 1, :] for b in range(n_seq)))
    for b in range(n_seq):
        hout_ref[0, b:b + 1, :] = hs[b]

    h_start = hs_scr[...].astype(MXU_DTYPE)
    for t in range(L):
        c_full = jnp.where(out_mask, jnp.concatenate([cs_ref[0, t]] * SSM_LANE_GROUPS, axis=0), zero)
        acc = _dot(h_start, c_full)
        for s in range(t + 1):
            acc = acc + _dot(us_scr[s], kt_ref[0, t - s])
        acc = acc + d_ref[0] * u_ref[0, pl.ds(t, rows, stride=L), :]
        y_ref[0, pl.ds(t, rows, stride=L), :] = acc


def _ssm(u_lb, kt, bs, cs, dvec, la, lb, h0, *, n_seq, n_chunk):
    nblk, n, w = u_lb.shape
    L = SSM_CHUNK
    rows = n_seq * n_chunk
    assert n == rows * L and w == LANES
    st = SSM_BLOCK_STATE

    def blk(shape):
        return pl.BlockSpec((1,) + shape, lambda i: (i,) + (0,) * len(shape))

    return pl.pallas_call(
        functools.partial(_ssm_kernel, n_seq=n_seq, n_chunk=n_chunk),
        grid=(nblk,),
        in_specs=[blk((n, w)), blk((L, w, w)), blk((L, w, w)), blk((L, w, w)), blk((1, w)), blk((1, st)),
                  blk((1, st)), blk((n_seq, st))],
        out_specs=[blk((n, w)), blk((n_seq, st))],
        out_shape=[jax.ShapeDtypeStruct((nblk, n, w), F32), jax.ShapeDtypeStruct((nblk, n_seq, st), F32)],
        scratch_shapes=[pltpu.VMEM((L, rows, w), MXU_DTYPE), pltpu.VMEM((rows, st), F32),
                        pltpu.VMEM((rows, st), F32)],
        compiler_params=_cparams(("parallel",)),
    )(u_lb, kt, bs, cs, dvec, la, lb, h0)


def _ssm_matrices(lam_re, lam_im, log_dt, b_re, b_im, c_re, c_im, d_skip, l_eff):
    g = lam_re.shape[0]
    L = SSM_CHUNK
    lg = SSM_LANE_GROUPS
    nblk = g // lg
    lam = lax.complex(lam_re.astype(F32), lam_im.astype(F32))
    dt = jnp.exp(log_dt.astype(F32))[:, None]
    lam_bar = jnp.exp(lam * dt)
    b_bar = ((lam_bar - 1.0) / lam)[..., None] * lax.complex(b_re.astype(F32), b_im.astype(F32))
    c_c = lax.complex(c_re.astype(F32), c_im.astype(F32))
    pw = [jnp.ones_like(lam_bar)]
    for _ in range(L):
        pw.append(pw[-1] * lam_bar)
    pw = jnp.stack(pw)
    pw_b = pw.reshape(L + 1, nblk, lg, SSM_STATE)
    bb = b_bar.reshape(nblk, lg, SSM_STATE, SSM_GROUP).transpose(0, 1, 3, 2)
    chan_group = np.arange(LANES) // SSM_GROUP
    to_chan = jnp.asarray(chan_group[None, :] == np.arange(lg)[:, None], F32)
    out_lane = jnp.asarray(np.arange(LANES)[None, :] % SSM_GROUP == np.arange(SSM_GROUP)[:, None], F32)
    same_group = jnp.asarray(chan_group[:, None] == chan_group[None, :], F32)
    kern = jnp.einsum("gop,kgp,gpi->gkio", c_c, pw[:L], b_bar).real
    kern = kern.reshape(nblk, lg, L, SSM_GROUP, SSM_GROUP).transpose(0, 2, 1, 3, 4)
    kern = kern.reshape(nblk, L, LANES, SSM_GROUP)
    kt = jnp.einsum("ltro,oc->ltrc", kern, out_lane, precision=lax.Precision.HIGHEST) * same_group
    exps = np.clip(l_eff - 1 - np.arange(L), 0, L)
    bx = pw_b[exps].transpose(1, 0, 2, 3)[:, :, :, None, :] * bb[:, None]
    bx = jnp.where(jnp.asarray(np.arange(L) < l_eff)[None, :, None, None, None], bx, 0.0)
    bs = jnp.concatenate([bx.real, bx.imag], axis=-1).reshape(nblk, L, LANES, 2 * SSM_STATE)
    c_t = c_c.reshape(nblk, LANES, SSM_STATE).transpose(0, 2, 1)
    pw_t = pw_b[1:L + 1].transpose(1, 0, 3, 2)
    expand = lambda a: jnp.einsum("ltpg,gc->ltpc", a, to_chan, precision=lax.Precision.HIGHEST)
    pr, pi = expand(pw_t.real), expand(pw_t.imag)
    cr, ci = c_t.real[:, None], c_t.imag[:, None]
    cs = jnp.concatenate([cr * pr - ci * pi, -(cr * pi + ci * pr)], axis=2)
    lam_l = pw[l_eff].reshape(nblk, lg, SSM_STATE)
    la = jnp.stack([lam_l.real, lam_l.real], axis=2).reshape(nblk, 1, SSM_BLOCK_STATE)
    lb = jnp.stack([-lam_l.imag, lam_l.imag], axis=2).reshape(nblk, 1, SSM_BLOCK_STATE)
    dvec = d_skip.astype(F32).reshape(nblk, 1, LANES)
    return kt.astype(MXU_DTYPE), bs.astype(MXU_DTYPE), cs.astype(MXU_DTYPE), dvec, la, lb


def _state_to_blocks(re, im):
    n_seq, g, p = re.shape
    h = jnp.stack([re, im], axis=2).astype(F32).reshape(n_seq, g // SSM_LANE_GROUPS, SSM_BLOCK_STATE)
    return h.transpose(1, 0, 2)


def _state_from_blocks(h):
    nblk, n_seq, _ = h.shape
    h = h.transpose(1, 0, 2).reshape(n_seq, nblk * SSM_LANE_GROUPS, 2, SSM_STATE)
    return h[:, :, 0], h[:, :, 1]


def _glu_kernel(y_ref, w_ref, b_ref, g_ref, o_ref):
    gl = _gelu(jnp.concatenate([y_ref[k] for k in range(y_ref.shape[0])], axis=1))
    z = _dot(gl.astype(MXU_DTYPE), w_ref[...]) + b_ref[...]
    o = gl * _sigmoid(z)
    ms = jnp.mean(o * o, axis=-1, keepdims=True)
    o_ref[...] = (o * lax.rsqrt(ms + EPS) * g_ref[...]).astype(o_ref.dtype)


def _glu(y_lb, w, b, gain):
    nblk, n, _ = y_lb.shape
    d = nblk * LANES
    tm = _row_tile(n, 256)
    vec = pl.BlockSpec((1, d), lambda i: (0, 0))
    return pl.pallas_call(
        _glu_kernel,
        grid=(n // tm,),
        in_specs=[pl.BlockSpec((nblk, tm, LANES), lambda i: (0, i, 0)), pl.BlockSpec((d, d), lambda i: (0, 0)),
                  vec, vec],
        out_specs=pl.BlockSpec((tm, d), lambda i: (i, 0)),
        out_shape=jax.ShapeDtypeStruct((n, d), MXU_DTYPE),
        compiler_params=_cparams(("parallel",)),
    )(y_lb, w, b.reshape(1, d), gain.reshape(1, d))


def _top_rows(x, k):
    n = x.shape[0]
    idx_iota = lax.broadcasted_iota(jnp.int32, x.shape, 0)
    vals, idxs = [], []
    cur = x
    for _ in range(k):
        m = jnp.max(cur, axis=0, keepdims=True)
        ix = jnp.min(jnp.where(cur == m, idx_iota, n), axis=0, keepdims=True)
        vals.append(m)
        idxs.append(ix)
        cur = jnp.where(idx_iota == ix, -jnp.inf, cur)
    return jnp.concatenate(vals, axis=0), jnp.concatenate(idxs, axis=0)


def _pick_rows(table, sel):
    out = jnp.zeros(sel.shape, table.dtype)
    for a in range(table.shape[0]):
        out = out + jnp.where(sel == a, table[a:a + 1, :], 0)
    return out


def _peer_topk_kernel(q_ref, keys_ref, i1_ref, i2_ref, gate_ref):
    q = q_ref[...].astype(MXU_DTYPE)
    s1 = _dot_nt(keys_ref[0, 0].astype(MXU_DTYPE), q[:, :PEER_HALF])
    s2 = _dot_nt(keys_ref[0, 1].astype(MXU_DTYPE), q[:, PEER_HALF:])
    v1, x1 = _top_rows(s1, PEER_TOPK)
    v2, x2 = _top_rows(s2, PEER_TOPK)
    t = v1.shape[1]
    pairs = [(a, b) for a in range(PEER_TOPK) for b in range(PEER_TOPK) if (a + 1) * (b + 1) <= PEER_TOPK]
    n_pad = -len(pairs) % SUBLANES
    cand = jnp.concatenate([v1[a:a + 1] + v2[b:b + 1] for a, b in pairs]
                           + [jnp.full((n_pad, t), -jnp.inf, F32)], axis=0)
    top, pos = _top_rows(cand, PEER_TOPK)
    i1_ref[0] = _pick_rows(jnp.concatenate([x1[a:a + 1] for a, _ in pairs], axis=0), pos)
    i2_ref[0] = _pick_rows(jnp.concatenate([x2[b:b + 1] for _, b in pairs], axis=0), pos)
    e = jnp.exp(top - jnp.max(top, axis=0, keepdims=True))
    gate_ref[0] = e / jnp.sum(e, axis=0, keepdims=True)


def _peer_topk(q, sub_keys):
    n = q.shape[0]
    tt = _row_tile(n, 512)
    out = pl.BlockSpec((1, PEER_TOPK, tt), lambda i, h: (h, 0, i))
    shp = (PEER_HEADS, PEER_TOPK, n)
    return pl.pallas_call(
        _peer_topk_kernel,
        grid=(n // tt, PEER_HEADS),
        in_specs=[pl.BlockSpec((tt, 2 * PEER_HALF), lambda i, h: (i, h)),
                  pl.BlockSpec((1, 2, PEER_KEYS, PEER_HALF), lambda i, h: (h, 0, 0, 0))],
        out_specs=[out, out, out],
        out_shape=[jax.ShapeDtypeStruct(shp, jnp.int32), jax.ShapeDtypeStruct(shp, jnp.int32),
                   jax.ShapeDtypeStruct(shp, F32)],
        compiler_params=_cparams(("parallel", "parallel")),
    )(q, sub_keys)


PEER_EXPERT_BLOCK = 4 * PEER_KEYS


PEER_GRID_HALF = PEER_KEYS // 2
HIGH_HALF = 0xFFFF0000
PEER_BUILD_UNROLL = 32


def _peer_kernel(x_ref, u_ref, v_ref, i1_ref, i2_ref, gate_ref, h_ref, o_ref, g_scr, w_scr, *, tm, n_blocks):
    j = pl.program_id(1)
    n_entries = PEER_HEADS * PEER_TOPK
    words = PEER_GRID_HALF // 2
    blocks_per_half = n_blocks // 2
    unroll = math.gcd(tm, PEER_BUILD_UNROLL)

    def build(base):
        r = lax.broadcasted_iota(jnp.int32, (PEER_GRID_HALF, n_entries), 0)
        i1_of_row = base + jnp.where(r < words, 2 * r, 2 * (r - words) + 1)
        i2_of_row = lax.broadcasted_iota(jnp.int32, (PEER_KEYS, n_entries), 0)

        def body(nb, carry):
            for k in range(unroll):
                n = nb * unroll + k
                a = jnp.where(i1_of_row == i1_ref[pl.ds(n, 1), :], gate_ref[pl.ds(n, 1), :], 0.0)
                b = jnp.where(i2_of_row == i2_ref[pl.ds(n, 1), :], 1.0, 0.0)
                grid = _dot_nt(a.astype(MXU_DTYPE), b.astype(MXU_DTYPE))
                bits = lax.bitcast_convert_type(grid.astype(jnp.bfloat16).astype(F32), jnp.uint32)
                g_scr[pl.ds(pl.multiple_of(n * words, words), words), :] = (
                    jnp.right_shift(bits[:words], jnp.uint32(16)) | (bits[words:] & jnp.uint32(HIGH_HALF)))
            return carry

        lax.fori_loop(0, tm // unroll, body, 0)

    def weights():
        act = _dot_nt(x_ref[...], u_ref[...])
        pairs = PEER_EXPERT_BLOCK // (2 * PEER_KEYS)
        m0 = (j - jnp.where(j >= blocks_per_half, blocks_per_half, 0)) * pairs
        pieces = []
        for q in range(pairs):
            word = g_scr[pl.ds(m0 + q, tm, stride=words), :]
            pieces.append(lax.bitcast_convert_type(jnp.left_shift(word, jnp.uint32(16)), F32))
            pieces.append(lax.bitcast_convert_type(word & jnp.uint32(HIGH_HALF), F32))
        g = jnp.concatenate(pieces, axis=1)
        w_scr[j % 2] = (g * _gelu(act)).astype(MXU_DTYPE)

    def values():
        o_ref[...] += _dot(w_scr[(j + 1) % 2], v_ref[...])

    @pl.when(j == 0)
    def _():
        o_ref[...] = h_ref[...]
        build(0)
        weights()

    @pl.when(j == blocks_per_half)
    def _():
        build(PEER_GRID_HALF)

    @pl.when((j > 0) & (j < n_blocks))
    def _():
        values()
        weights()

    @pl.when(j == n_blocks)
    def _():
        values()


def _peer(xn, u, v, i1, i2, gate, h):
    n, d = xn.shape
    n_blocks = u.shape[0] // PEER_EXPERT_BLOCK
    assert n_blocks * PEER_EXPERT_BLOCK == PEER_KEYS * PEER_KEYS and n_blocks % 2 == 0
    tm = _row_tile(n, 512)
    n_entries = PEER_HEADS * PEER_TOPK
    once = pl.Buffered(1)
    row_in = pl.BlockSpec((tm, d), lambda i, j: (i, 0), pipeline_mode=once)
    ent = pl.BlockSpec((tm, n_entries), lambda i, j: (i, 0))
    return pl.pallas_call(
        functools.partial(_peer_kernel, tm=tm, n_blocks=n_blocks),
        grid=(n // tm, n_blocks + 1),
        in_specs=[row_in,
                  pl.BlockSpec((PEER_EXPERT_BLOCK, d), lambda i, j: (jnp.minimum(j, n_blocks - 1), 0)),
                  pl.BlockSpec((PEER_EXPERT_BLOCK, d), lambda i, j: (jnp.maximum(j - 1, 0), 0)),
                  ent, ent, ent, row_in],
        out_specs=pl.BlockSpec((tm, d), lambda i, j: (i, 0), pipeline_mode=once),
        out_shape=jax.ShapeDtypeStruct((n, d), F32),
        scratch_shapes=[pltpu.VMEM((tm * PEER_GRID_HALF // 2, PEER_KEYS), jnp.uint32),
                        pltpu.VMEM((2, tm, PEER_EXPERT_BLOCK), MXU_DTYPE)],
        compiler_params=_cparams(("parallel", "arbitrary")),
    )(xn, u, v, i1, i2, gate, h)


def _prepare_weights(rel_table, norm_mix, w_in, q_norm, k_norm, w_phi, b_phi, w_glu, w_out, peer_w_query,
                     peer_u, peer_v):
    d_model = w_in.shape[0]
    off_gate = ATTN_WIDTH + 6 * KV_WIDTH
    off_ssm = off_gate + 3 * N_HEADS
    w_main = w_in[:, :off_gate].astype(MXU_DTYPE)
    w_ssm = w_in[:, off_ssm:].astype(MXU_DTYPE)
    w_gate = jnp.pad(w_in[:, off_gate:off_ssm], ((0, 0), (0, LANES - 3 * N_HEADS))).astype(MXU_DTYPE)
    n_main = w_main.shape[1]
    ones = jnp.ones((KV_WIDTH,), F32)
    zeros = jnp.zeros((KV_WIDTH,), F32)
    gain = jnp.concatenate([
        jnp.tile(q_norm.astype(F32), N_HEADS), ones, ones,
        jnp.tile(k_norm[1].astype(F32), N_KV_HEADS), ones,
        jnp.tile(k_norm[2].astype(F32), N_KV_HEADS), ones]).reshape(1, n_main)
    mode = jnp.concatenate([
        jnp.ones((ATTN_WIDTH,), F32), zeros, zeros, ones, zeros, ones, zeros]).reshape(1, n_main)
    w_cmp = w_phi.reshape(2, CMP_R, CMP_STRIDE, HEAD_DIM, HEAD_DIM).transpose(0, 2, 3, 1, 4)
    w_cmp = w_cmp.reshape(2, CMP_STRIDE * HEAD_DIM, CMP_R * HEAD_DIM).astype(MXU_DTYPE)
    return dict(
        d_model=d_model, w_main=w_main, w_ssm=w_ssm, w_gate=w_gate, gain=gain, mode=mode, w_cmp=w_cmp,
        tab=rel_table.astype(F32).T, w_glu=w_glu.astype(MXU_DTYPE), w_out=w_out.astype(MXU_DTYPE),
        w_query=peer_w_query.astype(MXU_DTYPE), peer_u=peer_u.astype(MXU_DTYPE),
        peer_v=peer_v.astype(MXU_DTYPE))


def _overlap_matrix(n_chunks, n_cmp, n_slc, n_pad):
    c0 = np.arange(n_chunks)[:, None] * CMP_STRIDE
    s0 = np.arange(n_pad)[None, :] * SLC_BLOCK
    overlap = np.clip(np.minimum(c0 + CMP_BLOCK, s0 + SLC_BLOCK) - np.maximum(c0, s0), 0, None) / CMP_BLOCK
    overlap = overlap * (np.arange(n_chunks)[:, None] < n_cmp) * (np.arange(n_pad)[None, :] < n_slc)
    return jnp.asarray(overlap, dtype=MXU_DTYPE)


def _expansion_matrix(n_pad, n_keys, n_cols):
    key = np.arange(n_cols)[None, :]
    e = ((key // SLC_BLOCK) == np.arange(n_pad)[:, None]) & (key < n_keys)
    return jnp.asarray(e, dtype=MXU_DTYPE)


def _pad_lanes(n):
    return -(-n // LANES) * LANES


def _layer(x3, t_real, pos0, cache_cmp3, cache_sel3, cache_win3, page_table, h0, wts, rel_table, norm_mix,
           k_norm, b_phi, ssm, b_glu, out_norm_attn, out_norm_ssm, norm_ffn, sub_keys):
    nb, t, d_model = x3.shape
    n = nb * t
    sample = page_table is not None
    x2 = x3.reshape(n, d_model)
    xn = _rmsnorm(x2, norm_mix, MXU_DTYPE)
    p, kv_cmp, kv_sel, kv_win = _in_proj(xn, wts["w_main"], wts["mode"], wts["gain"])
    gates = _matmul(xn, wts["w_gate"], "sigmoid")
    n_main = p.shape[1]
    p3 = p.reshape(nb, t, n_main)
    col = lambda off: off // HEAD_DIM
    off_cmp, off_sel, off_win = ATTN_WIDTH, ATTN_WIDTH + 2 * KV_WIDTH, ATTN_WIDTH + 4 * KV_WIDTH

    if sample:
        assert t_real < CMP_STRIDE and pos0 % PAGE_SIZE == 0
        n_pages = page_table.shape[1]
        pk = _compress(cache_cmp3, page_table, wts["w_cmp"])
        total = pos0 + t_real
    else:
        n_pages = t // PAGE_SIZE
        pt = jnp.arange(nb * n_pages, dtype=jnp.int32).reshape(nb, n_pages)
        pk = _compress(kv_cmp.reshape(nb * n_pages, PAGE_SIZE * 2 * N_KV_HEADS, HEAD_DIM), pt, wts["w_cmp"])
        total = t
    n_chunks = pk.shape[2]
    n_cmp = total // CMP_STRIDE - CMP_R + 1
    n_slc = -(-total // SLC_BLOCK)
    n_pad = _pad_lanes(n_slc)
    m_mat = _overlap_matrix(n_chunks, n_cmp, n_slc, n_pad)
    tq_cmp = _row_tile(t, 256)
    tab = wts["tab"]
    o_cmp, sel = _cmp_attention(tab, p3, pk, b_phi.astype(F32), k_norm[0].astype(F32).reshape(1, HEAD_DIM),
                                m_mat, pos0=pos0, n_cmp=n_cmp, n_slc=n_slc, tq=tq_cmp)

    if sample:
        kv_block = 2 * KV_WIDTH
        n_cols = (n_pages + 8) * KEY_TILE
        e_mat = _expansion_matrix(n_pad, (n_pages + 1) * KEY_TILE, n_cols)
        o_sel = _attn_sample(p3, cache_sel3, page_table, tab, "sel", off_sel // kv_block, pos0, 0, sel, e_mat)
        o_win = _attn_sample(p3, cache_win3, None, tab, "win", off_win // kv_block, pos0, pos0 - WINDOW)
    else:
        e3 = _expansion_matrix(n_pad, t, t).T.reshape(t // KEY_TILE, KEY_TILE, n_pad)
        o_sel = _attn_prompt(p3, tab, "sel", col(off_sel), col(off_sel + KV_WIDTH), sel, e3)
        o_win = _attn_prompt(p3, tab, "win", col(off_win), col(off_win + KV_WIDTH))
    mixed_attn = _combine(o_cmp.reshape(n, ATTN_WIDTH), o_sel.reshape(n, ATTN_WIDTH),
                          o_win.reshape(n, ATTN_WIDTH), gates, out_norm_attn)

    u_lb = _matmul(xn, wts["w_ssm"], "laneblocks")
    nblk = u_lb.shape[0]
    if t % SSM_CHUNK:
        assert t < SSM_CHUNK and t_real <= t
        u_lb = jnp.pad(u_lb.reshape(nblk, nb, t, LANES), ((0, 0), (0, 0), (0, SSM_CHUNK - t), (0, 0)))
        u_lb = u_lb.reshape(nblk, nb * SSM_CHUNK, LANES)
        l_eff, n_chunk = t_real, 1
    else:
        l_eff, n_chunk = SSM_CHUNK, t // SSM_CHUNK
    mats = _ssm_matrices(*ssm, l_eff)
    y_lb, h_last = _ssm(u_lb, *mats, _state_to_blocks(*h0), n_seq=nb, n_chunk=n_chunk)
    if t % SSM_CHUNK:
        y_lb = y_lb.reshape(nblk, nb, SSM_CHUNK, LANES)[:, :, :t].reshape(nblk, n, LANES)
    mixed_ssm = _glu(y_lb, wts["w_glu"], b_glu, out_norm_ssm)

    h = _out_proj(mixed_attn, mixed_ssm, wts["w_out"], x2)
    hn = _rmsnorm(h, norm_ffn, MXU_DTYPE)
    pq = _matmul(hn, wts["w_query"])
    i1, i2, gate = _peer_topk(pq, sub_keys)
    to_rows = lambda a: a.transpose(2, 0, 1).reshape(n, PEER_HEADS * PEER_TOPK)
    y_out = _peer(hn, wts["peer_u"], wts["peer_v"], to_rows(i1), to_rows(i2), to_rows(gate), h)
    new_kv = [a.reshape(nb, t, 2, N_KV_HEADS, HEAD_DIM) for a in (kv_cmp, kv_sel, kv_win)]
    return y_out.reshape(nb, t, d_model), new_kv, _state_from_blocks(h_last)


def kernel(x_prompt, x_sample, cache_kv_cmp, cache_kv_sel, cache_kv_win, state_ssm_re, state_ssm_im, page_table,
           rel_table, norm_mix, w_in, q_norm, k_norm, w_phi, b_phi, ssm_lam_re, ssm_lam_im, ssm_log_dt,
           ssm_b_re, ssm_b_im, ssm_c_re, ssm_c_im, ssm_d, w_glu, b_glu, out_norm_attn, out_norm_ssm, w_out,
           norm_ffn, peer_w_query, peer_sub_keys, peer_u, peer_v):
    depth = w_in.shape[0]
    bp, t_p, d_model = x_prompt.shape
    db, t_s, _ = x_sample.shape
    past_len = page_table.shape[1] * PAGE_SIZE
    kv_row = 2 * KV_WIDTH
    n_groups = ssm_lam_re.shape[1]
    assert cache_kv_win.shape[2] == WINDOW and t_p % KEY_TILE == 0 and t_s <= SAMPLE_ROWS

    yp, ys = x_prompt, jnp.pad(x_sample, ((0, 0), (0, SAMPLE_ROWS - t_s), (0, 0)))
    new_p = [[] for _ in range(5)]
    new_s = [[] for _ in range(5)]
    for l in range(depth):
        wts = _prepare_weights(rel_table, norm_mix[l], w_in[l], q_norm[l], k_norm[l], w_phi[l], b_phi[l],
                               w_glu[l], w_out[l], peer_w_query[l], peer_u[l], peer_v[l])
        ssm = (ssm_lam_re[l], ssm_lam_im[l], ssm_log_dt[l], ssm_b_re[l], ssm_b_im[l], ssm_c_re[l],
               ssm_c_im[l], ssm_d[l])
        shared = (wts, rel_table, norm_mix[l], k_norm[l], b_phi[l], ssm, b_glu[l], out_norm_attn[l],
                  out_norm_ssm[l], norm_ffn[l], peer_sub_keys[l])

        zero_state = jnp.zeros((bp, n_groups, SSM_STATE), F32)
        yp, new_kv, h_last = _layer(yp, t_p, 0, None, None, None, None, (zero_state, zero_state), *shared)
        keep = min(WINDOW, t_p)
        new_p[0].append(new_kv[0])
        new_p[1].append(new_kv[1])
        new_p[2].append(new_kv[2][:, t_p - keep:])
        new_p[3].append(h_last[0])
        new_p[4].append(h_last[1])

        n_phys = cache_kv_cmp.shape[1]
        h0 = (state_ssm_re[l], state_ssm_im[l])
        page_rows = PAGE_SIZE * 2 * N_KV_HEADS
        ys, new_kv, h_last = _layer(ys, t_s, past_len, cache_kv_cmp[l].reshape(n_phys, page_rows, HEAD_DIM),
                                cache_kv_sel[l].reshape(n_phys, page_rows, HEAD_DIM),
                                cache_kv_win[l].reshape(db * WIN_TILES, page_rows, HEAD_DIM), page_table, h0,
                                *shared)
        new_s[0].append(new_kv[0][:, :t_s])
        new_s[1].append(new_kv[1][:, :t_s])
        win_all = jnp.concatenate([cache_kv_win[l], new_kv[2][:, :t_s]], axis=1)
        keep = min(WINDOW, past_len + t_s)
        new_s[2].append(win_all[:, win_all.shape[1] - keep:])
        new_s[3].append(h_last[0])
        new_s[4].append(h_last[1])
    outs_p = [jnp.stack(v) for v in new_p]
    outs_s = [jnp.stack(v) for v in new_s]
    return (yp, ys[:, :t_s], *outs_p, *outs_s)
```

```python
import functools
import math

import numpy as np
import jax
import jax.numpy as jnp
from jax import lax
from jax.experimental import pallas as pl
from jax.experimental.pallas import tpu as pltpu

F32 = jnp.float32
MXU_DTYPE = jnp.bfloat16

HEAD_DIM = 128
N_KV_HEADS = 4
GQA_REP = 4
N_HEADS = N_KV_HEADS * GQA_REP
ATTN_WIDTH = N_HEADS * HEAD_DIM
KV_WIDTH = N_KV_HEADS * HEAD_DIM
ATTN_SCALE = HEAD_DIM ** -0.5
CMP_BLOCK = 32
CMP_STRIDE = 16
CMP_R = CMP_BLOCK // CMP_STRIDE
SLC_BLOCK = 64
N_SELECT = 16
FORCE_BONUS = 1.0e4
WINDOW = 512
NUM_BUCKETS = 32
MAX_DISTANCE = 128
SSM_GROUP = 16
SSM_STATE = 64
PEER_HEADS = 8
PEER_KEYS = 128
PEER_TOPK = 16
PEER_HALF = 128
EPS = 1e-6
NEG_INF = -1e30
PAGE_SIZE = 128

LANES = 128
SUBLANES = 8
VMEM_LIMIT_BYTES = 52 * 1024 * 1024

KEY_TILE = 128
WIN_TILES = WINDOW // KEY_TILE
SSM_CHUNK = 16
SAMPLE_ROWS = 8


def _cparams(sem):
    return pltpu.CompilerParams(dimension_semantics=sem, vmem_limit_bytes=VMEM_LIMIT_BYTES)


def _row_tile(n, pref):
    t = min(n, pref)
    while n % t:
        t -= SUBLANES
    return t


def _dot(a, b):
    return jnp.dot(a, b, preferred_element_type=F32)


def _dot_nt(a, b):
    return lax.dot_general(a, b, (((1,), (1,)), ((), ())), preferred_element_type=F32)


def _gelu(x):
    c = math.sqrt(2.0 / math.pi)
    return 0.5 * x * (1.0 + jnp.tanh(c * (x + 0.044715 * (x * x * x))))


def _sigmoid(x):
    return 1.0 / (1.0 + jnp.exp(-x))


def _bucket_lower_bounds():
    n = np.arange(2 * MAX_DISTANCE)
    max_exact = NUM_BUCKETS // 2
    nf = np.maximum(n, 1).astype(np.float32)
    large = max_exact + (np.log(nf / np.float32(max_exact)) / np.float32(math.log(MAX_DISTANCE / max_exact))
                         * np.float32(NUM_BUCKETS - max_exact)).astype(np.int32)
    bucket = np.where(n < max_exact, n, np.minimum(large, NUM_BUCKETS - 1))
    assert np.all(np.diff(bucket) >= 0) and bucket[-1] == NUM_BUCKETS - 1
    return [int(np.argmax(bucket >= b)) for b in range(NUM_BUCKETS)]


BUCKET_LO = _bucket_lower_bounds()


def _bias_from_dist(dist, table_entry):
    out = table_entry(0)
    for b in range(1, NUM_BUCKETS):
        out = jnp.where(dist >= BUCKET_LO[b], table_entry(b), out)
    return out


def _rmsnorm_kernel(x_ref, g_ref, o_ref):
    x = x_ref[...]
    ms = jnp.mean(x * x, axis=-1, keepdims=True)
    o_ref[...] = (x * lax.rsqrt(ms + EPS) * g_ref[...]).astype(o_ref.dtype)


def _rmsnorm(x, g, out_dtype):
    n, d = x.shape
    tm = _row_tile(n, 256)
    return pl.pallas_call(
        _rmsnorm_kernel,
        grid=(n // tm,),
        in_specs=[pl.BlockSpec((tm, d), lambda i: (i, 0)), pl.BlockSpec((1, d), lambda i: (0, 0))],
        out_specs=pl.BlockSpec((tm, d), lambda i: (i, 0)),
        out_shape=jax.ShapeDtypeStruct((n, d), out_dtype),
        compiler_params=_cparams(("parallel",)),
    )(x, g.reshape(1, d))


def _matmul_kernel(*refs, epilogue):
    a_ref, b_ref = refs[0], refs[1]
    o_ref = refs[-1]
    acc = _dot(a_ref[...], b_ref[...])
    if epilogue == "none":
        o_ref[...] = acc
    elif epilogue == "sigmoid":
        o_ref[...] = _sigmoid(acc)
    elif epilogue == "laneblocks":
        for c in range(acc.shape[1] // LANES):
            o_ref[c] = acc[:, c * LANES:(c + 1) * LANES]
    else:
        raise ValueError(epilogue)


def _matmul(a, b, epilogue="none", extras=(), tm_pref=512, tn_pref=1024):
    m, k = a.shape
    _, n = b.shape
    tm = _row_tile(m, tm_pref)
    tn = min(n, tn_pref)
    while n % tn:
        tn -= LANES
    in_specs = [pl.BlockSpec((tm, k), lambda i, j: (i, 0)), pl.BlockSpec((k, tn), lambda i, j: (0, j))]
    for e in extras:
        if e.shape[0] == 1:
            in_specs.append(pl.BlockSpec((1, tn), lambda i, j: (0, j)))
        else:
            in_specs.append(pl.BlockSpec((tm, tn), lambda i, j: (i, j)))
    if epilogue == "laneblocks":
        out_spec = pl.BlockSpec((tn // LANES, tm, LANES), lambda i, j: (j, i, 0))
        out_shape = jax.ShapeDtypeStruct((n // LANES, m, LANES), F32)
    else:
        out_spec = pl.BlockSpec((tm, tn), lambda i, j: (i, j))
        out_shape = jax.ShapeDtypeStruct((m, n), F32)
    return pl.pallas_call(
        functools.partial(_matmul_kernel, epilogue=epilogue),
        grid=(m // tm, n // tn),
        in_specs=in_specs,
        out_specs=out_spec,
        out_shape=out_shape,
        compiler_params=_cparams(("parallel", "parallel")),
    )(a, b, *extras)


def _in_proj_kernel(a_ref, b_ref, mode_ref, gain_ref, o_ref, *kv_refs, first_kv_tile):
    j = pl.program_id(1)
    acc = _dot(a_ref[...], b_ref[...])
    tm = acc.shape[0]
    blocks = []
    for c in range(acc.shape[1] // HEAD_DIM):
        sl = slice(c * HEAD_DIM, (c + 1) * HEAD_DIM)
        blk = acc[:, sl]
        ms = jnp.mean(blk * blk, axis=-1, keepdims=True)
        nrm = blk * lax.rsqrt(ms + EPS) * gain_ref[:, sl]
        blocks.append(jnp.where(mode_ref[:, sl] > 0.0, nrm, blk))
        o_ref[:, sl] = blocks[-1]
    for b, kv_ref in enumerate(kv_refs):
        @pl.when(j == first_kv_tile + b)
        def _(kv_ref=kv_ref):
            for c, blk in enumerate(blocks):
                kv_ref[pl.ds(c, tm, stride=len(blocks)), :] = blk


def _in_proj(a, b, mode, gain, tm_pref=512):
    m, k = a.shape
    n = b.shape[1]
    tn = 2 * KV_WIDTH
    assert ATTN_WIDTH % tn == 0 and n == ATTN_WIDTH + 3 * tn
    tm = _row_tile(m, tm_pref)
    heads = tn // HEAD_DIM
    vec = pl.BlockSpec((1, tn), lambda i, j: (0, j))
    kv_spec = pl.BlockSpec((tm * heads, HEAD_DIM), lambda i, j: (i, 0))
    kv_shape = jax.ShapeDtypeStruct((m * heads, HEAD_DIM), F32)
    return pl.pallas_call(
        functools.partial(_in_proj_kernel, first_kv_tile=ATTN_WIDTH // tn),
        grid=(m // tm, n // tn),
        in_specs=[pl.BlockSpec((tm, k), lambda i, j: (i, 0)), pl.BlockSpec((k, tn), lambda i, j: (0, j)), vec, vec],
        out_specs=[pl.BlockSpec((tm, tn), lambda i, j: (i, j)), kv_spec, kv_spec, kv_spec],
        out_shape=[jax.ShapeDtypeStruct((m, n), F32), kv_shape, kv_shape, kv_shape],
        compiler_params=_cparams(("parallel", "arbitrary")),
    )(a, b, mode, gain)


def _out_proj_kernel(a1_ref, a2_ref, b1_ref, b2_ref, res_ref, o_ref):
    o_ref[...] = res_ref[...] + _dot(a1_ref[...], b1_ref[...]) + _dot(a2_ref[...], b2_ref[...])


def _out_proj(a1, a2, w, res, tm_pref=512, tn_pref=1024):
    m, k1 = a1.shape
    k2 = a2.shape[1]
    assert k1 == k2 and w.shape[0] == k1 + k2
    n = w.shape[1]
    tm = _row_tile(m, tm_pref)
    tn = min(n, tn_pref)
    return pl.pallas_call(
        _out_proj_kernel,
        grid=(m // tm, n // tn),
        in_specs=[pl.BlockSpec((tm, k1), lambda i, j: (i, 0)), pl.BlockSpec((tm, k2), lambda i, j: (i, 0)),
                  pl.BlockSpec((k1, tn), lambda i, j: (0, j)), pl.BlockSpec((k2, tn), lambda i, j: (1, j)),
                  pl.BlockSpec((tm, tn), lambda i, j: (i, j))],
        out_specs=pl.BlockSpec((tm, tn), lambda i, j: (i, j)),
        out_shape=jax.ShapeDtypeStruct((m, n), F32),
        compiler_params=_cparams(("parallel", "parallel")),
    )(a1, a2, w, w, res)


CMP_PAGES_PER_STEP = 16


def _compress_kernel(pt_ref, *refs):
    del pt_ref
    pages = refs[:CMP_PAGES_PER_STEP]
    w_ref = refs[CMP_PAGES_PER_STEP]
    o_ref = refs[CMP_PAGES_PER_STEP + 1]
    x_scr = refs[CMP_PAGES_PER_STEP + 2]
    chunks_per_page = PAGE_SIZE // CMP_STRIDE
    rows_per_head = CMP_PAGES_PER_STEP * chunks_per_page
    heads_per_row = 2 * N_KV_HEADS
    for c in range(2):
        for g in range(N_KV_HEADS):
            for i in range(CMP_PAGES_PER_STEP):
                r0 = g * rows_per_head + i * chunks_per_page
                for s in range(CMP_STRIDE):
                    x_scr[r0:r0 + chunks_per_page, s * HEAD_DIM:(s + 1) * HEAD_DIM] = pages[i][
                        0, pl.ds(s * heads_per_row + c * N_KV_HEADS + g, chunks_per_page,
                                 stride=CMP_STRIDE * heads_per_row), :]
        res = _dot(x_scr[...].astype(MXU_DTYPE), w_ref[c])
        for g in range(N_KV_HEADS):
            o_ref[0, c * N_KV_HEADS + g] = res[g * rows_per_head:(g + 1) * rows_per_head]


def _compress(rows3, page_table, w_cmp):
    nb, n_pages = page_table.shape
    assert n_pages % CMP_PAGES_PER_STEP == 0
    n_steps = n_pages // CMP_PAGES_PER_STEP
    chunks_per_step = CMP_PAGES_PER_STEP * PAGE_SIZE // CMP_STRIDE
    n_chunks = n_pages * PAGE_SIZE // CMP_STRIDE

    def page_spec(i):
        return pl.BlockSpec((1, PAGE_SIZE * 2 * N_KV_HEADS, HEAD_DIM),
                            lambda b, j, pt: (pt[b, j * CMP_PAGES_PER_STEP + i], 0, 0))

    grid_spec = pltpu.PrefetchScalarGridSpec(
        num_scalar_prefetch=1,
        grid=(nb, n_steps),
        in_specs=[page_spec(i) for i in range(CMP_PAGES_PER_STEP)]
        + [pl.BlockSpec((2, CMP_STRIDE * HEAD_DIM, CMP_R * HEAD_DIM), lambda b, j, pt: (0, 0, 0))],
        out_specs=pl.BlockSpec((1, 2 * N_KV_HEADS, chunks_per_step, CMP_R * HEAD_DIM),
                               lambda b, j, pt: (b, 0, j, 0)),
        scratch_shapes=[pltpu.VMEM((N_KV_HEADS * chunks_per_step, CMP_STRIDE * HEAD_DIM), F32)],
    )
    return pl.pallas_call(
        _compress_kernel,
        grid_spec=grid_spec,
        out_shape=jax.ShapeDtypeStruct((nb, 2 * N_KV_HEADS, n_chunks, CMP_R * HEAD_DIM), F32),
        compiler_params=_cparams(("parallel", "arbitrary")),
    )(page_table, *([rows3] * CMP_PAGES_PER_STEP), w_cmp)


def _cmp_attn_kernel(tab_ref, q_ref, kp_ref, vp_ref, bphi_ref, kn_ref, m_ref, o_ref, sel_ref, bias_scr,
                     *, tq, pos0, n_cmp, n_slc):
    g = pl.program_id(0)
    i = pl.program_id(1)
    n_chunks = kp_ref.shape[2]
    kp = kp_ref[0, 0]
    vp = vp_ref[0, 0]
    k_c = bphi_ref[0:1, :] + kp[:, :HEAD_DIM] + pltpu.roll(kp[:, HEAD_DIM:], n_chunks - 1, 0)
    v_c = bphi_ref[1:2, :] + vp[:, :HEAD_DIM] + pltpu.roll(vp[:, HEAD_DIM:], n_chunks - 1, 0)
    ms = jnp.mean(k_c * k_c, axis=-1, keepdims=True)
    k_c = k_c * lax.rsqrt(ms + EPS) * kn_ref[...]

    q = jnp.concatenate([q_ref[0, :, r * HEAD_DIM:(r + 1) * HEAD_DIM] for r in range(GQA_REP)], axis=0)
    logits = _dot_nt(q.astype(MXU_DTYPE), k_c.astype(MXU_DTYPE)) * ATTN_SCALE
    row = lax.broadcasted_iota(jnp.int32, (tq, n_chunks), 0)
    col = lax.broadcasted_iota(jnp.int32, (tq, n_chunks), 1)
    dist = pos0 + i * tq + row - (col * CMP_STRIDE + (CMP_BLOCK - 1))

    @pl.when(pl.program_id(2) == 0)
    def _():
        bias_scr[...] = jnp.concatenate(
            [_bias_from_dist(dist, lambda b, r=r: tab_ref[g * GQA_REP + r, b]) for r in range(GQA_REP)], axis=0)

    bias = bias_scr[...]
    valid = (dist >= 0) & (col < n_cmp)
    valid = jnp.concatenate([valid] * GQA_REP, axis=0)
    logits = jnp.where(valid, logits + bias, NEG_INF)
    mx = jnp.max(logits, axis=-1, keepdims=True)
    e = jnp.exp(logits - mx)
    any_valid = (jnp.max(jnp.where(valid, 1.0, 0.0), axis=-1, keepdims=True))
    p = e / jnp.sum(e, axis=-1, keepdims=True) * any_valid
    out = _dot(p.astype(MXU_DTYPE), v_c.astype(MXU_DTYPE))
    for r in range(GQA_REP):
        o_ref[0, :, r * HEAD_DIM:(r + 1) * HEAD_DIM] = out[r * tq:(r + 1) * tq]

    psum = p[0:tq]
    for r in range(1, GQA_REP):
        psum = psum + p[r * tq:(r + 1) * tq]
    hi = psum.astype(MXU_DTYPE)
    lo = (psum - hi.astype(F32)).astype(MXU_DTYPE)
    imp = _dot(hi, m_ref[...]) + _dot(lo, m_ref[...])
    n_pad = imp.shape[1]
    j = lax.broadcasted_iota(jnp.int32, (tq, n_pad), 1)
    qp = pos0 + i * tq + lax.broadcasted_iota(jnp.int32, (tq, n_pad), 0)
    cur = jnp.right_shift(qp, int(math.log2(SLC_BLOCK)))
    ok = j * SLC_BLOCK <= qp
    forced = (j == 0) | (j == cur) | (j == cur - 1)
    score = jnp.where(ok, imp + jnp.where(forced, FORCE_BONUS, 0.0), NEG_INF)
    rank = jnp.zeros((tq, n_pad), F32)
    for s in range(n_slc):
        cs = score[:, s:s + 1]
        beats = (cs > score) | ((cs == score) & (j > s))
        rank = rank + jnp.where(beats, 1.0, 0.0)
    selected = (rank < float(min(N_SELECT, n_slc))) & (score > 0.5 * NEG_INF)
    sel_ref[0, 0] = jnp.where(selected, 1.0, 0.0)


def _cmp_attention(tab, p3, pk, bphi, k_norm0, m_mat, *, pos0, n_cmp, n_slc, tq):
    nb, t, _ = p3.shape
    n_chunks = pk.shape[2]
    n_pad = m_mat.shape[1]
    nq = t // tq
    kern = functools.partial(_cmp_attn_kernel, tq=tq, pos0=pos0, n_cmp=n_cmp, n_slc=n_slc)
    return pl.pallas_call(
        kern,
        grid=(N_KV_HEADS, nq, nb),
        in_specs=[
            pl.BlockSpec(memory_space=pltpu.SMEM),
            pl.BlockSpec((1, tq, GQA_REP * HEAD_DIM), lambda g, i, b: (b, i, g)),
            pl.BlockSpec((1, 1, n_chunks, CMP_R * HEAD_DIM), lambda g, i, b: (b, g, 0, 0)),
            pl.BlockSpec((1, 1, n_chunks, CMP_R * HEAD_DIM), lambda g, i, b: (b, N_KV_HEADS + g, 0, 0)),
            pl.BlockSpec((2, HEAD_DIM), lambda g, i, b: (0, 0)),
            pl.BlockSpec((1, HEAD_DIM), lambda g, i, b: (0, 0)),
            pl.BlockSpec((n_chunks, n_pad), lambda g, i, b: (0, 0)),
        ],
        out_specs=[
            pl.BlockSpec((1, tq, GQA_REP * HEAD_DIM), lambda g, i, b: (b, i, g)),
            pl.BlockSpec((1, 1, tq, n_pad), lambda g, i, b: (b, g, i, 0)),
        ],
        out_shape=[
            jax.ShapeDtypeStruct((nb, t, ATTN_WIDTH), F32),
            jax.ShapeDtypeStruct((nb, N_KV_HEADS, t, n_pad), F32),
        ],
        scratch_shapes=[pltpu.VMEM((GQA_REP * tq, n_chunks), F32)],
        compiler_params=_cparams(("parallel", "parallel", "arbitrary")),
    )(tab, p3, pk, pk, bphi, k_norm0, m_mat)


def _online_softmax_step(s, v, m_scr, l_scr, acc_scr):
    m_old = m_scr[...]
    m_new = jnp.maximum(m_old, jnp.max(s, axis=-1, keepdims=True))
    alpha = jnp.exp(m_old - m_new)
    p = jnp.exp(s - m_new)
    l_scr[...] = alpha * l_scr[...] + jnp.sum(p, axis=-1, keepdims=True)
    acc_scr[...] = alpha * acc_scr[...] + _dot(p.astype(MXU_DTYPE), v)
    m_scr[...] = m_new


ATTN_TILES_PER_ITER = 4


def _attn_prompt_kernel(*refs, mode, n_kt):
    if mode == "sel":
        q_ref, k_ref, v_ref, mb_ref, sel_ref, e_ref, o_ref, kb_scr, vt_scr, m_scr, l_scr, acc_scr = refs
    else:
        q_ref, k_ref, v_ref, mb_ref, o_ref, kb_scr, vt_scr, m_scr, l_scr, acc_scr = refs
    tq = KEY_TILE
    qt = pl.program_id(2)

    @pl.when(qt == 0)
    def _():
        for kt in range(n_kt):
            rows = slice(kt * KEY_TILE, (kt + 1) * KEY_TILE)
            kb_scr[kt] = k_ref[0, rows, :].astype(MXU_DTYPE)
            vt_scr[kt] = v_ref[0, rows, :].T.astype(MXU_DTYPE)

    q = jnp.concatenate([q_ref[0, :, r * HEAD_DIM:(r + 1) * HEAD_DIM] for r in range(GQA_REP)], axis=0)
    q = (q * ATTN_SCALE).astype(MXU_DTYPE)
    m_scr[...] = jnp.full(m_scr.shape, NEG_INF, F32)
    l_scr[...] = jnp.zeros(l_scr.shape, F32)
    acc_scr[...] = jnp.zeros(acc_scr.shape, F32)
    if mode == "sel":
        sel = sel_ref[0, 0].astype(MXU_DTYPE)

    n_kinds = mb_ref.shape[0]

    def logits(kt):
        d = qt - kt
        if mode == "sel":
            kind = jnp.where(d < 0, n_kinds - 1, jnp.minimum(d, 2))
        else:
            kind = jnp.where(kt < 0, n_kinds - 1, jnp.where(d == WIN_TILES, 3, jnp.minimum(d, 2)))
        kt = jnp.clip(kt, 0, n_kt - 1)
        s = _dot_nt(kb_scr[kt], q) + mb_ref[kind, 0]
        if mode == "sel":
            chosen = _dot_nt(e_ref[kt], sel)
            s = s + jnp.concatenate([(1.0 - chosen) * NEG_INF] * GQA_REP, axis=1)
        return kt, s

    def update(key_tiles):
        tiles = [logits(kt) for kt in key_tiles]
        m_old = m_scr[...]
        m_new = m_old
        for _, s in tiles:
            m_new = jnp.maximum(m_new, jnp.max(s, axis=0, keepdims=True))
        alpha = jnp.exp(m_old - m_new)
        l_new = alpha * l_scr[...]
        acc = alpha * acc_scr[...]
        for kt, s in tiles:
            p = jnp.exp(s - m_new)
            l_new = l_new + jnp.sum(p, axis=0, keepdims=True)
            acc = acc + _dot(vt_scr[kt], p.astype(MXU_DTYPE))
        l_scr[...] = l_new
        acc_scr[...] = acc
        m_scr[...] = m_new

    if mode == "sel":
        def body(it, carry):
            update([it * ATTN_TILES_PER_ITER + u for u in range(ATTN_TILES_PER_ITER)])
            return carry

        lax.fori_loop(0, (qt + ATTN_TILES_PER_ITER) // ATTN_TILES_PER_ITER, body, 0)
    else:
        update([qt - WIN_TILES + u for u in range(WIN_TILES + 1)])
    out = acc_scr[...] / l_scr[...]
    for r in range(GQA_REP):
        o_ref[0, :, r * HEAD_DIM:(r + 1) * HEAD_DIM] = out[:, r * tq:(r + 1) * tq].T


def _prompt_bias_tiles(tab, mode):
    ii = jnp.arange(KEY_TILE, dtype=jnp.int32)
    entry = lambda b: tab[:, b][:, None, None]

    def tile(offset):
        dist = (offset * KEY_TILE + ii[:, None] - ii[None, :])[None]
        return dist, jnp.broadcast_to(_bias_from_dist(dist, entry), (N_HEADS, KEY_TILE, KEY_TILE))

    d0, b0 = tile(0)
    tiles = [jnp.where(d0 >= 0, b0, NEG_INF), tile(1)[1], tile(2)[1]]
    if mode == "win":
        d4, b4 = tile(WIN_TILES)
        tiles.append(jnp.where(d4 <= WINDOW, b4, NEG_INF))
    tiles.append(jnp.full_like(b0, NEG_INF))
    mb = jnp.stack(tiles).reshape(len(tiles), N_KV_HEADS, GQA_REP, KEY_TILE, KEY_TILE)
    return mb.transpose(0, 1, 4, 2, 3).reshape(len(tiles), N_KV_HEADS, KEY_TILE, GQA_REP * KEY_TILE)


def _attn_prompt(p3, tab, mode, k_col, v_col, sel=None, e3=None):
    nb, t, _ = p3.shape
    tq = KEY_TILE
    nq = t // tq
    mb = _prompt_bias_tiles(tab, mode)
    cols = GQA_REP * tq
    in_specs = [
        pl.BlockSpec((1, tq, GQA_REP * HEAD_DIM), lambda b, g, i: (b, i, g)),
        pl.BlockSpec((1, t, HEAD_DIM), lambda b, g, i: (b, 0, k_col + g)),
        pl.BlockSpec((1, t, HEAD_DIM), lambda b, g, i: (b, 0, v_col + g)),
        pl.BlockSpec((mb.shape[0], 1, KEY_TILE, cols), lambda b, g, i: (0, g, 0, 0)),
    ]
    args = [p3, p3, p3, mb]
    if mode == "sel":
        n_pad = sel.shape[-1]
        in_specs += [
            pl.BlockSpec((1, 1, tq, n_pad), lambda b, g, i: (b, g, i, 0)),
            pl.BlockSpec((nq, KEY_TILE, n_pad), lambda b, g, i: (0, 0, 0)),
        ]
        args += [sel, e3]
    return pl.pallas_call(
        functools.partial(_attn_prompt_kernel, mode=mode, n_kt=nq),
        grid=(nb, N_KV_HEADS, nq),
        in_specs=in_specs,
        out_specs=pl.BlockSpec((1, tq, GQA_REP * HEAD_DIM), lambda b, g, i: (b, i, g)),
        out_shape=jax.ShapeDtypeStruct((nb, t, ATTN_WIDTH), F32),
        scratch_shapes=[pltpu.VMEM((nq, KEY_TILE, HEAD_DIM), MXU_DTYPE),
                        pltpu.VMEM((nq, HEAD_DIM, KEY_TILE), MXU_DTYPE),
                        pltpu.VMEM((1, cols), F32), pltpu.VMEM((1, cols), F32),
                        pltpu.VMEM((HEAD_DIM, cols), F32)],
        compiler_params=_cparams(("parallel", "parallel", "arbitrary")),
    )(*args)


def _attn_sample_kernel(*refs, mode, paged, pages, n_steps, tq, pos0, base_pos):
    it = iter(refs)
    if paged:
        next(it)
    q_ref = next(it)
    page_refs = [next(it) for _ in range(pages)]
    new_ref, rowtab_ref = next(it), next(it)
    if mode == "sel":
        sel_ref, e_ref = next(it), next(it)
    o_ref, m_scr, l_scr, acc_scr = next(it), next(it), next(it), next(it)
    kk = pl.program_id(1)
    rows = N_HEADS * tq
    heads_per_row = 2 * N_KV_HEADS

    @pl.when(kk == 0)
    def _():
        m_scr[...] = jnp.full(m_scr.shape, NEG_INF, F32)
        l_scr[...] = jnp.zeros(l_scr.shape, F32)
        acc_scr[...] = jnp.zeros(acc_scr.shape, F32)

    def queries():
        zero = jnp.zeros((tq, HEAD_DIM), F32)
        blocks = []
        for h in range(N_HEADS):
            qh = q_ref[0, :, h * HEAD_DIM:(h + 1) * HEAD_DIM] * ATTN_SCALE
            blocks.append(jnp.concatenate([qh if g == h // GQA_REP else zero for g in range(N_KV_HEADS)], axis=1))
        return jnp.concatenate(blocks, axis=0).astype(MXU_DTYPE)

    def step(k_all, v_all, key_pos0, chosen_cols, exact_bias):
        n_keys = k_all.shape[0]
        s = _dot_nt(queries(), k_all)
        row = lax.broadcasted_iota(jnp.int32, (rows, n_keys), 0) & (tq - 1)
        col = lax.broadcasted_iota(jnp.int32, (rows, n_keys), 1)
        dist = pos0 + row - (key_pos0 + col)
        if exact_bias:
            bias = _bias_from_dist(dist, lambda b: rowtab_ref[:, b:b + 1])
        else:
            bias = rowtab_ref[:, NUM_BUCKETS - 1:NUM_BUCKETS]
        if mode == "sel":
            sel = sel_ref[0].reshape(N_KV_HEADS * tq, sel_ref.shape[-1]).astype(MXU_DTYPE)
            chosen = _dot(sel, chosen_cols)
            chosen = jnp.concatenate(
                [chosen[g * tq:(g + 1) * tq] for g in range(N_KV_HEADS) for _ in range(GQA_REP)], axis=0)
            mask = (dist >= 0) & (chosen > 0.5)
        else:
            mask = (dist >= 0) & (dist <= WINDOW)
        s = jnp.where(mask, s + bias, NEG_INF)
        _online_softmax_step(s, v_all, m_scr, l_scr, acc_scr)

    def cached(kv):
        return jnp.concatenate([
            jnp.concatenate([pg[0, pl.ds(kv * N_KV_HEADS + g, PAGE_SIZE, stride=heads_per_row), :]
                             for g in range(N_KV_HEADS)], axis=1)
            for pg in page_refs], axis=0).astype(MXU_DTYPE)

    keys_per_step = pages * PAGE_SIZE
    key_pos0 = base_pos + kk * keys_per_step
    far = pos0 - (key_pos0 + keys_per_step - 1) >= MAX_DISTANCE
    is_cache = kk < n_steps
    chosen_cols = e_ref[...] if mode == "sel" else None

    @pl.when(is_cache & far)
    def _():
        step(cached(0), cached(1), key_pos0, chosen_cols, False)

    @pl.when(is_cache & jnp.logical_not(far))
    def _():
        step(cached(0), cached(1), key_pos0, chosen_cols, True)

    @pl.when(kk == n_steps)
    def _():
        pad = jnp.zeros((KEY_TILE - tq, N_KV_HEADS * HEAD_DIM), F32)
        k_new = jnp.concatenate([new_ref[0, :, :KV_WIDTH], pad], axis=0).astype(MXU_DTYPE)
        v_new = jnp.concatenate([new_ref[0, :, KV_WIDTH:], pad], axis=0).astype(MXU_DTYPE)
        cols = e_ref[:, :KEY_TILE] if mode == "sel" else None
        step(k_new, v_new, pos0, cols, True)
        out = acc_scr[...] / l_scr[...]
        for h in range(N_HEADS):
            g = h // GQA_REP
            o_ref[0, :, h * HEAD_DIM:(h + 1) * HEAD_DIM] = out[h * tq:(h + 1) * tq, g * HEAD_DIM:(g + 1) * HEAD_DIM]


def _attn_sample(p3, cache_rows, page_table, tab, mode, new_col, pos0, base_pos, sel=None, e_mat=None):
    nb, tq, _ = p3.shape
    paged = page_table is not None
    pages = 8 if paged else WIN_TILES
    n_tiles = page_table.shape[1] if paged else WIN_TILES
    assert n_tiles % pages == 0
    n_steps = n_tiles // pages
    rows = N_HEADS * tq
    rowtab = jnp.repeat(tab, tq, axis=0)

    def page_spec(i):
        def index(b, kk, *pt):
            page = jnp.minimum(kk, n_steps - 1) * pages + i
            return (pt[0][b, page] if paged else b * n_tiles + page, 0, 0)
        return pl.BlockSpec((1, PAGE_SIZE * 2 * N_KV_HEADS, HEAD_DIM), index)

    def fixed(shape, fn):
        return pl.BlockSpec(shape, lambda b, kk, *_: fn(b, kk))

    in_specs = [fixed((1, tq, ATTN_WIDTH), lambda b, kk: (b, 0, 0))]
    in_specs += [page_spec(i) for i in range(pages)]
    in_specs += [fixed((1, tq, 2 * KV_WIDTH), lambda b, kk: (b, 0, new_col)),
                 fixed((rows, NUM_BUCKETS), lambda b, kk: (0, 0))]
    args = [p3] + [cache_rows] * pages + [p3, rowtab]
    if mode == "sel":
        n_pad = sel.shape[-1]
        in_specs += [fixed((1, N_KV_HEADS, tq, n_pad), lambda b, kk: (b, 0, 0, 0)),
                     fixed((n_pad, pages * PAGE_SIZE), lambda b, kk: (0, kk))]
        args += [sel, e_mat]
    kern = functools.partial(_attn_sample_kernel, mode=mode, paged=paged, pages=pages, n_steps=n_steps, tq=tq,
                             pos0=pos0, base_pos=base_pos)
    out_spec = fixed((1, tq, ATTN_WIDTH), lambda b, kk: (b, 0, 0))
    out_shape = jax.ShapeDtypeStruct((nb, tq, ATTN_WIDTH), F32)
    scratch = [pltpu.VMEM((rows, 1), F32), pltpu.VMEM((rows, 1), F32),
               pltpu.VMEM((rows, N_KV_HEADS * HEAD_DIM), F32)]
    cp = _cparams(("parallel", "arbitrary"))
    grid = (nb, n_steps + 1)
    if paged:
        grid_spec = pltpu.PrefetchScalarGridSpec(num_scalar_prefetch=1, grid=grid, in_specs=in_specs,
                                                 out_specs=out_spec, scratch_shapes=scratch)
        return pl.pallas_call(kern, grid_spec=grid_spec, out_shape=out_shape, compiler_params=cp)(
            page_table, *args)
    return pl.pallas_call(kern, grid=grid, in_specs=in_specs, out_specs=out_spec, out_shape=out_shape,
                          scratch_shapes=scratch, compiler_params=cp)(*args)


def _combine_kernel(oc_ref, os_ref, ow_ref, gate_ref, g_ref, o_ref):
    gates = gate_ref[...]
    parts = []
    for h in range(N_HEADS):
        sl = slice(h * HEAD_DIM, (h + 1) * HEAD_DIM)
        parts.append(gates[:, h:h + 1] * oc_ref[:, sl]
                     + gates[:, N_HEADS + h:N_HEADS + h + 1] * os_ref[:, sl]
                     + gates[:, 2 * N_HEADS + h:2 * N_HEADS + h + 1] * ow_ref[:, sl])
    o = jnp.concatenate(parts, axis=1)
    ms = jnp.mean(o * o, axis=-1, keepdims=True)
    o_ref[...] = (o * lax.rsqrt(ms + EPS) * g_ref[...]).astype(o_ref.dtype)


def _combine(o_cmp, o_sel, o_win, gates, gain):
    n, w = o_cmp.shape
    tm = _row_tile(n, 256)
    row = pl.BlockSpec((tm, w), lambda i: (i, 0))
    return pl.pallas_call(
        _combine_kernel,
        grid=(n // tm,),
        in_specs=[row, row, row, pl.BlockSpec((tm, LANES), lambda i: (i, 0)),
                  pl.BlockSpec((1, w), lambda i: (0, 0))],
        out_specs=row,
        out_shape=jax.ShapeDtypeStruct((n, w), MXU_DTYPE),
        compiler_params=_cparams(("parallel",)),
    )(o_cmp, o_sel, o_win, gates, gain.reshape(1, w))


SSM_LANE_GROUPS = LANES // SSM_GROUP
SSM_BLOCK_STATE = SSM_LANE_GROUPS * 2 * SSM_STATE


def _ssm_kernel(u_ref, kt_ref, bs_ref, cs_ref, d_ref, la_ref, lb_ref, h0_ref, y_ref, hout_ref,
                us_scr, s_scr, hs_scr, *, n_seq, n_chunk):
    L = SSM_CHUNK
    rows = n_seq * n_chunk
    st = SSM_BLOCK_STATE
    group_shift = int(math.log2(SSM_GROUP))
    state_shift = int(math.log2(2 * SSM_STATE))
    in_mask = (jnp.right_shift(lax.broadcasted_iota(jnp.int32, (LANES, st), 0), group_shift)
               == jnp.right_shift(lax.broadcasted_iota(jnp.int32, (LANES, st), 1), state_shift))
    out_mask = (jnp.right_shift(lax.broadcasted_iota(jnp.int32, (st, LANES), 0), state_shift)
                == jnp.right_shift(lax.broadcasted_iota(jnp.int32, (st, LANES), 1), group_shift))
    zero = jnp.zeros((), MXU_DTYPE)

    s_acc = jnp.zeros((rows, st), F32)
    for s in range(L):
        us = u_ref[0, pl.ds(s, rows, stride=L), :].astype(MXU_DTYPE)
        us_scr[s] = us
        b_full = jnp.where(in_mask, jnp.concatenate([bs_ref[0, s]] * SSM_LANE_GROUPS, axis=1), zero)
        s_acc = s_acc + _dot(us, b_full)
    s_scr[...] = s_acc

    la = la_ref[0]
    lb = lb_ref[0]
    is_re = (lax.broadcasted_iota(jnp.int32, (1, st), 1) & SSM_STATE) == 0

    def swap(h):
        return jnp.where(is_re, pltpu.roll(h, st - SSM_STATE, 1), pltpu.roll(h, SSM_STATE, 1))

    def body(c, hs):
        new = []
        for b in range(n_seq):
            row = b * n_chunk + c
            hs_scr[pl.ds(row, 1), :] = hs[b]
            new.append(la * hs[b] + lb * swap(hs[b]) + s_scr[pl.ds(row, 1), :])
        return tuple(new)

    hs = lax.fori_loop(0, n_chunk, body, tuple(h0_ref[0, b:b + 1, :] for b in range(n_seq)))
    for b in range(n_seq):
        hout_ref[0, b:b + 1, :] = hs[b]

    h_start = hs_scr[...].astype(MXU_DTYPE)
    for t in range(L):
        c_full = jnp.where(out_mask, jnp.concatenate([cs_ref[0, t]] * SSM_LANE_GROUPS, axis=0), zero)
        acc = _dot(h_start, c_full)
        for s in range(t + 1):
            acc = acc + _dot(us_scr[s], kt_ref[0, t - s])
        acc = acc + d_ref[0] * u_ref[0, pl.ds(t, rows, stride=L), :]
        y_ref[0, pl.ds(t, rows, stride=L), :] = acc


def _ssm(u_lb, kt, bs, cs, dvec, la, lb, h0, *, n_seq, n_chunk):
    nblk, n, w = u_lb.shape
    L = SSM_CHUNK
    rows = n_seq * n_chunk
    assert n == rows * L and w == LANES
    st = SSM_BLOCK_STATE

    def blk(shape):
        return pl.BlockSpec((1,) + shape, lambda i: (i,) + (0,) * len(shape))

    return pl.pallas_call(
        functools.partial(_ssm_kernel, n_seq=n_seq, n_chunk=n_chunk),
        grid=(nblk,),
        in_specs=[blk((n, w)), blk((L, w, w)), blk((L, w, w)), blk((L, w, w)), blk((1, w)), blk((1, st)),
                  blk((1, st)), blk((n_seq, st))],
        out_specs=[blk((n, w)), blk((n_seq, st))],
        out_shape=[jax.ShapeDtypeStruct((nblk, n, w), F32), jax.ShapeDtypeStruct((nblk, n_seq, st), F32)],
        scratch_shapes=[pltpu.VMEM((L, rows, w), MXU_DTYPE), pltpu.VMEM((rows, st), F32),
                        pltpu.VMEM((rows, st), F32)],
        compiler_params=_cparams(("parallel",)),
    )(u_lb, kt, bs, cs, dvec, la, lb, h0)


def _ssm_matrices(lam_re, lam_im, log_dt, b_re, b_im, c_re, c_im, d_skip, l_eff):
    g = lam_re.shape[0]
    L = SSM_CHUNK
    lg = SSM_LANE_GROUPS
    nblk = g // lg
    lam = lax.complex(lam_re.astype(F32), lam_im.astype(F32))
    dt = jnp.exp(log_dt.astype(F32))[:, None]
    lam_bar = jnp.exp(lam * dt)
    b_bar = ((lam_bar - 1.0) / lam)[..., None] * lax.complex(b_re.astype(F32), b_im.astype(F32))
    c_c = lax.complex(c_re.astype(F32), c_im.astype(F32))
    pw = [jnp.ones_like(lam_bar)]
    for _ in range(L):
        pw.append(pw[-1] * lam_bar)
    pw = jnp.stack(pw)
    pw_b = pw.reshape(L + 1, nblk, lg, SSM_STATE)
    bb = b_bar.reshape(nblk, lg, SSM_STATE, SSM_GROUP).transpose(0, 1, 3, 2)
    chan_group = np.arange(LANES) // SSM_GROUP
    to_chan = jnp.asarray(chan_group[None, :] == np.arange(lg)[:, None], F32)
    out_lane = jnp.asarray(np.arange(LANES)[None, :] % SSM_GROUP == np.arange(SSM_GROUP)[:, None], F32)
    same_group = jnp.asarray(chan_group[:, None] == chan_group[None, :], F32)
    kern = jnp.einsum("gop,kgp,gpi->gkio", c_c, pw[:L], b_bar).real
    kern = kern.reshape(nblk, lg, L, SSM_GROUP, SSM_GROUP).transpose(0, 2, 1, 3, 4)
    kern = kern.reshape(nblk, L, LANES, SSM_GROUP)
    kt = jnp.einsum("ltro,oc->ltrc", kern, out_lane, precision=lax.Precision.HIGHEST) * same_group
    exps = np.clip(l_eff - 1 - np.arange(L), 0, L)
    bx = pw_b[exps].transpose(1, 0, 2, 3)[:, :, :, None, :] * bb[:, None]
    bx = jnp.where(jnp.asarray(np.arange(L) < l_eff)[None, :, None, None, None], bx, 0.0)
    bs = jnp.concatenate([bx.real, bx.imag], axis=-1).reshape(nblk, L, LANES, 2 * SSM_STATE)
    c_t = c_c.reshape(nblk, LANES, SSM_STATE).transpose(0, 2, 1)
    pw_t = pw_b[1:L + 1].transpose(1, 0, 3, 2)
    expand = lambda a: jnp.einsum("ltpg,gc->ltpc", a, to_chan, precision=lax.Precision.HIGHEST)
    pr, pi = expand(pw_t.real), expand(pw_t.imag)
    cr, ci = c_t.real[:, None], c_t.imag[:, None]
    cs = jnp.concatenate([cr * pr - ci * pi, -(cr * pi + ci * pr)], axis=2)
    lam_l = pw[l_eff].reshape(nblk, lg, SSM_STATE)
    la = jnp.stack([lam_l.real, lam_l.real], axis=2).reshape(nblk, 1, SSM_BLOCK_STATE)
    lb = jnp.stack([-lam_l.imag, lam_l.imag], axis=2).reshape(nblk, 1, SSM_BLOCK_STATE)
    dvec = d_skip.astype(F32).reshape(nblk, 1, LANES)
    return kt.astype(MXU_DTYPE), bs.astype(MXU_DTYPE), cs.astype(MXU_DTYPE), dvec, la, lb


def _state_to_blocks(re, im):
    n_seq, g, p = re.shape
    h = jnp.stack([re, im], axis=2).astype(F32).reshape(n_seq, g // SSM_LANE_GROUPS, SSM_BLOCK_STATE)
    return h.transpose(1, 0, 2)


def _state_from_blocks(h):
    nblk, n_seq, _ = h.shape
    h = h.transpose(1, 0, 2).reshape(n_seq, nblk * SSM_LANE_GROUPS, 2, SSM_STATE)
    return h[:, :, 0], h[:, :, 1]


def _glu_kernel(y_ref, w_ref, b_ref, g_ref, o_ref):
    gl = _gelu(jnp.concatenate([y_ref[k] for k in range(y_ref.shape[0])], axis=1))
    z = _dot(gl.astype(MXU_DTYPE), w_ref[...]) + b_ref[...]
    o = gl * _sigmoid(z)
    ms = jnp.mean(o * o, axis=-1, keepdims=True)
    o_ref[...] = (o * lax.rsqrt(ms + EPS) * g_ref[...]).astype(o_ref.dtype)


def _glu(y_lb, w, b, gain):
    nblk, n, _ = y_lb.shape
    d = nblk * LANES
    tm = _row_tile(n, 256)
    vec = pl.BlockSpec((1, d), lambda i: (0, 0))
    return pl.pallas_call(
        _glu_kernel,
        grid=(n // tm,),
        in_specs=[pl.BlockSpec((nblk, tm, LANES), lambda i: (0, i, 0)), pl.BlockSpec((d, d), lambda i: (0, 0)),
                  vec, vec],
        out_specs=pl.BlockSpec((tm, d), lambda i: (i, 0)),
        out_shape=jax.ShapeDtypeStruct((n, d), MXU_DTYPE),
        compiler_params=_cparams(("parallel",)),
    )(y_lb, w, b.reshape(1, d), gain.reshape(1, d))


def _top_rows(x, k):
    n = x.shape[0]
    idx_iota = lax.broadcasted_iota(jnp.int32, x.shape, 0)
    vals, idxs = [], []
    cur = x
    for _ in range(k):
        m = jnp.max(cur, axis=0, keepdims=True)
        ix = jnp.min(jnp.where(cur == m, idx_iota, n), axis=0, keepdims=True)
        vals.append(m)
        idxs.append(ix)
        cur = jnp.where(idx_iota == ix, -jnp.inf, cur)
    return jnp.concatenate(vals, axis=0), jnp.concatenate(idxs, axis=0)


def _pick_rows(table, sel):
    out = jnp.zeros(sel.shape, table.dtype)
    for a in range(table.shape[0]):
        out = out + jnp.where(sel == a, table[a:a + 1, :], 0)
    return out


def _peer_topk_kernel(q_ref, keys_ref, i1_ref, i2_ref, gate_ref):
    q = q_ref[...].astype(MXU_DTYPE)
    s1 = _dot_nt(keys_ref[0, 0].astype(MXU_DTYPE), q[:, :PEER_HALF])
    s2 = _dot_nt(keys_ref[0, 1].astype(MXU_DTYPE), q[:, PEER_HALF:])
    v1, x1 = _top_rows(s1, PEER_TOPK)
    v2, x2 = _top_rows(s2, PEER_TOPK)
    t = v1.shape[1]
    pairs = [(a, b) for a in range(PEER_TOPK) for b in range(PEER_TOPK) if (a + 1) * (b + 1) <= PEER_TOPK]
    n_pad = -len(pairs) % SUBLANES
    cand = jnp.concatenate([v1[a:a + 1] + v2[b:b + 1] for a, b in pairs]
                           + [jnp.full((n_pad, t), -jnp.inf, F32)], axis=0)
    top, pos = _top_rows(cand, PEER_TOPK)
    i1_ref[0] = _pick_rows(jnp.concatenate([x1[a:a + 1] for a, _ in pairs], axis=0), pos)
    i2_ref[0] = _pick_rows(jnp.concatenate([x2[b:b + 1] for _, b in pairs], axis=0), pos)
    e = jnp.exp(top - jnp.max(top, axis=0, keepdims=True))
    gate_ref[0] = e / jnp.sum(e, axis=0, keepdims=True)


def _peer_topk(q, sub_keys):
    n = q.shape[0]
    tt = _row_tile(n, 512)
    out = pl.BlockSpec((1, PEER_TOPK, tt), lambda i, h: (h, 0, i))
    shp = (PEER_HEADS, PEER_TOPK, n)
    return pl.pallas_call(
        _peer_topk_kernel,
        grid=(n // tt, PEER_HEADS),
        in_specs=[pl.BlockSpec((tt, 2 * PEER_HALF), lambda i, h: (i, h)),
                  pl.BlockSpec((1, 2, PEER_KEYS, PEER_HALF), lambda i, h: (h, 0, 0, 0))],
        out_specs=[out, out, out],
        out_shape=[jax.ShapeDtypeStruct(shp, jnp.int32), jax.ShapeDtypeStruct(shp, jnp.int32),
                   jax.ShapeDtypeStruct(shp, F32)],
        compiler_params=_cparams(("parallel", "parallel")),
    )(q, sub_keys)


PEER_EXPERT_BLOCK = 4 * PEER_KEYS


PEER_GRID_HALF = PEER_KEYS // 2
HIGH_HALF = 0xFFFF0000
PEER_BUILD_UNROLL = 64


def _peer_kernel(x_ref, u_ref, v_ref, i1_ref, i2_ref, gate_ref, h_ref, o_ref, g_scr, w_scr, *, tm, n_blocks):
    j = pl.program_id(1)
    n_entries = PEER_HEADS * PEER_TOPK
    words = PEER_GRID_HALF // 2
    blocks_per_half = n_blocks // 2
    unroll = math.gcd(tm, PEER_BUILD_UNROLL)

    def build(base):
        r = lax.broadcasted_iota(jnp.int32, (PEER_GRID_HALF, n_entries), 0)
        i1_of_row = base + jnp.where(r < words, 2 * r, 2 * (r - words) + 1)
        i2_of_row = lax.broadcasted_iota(jnp.int32, (PEER_KEYS, n_entries), 0)

        def body(nb, carry):
            for k in range(unroll):
                n = nb * unroll + k
                a = jnp.where(i1_of_row == i1_ref[pl.ds(n, 1), :], gate_ref[pl.ds(n, 1), :], 0.0)
                b = jnp.where(i2_of_row == i2_ref[pl.ds(n, 1), :], 1.0, 0.0)
                grid = _dot_nt(a.astype(MXU_DTYPE), b.astype(MXU_DTYPE))
                bits = lax.bitcast_convert_type(grid.astype(jnp.bfloat16).astype(F32), jnp.uint32)
                g_scr[pl.ds(pl.multiple_of(n * words, words), words), :] = (
                    jnp.right_shift(bits[:words], jnp.uint32(16)) | (bits[words:] & jnp.uint32(HIGH_HALF)))
            return carry

        lax.fori_loop(0, tm // unroll, body, 0)

    def weights():
        act = _dot_nt(x_ref[...], u_ref[...])
        pairs = PEER_EXPERT_BLOCK // (2 * PEER_KEYS)
        m0 = (j - jnp.where(j >= blocks_per_half, blocks_per_half, 0)) * pairs
        pieces = []
        for q in range(pairs):
            word = g_scr[pl.ds(m0 + q, tm, stride=words), :]
            pieces.append(lax.bitcast_convert_type(jnp.left_shift(word, jnp.uint32(16)), F32))
            pieces.append(lax.bitcast_convert_type(word & jnp.uint32(HIGH_HALF), F32))
        g = jnp.concatenate(pieces, axis=1)
        w_scr[j % 2] = (g * _gelu(act)).astype(MXU_DTYPE)

    def values():
        o_ref[...] += _dot(w_scr[(j + 1) % 2], v_ref[...])

    @pl.when(j == 0)
    def _():
        o_ref[...] = h_ref[...]
        build(0)
        weights()

    @pl.when(j == blocks_per_half)
    def _():
        build(PEER_GRID_HALF)

    @pl.when((j > 0) & (j < n_blocks))
    def _():
        values()
        weights()

    @pl.when(j == n_blocks)
    def _():
        values()


def _peer(xn, u, v, i1, i2, gate, h):
    n, d = xn.shape
    n_blocks = u.shape[0] // PEER_EXPERT_BLOCK
    assert n_blocks * PEER_EXPERT_BLOCK == PEER_KEYS * PEER_KEYS and n_blocks % 2 == 0
    tm = _row_tile(n, 512)
    n_entries = PEER_HEADS * PEER_TOPK
    once = pl.Buffered(1)
    row_in = pl.BlockSpec((tm, d), lambda i, j: (i, 0), pipeline_mode=once)
    ent = pl.BlockSpec((tm, n_entries), lambda i, j: (i, 0))
    return pl.pallas_call(
        functools.partial(_peer_kernel, tm=tm, n_blocks=n_blocks),
        grid=(n // tm, n_blocks + 1),
        in_specs=[pl.BlockSpec((tm, d), lambda i, j: (i, 0)),
                  pl.BlockSpec((PEER_EXPERT_BLOCK, d), lambda i, j: (jnp.minimum(j, n_blocks - 1), 0)),
                  pl.BlockSpec((PEER_EXPERT_BLOCK, d), lambda i, j: (jnp.maximum(j - 1, 0), 0)),
                  ent, ent, ent, row_in],
        out_specs=pl.BlockSpec((tm, d), lambda i, j: (i, 0), pipeline_mode=once),
        out_shape=jax.ShapeDtypeStruct((n, d), F32),
        scratch_shapes=[pltpu.VMEM((tm * PEER_GRID_HALF // 2, PEER_KEYS), jnp.uint32),
                        pltpu.VMEM((2, tm, PEER_EXPERT_BLOCK), MXU_DTYPE)],
        compiler_params=_cparams(("parallel", "arbitrary")),
    )(xn, u, v, i1, i2, gate, h)


def _prepare_weights(rel_table, norm_mix, w_in, q_norm, k_norm, w_phi, b_phi, w_glu, w_out, peer_w_query,
                     peer_u, peer_v):
    d_model = w_in.shape[0]
    off_gate = ATTN_WIDTH + 6 * KV_WIDTH
    off_ssm = off_gate + 3 * N_HEADS
    w_main = w_in[:, :off_gate].astype(MXU_DTYPE)
    w_ssm = w_in[:, off_ssm:].astype(MXU_DTYPE)
    w_gate = jnp.pad(w_in[:, off_gate:off_ssm], ((0, 0), (0, LANES - 3 * N_HEADS))).astype(MXU_DTYPE)
    n_main = w_main.shape[1]
    ones = jnp.ones((KV_WIDTH,), F32)
    zeros = jnp.zeros((KV_WIDTH,), F32)
    gain = jnp.concatenate([
        jnp.tile(q_norm.astype(F32), N_HEADS), ones, ones,
        jnp.tile(k_norm[1].astype(F32), N_KV_HEADS), ones,
        jnp.tile(k_norm[2].astype(F32), N_KV_HEADS), ones]).reshape(1, n_main)
    mode = jnp.concatenate([
        jnp.ones((ATTN_WIDTH,), F32), zeros, zeros, ones, zeros, ones, zeros]).reshape(1, n_main)
    w_cmp = w_phi.reshape(2, CMP_R, CMP_STRIDE, HEAD_DIM, HEAD_DIM).transpose(0, 2, 3, 1, 4)
    w_cmp = w_cmp.reshape(2, CMP_STRIDE * HEAD_DIM, CMP_R * HEAD_DIM).astype(MXU_DTYPE)
    return dict(
        d_model=d_model, w_main=w_main, w_ssm=w_ssm, w_gate=w_gate, gain=gain, mode=mode, w_cmp=w_cmp,
        tab=rel_table.astype(F32).T, w_glu=w_glu.astype(MXU_DTYPE), w_out=w_out.astype(MXU_DTYPE),
        w_query=peer_w_query.astype(MXU_DTYPE), peer_u=peer_u.astype(MXU_DTYPE),
        peer_v=peer_v.astype(MXU_DTYPE))


def _overlap_matrix(n_chunks, n_cmp, n_slc, n_pad):
    c0 = np.arange(n_chunks)[:, None] * CMP_STRIDE
    s0 = np.arange(n_pad)[None, :] * SLC_BLOCK
    overlap = np.clip(np.minimum(c0 + CMP_BLOCK, s0 + SLC_BLOCK) - np.maximum(c0, s0), 0, None) / CMP_BLOCK
    overlap = overlap * (np.arange(n_chunks)[:, None] < n_cmp) * (np.arange(n_pad)[None, :] < n_slc)
    return jnp.asarray(overlap, dtype=MXU_DTYPE)


def _expansion_matrix(n_pad, n_keys, n_cols):
    key = np.arange(n_cols)[None, :]
    e = ((key // SLC_BLOCK) == np.arange(n_pad)[:, None]) & (key < n_keys)
    return jnp.asarray(e, dtype=MXU_DTYPE)


def _pad_lanes(n):
    return -(-n // LANES) * LANES


def _layer(x3, t_real, pos0, cache_cmp3, cache_sel3, cache_win3, page_table, h0, wts, rel_table, norm_mix,
           k_norm, b_phi, ssm, b_glu, out_norm_attn, out_norm_ssm, norm_ffn, sub_keys):
    nb, t, d_model = x3.shape
    n = nb * t
    sample = page_table is not None
    x2 = x3.reshape(n, d_model)
    xn = _rmsnorm(x2, norm_mix, MXU_DTYPE)
    p, kv_cmp, kv_sel, kv_win = _in_proj(xn, wts["w_main"], wts["mode"], wts["gain"])
    gates = _matmul(xn, wts["w_gate"], "sigmoid")
    n_main = p.shape[1]
    p3 = p.reshape(nb, t, n_main)
    col = lambda off: off // HEAD_DIM
    off_sel, off_win = ATTN_WIDTH + 2 * KV_WIDTH, ATTN_WIDTH + 4 * KV_WIDTH

    if sample:
        assert t_real < CMP_STRIDE and pos0 % PAGE_SIZE == 0
        n_pages = page_table.shape[1]
        pk = _compress(cache_cmp3, page_table, wts["w_cmp"])
        total = pos0 + t_real
    else:
        n_pages = t // PAGE_SIZE
        pt = jnp.arange(nb * n_pages, dtype=jnp.int32).reshape(nb, n_pages)
        pk = _compress(kv_cmp.reshape(nb * n_pages, PAGE_SIZE * 2 * N_KV_HEADS, HEAD_DIM), pt, wts["w_cmp"])
        total = t
    n_chunks = pk.shape[2]
    n_cmp = total // CMP_STRIDE - CMP_R + 1
    n_slc = -(-total // SLC_BLOCK)
    n_pad = _pad_lanes(n_slc)
    m_mat = _overlap_matrix(n_chunks, n_cmp, n_slc, n_pad)
    tq_cmp = _row_tile(t, 256)
    tab = wts["tab"]
    o_cmp, sel = _cmp_attention(tab, p3, pk, b_phi.astype(F32), k_norm[0].astype(F32).reshape(1, HEAD_DIM),
                                m_mat, pos0=pos0, n_cmp=n_cmp, n_slc=n_slc, tq=tq_cmp)

    if sample:
        kv_block = 2 * KV_WIDTH
        n_cols = (n_pages + 8) * KEY_TILE
        e_mat = _expansion_matrix(n_pad, (n_pages + 1) * KEY_TILE, n_cols)
        o_sel = _attn_sample(p3, cache_sel3, page_table, tab, "sel", off_sel // kv_block, pos0, 0, sel, e_mat)
        o_win = _attn_sample(p3, cache_win3, None, tab, "win", off_win // kv_block, pos0, pos0 - WINDOW)
    else:
        e3 = _expansion_matrix(n_pad, t, t).T.reshape(t // KEY_TILE, KEY_TILE, n_pad)
        o_sel = _attn_prompt(p3, tab, "sel", col(off_sel), col(off_sel + KV_WIDTH), sel, e3)
        o_win = _attn_prompt(p3, tab, "win", col(off_win), col(off_win + KV_WIDTH))
    mixed_attn = _combine(o_cmp.reshape(n, ATTN_WIDTH), o_sel.reshape(n, ATTN_WIDTH),
                          o_win.reshape(n, ATTN_WIDTH), gates, out_norm_attn)

    u_lb = _matmul(xn, wts["w_ssm"], "laneblocks")
    nblk = u_lb.shape[0]
    if t % SSM_CHUNK:
        assert t < SSM_CHUNK and t_real <= t
        u_lb = jnp.pad(u_lb.reshape(nblk, nb, t, LANES), ((0, 0), (0, 0), (0, SSM_CHUNK - t), (0, 0)))
        u_lb = u_lb.reshape(nblk, nb * SSM_CHUNK, LANES)
        l_eff, n_chunk = t_real, 1
    else:
        l_eff, n_chunk = SSM_CHUNK, t // SSM_CHUNK
    mats = _ssm_matrices(*ssm, l_eff)
    y_lb, h_last = _ssm(u_lb, *mats, _state_to_blocks(*h0), n_seq=nb, n_chunk=n_chunk)
    if t % SSM_CHUNK:
        y_lb = y_lb.reshape(nblk, nb, SSM_CHUNK, LANES)[:, :, :t].reshape(nblk, n, LANES)
    mixed_ssm = _glu(y_lb, wts["w_glu"], b_glu, out_norm_ssm)

    h = _out_proj(mixed_attn, mixed_ssm, wts["w_out"], x2)
    hn = _rmsnorm(h, norm_ffn, MXU_DTYPE)
    pq = _matmul(hn, wts["w_query"])
    i1, i2, gate = _peer_topk(pq, sub_keys)
    to_rows = lambda a: a.transpose(2, 0, 1).reshape(n, PEER_HEADS * PEER_TOPK)
    y_out = _peer(hn, wts["peer_u"], wts["peer_v"], to_rows(i1), to_rows(i2), to_rows(gate), h)
    new_kv = [a.reshape(nb, t, 2, N_KV_HEADS, HEAD_DIM) for a in (kv_cmp, kv_sel, kv_win)]
    return y_out.reshape(nb, t, d_model), new_kv, _state_from_blocks(h_last)


def kernel(x_prompt, x_sample, cache_kv_cmp, cache_kv_sel, cache_kv_win, state_ssm_re, state_ssm_im, page_table,
           rel_table, norm_mix, w_in, q_norm, k_norm, w_phi, b_phi, ssm_lam_re, ssm_lam_im, ssm_log_dt,
           ssm_b_re, ssm_b_im, ssm_c_re, ssm_c_im, ssm_d, w_glu, b_glu, out_norm_attn, out_norm_ssm, w_out,
           norm_ffn, peer_w_query, peer_sub_keys, peer_u, peer_v):
    depth = w_in.shape[0]
    bp, t_p, d_model = x_prompt.shape
    db, t_s, _ = x_sample.shape
    past_len = page_table.shape[1] * PAGE_SIZE
    kv_row = 2 * KV_WIDTH
    n_groups = ssm_lam_re.shape[1]
    assert cache_kv_win.shape[2] == WINDOW and t_p % KEY_TILE == 0 and t_s <= SAMPLE_ROWS

    yp, ys = x_prompt, jnp.pad(x_sample, ((0, 0), (0, SAMPLE_ROWS - t_s), (0, 0)))
    new_p = [[] for _ in range(5)]
    new_s = [[] for _ in range(5)]
    for l in range(depth):
        wts = _prepare_weights(rel_table, norm_mix[l], w_in[l], q_norm[l], k_norm[l], w_phi[l], b_phi[l],
                               w_glu[l], w_out[l], peer_w_query[l], peer_u[l], peer_v[l])
        ssm = (ssm_lam_re[l], ssm_lam_im[l], ssm_log_dt[l], ssm_b_re[l], ssm_b_im[l], ssm_c_re[l],
               ssm_c_im[l], ssm_d[l])
        shared = (wts, rel_table, norm_mix[l], k_norm[l], b_phi[l], ssm, b_glu[l], out_norm_attn[l],
                  out_norm_ssm[l], norm_ffn[l], peer_sub_keys[l])

        zero_state = jnp.zeros((bp, n_groups, SSM_STATE), F32)
        yp, new_kv, h_last = _layer(yp, t_p, 0, None, None, None, None, (zero_state, zero_state), *shared)
        keep = min(WINDOW, t_p)
        new_p[0].append(new_kv[0])
        new_p[1].append(new_kv[1])
        new_p[2].append(new_kv[2][:, t_p - keep:])
        new_p[3].append(h_last[0])
        new_p[4].append(h_last[1])

        n_phys = cache_kv_cmp.shape[1]
        h0 = (state_ssm_re[l], state_ssm_im[l])
        page_rows = PAGE_SIZE * 2 * N_KV_HEADS
        ys, new_kv, h_last = _layer(ys, t_s, past_len, cache_kv_cmp[l].reshape(n_phys, page_rows, HEAD_DIM),
                                cache_kv_sel[l].reshape(n_phys, page_rows, HEAD_DIM),
                                cache_kv_win[l].reshape(db * WIN_TILES, page_rows, HEAD_DIM), page_table, h0,
                                *shared)
        new_s[0].append(new_kv[0][:, :t_s])
        new_s[1].append(new_kv[1][:, :t_s])
        win_all = jnp.concatenate([cache_kv_win[l], new_kv[2][:, :t_s]], axis=1)
        keep = min(WINDOW, past_len + t_s)
        new_s[2].append(win_all[:, win_all.shape[1] - keep:])
        new_s[3].append(h_last[0])
        new_s[4].append(h_last[1])
    outs_p = [jnp.stack(v) for v in new_p]
    outs_s = [jnp.stack(v) for v in new_s]
    return (yp, ys[:, :t_s], *outs_p, *outs_s)
```

```python
import functools
import math

import numpy as np
import jax
import jax.numpy as jnp
from jax import lax
from jax.experimental import pallas as pl
from jax.experimental.pallas import tpu as pltpu

F32 = jnp.float32
MXU_DTYPE = jnp.bfloat16

HEAD_DIM = 128
N_KV_HEADS = 4
GQA_REP = 4
N_HEADS = N_KV_HEADS * GQA_REP
ATTN_WIDTH = N_HEADS * HEAD_DIM
KV_WIDTH = N_KV_HEADS * HEAD_DIM
ATTN_SCALE = HEAD_DIM ** -0.5
CMP_BLOCK = 32
CMP_STRIDE = 16
CMP_R = CMP_BLOCK // CMP_STRIDE
SLC_BLOCK = 64
N_SELECT = 16
FORCE_BONUS = 1.0e4
WINDOW = 512
NUM_BUCKETS = 32
MAX_DISTANCE = 128
SSM_GROUP = 16
SSM_STATE = 64
PEER_HEADS = 8
PEER_KEYS = 128
PEER_TOPK = 16
PEER_HALF = 128
EPS = 1e-6
NEG_INF = -1e30
PAGE_SIZE = 128

LANES = 128
SUBLANES = 8
VMEM_LIMIT_BYTES = 52 * 1024 * 1024

ROW_TILE = 512
SEL_PAGES_PER_STEP = 16
KEY_TILE = 128
WIN_TILES = WINDOW // KEY_TILE
SSM_CHUNK = 16
SAMPLE_ROWS = 8


def _cparams(sem):
    return pltpu.CompilerParams(dimension_semantics=sem, vmem_limit_bytes=VMEM_LIMIT_BYTES)


def _row_tile(n, pref):
    t = min(n, pref)
    while n % t:
        t -= SUBLANES
    return t


def _dot(a, b):
    return jnp.dot(a, b, preferred_element_type=F32)


def _dot_nt(a, b):
    return lax.dot_general(a, b, (((1,), (1,)), ((), ())), preferred_element_type=F32)


def _gelu(x):
    c = math.sqrt(2.0 / math.pi)
    return 0.5 * x * (1.0 + jnp.tanh(c * (x + 0.044715 * (x * x * x))))


def _sigmoid(x):
    return 1.0 / (1.0 + jnp.exp(-x))


def _bucket_lower_bounds():
    n = np.arange(2 * MAX_DISTANCE)
    max_exact = NUM_BUCKETS // 2
    nf = np.maximum(n, 1).astype(np.float32)
    large = max_exact + (np.log(nf / np.float32(max_exact)) / np.float32(math.log(MAX_DISTANCE / max_exact))
                         * np.float32(NUM_BUCKETS - max_exact)).astype(np.int32)
    bucket = np.where(n < max_exact, n, np.minimum(large, NUM_BUCKETS - 1))
    assert np.all(np.diff(bucket) >= 0) and bucket[-1] == NUM_BUCKETS - 1
    return [int(np.argmax(bucket >= b)) for b in range(NUM_BUCKETS)]


BUCKET_LO = _bucket_lower_bounds()


def _bias_from_dist(dist, table_entry):
    out = table_entry(0)
    for b in range(1, NUM_BUCKETS):
        out = jnp.where(dist >= BUCKET_LO[b], table_entry(b), out)
    return out


def _rmsnorm_kernel(x_ref, g_ref, o_ref):
    x = x_ref[...]
    ms = jnp.mean(x * x, axis=-1, keepdims=True)
    o_ref[...] = (x * lax.rsqrt(ms + EPS) * g_ref[...]).astype(o_ref.dtype)


def _rmsnorm(x, g, out_dtype):
    n, d = x.shape
    tm = _row_tile(n, ROW_TILE)
    return pl.pallas_call(
        _rmsnorm_kernel,
        grid=(n // tm,),
        in_specs=[pl.BlockSpec((tm, d), lambda i: (i, 0)), pl.BlockSpec((1, d), lambda i: (0, 0))],
        out_specs=pl.BlockSpec((tm, d), lambda i: (i, 0)),
        out_shape=jax.ShapeDtypeStruct((n, d), out_dtype),
        compiler_params=_cparams(("parallel",)),
    )(x, g.reshape(1, d))


def _matmul_kernel(*refs, epilogue):
    a_ref, b_ref = refs[0], refs[1]
    o_ref = refs[-1]
    acc = _dot(a_ref[...], b_ref[...])
    if epilogue == "none":
        o_ref[...] = acc
    elif epilogue == "sigmoid":
        o_ref[...] = _sigmoid(acc)
    elif epilogue == "laneblocks":
        for c in range(acc.shape[1] // LANES):
            o_ref[c] = acc[:, c * LANES:(c + 1) * LANES]
    else:
        raise ValueError(epilogue)


def _matmul(a, b, epilogue="none", extras=(), tm_pref=512, tn_pref=1024):
    m, k = a.shape
    _, n = b.shape
    tm = _row_tile(m, tm_pref)
    tn = min(n, tn_pref)
    while n % tn:
        tn -= LANES
    in_specs = [pl.BlockSpec((tm, k), lambda i, j: (i, 0)), pl.BlockSpec((k, tn), lambda i, j: (0, j))]
    for e in extras:
        if e.shape[0] == 1:
            in_specs.append(pl.BlockSpec((1, tn), lambda i, j: (0, j)))
        else:
            in_specs.append(pl.BlockSpec((tm, tn), lambda i, j: (i, j)))
    if epilogue == "laneblocks":
        out_spec = pl.BlockSpec((tn // LANES, tm, LANES), lambda i, j: (j, i, 0))
        out_shape = jax.ShapeDtypeStruct((n // LANES, m, LANES), F32)
    else:
        out_spec = pl.BlockSpec((tm, tn), lambda i, j: (i, j))
        out_shape = jax.ShapeDtypeStruct((m, n), F32)
    return pl.pallas_call(
        functools.partial(_matmul_kernel, epilogue=epilogue),
        grid=(m // tm, n // tn),
        in_specs=in_specs,
        out_specs=out_spec,
        out_shape=out_shape,
        compiler_params=_cparams(("parallel", "parallel")),
    )(a, b, *extras)


def _in_proj_kernel(a_ref, b_ref, mode_ref, gain_ref, o_ref, *kv_refs, first_kv_tile):
    j = pl.program_id(1)
    acc = _dot(a_ref[...], b_ref[...])
    tm = acc.shape[0]
    blocks = []
    for c in range(acc.shape[1] // HEAD_DIM):
        sl = slice(c * HEAD_DIM, (c + 1) * HEAD_DIM)
        blk = acc[:, sl]
        ms = jnp.mean(blk * blk, axis=-1, keepdims=True)
        nrm = blk * lax.rsqrt(ms + EPS) * gain_ref[:, sl]
        blocks.append(jnp.where(mode_ref[:, sl] > 0.0, nrm, blk))
        o_ref[:, sl] = blocks[-1]
    for b, kv_ref in enumerate(kv_refs):
        @pl.when(j == first_kv_tile + b)
        def _(kv_ref=kv_ref):
            for c, blk in enumerate(blocks):
                kv_ref[pl.ds(c, tm, stride=len(blocks)), :] = blk


def _in_proj(a, b, mode, gain, tm_pref=512):
    m, k = a.shape
    n = b.shape[1]
    tn = 2 * KV_WIDTH
    assert ATTN_WIDTH % tn == 0 and n == ATTN_WIDTH + 3 * tn
    tm = _row_tile(m, tm_pref)
    heads = tn // HEAD_DIM
    vec = pl.BlockSpec((1, tn), lambda i, j: (0, j))
    kv_spec = pl.BlockSpec((tm * heads, HEAD_DIM), lambda i, j: (i, 0))
    kv_shape = jax.ShapeDtypeStruct((m * heads, HEAD_DIM), F32)
    return pl.pallas_call(
        functools.partial(_in_proj_kernel, first_kv_tile=ATTN_WIDTH // tn),
        grid=(m // tm, n // tn),
        in_specs=[pl.BlockSpec((tm, k), lambda i, j: (i, 0)), pl.BlockSpec((k, tn), lambda i, j: (0, j)), vec, vec],
        out_specs=[pl.BlockSpec((tm, tn), lambda i, j: (i, j)), kv_spec, kv_spec, kv_spec],
        out_shape=[jax.ShapeDtypeStruct((m, n), F32), kv_shape, kv_shape, kv_shape],
        compiler_params=_cparams(("parallel", "arbitrary")),
    )(a, b, mode, gain)


def _out_proj_kernel(a1_ref, a2_ref, b1_ref, b2_ref, res_ref, o_ref):
    o_ref[...] = res_ref[...] + _dot(a1_ref[...], b1_ref[...]) + _dot(a2_ref[...], b2_ref[...])


def _out_proj(a1, a2, w, res, tm_pref=512, tn_pref=1024):
    m, k1 = a1.shape
    k2 = a2.shape[1]
    assert k1 == k2 and w.shape[0] == k1 + k2
    n = w.shape[1]
    tm = _row_tile(m, tm_pref)
    tn = min(n, tn_pref)
    return pl.pallas_call(
        _out_proj_kernel,
        grid=(m // tm, n // tn),
        in_specs=[pl.BlockSpec((tm, k1), lambda i, j: (i, 0)), pl.BlockSpec((tm, k2), lambda i, j: (i, 0)),
                  pl.BlockSpec((k1, tn), lambda i, j: (0, j)), pl.BlockSpec((k2, tn), lambda i, j: (1, j)),
                  pl.BlockSpec((tm, tn), lambda i, j: (i, j))],
        out_specs=pl.BlockSpec((tm, tn), lambda i, j: (i, j)),
        out_shape=jax.ShapeDtypeStruct((m, n), F32),
        compiler_params=_cparams(("parallel", "parallel")),
    )(a1, a2, w, w, res)


CMP_PAGES_PER_STEP = 16


def _compress_kernel(pt_ref, *refs):
    del pt_ref
    pages = refs[:CMP_PAGES_PER_STEP]
    w_ref = refs[CMP_PAGES_PER_STEP]
    o_ref = refs[CMP_PAGES_PER_STEP + 1]
    x_scr = refs[CMP_PAGES_PER_STEP + 2]
    chunks_per_page = PAGE_SIZE // CMP_STRIDE
    rows_per_head = CMP_PAGES_PER_STEP * chunks_per_page
    heads_per_row = 2 * N_KV_HEADS
    for c in range(2):
        for g in range(N_KV_HEADS):
            for i in range(CMP_PAGES_PER_STEP):
                r0 = g * rows_per_head + i * chunks_per_page
                for s in range(CMP_STRIDE):
                    x_scr[r0:r0 + chunks_per_page, s * HEAD_DIM:(s + 1) * HEAD_DIM] = pages[i][
                        0, pl.ds(s * heads_per_row + c * N_KV_HEADS + g, chunks_per_page,
                                 stride=CMP_STRIDE * heads_per_row), :]
        res = _dot(x_scr[...].astype(MXU_DTYPE), w_ref[c])
        for g in range(N_KV_HEADS):
            o_ref[0, c * N_KV_HEADS + g] = res[g * rows_per_head:(g + 1) * rows_per_head]


def _compress(rows3, page_table, w_cmp):
    nb, n_pages = page_table.shape
    assert n_pages % CMP_PAGES_PER_STEP == 0
    n_steps = n_pages // CMP_PAGES_PER_STEP
    chunks_per_step = CMP_PAGES_PER_STEP * PAGE_SIZE // CMP_STRIDE
    n_chunks = n_pages * PAGE_SIZE // CMP_STRIDE

    def page_spec(i):
        return pl.BlockSpec((1, PAGE_SIZE * 2 * N_KV_HEADS, HEAD_DIM),
                            lambda b, j, pt: (pt[b, j * CMP_PAGES_PER_STEP + i], 0, 0))

    grid_spec = pltpu.PrefetchScalarGridSpec(
        num_scalar_prefetch=1,
        grid=(nb, n_steps),
        in_specs=[page_spec(i) for i in range(CMP_PAGES_PER_STEP)]
        + [pl.BlockSpec((2, CMP_STRIDE * HEAD_DIM, CMP_R * HEAD_DIM), lambda b, j, pt: (0, 0, 0))],
        out_specs=pl.BlockSpec((1, 2 * N_KV_HEADS, chunks_per_step, CMP_R * HEAD_DIM),
                               lambda b, j, pt: (b, 0, j, 0)),
        scratch_shapes=[pltpu.VMEM((N_KV_HEADS * chunks_per_step, CMP_STRIDE * HEAD_DIM), F32)],
    )
    return pl.pallas_call(
        _compress_kernel,
        grid_spec=grid_spec,
        out_shape=jax.ShapeDtypeStruct((nb, 2 * N_KV_HEADS, n_chunks, CMP_R * HEAD_DIM), F32),
        compiler_params=_cparams(("parallel", "arbitrary")),
    )(page_table, *([rows3] * CMP_PAGES_PER_STEP), w_cmp)


def _cmp_attn_kernel(tab_ref, q_ref, kp_ref, vp_ref, bphi_ref, kn_ref, m_ref, o_ref, sel_ref, bias_scr,
                     *, tq, pos0, n_cmp, n_slc):
    g = pl.program_id(0)
    i = pl.program_id(1)
    n_chunks = kp_ref.shape[2]
    kp = kp_ref[0, 0]
    vp = vp_ref[0, 0]
    k_c = bphi_ref[0:1, :] + kp[:, :HEAD_DIM] + pltpu.roll(kp[:, HEAD_DIM:], n_chunks - 1, 0)
    v_c = bphi_ref[1:2, :] + vp[:, :HEAD_DIM] + pltpu.roll(vp[:, HEAD_DIM:], n_chunks - 1, 0)
    ms = jnp.mean(k_c * k_c, axis=-1, keepdims=True)
    k_c = k_c * lax.rsqrt(ms + EPS) * kn_ref[...]

    q = jnp.concatenate([q_ref[0, :, r * HEAD_DIM:(r + 1) * HEAD_DIM] for r in range(GQA_REP)], axis=0)
    logits = _dot_nt(q.astype(MXU_DTYPE), k_c.astype(MXU_DTYPE)) * ATTN_SCALE
    row = lax.broadcasted_iota(jnp.int32, (tq, n_chunks), 0)
    col = lax.broadcasted_iota(jnp.int32, (tq, n_chunks), 1)
    dist = pos0 + i * tq + row - (col * CMP_STRIDE + (CMP_BLOCK - 1))

    @pl.when(pl.program_id(2) == 0)
    def _():
        bias_scr[...] = jnp.concatenate(
            [_bias_from_dist(dist, lambda b, r=r: tab_ref[g * GQA_REP + r, b]) for r in range(GQA_REP)], axis=0)

    bias = bias_scr[...]
    valid = (dist >= 0) & (col < n_cmp)
    valid = jnp.concatenate([valid] * GQA_REP, axis=0)
    logits = jnp.where(valid, logits + bias, NEG_INF)
    mx = jnp.max(logits, axis=-1, keepdims=True)
    e = jnp.exp(logits - mx)
    any_valid = (jnp.max(jnp.where(valid, 1.0, 0.0), axis=-1, keepdims=True))
    p = e / jnp.sum(e, axis=-1, keepdims=True) * any_valid
    out = _dot(p.astype(MXU_DTYPE), v_c.astype(MXU_DTYPE))
    for r in range(GQA_REP):
        o_ref[0, :, r * HEAD_DIM:(r + 1) * HEAD_DIM] = out[r * tq:(r + 1) * tq]

    psum = p[0:tq]
    for r in range(1, GQA_REP):
        psum = psum + p[r * tq:(r + 1) * tq]
    hi = psum.astype(MXU_DTYPE)
    lo = (psum - hi.astype(F32)).astype(MXU_DTYPE)
    imp = _dot(hi, m_ref[...]) + _dot(lo, m_ref[...])
    n_pad = imp.shape[1]
    j = lax.broadcasted_iota(jnp.int32, (tq, n_pad), 1)
    qp = pos0 + i * tq + lax.broadcasted_iota(jnp.int32, (tq, n_pad), 0)
    cur = jnp.right_shift(qp, int(math.log2(SLC_BLOCK)))
    ok = j * SLC_BLOCK <= qp
    forced = (j == 0) | (j == cur) | (j == cur - 1)
    score = jnp.where(ok, imp + jnp.where(forced, FORCE_BONUS, 0.0), NEG_INF)
    rank = jnp.zeros((tq, n_pad), F32)
    for s in range(n_slc):
        cs = score[:, s:s + 1]
        beats = (cs > score) | ((cs == score) & (j > s))
        rank = rank + jnp.where(beats, 1.0, 0.0)
    selected = (rank < float(min(N_SELECT, n_slc))) & (score > 0.5 * NEG_INF)
    sel_ref[0, 0] = jnp.where(selected, 1.0, 0.0)


def _cmp_attention(tab, p3, pk, bphi, k_norm0, m_mat, *, pos0, n_cmp, n_slc, tq):
    nb, t, _ = p3.shape
    n_chunks = pk.shape[2]
    n_pad = m_mat.shape[1]
    nq = t // tq
    kern = functools.partial(_cmp_attn_kernel, tq=tq, pos0=pos0, n_cmp=n_cmp, n_slc=n_slc)
    return pl.pallas_call(
        kern,
        grid=(N_KV_HEADS, nq, nb),
        in_specs=[
            pl.BlockSpec(memory_space=pltpu.SMEM),
            pl.BlockSpec((1, tq, GQA_REP * HEAD_DIM), lambda g, i, b: (b, i, g)),
            pl.BlockSpec((1, 1, n_chunks, CMP_R * HEAD_DIM), lambda g, i, b: (b, g, 0, 0)),
            pl.BlockSpec((1, 1, n_chunks, CMP_R * HEAD_DIM), lambda g, i, b: (b, N_KV_HEADS + g, 0, 0)),
            pl.BlockSpec((2, HEAD_DIM), lambda g, i, b: (0, 0)),
            pl.BlockSpec((1, HEAD_DIM), lambda g, i, b: (0, 0)),
            pl.BlockSpec((n_chunks, n_pad), lambda g, i, b: (0, 0)),
        ],
        out_specs=[
            pl.BlockSpec((1, tq, GQA_REP * HEAD_DIM), lambda g, i, b: (b, i, g)),
            pl.BlockSpec((1, 1, tq, n_pad), lambda g, i, b: (b, g, i, 0)),
        ],
        out_shape=[
            jax.ShapeDtypeStruct((nb, t, ATTN_WIDTH), F32),
            jax.ShapeDtypeStruct((nb, N_KV_HEADS, t, n_pad), F32),
        ],
        scratch_shapes=[pltpu.VMEM((GQA_REP * tq, n_chunks), F32)],
        compiler_params=_cparams(("parallel", "parallel", "arbitrary")),
    )(tab, p3, pk, pk, bphi, k_norm0, m_mat)


def _online_softmax_step(s, v, m_scr, l_scr, acc_scr):
    m_old = m_scr[...]
    m_new = jnp.maximum(m_old, jnp.max(s, axis=-1, keepdims=True))
    alpha = jnp.exp(m_old - m_new)
    p = jnp.exp(s - m_new)
    l_scr[...] = alpha * l_scr[...] + jnp.sum(p, axis=-1, keepdims=True)
    acc_scr[...] = alpha * acc_scr[...] + _dot(p.astype(MXU_DTYPE), v)
    m_scr[...] = m_new


ATTN_TILES_PER_ITER = 4


def _attn_prompt_kernel(*refs, mode, n_kt):
    if mode == "sel":
        q_ref, k_ref, v_ref, mb_ref, sel_ref, e_ref, o_ref, kb_scr, vt_scr, m_scr, l_scr, acc_scr = refs
    else:
        q_ref, k_ref, v_ref, mb_ref, o_ref, kb_scr, vt_scr, m_scr, l_scr, acc_scr = refs
    tq = KEY_TILE
    qt = pl.program_id(2)

    @pl.when(qt == 0)
    def _():
        for kt in range(n_kt):
            rows = slice(kt * KEY_TILE, (kt + 1) * KEY_TILE)
            kb_scr[kt] = k_ref[0, rows, :].astype(MXU_DTYPE)
            vt_scr[kt] = v_ref[0, rows, :].T.astype(MXU_DTYPE)

    q = jnp.concatenate([q_ref[0, :, r * HEAD_DIM:(r + 1) * HEAD_DIM] for r in range(GQA_REP)], axis=0)
    q = (q * ATTN_SCALE).astype(MXU_DTYPE)
    m_scr[...] = jnp.full(m_scr.shape, NEG_INF, F32)
    l_scr[...] = jnp.zeros(l_scr.shape, F32)
    acc_scr[...] = jnp.zeros(acc_scr.shape, F32)
    if mode == "sel":
        sel = sel_ref[0, 0].astype(MXU_DTYPE)

    n_kinds = mb_ref.shape[0]

    def logits(kt):
        d = qt - kt
        if mode == "sel":
            kind = jnp.where(d < 0, n_kinds - 1, jnp.minimum(d, 2))
        else:
            kind = jnp.where(kt < 0, n_kinds - 1, jnp.where(d == WIN_TILES, 3, jnp.minimum(d, 2)))
        kt = jnp.clip(kt, 0, n_kt - 1)
        s = _dot_nt(kb_scr[kt], q) + mb_ref[kind, 0]
        if mode == "sel":
            chosen = _dot_nt(e_ref[kt], sel)
            s = s + jnp.concatenate([(1.0 - chosen) * NEG_INF] * GQA_REP, axis=1)
        return kt, s

    def update(key_tiles):
        tiles = [logits(kt) for kt in key_tiles]
        m_old = m_scr[...]
        m_new = m_old
        for _, s in tiles:
            m_new = jnp.maximum(m_new, jnp.max(s, axis=0, keepdims=True))
        alpha = jnp.exp(m_old - m_new)
        l_new = alpha * l_scr[...]
        acc = alpha * acc_scr[...]
        for kt, s in tiles:
            p = jnp.exp(s - m_new)
            l_new = l_new + jnp.sum(p, axis=0, keepdims=True)
            acc = acc + _dot(vt_scr[kt], p.astype(MXU_DTYPE))
        l_scr[...] = l_new
        acc_scr[...] = acc
        m_scr[...] = m_new

    if mode == "sel":
        def body(it, carry):
            update([it * ATTN_TILES_PER_ITER + u for u in range(ATTN_TILES_PER_ITER)])
            return carry

        lax.fori_loop(0, (qt + ATTN_TILES_PER_ITER) // ATTN_TILES_PER_ITER, body, 0)
    else:
        update([qt - WIN_TILES + u for u in range(WIN_TILES + 1)])
    out = acc_scr[...] / l_scr[...]
    for r in range(GQA_REP):
        o_ref[0, :, r * HEAD_DIM:(r + 1) * HEAD_DIM] = out[:, r * tq:(r + 1) * tq].T


def _prompt_bias_tiles(tab, mode):
    ii = jnp.arange(KEY_TILE, dtype=jnp.int32)
    entry = lambda b: tab[:, b][:, None, None]

    def tile(offset):
        dist = (offset * KEY_TILE + ii[:, None] - ii[None, :])[None]
        return dist, jnp.broadcast_to(_bias_from_dist(dist, entry), (N_HEADS, KEY_TILE, KEY_TILE))

    d0, b0 = tile(0)
    tiles = [jnp.where(d0 >= 0, b0, NEG_INF), tile(1)[1], tile(2)[1]]
    if mode == "win":
        d4, b4 = tile(WIN_TILES)
        tiles.append(jnp.where(d4 <= WINDOW, b4, NEG_INF))
    tiles.append(jnp.full_like(b0, NEG_INF))
    mb = jnp.stack(tiles).reshape(len(tiles), N_KV_HEADS, GQA_REP, KEY_TILE, KEY_TILE)
    return mb.transpose(0, 1, 4, 2, 3).reshape(len(tiles), N_KV_HEADS, KEY_TILE, GQA_REP * KEY_TILE)


def _attn_prompt(p3, tab, mode, k_col, v_col, sel=None, e3=None):
    nb, t, _ = p3.shape
    tq = KEY_TILE
    nq = t // tq
    mb = _prompt_bias_tiles(tab, mode)
    cols = GQA_REP * tq
    in_specs = [
        pl.BlockSpec((1, tq, GQA_REP * HEAD_DIM), lambda b, g, i: (b, i, g)),
        pl.BlockSpec((1, t, HEAD_DIM), lambda b, g, i: (b, 0, k_col + g)),
        pl.BlockSpec((1, t, HEAD_DIM), lambda b, g, i: (b, 0, v_col + g)),
        pl.BlockSpec((mb.shape[0], 1, KEY_TILE, cols), lambda b, g, i: (0, g, 0, 0)),
    ]
    args = [p3, p3, p3, mb]
    if mode == "sel":
        n_pad = sel.shape[-1]
        in_specs += [
            pl.BlockSpec((1, 1, tq, n_pad), lambda b, g, i: (b, g, i, 0)),
            pl.BlockSpec((nq, KEY_TILE, n_pad), lambda b, g, i: (0, 0, 0)),
        ]
        args += [sel, e3]
    return pl.pallas_call(
        functools.partial(_attn_prompt_kernel, mode=mode, n_kt=nq),
        grid=(nb, N_KV_HEADS, nq),
        in_specs=in_specs,
        out_specs=pl.BlockSpec((1, tq, GQA_REP * HEAD_DIM), lambda b, g, i: (b, i, g)),
        out_shape=jax.ShapeDtypeStruct((nb, t, ATTN_WIDTH), F32),
        scratch_shapes=[pltpu.VMEM((nq, KEY_TILE, HEAD_DIM), MXU_DTYPE),
                        pltpu.VMEM((nq, HEAD_DIM, KEY_TILE), MXU_DTYPE),
                        pltpu.VMEM((1, cols), F32), pltpu.VMEM((1, cols), F32),
                        pltpu.VMEM((HEAD_DIM, cols), F32)],
        compiler_params=_cparams(("parallel", "parallel", "arbitrary")),
    )(*args)


def _attn_sample_kernel(*refs, mode, paged, pages, n_steps, tq, pos0, base_pos, near_keys):
    it = iter(refs)
    if paged:
        next(it)
    q_ref = next(it)
    page_refs = [next(it) for _ in range(pages)]
    new_ref, rowtab_ref = next(it), next(it)
    if mode == "sel":
        sel_ref, e_ref = next(it), next(it)
    o_ref, m_scr, l_scr, acc_scr = next(it), next(it), next(it), next(it)
    kk = pl.program_id(1)
    rows = N_HEADS * tq
    heads_per_row = 2 * N_KV_HEADS

    @pl.when(kk == 0)
    def _():
        m_scr[...] = jnp.full(m_scr.shape, NEG_INF, F32)
        l_scr[...] = jnp.zeros(l_scr.shape, F32)
        acc_scr[...] = jnp.zeros(acc_scr.shape, F32)

    def queries():
        zero = jnp.zeros((tq, HEAD_DIM), F32)
        blocks = []
        for h in range(N_HEADS):
            qh = q_ref[0, :, h * HEAD_DIM:(h + 1) * HEAD_DIM] * ATTN_SCALE
            blocks.append(jnp.concatenate([qh if g == h // GQA_REP else zero for g in range(N_KV_HEADS)], axis=1))
        return jnp.concatenate(blocks, axis=0).astype(MXU_DTYPE)

    def step(k_all, v_all, key_pos0, chosen_cols, near_keys):
        n_keys = k_all.shape[0]
        s = _dot_nt(queries(), k_all)
        row = lax.broadcasted_iota(jnp.int32, (rows, n_keys), 0) & (tq - 1)
        col = lax.broadcasted_iota(jnp.int32, (rows, n_keys), 1)
        dist = pos0 + row - (key_pos0 + col)
        bias = rowtab_ref[:, NUM_BUCKETS - 1:NUM_BUCKETS]
        if near_keys:
            near = _bias_from_dist(dist[:, n_keys - near_keys:], lambda b: rowtab_ref[:, b:b + 1])
            if near_keys < n_keys:
                near = jnp.concatenate([jnp.broadcast_to(bias, (rows, n_keys - near_keys)), near], axis=1)
            bias = near
        if mode == "sel":
            sel = sel_ref[0].reshape(N_KV_HEADS * tq, sel_ref.shape[-1]).astype(MXU_DTYPE)
            chosen = _dot(sel, chosen_cols)
            chosen = jnp.concatenate(
                [chosen[g * tq:(g + 1) * tq] for g in range(N_KV_HEADS) for _ in range(GQA_REP)], axis=0)
            mask = (dist >= 0) & (chosen > 0.5)
        else:
            mask = (dist >= 0) & (dist <= WINDOW)
        s = jnp.where(mask, s + bias, NEG_INF)
        _online_softmax_step(s, v_all, m_scr, l_scr, acc_scr)

    def cached(kv):
        return jnp.concatenate([
            jnp.concatenate([pg[0, pl.ds(kv * N_KV_HEADS + g, PAGE_SIZE, stride=heads_per_row), :]
                             for g in range(N_KV_HEADS)], axis=1)
            for pg in page_refs], axis=0).astype(MXU_DTYPE)

    keys_per_step = pages * PAGE_SIZE
    key_pos0 = base_pos + kk * keys_per_step
    far = pos0 - (key_pos0 + keys_per_step - 1) >= MAX_DISTANCE
    is_cache = kk < n_steps
    chosen_cols = e_ref[...] if mode == "sel" else None

    @pl.when(is_cache & far)
    def _():
        step(cached(0), cached(1), key_pos0, chosen_cols, 0)

    @pl.when(is_cache & jnp.logical_not(far))
    def _():
        step(cached(0), cached(1), key_pos0, chosen_cols, near_keys)

    @pl.when(kk == n_steps)
    def _():
        pad = jnp.zeros((KEY_TILE - tq, N_KV_HEADS * HEAD_DIM), F32)
        k_new = jnp.concatenate([new_ref[0, :, :KV_WIDTH], pad], axis=0).astype(MXU_DTYPE)
        v_new = jnp.concatenate([new_ref[0, :, KV_WIDTH:], pad], axis=0).astype(MXU_DTYPE)
        cols = e_ref[:, :KEY_TILE] if mode == "sel" else None
        step(k_new, v_new, pos0, cols, KEY_TILE)
        out = acc_scr[...] / l_scr[...]
        for h in range(N_HEADS):
            g = h // GQA_REP
            o_ref[0, :, h * HEAD_DIM:(h + 1) * HEAD_DIM] = out[h * tq:(h + 1) * tq, g * HEAD_DIM:(g + 1) * HEAD_DIM]


def _attn_sample(p3, cache_rows, page_table, tab, mode, new_col, pos0, base_pos, sel=None, e_mat=None):
    nb, tq, _ = p3.shape
    paged = page_table is not None
    pages = SEL_PAGES_PER_STEP if paged else WIN_TILES
    n_tiles = page_table.shape[1] if paged else WIN_TILES
    assert n_tiles % pages == 0
    n_steps = n_tiles // pages
    rows = N_HEADS * tq
    rowtab = jnp.repeat(tab, tq, axis=0)

    def page_spec(i):
        def index(b, kk, *pt):
            page = jnp.minimum(kk, n_steps - 1) * pages + i
            return (pt[0][b, page] if paged else b * n_tiles + page, 0, 0)
        return pl.BlockSpec((1, PAGE_SIZE * 2 * N_KV_HEADS, HEAD_DIM), index)

    def fixed(shape, fn):
        return pl.BlockSpec(shape, lambda b, kk, *_: fn(b, kk))

    in_specs = [fixed((1, tq, ATTN_WIDTH), lambda b, kk: (b, 0, 0))]
    in_specs += [page_spec(i) for i in range(pages)]
    in_specs += [fixed((1, tq, 2 * KV_WIDTH), lambda b, kk: (b, 0, new_col)),
                 fixed((rows, NUM_BUCKETS), lambda b, kk: (0, 0))]
    args = [p3] + [cache_rows] * pages + [p3, rowtab]
    if mode == "sel":
        n_pad = sel.shape[-1]
        in_specs += [fixed((1, N_KV_HEADS, tq, n_pad), lambda b, kk: (b, 0, 0, 0)),
                     fixed((n_pad, pages * PAGE_SIZE), lambda b, kk: (0, kk))]
        args += [sel, e_mat]
    keys_per_step = pages * PAGE_SIZE
    near_keys = PAGE_SIZE if base_pos + n_tiles * PAGE_SIZE == pos0 else keys_per_step
    kern = functools.partial(_attn_sample_kernel, mode=mode, paged=paged, pages=pages, n_steps=n_steps, tq=tq,
                             pos0=pos0, base_pos=base_pos, near_keys=near_keys)
    out_spec = fixed((1, tq, ATTN_WIDTH), lambda b, kk: (b, 0, 0))
    out_shape = jax.ShapeDtypeStruct((nb, tq, ATTN_WIDTH), F32)
    scratch = [pltpu.VMEM((rows, 1), F32), pltpu.VMEM((rows, 1), F32),
               pltpu.VMEM((rows, N_KV_HEADS * HEAD_DIM), F32)]
    cp = _cparams(("parallel", "arbitrary"))
    grid = (nb, n_steps + 1)
    if paged:
        grid_spec = pltpu.PrefetchScalarGridSpec(num_scalar_prefetch=1, grid=grid, in_specs=in_specs,
                                                 out_specs=out_spec, scratch_shapes=scratch)
        return pl.pallas_call(kern, grid_spec=grid_spec, out_shape=out_shape, compiler_params=cp)(
            page_table, *args)
    return pl.pallas_call(kern, grid=grid, in_specs=in_specs, out_specs=out_spec, out_shape=out_shape,
                          scratch_shapes=scratch, compiler_params=cp)(*args)


def _combine_kernel(oc_ref, os_ref, ow_ref, gate_ref, g_ref, o_ref):
    gates = gate_ref[...]
    parts = []
    for h in range(N_HEADS):
        sl = slice(h * HEAD_DIM, (h + 1) * HEAD_DIM)
        parts.append(gates[:, h:h + 1] * oc_ref[:, sl]
                     + gates[:, N_HEADS + h:N_HEADS + h + 1] * os_ref[:, sl]
                     + gates[:, 2 * N_HEADS + h:2 * N_HEADS + h + 1] * ow_ref[:, sl])
    o = jnp.concatenate(parts, axis=1)
    ms = jnp.mean(o * o, axis=-1, keepdims=True)
    o_ref[...] = (o * lax.rsqrt(ms + EPS) * g_ref[...]).astype(o_ref.dtype)


def _combine(o_cmp, o_sel, o_win, gates, gain):
    n, w = o_cmp.shape
    tm = _row_tile(n, ROW_TILE)
    row = pl.BlockSpec((tm, w), lambda i: (i, 0))
    return pl.pallas_call(
        _combine_kernel,
        grid=(n // tm,),
        in_specs=[row, row, row, pl.BlockSpec((tm, LANES), lambda i: (i, 0)),
                  pl.BlockSpec((1, w), lambda i: (0, 0))],
        out_specs=row,
        out_shape=jax.ShapeDtypeStruct((n, w), MXU_DTYPE),
        compiler_params=_cparams(("parallel",)),
    )(o_cmp, o_sel, o_win, gates, gain.reshape(1, w))


SSM_LANE_GROUPS = LANES // SSM_GROUP
SSM_BLOCK_STATE = SSM_LANE_GROUPS * 2 * SSM_STATE


def _ssm_kernel(u_ref, kt_ref, bs_ref, cs_ref, d_ref, la_ref, lb_ref, h0_ref, y_ref, hout_ref,
                us_scr, s_scr, hs_scr, *, n_seq, n_chunk):
    L = SSM_CHUNK
    rows = n_seq * n_chunk
    st = SSM_BLOCK_STATE
    group_shift = int(math.log2(SSM_GROUP))
    state_shift = int(math.log2(2 * SSM_STATE))
    in_mask = (jnp.right_shift(lax.broadcasted_iota(jnp.int32, (LANES, st), 0), group_shift)
               == jnp.right_shift(lax.broadcasted_iota(jnp.int32, (LANES, st), 1), state_shift))
    out_mask = (jnp.right_shift(lax.broadcasted_iota(jnp.int32, (st, LANES), 0), state_shift)
                == jnp.right_shift(lax.broadcasted_iota(jnp.int32, (st, LANES), 1), group_shift))
    zero = jnp.zeros((), MXU_DTYPE)

    s_acc = jnp.zeros((rows, st), F32)
    for s in range(L):
        us = u_ref[0, pl.ds(s, rows, stride=L), :].astype(MXU_DTYPE)
        us_scr[s] = us
        b_full = jnp.where(in_mask, jnp.concatenate([bs_ref[0, s]] * SSM_LANE_GROUPS, axis=1), zero)
        s_acc = s_acc + _dot(us, b_full)
    s_scr[...] = s_acc

    la = la_ref[0]
    lb = lb_ref[0]
    is_re = (lax.broadcasted_iota(jnp.int32, (1, st), 1) & SSM_STATE) == 0

    def swap(h):
        return jnp.where(is_re, pltpu.roll(h, st - SSM_STATE, 1), pltpu.roll(h, SSM_STATE, 1))

    def body(c, hs):
        new = []
        for b in range(n_seq):
            row = b * n_chunk + c
            hs_scr[pl.ds(row, 1), :] = hs[b]
            new.append(la * hs[b] + lb * swap(hs[b]) + s_scr[pl.ds(row, 1), :])
        return tuple(new)

    hs = lax.fori_loop(0, n_chunk, body, tuple(h0_ref[0, b:b + 1, :] for b in range(n_seq)))
    for b in range(n_seq):
        hout_ref[0, b:b + 1, :] = hs[b]

    h_start = hs_scr[...].astype(MXU_DTYPE)
    for t in range(L):
        c_full = jnp.where(out_mask, jnp.concatenate([cs_ref[0, t]] * SSM_LANE_GROUPS, axis=0), zero)
        acc = _dot(h_start, c_full)
        for s in range(t + 1):
            acc = acc + _dot(us_scr[s], kt_ref[0, t - s])
        acc = acc + d_ref[0] * u_ref[0, pl.ds(t, rows, stride=L), :]
        y_ref[0, pl.ds(t, rows, stride=L), :] = acc


def _ssm(u_lb, kt, bs, cs, dvec, la, lb, h0, *, n_seq, n_chunk):
    nblk, n, w = u_lb.shape
    L = SSM_CHUNK
    rows = n_seq * n_chunk
    assert n == rows * L and w == LANES
    st = SSM_BLOCK_STATE

    def blk(shape):
        return pl.BlockSpec((1,) + shape, lambda i: (i,) + (0,) * len(shape))

    return pl.pallas_call(
        functools.partial(_ssm_kernel, n_seq=n_seq, n_chunk=n_chunk),
        grid=(nblk,),
        in_specs=[blk((n, w)), blk((L, w, w)), blk((L, w, w)), blk((L, w, w)), blk((1, w)), blk((1, st)),
                  blk((1, st)), blk((n_seq, st))],
        out_specs=[blk((n, w)), blk((n_seq, st))],
        out_shape=[jax.ShapeDtypeStruct((nblk, n, w), F32), jax.ShapeDtypeStruct((nblk, n_seq, st), F32)],
        scratch_shapes=[pltpu.VMEM((L, rows, w), MXU_DTYPE), pltpu.VMEM((rows, st), F32),
                        pltpu.VMEM((rows, st), F32)],
        compiler_params=_cparams(("parallel",)),
    )(u_lb, kt, bs, cs, dvec, la, lb, h0)


def _ssm_matrices(lam_re, lam_im, log_dt, b_re, b_im, c_re, c_im, d_skip, l_eff):
    g = lam_re.shape[0]
    L = SSM_CHUNK
    lg = SSM_LANE_GROUPS
    nblk = g // lg
    lam = lax.complex(lam_re.astype(F32), lam_im.astype(F32))
    dt = jnp.exp(log_dt.astype(F32))[:, None]
    lam_bar = jnp.exp(lam * dt)
    b_bar = ((lam_bar - 1.0) / lam)[..., None] * lax.complex(b_re.astype(F32), b_im.astype(F32))
    c_c = lax.complex(c_re.astype(F32), c_im.astype(F32))
    pw = [jnp.ones_like(lam_bar)]
    for _ in range(L):
        pw.append(pw[-1] * lam_bar)
    pw = jnp.stack(pw)
    pw_b = pw.reshape(L + 1, nblk, lg, SSM_STATE)
    bb = b_bar.reshape(nblk, lg, SSM_STATE, SSM_GROUP).transpose(0, 1, 3, 2)
    chan_group = np.arange(LANES) // SSM_GROUP
    to_chan = jnp.asarray(chan_group[None, :] == np.arange(lg)[:, None], F32)
    out_lane = jnp.asarray(np.arange(LANES)[None, :] % SSM_GROUP == np.arange(SSM_GROUP)[:, None], F32)
    same_group = jnp.asarray(chan_group[:, None] == chan_group[None, :], F32)
    kern = jnp.einsum("gop,kgp,gpi->gkio", c_c, pw[:L], b_bar).real
    kern = kern.reshape(nblk, lg, L, SSM_GROUP, SSM_GROUP).transpose(0, 2, 1, 3, 4)
    kern = kern.reshape(nblk, L, LANES, SSM_GROUP)
    kt = jnp.einsum("ltro,oc->ltrc", kern, out_lane, precision=lax.Precision.HIGHEST) * same_group
    exps = np.clip(l_eff - 1 - np.arange(L), 0, L)
    bx = pw_b[exps].transpose(1, 0, 2, 3)[:, :, :, None, :] * bb[:, None]
    bx = jnp.where(jnp.asarray(np.arange(L) < l_eff)[None, :, None, None, None], bx, 0.0)
    bs = jnp.concatenate([bx.real, bx.imag], axis=-1).reshape(nblk, L, LANES, 2 * SSM_STATE)
    c_t = c_c.reshape(nblk, LANES, SSM_STATE).transpose(0, 2, 1)
    pw_t = pw_b[1:L + 1].transpose(1, 0, 3, 2)
    expand = lambda a: jnp.einsum("ltpg,gc->ltpc", a, to_chan, precision=lax.Precision.HIGHEST)
    pr, pi = expand(pw_t.real), expand(pw_t.imag)
    cr, ci = c_t.real[:, None], c_t.imag[:, None]
    cs = jnp.concatenate([cr * pr - ci * pi, -(cr * pi + ci * pr)], axis=2)
    lam_l = pw[l_eff].reshape(nblk, lg, SSM_STATE)
    la = jnp.stack([lam_l.real, lam_l.real], axis=2).reshape(nblk, 1, SSM_BLOCK_STATE)
    lb = jnp.stack([-lam_l.imag, lam_l.imag], axis=2).reshape(nblk, 1, SSM_BLOCK_STATE)
    dvec = d_skip.astype(F32).reshape(nblk, 1, LANES)
    return kt.astype(MXU_DTYPE), bs.astype(MXU_DTYPE), cs.astype(MXU_DTYPE), dvec, la, lb


def _state_to_blocks(re, im):
    n_seq, g, p = re.shape
    h = jnp.stack([re, im], axis=2).astype(F32).reshape(n_seq, g // SSM_LANE_GROUPS, SSM_BLOCK_STATE)
    return h.transpose(1, 0, 2)


def _state_from_blocks(h):
    nblk, n_seq, _ = h.shape
    h = h.transpose(1, 0, 2).reshape(n_seq, nblk * SSM_LANE_GROUPS, 2, SSM_STATE)
    return h[:, :, 0], h[:, :, 1]


def _glu_kernel(y_ref, w_ref, b_ref, g_ref, o_ref):
    gl = _gelu(jnp.concatenate([y_ref[k] for k in range(y_ref.shape[0])], axis=1))
    z = _dot(gl.astype(MXU_DTYPE), w_ref[...]) + b_ref[...]
    o = gl * _sigmoid(z)
    ms = jnp.mean(o * o, axis=-1, keepdims=True)
    o_ref[...] = (o * lax.rsqrt(ms + EPS) * g_ref[...]).astype(o_ref.dtype)


def _glu(y_lb, w, b, gain):
    nblk, n, _ = y_lb.shape
    d = nblk * LANES
    tm = _row_tile(n, ROW_TILE)
    vec = pl.BlockSpec((1, d), lambda i: (0, 0))
    return pl.pallas_call(
        _glu_kernel,
        grid=(n // tm,),
        in_specs=[pl.BlockSpec((nblk, tm, LANES), lambda i: (0, i, 0)), pl.BlockSpec((d, d), lambda i: (0, 0)),
                  vec, vec],
        out_specs=pl.BlockSpec((tm, d), lambda i: (i, 0)),
        out_shape=jax.ShapeDtypeStruct((n, d), MXU_DTYPE),
        compiler_params=_cparams(("parallel",)),
    )(y_lb, w, b.reshape(1, d), gain.reshape(1, d))


def _top_rows(x, k):
    n = x.shape[0]
    idx_iota = lax.broadcasted_iota(jnp.int32, x.shape, 0)
    vals, idxs = [], []
    cur = x
    for _ in range(k):
        m = jnp.max(cur, axis=0, keepdims=True)
        ix = jnp.min(jnp.where(cur == m, idx_iota, n), axis=0, keepdims=True)
        vals.append(m)
        idxs.append(ix)
        cur = jnp.where(idx_iota == ix, -jnp.inf, cur)
    return jnp.concatenate(vals, axis=0), jnp.concatenate(idxs, axis=0)


def _pick_rows(table, sel):
    out = jnp.zeros(sel.shape, table.dtype)
    for a in range(table.shape[0]):
        out = out + jnp.where(sel == a, table[a:a + 1, :], 0)
    return out


def _peer_topk_kernel(q_ref, keys_ref, i1_ref, i2_ref, gate_ref):
    q = q_ref[...].astype(MXU_DTYPE)
    s1 = _dot_nt(keys_ref[0, 0].astype(MXU_DTYPE), q[:, :PEER_HALF])
    s2 = _dot_nt(keys_ref[0, 1].astype(MXU_DTYPE), q[:, PEER_HALF:])
    v1, x1 = _top_rows(s1, PEER_TOPK)
    v2, x2 = _top_rows(s2, PEER_TOPK)
    t = v1.shape[1]
    pairs = [(a, b) for a in range(PEER_TOPK) for b in range(PEER_TOPK) if (a + 1) * (b + 1) <= PEER_TOPK]
    n_pad = -len(pairs) % SUBLANES
    cand = jnp.concatenate([v1[a:a + 1] + v2[b:b + 1] for a, b in pairs]
                           + [jnp.full((n_pad, t), -jnp.inf, F32)], axis=0)
    top, pos = _top_rows(cand, PEER_TOPK)
    i1_ref[0] = _pick_rows(jnp.concatenate([x1[a:a + 1] for a, _ in pairs], axis=0), pos)
    i2_ref[0] = _pick_rows(jnp.concatenate([x2[b:b + 1] for _, b in pairs], axis=0), pos)
    e = jnp.exp(top - jnp.max(top, axis=0, keepdims=True))
    gate_ref[0] = e / jnp.sum(e, axis=0, keepdims=True)


def _peer_topk(q, sub_keys):
    n = q.shape[0]
    tt = _row_tile(n, 512)
    out = pl.BlockSpec((1, PEER_TOPK, tt), lambda i, h: (h, 0, i))
    shp = (PEER_HEADS, PEER_TOPK, n)
    return pl.pallas_call(
        _peer_topk_kernel,
        grid=(n // tt, PEER_HEADS),
        in_specs=[pl.BlockSpec((tt, 2 * PEER_HALF), lambda i, h: (i, h)),
                  pl.BlockSpec((1, 2, PEER_KEYS, PEER_HALF), lambda i, h: (h, 0, 0, 0))],
        out_specs=[out, out, out],
        out_shape=[jax.ShapeDtypeStruct(shp, jnp.int32), jax.ShapeDtypeStruct(shp, jnp.int32),
                   jax.ShapeDtypeStruct(shp, F32)],
        compiler_params=_cparams(("parallel", "parallel")),
    )(q, sub_keys)


PEER_EXPERT_BLOCK = 4 * PEER_KEYS


PEER_GRID_HALF = PEER_KEYS // 2
HIGH_HALF = 0xFFFF0000
PEER_BUILD_UNROLL = 64


def _peer_kernel(x_ref, u_ref, v_ref, i1_ref, i2_ref, gate_ref, h_ref, o_ref, g_scr, w_scr, *, tm, n_blocks):
    j = pl.program_id(1)
    n_entries = PEER_HEADS * PEER_TOPK
    words = PEER_GRID_HALF // 2
    blocks_per_half = n_blocks // 2
    unroll = math.gcd(tm, PEER_BUILD_UNROLL)

    def build(base):
        r = lax.broadcasted_iota(jnp.int32, (PEER_GRID_HALF, n_entries), 0)
        i1_of_row = base + jnp.where(r < words, 2 * r, 2 * (r - words) + 1)
        i2_of_row = lax.broadcasted_iota(jnp.int32, (PEER_KEYS, n_entries), 0)

        def body(nb, carry):
            for k in range(unroll):
                n = nb * unroll + k
                a = jnp.where(i1_of_row == i1_ref[pl.ds(n, 1), :], gate_ref[pl.ds(n, 1), :], 0.0)
                b = jnp.where(i2_of_row == i2_ref[pl.ds(n, 1), :], 1.0, 0.0)
                grid = _dot_nt(a.astype(MXU_DTYPE), b.astype(MXU_DTYPE))
                bits = lax.bitcast_convert_type(grid.astype(jnp.bfloat16).astype(F32), jnp.uint32)
                g_scr[pl.ds(pl.multiple_of(n * words, words), words), :] = (
                    jnp.right_shift(bits[:words], jnp.uint32(16)) | (bits[words:] & jnp.uint32(HIGH_HALF)))
            return carry

        lax.fori_loop(0, tm // unroll, body, 0)

    def weights():
        act = _dot_nt(x_ref[...], u_ref[...])
        pairs = PEER_EXPERT_BLOCK // (2 * PEER_KEYS)
        m0 = (j - jnp.where(j >= blocks_per_half, blocks_per_half, 0)) * pairs
        pieces = []
        for q in range(pairs):
            word = g_scr[pl.ds(m0 + q, tm, stride=words), :]
            pieces.append(lax.bitcast_convert_type(jnp.left_shift(word, jnp.uint32(16)), F32))
            pieces.append(lax.bitcast_convert_type(word & jnp.uint32(HIGH_HALF), F32))
        g = jnp.concatenate(pieces, axis=1)
        w_scr[j % 2] = (g * _gelu(act)).astype(MXU_DTYPE)

    def values():
        o_ref[...] += _dot(w_scr[(j + 1) % 2], v_ref[...])

    @pl.when(j == 0)
    def _():
        o_ref[...] = h_ref[...]
        build(0)
        weights()

    @pl.when(j == blocks_per_half)
    def _():
        build(PEER_GRID_HALF)

    @pl.when((j > 0) & (j < n_blocks))
    def _():
        values()
        weights()

    @pl.when(j == n_blocks)
    def _():
        values()


def _peer(xn, u, v, i1, i2, gate, h):
    n, d = xn.shape
    n_blocks = u.shape[0] // PEER_EXPERT_BLOCK
    assert n_blocks * PEER_EXPERT_BLOCK == PEER_KEYS * PEER_KEYS and n_blocks % 2 == 0
    tm = _row_tile(n, 512)
    n_entries = PEER_HEADS * PEER_TOPK
    once = pl.Buffered(1)
    row_in = pl.BlockSpec((tm, d), lambda i, j: (i, 0), pipeline_mode=once)
    ent = pl.BlockSpec((tm, n_entries), lambda i, j: (i, 0))
    return pl.pallas_call(
        functools.partial(_peer_kernel, tm=tm, n_blocks=n_blocks),
        grid=(n // tm, n_blocks + 1),
        in_specs=[pl.BlockSpec((tm, d), lambda i, j: (i, 0)),
                  pl.BlockSpec((PEER_EXPERT_BLOCK, d), lambda i, j: (jnp.minimum(j, n_blocks - 1), 0)),
                  pl.BlockSpec((PEER_EXPERT_BLOCK, d), lambda i, j: (jnp.maximum(j - 1, 0), 0)),
                  ent, ent, ent, row_in],
        out_specs=pl.BlockSpec((tm, d), lambda i, j: (i, 0), pipeline_mode=once),
        out_shape=jax.ShapeDtypeStruct((n, d), F32),
        scratch_shapes=[pltpu.VMEM((tm * PEER_GRID_HALF // 2, PEER_KEYS), jnp.uint32),
                        pltpu.VMEM((2, tm, PEER_EXPERT_BLOCK), MXU_DTYPE)],
        compiler_params=_cparams(("parallel", "arbitrary")),
    )(xn, u, v, i1, i2, gate, h)


def _prepare_weights(rel_table, w_in, q_norm, k_norm, w_phi, w_glu, w_out, peer_w_query, peer_u, peer_v):
    off_gate = ATTN_WIDTH + 6 * KV_WIDTH
    off_ssm = off_gate + 3 * N_HEADS
    w_main = w_in[:, :off_gate].astype(MXU_DTYPE)
    w_ssm = w_in[:, off_ssm:].astype(MXU_DTYPE)
    w_gate = jnp.pad(w_in[:, off_gate:off_ssm], ((0, 0), (0, LANES - 3 * N_HEADS))).astype(MXU_DTYPE)
    n_main = w_main.shape[1]
    ones = jnp.ones((KV_WIDTH,), F32)
    zeros = jnp.zeros((KV_WIDTH,), F32)
    gain = jnp.concatenate([
        jnp.tile(q_norm.astype(F32), N_HEADS), ones, ones,
        jnp.tile(k_norm[1].astype(F32), N_KV_HEADS), ones,
        jnp.tile(k_norm[2].astype(F32), N_KV_HEADS), ones]).reshape(1, n_main)
    mode = jnp.concatenate([
        jnp.ones((ATTN_WIDTH,), F32), zeros, zeros, ones, zeros, ones, zeros]).reshape(1, n_main)
    w_cmp = w_phi.reshape(2, CMP_R, CMP_STRIDE, HEAD_DIM, HEAD_DIM).transpose(0, 2, 3, 1, 4)
    w_cmp = w_cmp.reshape(2, CMP_STRIDE * HEAD_DIM, CMP_R * HEAD_DIM).astype(MXU_DTYPE)
    return dict(
        w_main=w_main, w_ssm=w_ssm, w_gate=w_gate, gain=gain, mode=mode, w_cmp=w_cmp,
        tab=rel_table.astype(F32).T, w_glu=w_glu.astype(MXU_DTYPE), w_out=w_out.astype(MXU_DTYPE),
        w_query=peer_w_query.astype(MXU_DTYPE), peer_u=peer_u.astype(MXU_DTYPE),
        peer_v=peer_v.astype(MXU_DTYPE))


def _overlap_matrix(n_chunks, n_cmp, n_slc, n_pad):
    c0 = np.arange(n_chunks)[:, None] * CMP_STRIDE
    s0 = np.arange(n_pad)[None, :] * SLC_BLOCK
    overlap = np.clip(np.minimum(c0 + CMP_BLOCK, s0 + SLC_BLOCK) - np.maximum(c0, s0), 0, None) / CMP_BLOCK
    overlap = overlap * (np.arange(n_chunks)[:, None] < n_cmp) * (np.arange(n_pad)[None, :] < n_slc)
    return jnp.asarray(overlap, dtype=MXU_DTYPE)


def _expansion_matrix(n_pad, n_keys, n_cols):
    key = np.arange(n_cols)[None, :]
    e = ((key // SLC_BLOCK) == np.arange(n_pad)[:, None]) & (key < n_keys)
    return jnp.asarray(e, dtype=MXU_DTYPE)


def _pad_lanes(n):
    return -(-n // LANES) * LANES


def _layer(x3, t_real, pos0, cache_cmp3, cache_sel3, cache_win3, page_table, h0, wts, norm_mix,
           k_norm, b_phi, ssm, b_glu, out_norm_attn, out_norm_ssm, norm_ffn, sub_keys):
    nb, t, d_model = x3.shape
    n = nb * t
    sample = page_table is not None
    x2 = x3.reshape(n, d_model)
    xn = _rmsnorm(x2, norm_mix, MXU_DTYPE)
    p, kv_cmp, kv_sel, kv_win = _in_proj(xn, wts["w_main"], wts["mode"], wts["gain"])
    gates = _matmul(xn, wts["w_gate"], "sigmoid")
    n_main = p.shape[1]
    p3 = p.reshape(nb, t, n_main)
    col = lambda off: off // HEAD_DIM
    off_sel, off_win = ATTN_WIDTH + 2 * KV_WIDTH, ATTN_WIDTH + 4 * KV_WIDTH

    if sample:
        assert t_real < CMP_STRIDE and pos0 % PAGE_SIZE == 0
        n_pages = page_table.shape[1]
        pk = _compress(cache_cmp3, page_table, wts["w_cmp"])
        total = pos0 + t_real
    else:
        n_pages = t // PAGE_SIZE
        pt = jnp.arange(nb * n_pages, dtype=jnp.int32).reshape(nb, n_pages)
        pk = _compress(kv_cmp.reshape(nb * n_pages, PAGE_SIZE * 2 * N_KV_HEADS, HEAD_DIM), pt, wts["w_cmp"])
        total = t
    n_chunks = pk.shape[2]
    n_cmp = total // CMP_STRIDE - CMP_R + 1
    n_slc = -(-total // SLC_BLOCK)
    n_pad = _pad_lanes(n_slc)
    m_mat = _overlap_matrix(n_chunks, n_cmp, n_slc, n_pad)
    tq_cmp = _row_tile(t, 256)
    tab = wts["tab"]
    o_cmp, sel = _cmp_attention(tab, p3, pk, b_phi.astype(F32), k_norm[0].astype(F32).reshape(1, HEAD_DIM),
                                m_mat, pos0=pos0, n_cmp=n_cmp, n_slc=n_slc, tq=tq_cmp)

    if sample:
        kv_block = 2 * KV_WIDTH
        n_cols = (n_pages + SEL_PAGES_PER_STEP) * KEY_TILE
        e_mat = _expansion_matrix(n_pad, (n_pages + 1) * KEY_TILE, n_cols)
        o_sel = _attn_sample(p3, cache_sel3, page_table, tab, "sel", off_sel // kv_block, pos0, 0, sel, e_mat)
        o_win = _attn_sample(p3, cache_win3, None, tab, "win", off_win // kv_block, pos0, pos0 - WINDOW)
    else:
        e3 = _expansion_matrix(n_pad, t, t).T.reshape(t // KEY_TILE, KEY_TILE, n_pad)
        o_sel = _attn_prompt(p3, tab, "sel", col(off_sel), col(off_sel + KV_WIDTH), sel, e3)
        o_win = _attn_prompt(p3, tab, "win", col(off_win), col(off_win + KV_WIDTH))
    mixed_attn = _combine(o_cmp.reshape(n, ATTN_WIDTH), o_sel.reshape(n, ATTN_WIDTH),
                          o_win.reshape(n, ATTN_WIDTH), gates, out_norm_attn)

    u_lb = _matmul(xn, wts["w_ssm"], "laneblocks")
    nblk = u_lb.shape[0]
    if t % SSM_CHUNK:
        assert t < SSM_CHUNK and t_real <= t
        u_lb = jnp.pad(u_lb.reshape(nblk, nb, t, LANES), ((0, 0), (0, 0), (0, SSM_CHUNK - t), (0, 0)))
        u_lb = u_lb.reshape(nblk, nb * SSM_CHUNK, LANES)
        l_eff, n_chunk = t_real, 1
    else:
        l_eff, n_chunk = SSM_CHUNK, t // SSM_CHUNK
    mats = _ssm_matrices(*ssm, l_eff)
    y_lb, h_last = _ssm(u_lb, *mats, _state_to_blocks(*h0), n_seq=nb, n_chunk=n_chunk)
    if t % SSM_CHUNK:
        y_lb = y_lb.reshape(nblk, nb, SSM_CHUNK, LANES)[:, :, :t].reshape(nblk, n, LANES)
    mixed_ssm = _glu(y_lb, wts["w_glu"], b_glu, out_norm_ssm)

    h = _out_proj(mixed_attn, mixed_ssm, wts["w_out"], x2)
    hn = _rmsnorm(h, norm_ffn, MXU_DTYPE)
    pq = _matmul(hn, wts["w_query"])
    i1, i2, gate = _peer_topk(pq, sub_keys)
    to_rows = lambda a: a.transpose(2, 0, 1).reshape(n, PEER_HEADS * PEER_TOPK)
    y_out = _peer(hn, wts["peer_u"], wts["peer_v"], to_rows(i1), to_rows(i2), to_rows(gate), h)
    new_kv = [a.reshape(nb, t, 2, N_KV_HEADS, HEAD_DIM) for a in (kv_cmp, kv_sel, kv_win)]
    return y_out.reshape(nb, t, d_model), new_kv, _state_from_blocks(h_last)


def kernel(x_prompt, x_sample, cache_kv_cmp, cache_kv_sel, cache_kv_win, state_ssm_re, state_ssm_im, page_table,
           rel_table, norm_mix, w_in, q_norm, k_norm, w_phi, b_phi, ssm_lam_re, ssm_lam_im, ssm_log_dt,
           ssm_b_re, ssm_b_im, ssm_c_re, ssm_c_im, ssm_d, w_glu, b_glu, out_norm_attn, out_norm_ssm, w_out,
           norm_ffn, peer_w_query, peer_sub_keys, peer_u, peer_v):
    depth = w_in.shape[0]
    bp, t_p, d_model = x_prompt.shape
    db, t_s, _ = x_sample.shape
    past_len = page_table.shape[1] * PAGE_SIZE
    kv_row = 2 * KV_WIDTH
    n_groups = ssm_lam_re.shape[1]
    assert cache_kv_win.shape[2] == WINDOW and t_p % KEY_TILE == 0 and t_s <= SAMPLE_ROWS

    yp, ys = x_prompt, jnp.pad(x_sample, ((0, 0), (0, SAMPLE_ROWS - t_s), (0, 0)))
    new_p = [[] for _ in range(5)]
    new_s = [[] for _ in range(5)]
    for l in range(depth):
        wts = _prepare_weights(rel_table, w_in[l], q_norm[l], k_norm[l], w_phi[l], w_glu[l], w_out[l],
                               peer_w_query[l], peer_u[l], peer_v[l])
        ssm = (ssm_lam_re[l], ssm_lam_im[l], ssm_log_dt[l], ssm_b_re[l], ssm_b_im[l], ssm_c_re[l],
               ssm_c_im[l], ssm_d[l])
        shared = (wts, norm_mix[l], k_norm[l], b_phi[l], ssm, b_glu[l], out_norm_attn[l],
                  out_norm_ssm[l], norm_ffn[l], peer_sub_keys[l])

        zero_state = jnp.zeros((bp, n_groups, SSM_STATE), F32)
        yp, new_kv, h_last = _layer(yp, t_p, 0, None, None, None, None, (zero_state, zero_state), *shared)
        keep = min(WINDOW, t_p)
        new_p[0].append(new_kv[0])
        new_p[1].append(new_kv[1])
        new_p[2].append(new_kv[2][:, t_p - keep:])
        new_p[3].append(h_last[0])
        new_p[4].append(h_last[1])

        n_phys = cache_kv_cmp.shape[1]
        h0 = (state_ssm_re[l], state_ssm_im[l])
        page_rows = PAGE_SIZE * 2 * N_KV_HEADS
        ys, new_kv, h_last = _layer(ys, t_s, past_len, cache_kv_cmp[l].reshape(n_phys, page_rows, HEAD_DIM),
                                cache_kv_sel[l].reshape(n_phys, page_rows, HEAD_DIM),
                                cache_kv_win[l].reshape(db * WIN_TILES, page_rows, HEAD_DIM), page_table, h0,
                                *shared)
        new_s[0].append(new_kv[0][:, :t_s])
        new_s[1].append(new_kv[1][:, :t_s])
        win_all = jnp.concatenate([cache_kv_win[l], new_kv[2][:, :t_s]], axis=1)
        keep = min(WINDOW, past_len + t_s)
        new_s[2].append(win_all[:, win_all.shape[1] - keep:])
        new_s[3].append(h_last[0])
        new_s[4].append(h_last[1])
    outs_p = [jnp.stack(v) for v in new_p]
    outs_s = [jnp.stack(v) for v in new_s]
    return (yp, ys[:, :t_s], *outs_p, *outs_s)
```

```python
import functools
import math

import numpy as np
import jax
import jax.numpy as jnp
from jax import lax
from jax.experimental import pallas as pl
from jax.experimental.pallas import tpu as pltpu

F32 = jnp.float32
MXU_DTYPE = jnp.bfloat16

HEAD_DIM = 128
N_KV_HEADS = 4
GQA_REP = 4
N_HEADS = N_KV_HEADS * GQA_REP
ATTN_WIDTH = N_HEADS * HEAD_DIM
KV_WIDTH = N_KV_HEADS * HEAD_DIM
ATTN_SCALE = HEAD_DIM ** -0.5
CMP_BLOCK = 32
CMP_STRIDE = 16
CMP_R = CMP_BLOCK // CMP_STRIDE
SLC_BLOCK = 64
N_SELECT = 16
FORCE_BONUS = 1.0e4
WINDOW = 512
NUM_BUCKETS = 32
MAX_DISTANCE = 128
SSM_GROUP = 16
SSM_STATE = 64
PEER_HEADS = 8
PEER_KEYS = 128
PEER_TOPK = 16
PEER_HALF = 128
EPS = 1e-6
NEG_INF = -1e30
PAGE_SIZE = 128

LANES = 128
SUBLANES = 8
VMEM_LIMIT_BYTES = 52 * 1024 * 1024

ROW_TILE = 512
SEL_PAGES_PER_STEP = 16
KEY_TILE = 128
WIN_TILES = WINDOW // KEY_TILE
SSM_CHUNK = 16
SAMPLE_ROWS = 8


def _cparams(sem):
    return pltpu.CompilerParams(dimension_semantics=sem, vmem_limit_bytes=VMEM_LIMIT_BYTES)


def _row_tile(n, pref):
    t = min(n, pref)
    while n % t:
        t -= SUBLANES
    return t


def _dot(a, b):
    return jnp.dot(a, b, preferred_element_type=F32)


def _dot_nt(a, b):
    return lax.dot_general(a, b, (((1,), (1,)), ((), ())), preferred_element_type=F32)


def _gelu(x):
    c = math.sqrt(2.0 / math.pi)
    return 0.5 * x * (1.0 + jnp.tanh(c * (x + 0.044715 * (x * x * x))))


def _sigmoid(x):
    return 1.0 / (1.0 + jnp.exp(-x))


def _bucket_lower_bounds():
    n = np.arange(2 * MAX_DISTANCE)
    max_exact = NUM_BUCKETS // 2
    nf = np.maximum(n, 1).astype(np.float32)
    large = max_exact + (np.log(nf / np.float32(max_exact)) / np.float32(math.log(MAX_DISTANCE / max_exact))
                         * np.float32(NUM_BUCKETS - max_exact)).astype(np.int32)
    bucket = np.where(n < max_exact, n, np.minimum(large, NUM_BUCKETS - 1))
    assert np.all(np.diff(bucket) >= 0) and bucket[-1] == NUM_BUCKETS - 1
    return [int(np.argmax(bucket >= b)) for b in range(NUM_BUCKETS)]


BUCKET_LO = _bucket_lower_bounds()


def _bias_from_dist(dist, table_entry):
    out = table_entry(0)
    for b in range(1, NUM_BUCKETS):
        out = jnp.where(dist >= BUCKET_LO[b], table_entry(b), out)
    return out


def _rmsnorm_kernel(x_ref, g_ref, o_ref):
    x = x_ref[...]
    ms = jnp.mean(x * x, axis=-1, keepdims=True)
    o_ref[...] = (x * lax.rsqrt(ms + EPS) * g_ref[...]).astype(o_ref.dtype)


def _rmsnorm(x, g, out_dtype):
    n, d = x.shape
    tm = _row_tile(n, ROW_TILE)
    return pl.pallas_call(
        _rmsnorm_kernel,
        grid=(n // tm,),
        in_specs=[pl.BlockSpec((tm, d), lambda i: (i, 0)), pl.BlockSpec((1, d), lambda i: (0, 0))],
        out_specs=pl.BlockSpec((tm, d), lambda i: (i, 0)),
        out_shape=jax.ShapeDtypeStruct((n, d), out_dtype),
        compiler_params=_cparams(("parallel",)),
    )(x, g.reshape(1, d))


def _matmul_kernel(*refs, epilogue):
    a_ref, b_ref = refs[0], refs[1]
    o_ref = refs[-1]
    acc = _dot(a_ref[...], b_ref[...])
    if epilogue == "none":
        o_ref[...] = acc
    elif epilogue == "sigmoid":
        o_ref[...] = _sigmoid(acc)
    elif epilogue == "laneblocks":
        for c in range(acc.shape[1] // LANES):
            o_ref[c] = acc[:, c * LANES:(c + 1) * LANES]
    else:
        raise ValueError(epilogue)


def _matmul(a, b, epilogue="none", extras=(), tm_pref=512, tn_pref=1024):
    m, k = a.shape
    _, n = b.shape
    tm = _row_tile(m, tm_pref)
    tn = min(n, tn_pref)
    while n % tn:
        tn -= LANES
    in_specs = [pl.BlockSpec((tm, k), lambda i, j: (i, 0)), pl.BlockSpec((k, tn), lambda i, j: (0, j))]
    for e in extras:
        if e.shape[0] == 1:
            in_specs.append(pl.BlockSpec((1, tn), lambda i, j: (0, j)))
        else:
            in_specs.append(pl.BlockSpec((tm, tn), lambda i, j: (i, j)))
    if epilogue == "laneblocks":
        out_spec = pl.BlockSpec((tn // LANES, tm, LANES), lambda i, j: (j, i, 0))
        out_shape = jax.ShapeDtypeStruct((n // LANES, m, LANES), F32)
    else:
        out_spec = pl.BlockSpec((tm, tn), lambda i, j: (i, j))
        out_shape = jax.ShapeDtypeStruct((m, n), F32)
    return pl.pallas_call(
        functools.partial(_matmul_kernel, epilogue=epilogue),
        grid=(m // tm, n // tn),
        in_specs=in_specs,
        out_specs=out_spec,
        out_shape=out_shape,
        compiler_params=_cparams(("parallel", "parallel")),
    )(a, b, *extras)


def _in_proj_kernel(a_ref, b_ref, mode_ref, gain_ref, o_ref, *kv_refs, first_kv_tile):
    j = pl.program_id(1)
    acc = _dot(a_ref[...], b_ref[...])
    tm = acc.shape[0]
    blocks = []
    for c in range(acc.shape[1] // HEAD_DIM):
        sl = slice(c * HEAD_DIM, (c + 1) * HEAD_DIM)
        blk = acc[:, sl]
        ms = jnp.mean(blk * blk, axis=-1, keepdims=True)
        nrm = blk * lax.rsqrt(ms + EPS) * gain_ref[:, sl]
        blocks.append(jnp.where(mode_ref[:, sl] > 0.0, nrm, blk))
        o_ref[:, sl] = blocks[-1]
    for b, kv_ref in enumerate(kv_refs):
        @pl.when(j == first_kv_tile + b)
        def _(kv_ref=kv_ref):
            for c, blk in enumerate(blocks):
                kv_ref[pl.ds(c, tm, stride=len(blocks)), :] = blk


def _in_proj(a, b, mode, gain, tm_pref=512):
    m, k = a.shape
    n = b.shape[1]
    tn = 2 * KV_WIDTH
    assert ATTN_WIDTH % tn == 0 and n == ATTN_WIDTH + 3 * tn
    tm = _row_tile(m, tm_pref)
    heads = tn // HEAD_DIM
    vec = pl.BlockSpec((1, tn), lambda i, j: (0, j))
    kv_spec = pl.BlockSpec((tm * heads, HEAD_DIM), lambda i, j: (i, 0))
    kv_shape = jax.ShapeDtypeStruct((m * heads, HEAD_DIM), F32)
    return pl.pallas_call(
        functools.partial(_in_proj_kernel, first_kv_tile=ATTN_WIDTH // tn),
        grid=(m // tm, n // tn),
        in_specs=[pl.BlockSpec((tm, k), lambda i, j: (i, 0)), pl.BlockSpec((k, tn), lambda i, j: (0, j)), vec, vec],
        out_specs=[pl.BlockSpec((tm, tn), lambda i, j: (i, j)), kv_spec, kv_spec, kv_spec],
        out_shape=[jax.ShapeDtypeStruct((m, n), F32), kv_shape, kv_shape, kv_shape],
        compiler_params=_cparams(("parallel", "arbitrary")),
    )(a, b, mode, gain)


def _out_proj_kernel(a1_ref, a2_ref, b1_ref, b2_ref, res_ref, o_ref):
    o_ref[...] = res_ref[...] + _dot(a1_ref[...], b1_ref[...]) + _dot(a2_ref[...], b2_ref[...])


def _out_proj(a1, a2, w, res, tm_pref=512, tn_pref=1024):
    m, k1 = a1.shape
    k2 = a2.shape[1]
    assert k1 == k2 and w.shape[0] == k1 + k2
    n = w.shape[1]
    tm = _row_tile(m, tm_pref)
    tn = min(n, tn_pref)
    return pl.pallas_call(
        _out_proj_kernel,
        grid=(m // tm, n // tn),
        in_specs=[pl.BlockSpec((tm, k1), lambda i, j: (i, 0)), pl.BlockSpec((tm, k2), lambda i, j: (i, 0)),
                  pl.BlockSpec((k1, tn), lambda i, j: (0, j)), pl.BlockSpec((k2, tn), lambda i, j: (1, j)),
                  pl.BlockSpec((tm, tn), lambda i, j: (i, j))],
        out_specs=pl.BlockSpec((tm, tn), lambda i, j: (i, j)),
        out_shape=jax.ShapeDtypeStruct((m, n), F32),
        compiler_params=_cparams(("parallel", "parallel")),
    )(a1, a2, w, w, res)


CMP_PAGES_PER_STEP = 16


def _compress_kernel(pt_ref, *refs):
    del pt_ref
    pages = refs[:CMP_PAGES_PER_STEP]
    w_ref = refs[CMP_PAGES_PER_STEP]
    o_ref = refs[CMP_PAGES_PER_STEP + 1]
    x_scr = refs[CMP_PAGES_PER_STEP + 2]
    chunks_per_page = PAGE_SIZE // CMP_STRIDE
    rows_per_head = CMP_PAGES_PER_STEP * chunks_per_page
    heads_per_row = 2 * N_KV_HEADS
    for c in range(2):
        for g in range(N_KV_HEADS):
            for i in range(CMP_PAGES_PER_STEP):
                r0 = g * rows_per_head + i * chunks_per_page
                for s in range(CMP_STRIDE):
                    x_scr[r0:r0 + chunks_per_page, s * HEAD_DIM:(s + 1) * HEAD_DIM] = pages[i][
                        0, pl.ds(s * heads_per_row + c * N_KV_HEADS + g, chunks_per_page,
                                 stride=CMP_STRIDE * heads_per_row), :]
        res = _dot(x_scr[...].astype(MXU_DTYPE), w_ref[c])
        for g in range(N_KV_HEADS):
            o_ref[0, c * N_KV_HEADS + g] = res[g * rows_per_head:(g + 1) * rows_per_head]


def _compress(rows3, page_table, w_cmp):
    nb, n_pages = page_table.shape
    assert n_pages % CMP_PAGES_PER_STEP == 0
    n_steps = n_pages // CMP_PAGES_PER_STEP
    chunks_per_step = CMP_PAGES_PER_STEP * PAGE_SIZE // CMP_STRIDE
    n_chunks = n_pages * PAGE_SIZE // CMP_STRIDE

    def page_spec(i):
        return pl.BlockSpec((1, PAGE_SIZE * 2 * N_KV_HEADS, HEAD_DIM),
                            lambda b, j, pt: (pt[b, j * CMP_PAGES_PER_STEP + i], 0, 0))

    grid_spec = pltpu.PrefetchScalarGridSpec(
        num_scalar_prefetch=1,
        grid=(nb, n_steps),
        in_specs=[page_spec(i) for i in range(CMP_PAGES_PER_STEP)]
        + [pl.BlockSpec((2, CMP_STRIDE * HEAD_DIM, CMP_R * HEAD_DIM), lambda b, j, pt: (0, 0, 0))],
        out_specs=pl.BlockSpec((1, 2 * N_KV_HEADS, chunks_per_step, CMP_R * HEAD_DIM),
                               lambda b, j, pt: (b, 0, j, 0)),
        scratch_shapes=[pltpu.VMEM((N_KV_HEADS * chunks_per_step, CMP_STRIDE * HEAD_DIM), F32)],
    )
    return pl.pallas_call(
        _compress_kernel,
        grid_spec=grid_spec,
        out_shape=jax.ShapeDtypeStruct((nb, 2 * N_KV_HEADS, n_chunks, CMP_R * HEAD_DIM), F32),
        compiler_params=_cparams(("parallel", "arbitrary")),
    )(page_table, *([rows3] * CMP_PAGES_PER_STEP), w_cmp)


def _cmp_attn_kernel(tab_ref, q_ref, kp_ref, vp_ref, bphi_ref, kn_ref, m_ref, o_ref, sel_ref, bias_scr,
                     *, tq, pos0, n_cmp, n_slc):
    g = pl.program_id(0)
    i = pl.program_id(1)
    n_chunks = kp_ref.shape[2]
    kp = kp_ref[0, 0]
    vp = vp_ref[0, 0]
    k_c = bphi_ref[0:1, :] + kp[:, :HEAD_DIM] + pltpu.roll(kp[:, HEAD_DIM:], n_chunks - 1, 0)
    v_c = bphi_ref[1:2, :] + vp[:, :HEAD_DIM] + pltpu.roll(vp[:, HEAD_DIM:], n_chunks - 1, 0)
    ms = jnp.mean(k_c * k_c, axis=-1, keepdims=True)
    k_c = k_c * lax.rsqrt(ms + EPS) * kn_ref[...]

    q = jnp.concatenate([q_ref[0, :, r * HEAD_DIM:(r + 1) * HEAD_DIM] for r in range(GQA_REP)], axis=0)
    logits = _dot_nt(q.astype(MXU_DTYPE), k_c.astype(MXU_DTYPE)) * ATTN_SCALE
    row = lax.broadcasted_iota(jnp.int32, (tq, n_chunks), 0)
    col = lax.broadcasted_iota(jnp.int32, (tq, n_chunks), 1)
    dist = pos0 + i * tq + row - (col * CMP_STRIDE + (CMP_BLOCK - 1))

    @pl.when(pl.program_id(2) == 0)
    def _():
        bias_scr[...] = jnp.concatenate(
            [_bias_from_dist(dist, lambda b, r=r: tab_ref[g * GQA_REP + r, b]) for r in range(GQA_REP)], axis=0)

    bias = bias_scr[...]
    valid = (dist >= 0) & (col < n_cmp)
    valid = jnp.concatenate([valid] * GQA_REP, axis=0)
    logits = jnp.where(valid, logits + bias, NEG_INF)
    mx = jnp.max(logits, axis=-1, keepdims=True)
    e = jnp.exp(logits - mx)
    any_valid = (jnp.max(jnp.where(valid, 1.0, 0.0), axis=-1, keepdims=True))
    p = e / jnp.sum(e, axis=-1, keepdims=True) * any_valid
    out = _dot(p.astype(MXU_DTYPE), v_c.astype(MXU_DTYPE))
    for r in range(GQA_REP):
        o_ref[0, :, r * HEAD_DIM:(r + 1) * HEAD_DIM] = out[r * tq:(r + 1) * tq]

    psum = p[0:tq]
    for r in range(1, GQA_REP):
        psum = psum + p[r * tq:(r + 1) * tq]
    hi = psum.astype(MXU_DTYPE)
    lo = (psum - hi.astype(F32)).astype(MXU_DTYPE)
    imp = _dot(hi, m_ref[...]) + _dot(lo, m_ref[...])
    n_pad = imp.shape[1]
    j = lax.broadcasted_iota(jnp.int32, (tq, n_pad), 1)
    qp = pos0 + i * tq + lax.broadcasted_iota(jnp.int32, (tq, n_pad), 0)
    cur = jnp.right_shift(qp, int(math.log2(SLC_BLOCK)))
    ok = j * SLC_BLOCK <= qp
    forced = (j == 0) | (j == cur) | (j == cur - 1)
    score = jnp.where(ok, imp + jnp.where(forced, FORCE_BONUS, 0.0), NEG_INF)
    rank = jnp.zeros((tq, n_pad), F32)
    for s in range(n_slc):
        cs = score[:, s:s + 1]
        beats = (cs > score) | ((cs == score) & (j > s))
        rank = rank + jnp.where(beats, 1.0, 0.0)
    selected = (rank < float(min(N_SELECT, n_slc))) & (score > 0.5 * NEG_INF)
    sel_ref[0, 0] = jnp.where(selected, 1.0, 0.0)


def _cmp_attention(tab, p3, pk, bphi, k_norm0, m_mat, *, pos0, n_cmp, n_slc, tq):
    nb, t, _ = p3.shape
    n_chunks = pk.shape[2]
    n_pad = m_mat.shape[1]
    nq = t // tq
    kern = functools.partial(_cmp_attn_kernel, tq=tq, pos0=pos0, n_cmp=n_cmp, n_slc=n_slc)
    return pl.pallas_call(
        kern,
        grid=(N_KV_HEADS, nq, nb),
        in_specs=[
            pl.BlockSpec(memory_space=pltpu.SMEM),
            pl.BlockSpec((1, tq, GQA_REP * HEAD_DIM), lambda g, i, b: (b, i, g)),
            pl.BlockSpec((1, 1, n_chunks, CMP_R * HEAD_DIM), lambda g, i, b: (b, g, 0, 0)),
            pl.BlockSpec((1, 1, n_chunks, CMP_R * HEAD_DIM), lambda g, i, b: (b, N_KV_HEADS + g, 0, 0)),
            pl.BlockSpec((2, HEAD_DIM), lambda g, i, b: (0, 0)),
            pl.BlockSpec((1, HEAD_DIM), lambda g, i, b: (0, 0)),
            pl.BlockSpec((n_chunks, n_pad), lambda g, i, b: (0, 0)),
        ],
        out_specs=[
            pl.BlockSpec((1, tq, GQA_REP * HEAD_DIM), lambda g, i, b: (b, i, g)),
            pl.BlockSpec((1, 1, tq, n_pad), lambda g, i, b: (b, g, i, 0)),
        ],
        out_shape=[
            jax.ShapeDtypeStruct((nb, t, ATTN_WIDTH), F32),
            jax.ShapeDtypeStruct((nb, N_KV_HEADS, t, n_pad), F32),
        ],
        scratch_shapes=[pltpu.VMEM((GQA_REP * tq, n_chunks), F32)],
        compiler_params=_cparams(("parallel", "parallel", "arbitrary")),
    )(tab, p3, pk, pk, bphi, k_norm0, m_mat)


def _online_softmax_step(s, v, m_scr, l_scr, acc_scr):
    m_old = m_scr[...]
    m_new = jnp.maximum(m_old, jnp.max(s, axis=-1, keepdims=True))
    alpha = jnp.exp(m_old - m_new)
    p = jnp.exp(s - m_new)
    l_scr[...] = alpha * l_scr[...] + jnp.sum(p, axis=-1, keepdims=True)
    acc_scr[...] = alpha * acc_scr[...] + _dot(p.astype(MXU_DTYPE), v)
    m_scr[...] = m_new


ATTN_TILES_PER_ITER = 4


def _attn_prompt_kernel(*refs, mode, n_kt):
    if mode == "sel":
        q_ref, k_ref, v_ref, mb_ref, sel_ref, e_ref, o_ref, kb_scr, vt_scr, m_scr, l_scr, acc_scr = refs
    else:
        q_ref, k_ref, v_ref, mb_ref, o_ref, kb_scr, vt_scr, m_scr, l_scr, acc_scr = refs
    tq = KEY_TILE
    qt = pl.program_id(2)

    @pl.when(qt == 0)
    def _():
        for kt in range(n_kt):
            rows = slice(kt * KEY_TILE, (kt + 1) * KEY_TILE)
            kb_scr[kt] = k_ref[0, rows, :].astype(MXU_DTYPE)
            vt_scr[kt] = v_ref[0, rows, :].T.astype(MXU_DTYPE)

    q = jnp.concatenate([q_ref[0, :, r * HEAD_DIM:(r + 1) * HEAD_DIM] for r in range(GQA_REP)], axis=0)
    q = (q * ATTN_SCALE).astype(MXU_DTYPE)
    m_scr[...] = jnp.full(m_scr.shape, NEG_INF, F32)
    l_scr[...] = jnp.zeros(l_scr.shape, F32)
    acc_scr[...] = jnp.zeros(acc_scr.shape, F32)
    if mode == "sel":
        sel = sel_ref[0, 0].astype(MXU_DTYPE)

    n_kinds = mb_ref.shape[0]

    def logits(kt):
        d = qt - kt
        if mode == "sel":
            kind = jnp.where(d < 0, n_kinds - 1, jnp.minimum(d, 2))
        else:
            kind = jnp.where(kt < 0, n_kinds - 1, jnp.where(d == WIN_TILES, 3, jnp.minimum(d, 2)))
        kt = jnp.clip(kt, 0, n_kt - 1)
        s = _dot_nt(kb_scr[kt], q) + mb_ref[kind, 0]
        if mode == "sel":
            chosen = _dot_nt(e_ref[kt], sel)
            s = s + jnp.concatenate([(1.0 - chosen) * NEG_INF] * GQA_REP, axis=1)
        return kt, s

    def update(key_tiles):
        tiles = [logits(kt) for kt in key_tiles]
        m_old = m_scr[...]
        m_new = m_old
        for _, s in tiles:
            m_new = jnp.maximum(m_new, jnp.max(s, axis=0, keepdims=True))
        alpha = jnp.exp(m_old - m_new)
        l_new = alpha * l_scr[...]
        acc = alpha * acc_scr[...]
        for kt, s in tiles:
            p = jnp.exp(s - m_new)
            l_new = l_new + jnp.sum(p, axis=0, keepdims=True)
            acc = acc + _dot(vt_scr[kt], p.astype(MXU_DTYPE))
        l_scr[...] = l_new
        acc_scr[...] = acc
        m_scr[...] = m_new

    if mode == "sel":
        def body(it, carry):
            update([it * ATTN_TILES_PER_ITER + u for u in range(ATTN_TILES_PER_ITER)])
            return carry

        lax.fori_loop(0, (qt + ATTN_TILES_PER_ITER) // ATTN_TILES_PER_ITER, body, 0)
    else:
        update([qt - WIN_TILES + u for u in range(WIN_TILES + 1)])
    out = acc_scr[...] / l_scr[...]
    for r in range(GQA_REP):
        o_ref[0, :, r * HEAD_DIM:(r + 1) * HEAD_DIM] = out[:, r * tq:(r + 1) * tq].T


def _prompt_bias_tiles(tab, mode):
    ii = jnp.arange(KEY_TILE, dtype=jnp.int32)
    entry = lambda b: tab[:, b][:, None, None]

    def tile(offset):
        dist = (offset * KEY_TILE + ii[:, None] - ii[None, :])[None]
        return dist, jnp.broadcast_to(_bias_from_dist(dist, entry), (N_HEADS, KEY_TILE, KEY_TILE))

    d0, b0 = tile(0)
    tiles = [jnp.where(d0 >= 0, b0, NEG_INF), tile(1)[1], tile(2)[1]]
    if mode == "win":
        d4, b4 = tile(WIN_TILES)
        tiles.append(jnp.where(d4 <= WINDOW, b4, NEG_INF))
    tiles.append(jnp.full_like(b0, NEG_INF))
    mb = jnp.stack(tiles).reshape(len(tiles), N_KV_HEADS, GQA_REP, KEY_TILE, KEY_TILE)
    return mb.transpose(0, 1, 4, 2, 3).reshape(len(tiles), N_KV_HEADS, KEY_TILE, GQA_REP * KEY_TILE)


def _attn_prompt(p3, tab, mode, k_col, v_col, sel=None, e3=None):
    nb, t, _ = p3.shape
    tq = KEY_TILE
    nq = t // tq
    mb = _prompt_bias_tiles(tab, mode)
    cols = GQA_REP * tq
    in_specs = [
        pl.BlockSpec((1, tq, GQA_REP * HEAD_DIM), lambda b, g, i: (b, i, g)),
        pl.BlockSpec((1, t, HEAD_DIM), lambda b, g, i: (b, 0, k_col + g)),
        pl.BlockSpec((1, t, HEAD_DIM), lambda b, g, i: (b, 0, v_col + g)),
        pl.BlockSpec((mb.shape[0], 1, KEY_TILE, cols), lambda b, g, i: (0, g, 0, 0)),
    ]
    args = [p3, p3, p3, mb]
    if mode == "sel":
        n_pad = sel.shape[-1]
        in_specs += [
            pl.BlockSpec((1, 1, tq, n_pad), lambda b, g, i: (b, g, i, 0)),
            pl.BlockSpec((nq, KEY_TILE, n_pad), lambda b, g, i: (0, 0, 0)),
        ]
        args += [sel, e3]
    return pl.pallas_call(
        functools.partial(_attn_prompt_kernel, mode=mode, n_kt=nq),
        grid=(nb, N_KV_HEADS, nq),
        in_specs=in_specs,
        out_specs=pl.BlockSpec((1, tq, GQA_REP * HEAD_DIM), lambda b, g, i: (b, i, g)),
        out_shape=jax.ShapeDtypeStruct((nb, t, ATTN_WIDTH), F32),
        scratch_shapes=[pltpu.VMEM((nq, KEY_TILE, HEAD_DIM), MXU_DTYPE),
                        pltpu.VMEM((nq, HEAD_DIM, KEY_TILE), MXU_DTYPE),
                        pltpu.VMEM((1, cols), F32), pltpu.VMEM((1, cols), F32),
                        pltpu.VMEM((HEAD_DIM, cols), F32)],
        compiler_params=_cparams(("parallel", "parallel", "arbitrary")),
    )(*args)


def _attn_sample_kernel(*refs, mode, paged, pages, n_steps, tq, pos0, base_pos, near_keys):
    it = iter(refs)
    if paged:
        next(it)
    q_ref = next(it)
    page_refs = [next(it) for _ in range(pages)]
    new_ref, rowtab_ref = next(it), next(it)
    if mode == "sel":
        sel_ref, e_ref = next(it), next(it)
    o_ref, m_scr, l_scr, acc_scr = next(it), next(it), next(it), next(it)
    kk = pl.program_id(1)
    rows = N_HEADS * tq
    heads_per_row = 2 * N_KV_HEADS

    @pl.when(kk == 0)
    def _():
        m_scr[...] = jnp.full(m_scr.shape, NEG_INF, F32)
        l_scr[...] = jnp.zeros(l_scr.shape, F32)
        acc_scr[...] = jnp.zeros(acc_scr.shape, F32)

    def queries():
        zero = jnp.zeros((tq, HEAD_DIM), F32)
        blocks = []
        for h in range(N_HEADS):
            qh = q_ref[0, :, h * HEAD_DIM:(h + 1) * HEAD_DIM] * ATTN_SCALE
            blocks.append(jnp.concatenate([qh if g == h // GQA_REP else zero for g in range(N_KV_HEADS)], axis=1))
        return jnp.concatenate(blocks, axis=0).astype(MXU_DTYPE)

    def step(k_all, v_all, key_pos0, chosen_cols, near_keys):
        n_keys = k_all.shape[0]
        s = _dot_nt(queries(), k_all)
        row = lax.broadcasted_iota(jnp.int32, (rows, n_keys), 0) & (tq - 1)
        col = lax.broadcasted_iota(jnp.int32, (rows, n_keys), 1)
        dist = pos0 + row - (key_pos0 + col)
        bias = rowtab_ref[:, NUM_BUCKETS - 1:NUM_BUCKETS]
        if near_keys:
            near = _bias_from_dist(dist[:, n_keys - near_keys:], lambda b: rowtab_ref[:, b:b + 1])
            if near_keys < n_keys:
                near = jnp.concatenate([jnp.broadcast_to(bias, (rows, n_keys - near_keys)), near], axis=1)
            bias = near
        if mode == "sel":
            sel = sel_ref[0].reshape(N_KV_HEADS * tq, sel_ref.shape[-1]).astype(MXU_DTYPE)
            chosen = _dot(sel, chosen_cols)
            chosen = jnp.concatenate(
                [chosen[g * tq:(g + 1) * tq] for g in range(N_KV_HEADS) for _ in range(GQA_REP)], axis=0)
            mask = (dist >= 0) & (chosen > 0.5)
        else:
            mask = (dist >= 0) & (dist <= WINDOW)
        s = jnp.where(mask, s + bias, NEG_INF)
        _online_softmax_step(s, v_all, m_scr, l_scr, acc_scr)

    def cached(kv):
        return jnp.concatenate([
            jnp.concatenate([pg[0, pl.ds(kv * N_KV_HEADS + g, PAGE_SIZE, stride=heads_per_row), :]
                             for g in range(N_KV_HEADS)], axis=1)
            for pg in page_refs], axis=0).astype(MXU_DTYPE)

    keys_per_step = pages * PAGE_SIZE
    key_pos0 = base_pos + kk * keys_per_step
    far = pos0 - (key_pos0 + keys_per_step - 1) >= MAX_DISTANCE
    is_cache = kk < n_steps
    chosen_cols = e_ref[...] if mode == "sel" else None

    @pl.when(is_cache & far)
    def _():
        step(cached(0), cached(1), key_pos0, chosen_cols, 0)

    @pl.when(is_cache & jnp.logical_not(far))
    def _():
        step(cached(0), cached(1), key_pos0, chosen_cols, near_keys)

    @pl.when(kk == n_steps)
    def _():
        pad = jnp.zeros((KEY_TILE - tq, N_KV_HEADS * HEAD_DIM), F32)
        k_new = jnp.concatenate([new_ref[0, :, :KV_WIDTH], pad], axis=0).astype(MXU_DTYPE)
        v_new = jnp.concatenate([new_ref[0, :, KV_WIDTH:], pad], axis=0).astype(MXU_DTYPE)
        cols = e_ref[:, :KEY_TILE] if mode == "sel" else None
        step(k_new, v_new, pos0, cols, KEY_TILE)
        out = acc_scr[...] / l_scr[...]
        for h in range(N_HEADS):
            g = h // GQA_REP
            o_ref[0, :, h * HEAD_DIM:(h + 1) * HEAD_DIM] = out[h * tq:(h + 1) * tq, g * HEAD_DIM:(g + 1) * HEAD_DIM]


def _attn_sample(p3, cache_rows, page_table, tab, mode, new_col, pos0, base_pos, sel=None, e_mat=None):
    nb, tq, _ = p3.shape
    paged = page_table is not None
    pages = SEL_PAGES_PER_STEP if paged else WIN_TILES
    n_tiles = page_table.shape[1] if paged else WIN_TILES
    assert n_tiles % pages == 0
    n_steps = n_tiles // pages
    rows = N_HEADS * tq
    rowtab = jnp.repeat(tab, tq, axis=0)

    def page_spec(i):
        def index(b, kk, *pt):
            page = jnp.minimum(kk, n_steps - 1) * pages + i
            return (pt[0][b, page] if paged else b * n_tiles + page, 0, 0)
        return pl.BlockSpec((1, PAGE_SIZE * 2 * N_KV_HEADS, HEAD_DIM), index)

    def fixed(shape, fn):
        return pl.BlockSpec(shape, lambda b, kk, *_: fn(b, kk))

    in_specs = [fixed((1, tq, ATTN_WIDTH), lambda b, kk: (b, 0, 0))]
    in_specs += [page_spec(i) for i in range(pages)]
    in_specs += [fixed((1, tq, 2 * KV_WIDTH), lambda b, kk: (b, 0, new_col)),
                 fixed((rows, NUM_BUCKETS), lambda b, kk: (0, 0))]
    args = [p3] + [cache_rows] * pages + [p3, rowtab]
    if mode == "sel":
        n_pad = sel.shape[-1]
        in_specs += [fixed((1, N_KV_HEADS, tq, n_pad), lambda b, kk: (b, 0, 0, 0)),
                     fixed((n_pad, pages * PAGE_SIZE), lambda b, kk: (0, kk))]
        args += [sel, e_mat]
    keys_per_step = pages * PAGE_SIZE
    near_keys = PAGE_SIZE if base_pos + n_tiles * PAGE_SIZE == pos0 else keys_per_step
    kern = functools.partial(_attn_sample_kernel, mode=mode, paged=paged, pages=pages, n_steps=n_steps, tq=tq,
                             pos0=pos0, base_pos=base_pos, near_keys=near_keys)
    out_spec = fixed((1, tq, ATTN_WIDTH), lambda b, kk: (b, 0, 0))
    out_shape = jax.ShapeDtypeStruct((nb, tq, ATTN_WIDTH), F32)
    scratch = [pltpu.VMEM((rows, 1), F32), pltpu.VMEM((rows, 1), F32),
               pltpu.VMEM((rows, N_KV_HEADS * HEAD_DIM), F32)]
    cp = _cparams(("parallel", "arbitrary"))
    grid = (nb, n_steps + 1)
    if paged:
        grid_spec = pltpu.PrefetchScalarGridSpec(num_scalar_prefetch=1, grid=grid, in_specs=in_specs,
                                                 out_specs=out_spec, scratch_shapes=scratch)
        return pl.pallas_call(kern, grid_spec=grid_spec, out_shape=out_shape, compiler_params=cp)(
            page_table, *args)
    return pl.pallas_call(kern, grid=grid, in_specs=in_specs, out_specs=out_spec, out_shape=out_shape,
                          scratch_shapes=scratch, compiler_params=cp)(*args)


def _combine_kernel(oc_ref, os_ref, ow_ref, gate_ref, g_ref, o_ref):
    gates = gate_ref[...]
    parts = []
    for h in range(N_HEADS):
        sl = slice(h * HEAD_DIM, (h + 1) * HEAD_DIM)
        parts.append(gates[:, h:h + 1] * oc_ref[:, sl]
                     + gates[:, N_HEADS + h:N_HEADS + h + 1] * os_ref[:, sl]
                     + gates[:, 2 * N_HEADS + h:2 * N_HEADS + h + 1] * ow_ref[:, sl])
    o = jnp.concatenate(parts, axis=1)
    ms = jnp.mean(o * o, axis=-1, keepdims=True)
    o_ref[...] = (o * lax.rsqrt(ms + EPS) * g_ref[...]).astype(o_ref.dtype)


def _combine(o_cmp, o_sel, o_win, gates, gain):
    n, w = o_cmp.shape
    tm = _row_tile(n, ROW_TILE)
    row = pl.BlockSpec((tm, w), lambda i: (i, 0))
    return pl.pallas_call(
        _combine_kernel,
        grid=(n // tm,),
        in_specs=[row, row, row, pl.BlockSpec((tm, LANES), lambda i: (i, 0)),
                  pl.BlockSpec((1, w), lambda i: (0, 0))],
        out_specs=row,
        out_shape=jax.ShapeDtypeStruct((n, w), MXU_DTYPE),
        compiler_params=_cparams(("parallel",)),
    )(o_cmp, o_sel, o_win, gates, gain.reshape(1, w))


SSM_LANE_GROUPS = LANES // SSM_GROUP
SSM_BLOCK_STATE = SSM_LANE_GROUPS * 2 * SSM_STATE


def _ssm_kernel(u_ref, kt_ref, bs_ref, cs_ref, d_ref, la_ref, lb_ref, h0_ref, y_ref, hout_ref,
                us_scr, s_scr, hs_scr, *, n_seq, n_chunk):
    L = SSM_CHUNK
    rows = n_seq * n_chunk
    st = SSM_BLOCK_STATE
    group_shift = int(math.log2(SSM_GROUP))
    state_shift = int(math.log2(2 * SSM_STATE))
    in_mask = (jnp.right_shift(lax.broadcasted_iota(jnp.int32, (LANES, st), 0), group_shift)
               == jnp.right_shift(lax.broadcasted_iota(jnp.int32, (LANES, st), 1), state_shift))
    out_mask = (jnp.right_shift(lax.broadcasted_iota(jnp.int32, (st, LANES), 0), state_shift)
                == jnp.right_shift(lax.broadcasted_iota(jnp.int32, (st, LANES), 1), group_shift))
    zero = jnp.zeros((), MXU_DTYPE)

    def pair(s):
        return jnp.concatenate([us_scr[s], us_scr[s + 1]], axis=1)

    for s in range(L):
        us_scr[s] = u_ref[0, pl.ds(s, rows, stride=L), :].astype(MXU_DTYPE)
    s_acc = jnp.zeros((rows, st), F32)
    for s in range(0, L, 2):
        b_full = jnp.concatenate(
            [jnp.where(in_mask, jnp.concatenate([bs_ref[0, s + k]] * SSM_LANE_GROUPS, axis=1), zero)
             for k in range(2)], axis=0)
        s_acc = s_acc + _dot(pair(s), b_full)
    s_scr[...] = s_acc

    la = la_ref[0]
    lb = lb_ref[0]
    is_re = (lax.broadcasted_iota(jnp.int32, (1, st), 1) & SSM_STATE) == 0

    def swap(h):
        return jnp.where(is_re, pltpu.roll(h, st - SSM_STATE, 1), pltpu.roll(h, SSM_STATE, 1))

    def body(c, hs):
        new = []
        for b in range(n_seq):
            row = b * n_chunk + c
            hs_scr[pl.ds(row, 1), :] = hs[b]
            new.append(la * hs[b] + lb * swap(hs[b]) + s_scr[pl.ds(row, 1), :])
        return tuple(new)

    hs = lax.fori_loop(0, n_chunk, body, tuple(h0_ref[0, b:b + 1, :] for b in range(n_seq)))
    for b in range(n_seq):
        hout_ref[0, b:b + 1, :] = hs[b]

    h_start = hs_scr[...].astype(MXU_DTYPE)
    no_k = jnp.zeros((LANES, LANES), MXU_DTYPE)

    def k_tile(tau):
        return kt_ref[0, tau] if tau >= 0 else no_k

    for t in range(0, L, 2):
        c_full = jnp.concatenate(
            [jnp.where(out_mask, jnp.concatenate([cs_ref[0, t + k]] * SSM_LANE_GROUPS, axis=0), zero)
             for k in range(2)], axis=1)
        acc = _dot(h_start, c_full)
        for s in range(0, t + 1, 2):
            from_s = jnp.concatenate([k_tile(t - s), k_tile(t + 1 - s)], axis=1)
            from_s1 = jnp.concatenate([k_tile(t - s - 1), k_tile(t - s)], axis=1)
            acc = acc + _dot(pair(s), jnp.concatenate([from_s, from_s1], axis=0))
        for k in range(2):
            y_ref[0, pl.ds(t + k, rows, stride=L), :] = (
                acc[:, k * LANES:(k + 1) * LANES] + d_ref[0] * u_ref[0, pl.ds(t + k, rows, stride=L), :])


def _ssm(u_lb, kt, bs, cs, dvec, la, lb, h0, *, n_seq, n_chunk):
    nblk, n, w = u_lb.shape
    L = SSM_CHUNK
    rows = n_seq * n_chunk
    assert n == rows * L and w == LANES
    st = SSM_BLOCK_STATE

    def blk(shape):
        return pl.BlockSpec((1,) + shape, lambda i: (i,) + (0,) * len(shape))

    return pl.pallas_call(
        functools.partial(_ssm_kernel, n_seq=n_seq, n_chunk=n_chunk),
        grid=(nblk,),
        in_specs=[blk((n, w)), blk((L, w, w)), blk((L, w, w)), blk((L, w, w)), blk((1, w)), blk((1, st)),
                  blk((1, st)), blk((n_seq, st))],
        out_specs=[blk((n, w)), blk((n_seq, st))],
        out_shape=[jax.ShapeDtypeStruct((nblk, n, w), F32), jax.ShapeDtypeStruct((nblk, n_seq, st), F32)],
        scratch_shapes=[pltpu.VMEM((L, rows, w), MXU_DTYPE), pltpu.VMEM((rows, st), F32),
                        pltpu.VMEM((rows, st), F32)],
        compiler_params=_cparams(("parallel",)),
    )(u_lb, kt, bs, cs, dvec, la, lb, h0)


def _ssm_matrices(lam_re, lam_im, log_dt, b_re, b_im, c_re, c_im, d_skip, l_eff):
    g = lam_re.shape[0]
    L = SSM_CHUNK
    lg = SSM_LANE_GROUPS
    nblk = g // lg
    lam = lax.complex(lam_re.astype(F32), lam_im.astype(F32))
    dt = jnp.exp(log_dt.astype(F32))[:, None]
    lam_bar = jnp.exp(lam * dt)
    b_bar = ((lam_bar - 1.0) / lam)[..., None] * lax.complex(b_re.astype(F32), b_im.astype(F32))
    c_c = lax.complex(c_re.astype(F32), c_im.astype(F32))
    pw = [jnp.ones_like(lam_bar)]
    for _ in range(L):
        pw.append(pw[-1] * lam_bar)
    pw = jnp.stack(pw)
    pw_b = pw.reshape(L + 1, nblk, lg, SSM_STATE)
    bb = b_bar.reshape(nblk, lg, SSM_STATE, SSM_GROUP).transpose(0, 1, 3, 2)
    chan_group = np.arange(LANES) // SSM_GROUP
    to_chan = jnp.asarray(chan_group[None, :] == np.arange(lg)[:, None], F32)
    out_lane = jnp.asarray(np.arange(LANES)[None, :] % SSM_GROUP == np.arange(SSM_GROUP)[:, None], F32)
    same_group = jnp.asarray(chan_group[:, None] == chan_group[None, :], F32)
    kern = jnp.einsum("gop,kgp,gpi->gkio", c_c, pw[:L], b_bar).real
    kern = kern.reshape(nblk, lg, L, SSM_GROUP, SSM_GROUP).transpose(0, 2, 1, 3, 4)
    kern = kern.reshape(nblk, L, LANES, SSM_GROUP)
    kt = jnp.einsum("ltro,oc->ltrc", kern, out_lane, precision=lax.Precision.HIGHEST) * same_group
    exps = np.clip(l_eff - 1 - np.arange(L), 0, L)
    bx = pw_b[exps].transpose(1, 0, 2, 3)[:, :, :, None, :] * bb[:, None]
    bx = jnp.where(jnp.asarray(np.arange(L) < l_eff)[None, :, None, None, None], bx, 0.0)
    bs = jnp.concatenate([bx.real, bx.imag], axis=-1).reshape(nblk, L, LANES, 2 * SSM_STATE)
    c_t = c_c.reshape(nblk, LANES, SSM_STATE).transpose(0, 2, 1)
    pw_t = pw_b[1:L + 1].transpose(1, 0, 3, 2)
    expand = lambda a: jnp.einsum("ltpg,gc->ltpc", a, to_chan, precision=lax.Precision.HIGHEST)
    pr, pi = expand(pw_t.real), expand(pw_t.imag)
    cr, ci = c_t.real[:, None], c_t.imag[:, None]
    cs = jnp.concatenate([cr * pr - ci * pi, -(cr * pi + ci * pr)], axis=2)
    lam_l = pw[l_eff].reshape(nblk, lg, SSM_STATE)
    la = jnp.stack([lam_l.real, lam_l.real], axis=2).reshape(nblk, 1, SSM_BLOCK_STATE)
    lb = jnp.stack([-lam_l.imag, lam_l.imag], axis=2).reshape(nblk, 1, SSM_BLOCK_STATE)
    dvec = d_skip.astype(F32).reshape(nblk, 1, LANES)
    return kt.astype(MXU_DTYPE), bs.astype(MXU_DTYPE), cs.astype(MXU_DTYPE), dvec, la, lb


def _state_to_blocks(re, im):
    n_seq, g, p = re.shape
    h = jnp.stack([re, im], axis=2).astype(F32).reshape(n_seq, g // SSM_LANE_GROUPS, SSM_BLOCK_STATE)
    return h.transpose(1, 0, 2)


def _state_from_blocks(h):
    nblk, n_seq, _ = h.shape
    h = h.transpose(1, 0, 2).reshape(n_seq, nblk * SSM_LANE_GROUPS, 2, SSM_STATE)
    return h[:, :, 0], h[:, :, 1]


def _glu_kernel(y_ref, w_ref, b_ref, g_ref, o_ref):
    gl = _gelu(jnp.concatenate([y_ref[k] for k in range(y_ref.shape[0])], axis=1))
    z = _dot(gl.astype(MXU_DTYPE), w_ref[...]) + b_ref[...]
    o = gl * _sigmoid(z)
    ms = jnp.mean(o * o, axis=-1, keepdims=True)
    o_ref[...] = (o * lax.rsqrt(ms + EPS) * g_ref[...]).astype(o_ref.dtype)


def _glu(y_lb, w, b, gain):
    nblk, n, _ = y_lb.shape
    d = nblk * LANES
    tm = _row_tile(n, ROW_TILE)
    vec = pl.BlockSpec((1, d), lambda i: (0, 0))
    return pl.pallas_call(
        _glu_kernel,
        grid=(n // tm,),
        in_specs=[pl.BlockSpec((nblk, tm, LANES), lambda i: (0, i, 0)), pl.BlockSpec((d, d), lambda i: (0, 0)),
                  vec, vec],
        out_specs=pl.BlockSpec((tm, d), lambda i: (i, 0)),
        out_shape=jax.ShapeDtypeStruct((n, d), MXU_DTYPE),
        compiler_params=_cparams(("parallel",)),
    )(y_lb, w, b.reshape(1, d), gain.reshape(1, d))


def _top_rows(x, k):
    n = x.shape[0]
    idx_iota = lax.broadcasted_iota(jnp.int32, x.shape, 0)
    vals, idxs = [], []
    cur = x
    for _ in range(k):
        m = jnp.max(cur, axis=0, keepdims=True)
        ix = jnp.min(jnp.where(cur == m, idx_iota, n), axis=0, keepdims=True)
        vals.append(m)
        idxs.append(ix)
        cur = jnp.where(idx_iota == ix, -jnp.inf, cur)
    return jnp.concatenate(vals, axis=0), jnp.concatenate(idxs, axis=0)


def _pick_rows(table, sel):
    out = jnp.zeros(sel.shape, table.dtype)
    for a in range(table.shape[0]):
        out = out + jnp.where(sel == a, table[a:a + 1, :], 0)
    return out


def _peer_topk_kernel(q_ref, keys_ref, i1_ref, i2_ref, gate_ref):
    q = q_ref[...].astype(MXU_DTYPE)
    s1 = _dot_nt(keys_ref[0, 0].astype(MXU_DTYPE), q[:, :PEER_HALF])
    s2 = _dot_nt(keys_ref[0, 1].astype(MXU_DTYPE), q[:, PEER_HALF:])
    v1, x1 = _top_rows(s1, PEER_TOPK)
    v2, x2 = _top_rows(s2, PEER_TOPK)
    t = v1.shape[1]
    pairs = [(a, b) for a in range(PEER_TOPK) for b in range(PEER_TOPK) if (a + 1) * (b + 1) <= PEER_TOPK]
    n_pad = -len(pairs) % SUBLANES
    cand = jnp.concatenate([v1[a:a + 1] + v2[b:b + 1] for a, b in pairs]
                           + [jnp.full((n_pad, t), -jnp.inf, F32)], axis=0)
    top, pos = _top_rows(cand, PEER_TOPK)
    i1_ref[0] = _pick_rows(jnp.concatenate([x1[a:a + 1] for a, _ in pairs], axis=0), pos)
    i2_ref[0] = _pick_rows(jnp.concatenate([x2[b:b + 1] for _, b in pairs], axis=0), pos)
    e = jnp.exp(top - jnp.max(top, axis=0, keepdims=True))
    gate_ref[0] = e / jnp.sum(e, axis=0, keepdims=True)


def _peer_topk(q, sub_keys):
    n = q.shape[0]
    tt = _row_tile(n, 512)
    out = pl.BlockSpec((1, PEER_TOPK, tt), lambda i, h: (h, 0, i))
    shp = (PEER_HEADS, PEER_TOPK, n)
    return pl.pallas_call(
        _peer_topk_kernel,
        grid=(n // tt, PEER_HEADS),
        in_specs=[pl.BlockSpec((tt, 2 * PEER_HALF), lambda i, h: (i, h)),
                  pl.BlockSpec((1, 2, PEER_KEYS, PEER_HALF), lambda i, h: (h, 0, 0, 0))],
        out_specs=[out, out, out],
        out_shape=[jax.ShapeDtypeStruct(shp, jnp.int32), jax.ShapeDtypeStruct(shp, jnp.int32),
                   jax.ShapeDtypeStruct(shp, F32)],
        compiler_params=_cparams(("parallel", "parallel")),
    )(q, sub_keys)


PEER_EXPERT_BLOCK = 4 * PEER_KEYS


PEER_GRID_HALF = PEER_KEYS // 2
HIGH_HALF = 0xFFFF0000
PEER_BUILD_UNROLL = 64


def _peer_kernel(x_ref, u_ref, v_ref, i1_ref, i2_ref, gate_ref, h_ref, o_ref, g_scr, w_scr, *, tm, n_blocks):
    j = pl.program_id(1)
    n_entries = PEER_HEADS * PEER_TOPK
    words = PEER_GRID_HALF // 2
    blocks_per_half = n_blocks // 2
    unroll = math.gcd(tm, PEER_BUILD_UNROLL)

    def build(base):
        r = lax.broadcasted_iota(jnp.int32, (PEER_GRID_HALF, n_entries), 0)
        i1_of_row = base + jnp.where(r < words, 2 * r, 2 * (r - words) + 1)
        i2_of_row = lax.broadcasted_iota(jnp.int32, (PEER_KEYS, n_entries), 0)

        def body(nb, carry):
            for k in range(unroll):
                n = nb * unroll + k
                a = jnp.where(i1_of_row == i1_ref[pl.ds(n, 1), :], gate_ref[pl.ds(n, 1), :], 0.0)
                b = jnp.where(i2_of_row == i2_ref[pl.ds(n, 1), :], 1.0, 0.0)
                grid = _dot_nt(a.astype(MXU_DTYPE), b.astype(MXU_DTYPE))
                bits = lax.bitcast_convert_type(grid.astype(jnp.bfloat16).astype(F32), jnp.uint32)
                g_scr[pl.ds(pl.multiple_of(n * words, words), words), :] = (
                    jnp.right_shift(bits[:words], jnp.uint32(16)) | (bits[words:] & jnp.uint32(HIGH_HALF)))
            return carry

        lax.fori_loop(0, tm // unroll, body, 0)

    def weights():
        act = _dot_nt(x_ref[...], u_ref[...])
        pairs = PEER_EXPERT_BLOCK // (2 * PEER_KEYS)
        m0 = (j - jnp.where(j >= blocks_per_half, blocks_per_half, 0)) * pairs
        pieces = []
        for q in range(pairs):
            word = g_scr[pl.ds(m0 + q, tm, stride=words), :]
            pieces.append(lax.bitcast_convert_type(jnp.left_shift(word, jnp.uint32(16)), F32))
            pieces.append(lax.bitcast_convert_type(word & jnp.uint32(HIGH_HALF), F32))
        g = jnp.concatenate(pieces, axis=1)
        w_scr[j % 2] = (g * _gelu(act)).astype(MXU_DTYPE)

    def values():
        o_ref[...] += _dot(w_scr[(j + 1) % 2], v_ref[...])

    @pl.when(j == 0)
    def _():
        o_ref[...] = h_ref[...]
        build(0)
        weights()

    @pl.when(j == blocks_per_half)
    def _():
        build(PEER_GRID_HALF)

    @pl.when((j > 0) & (j < n_blocks))
    def _():
        values()
        weights()

    @pl.when(j == n_blocks)
    def _():
        values()


def _peer(xn, u, v, i1, i2, gate, h):
    n, d = xn.shape
    n_blocks = u.shape[0] // PEER_EXPERT_BLOCK
    assert n_blocks * PEER_EXPERT_BLOCK == PEER_KEYS * PEER_KEYS and n_blocks % 2 == 0
    tm = _row_tile(n, 512)
    n_entries = PEER_HEADS * PEER_TOPK
    once = pl.Buffered(1)
    row_in = pl.BlockSpec((tm, d), lambda i, j: (i, 0), pipeline_mode=once)
    ent = pl.BlockSpec((tm, n_entries), lambda i, j: (i, 0))
    return pl.pallas_call(
        functools.partial(_peer_kernel, tm=tm, n_blocks=n_blocks),
        grid=(n // tm, n_blocks + 1),
        in_specs=[pl.BlockSpec((tm, d), lambda i, j: (i, 0)),
                  pl.BlockSpec((PEER_EXPERT_BLOCK, d), lambda i, j: (jnp.minimum(j, n_blocks - 1), 0)),
                  pl.BlockSpec((PEER_EXPERT_BLOCK, d), lambda i, j: (jnp.maximum(j - 1, 0), 0)),
                  ent, ent, ent, row_in],
        out_specs=pl.BlockSpec((tm, d), lambda i, j: (i, 0), pipeline_mode=once),
        out_shape=jax.ShapeDtypeStruct((n, d), F32),
        scratch_shapes=[pltpu.VMEM((tm * PEER_GRID_HALF // 2, PEER_KEYS), jnp.uint32),
                        pltpu.VMEM((2, tm, PEER_EXPERT_BLOCK), MXU_DTYPE)],
        compiler_params=_cparams(("parallel", "arbitrary")),
    )(xn, u, v, i1, i2, gate, h)


def _prepare_weights(rel_table, w_in, q_norm, k_norm, w_phi, w_glu, w_out, peer_w_query, peer_u, peer_v):
    off_gate = ATTN_WIDTH + 6 * KV_WIDTH
    off_ssm = off_gate + 3 * N_HEADS
    w_main = w_in[:, :off_gate].astype(MXU_DTYPE)
    w_ssm = w_in[:, off_ssm:].astype(MXU_DTYPE)
    w_gate = jnp.pad(w_in[:, off_gate:off_ssm], ((0, 0), (0, LANES - 3 * N_HEADS))).astype(MXU_DTYPE)
    n_main = w_main.shape[1]
    ones = jnp.ones((KV_WIDTH,), F32)
    zeros = jnp.zeros((KV_WIDTH,), F32)
    gain = jnp.concatenate([
        jnp.tile(q_norm.astype(F32), N_HEADS), ones, ones,
        jnp.tile(k_norm[1].astype(F32), N_KV_HEADS), ones,
        jnp.tile(k_norm[2].astype(F32), N_KV_HEADS), ones]).reshape(1, n_main)
    mode = jnp.concatenate([
        jnp.ones((ATTN_WIDTH,), F32), zeros, zeros, ones, zeros, ones, zeros]).reshape(1, n_main)
    w_cmp = w_phi.reshape(2, CMP_R, CMP_STRIDE, HEAD_DIM, HEAD_DIM).transpose(0, 2, 3, 1, 4)
    w_cmp = w_cmp.reshape(2, CMP_STRIDE * HEAD_DIM, CMP_R * HEAD_DIM).astype(MXU_DTYPE)
    return dict(
        w_main=w_main, w_ssm=w_ssm, w_gate=w_gate, gain=gain, mode=mode, w_cmp=w_cmp,
        tab=rel_table.astype(F32).T, w_glu=w_glu.astype(MXU_DTYPE), w_out=w_out.astype(MXU_DTYPE),
        w_query=peer_w_query.astype(MXU_DTYPE), peer_u=peer_u.astype(MXU_DTYPE),
        peer_v=peer_v.astype(MXU_DTYPE))


def _overlap_matrix(n_chunks, n_cmp, n_slc, n_pad):
    c0 = np.arange(n_chunks)[:, None] * CMP_STRIDE
    s0 = np.arange(n_pad)[None, :] * SLC_BLOCK
    overlap = np.clip(np.minimum(c0 + CMP_BLOCK, s0 + SLC_BLOCK) - np.maximum(c0, s0), 0, None) / CMP_BLOCK
    overlap = overlap * (np.arange(n_chunks)[:, None] < n_cmp) * (np.arange(n_pad)[None, :] < n_slc)
    return jnp.asarray(overlap, dtype=MXU_DTYPE)


def _expansion_matrix(n_pad, n_keys, n_cols):
    key = np.arange(n_cols)[None, :]
    e = ((key // SLC_BLOCK) == np.arange(n_pad)[:, None]) & (key < n_keys)
    return jnp.asarray(e, dtype=MXU_DTYPE)


def _pad_lanes(n):
    return -(-n // LANES) * LANES


def _layer(x3, t_real, pos0, cache_cmp3, cache_sel3, cache_win3, page_table, h0, wts, norm_mix,
           k_norm, b_phi, ssm, b_glu, out_norm_attn, out_norm_ssm, norm_ffn, sub_keys):
    nb, t, d_model = x3.shape
    n = nb * t
    sample = page_table is not None
    x2 = x3.reshape(n, d_model)
    xn = _rmsnorm(x2, norm_mix, MXU_DTYPE)
    p, kv_cmp, kv_sel, kv_win = _in_proj(xn, wts["w_main"], wts["mode"], wts["gain"])
    gates = _matmul(xn, wts["w_gate"], "sigmoid")
    n_main = p.shape[1]
    p3 = p.reshape(nb, t, n_main)
    col = lambda off: off // HEAD_DIM
    off_sel, off_win = ATTN_WIDTH + 2 * KV_WIDTH, ATTN_WIDTH + 4 * KV_WIDTH

    if sample:
        assert t_real < CMP_STRIDE and pos0 % PAGE_SIZE == 0
        n_pages = page_table.shape[1]
        pk = _compress(cache_cmp3, page_table, wts["w_cmp"])
        total = pos0 + t_real
    else:
        n_pages = t // PAGE_SIZE
        pt = jnp.arange(nb * n_pages, dtype=jnp.int32).reshape(nb, n_pages)
        pk = _compress(kv_cmp.reshape(nb * n_pages, PAGE_SIZE * 2 * N_KV_HEADS, HEAD_DIM), pt, wts["w_cmp"])
        total = t
    n_chunks = pk.shape[2]
    n_cmp = total // CMP_STRIDE - CMP_R + 1
    n_slc = -(-total // SLC_BLOCK)
    n_pad = _pad_lanes(n_slc)
    m_mat = _overlap_matrix(n_chunks, n_cmp, n_slc, n_pad)
    tq_cmp = _row_tile(t, 256)
    tab = wts["tab"]
    o_cmp, sel = _cmp_attention(tab, p3, pk, b_phi.astype(F32), k_norm[0].astype(F32).reshape(1, HEAD_DIM),
                                m_mat, pos0=pos0, n_cmp=n_cmp, n_slc=n_slc, tq=tq_cmp)

    if sample:
        kv_block = 2 * KV_WIDTH
        n_cols = (n_pages + SEL_PAGES_PER_STEP) * KEY_TILE
        e_mat = _expansion_matrix(n_pad, (n_pages + 1) * KEY_TILE, n_cols)
        o_sel = _attn_sample(p3, cache_sel3, page_table, tab, "sel", off_sel // kv_block, pos0, 0, sel, e_mat)
        o_win = _attn_sample(p3, cache_win3, None, tab, "win", off_win // kv_block, pos0, pos0 - WINDOW)
    else:
        e3 = _expansion_matrix(n_pad, t, t).T.reshape(t // KEY_TILE, KEY_TILE, n_pad)
        o_sel = _attn_prompt(p3, tab, "sel", col(off_sel), col(off_sel + KV_WIDTH), sel, e3)
        o_win = _attn_prompt(p3, tab, "win", col(off_win), col(off_win + KV_WIDTH))
    mixed_attn = _combine(o_cmp.reshape(n, ATTN_WIDTH), o_sel.reshape(n, ATTN_WIDTH),
                          o_win.reshape(n, ATTN_WIDTH), gates, out_norm_attn)

    u_lb = _matmul(xn, wts["w_ssm"], "laneblocks")
    nblk = u_lb.shape[0]
    if t % SSM_CHUNK:
        assert t < SSM_CHUNK and t_real <= t
        u_lb = jnp.pad(u_lb.reshape(nblk, nb, t, LANES), ((0, 0), (0, 0), (0, SSM_CHUNK - t), (0, 0)))
        u_lb = u_lb.reshape(nblk, nb * SSM_CHUNK, LANES)
        l_eff, n_chunk = t_real, 1
    else:
        l_eff, n_chunk = SSM_CHUNK, t // SSM_CHUNK
    mats = _ssm_matrices(*ssm, l_eff)
    y_lb, h_last = _ssm(u_lb, *mats, _state_to_blocks(*h0), n_seq=nb, n_chunk=n_chunk)
    if t % SSM_CHUNK:
        y_lb = y_lb.reshape(nblk, nb, SSM_CHUNK, LANES)[:, :, :t].reshape(nblk, n, LANES)
    mixed_ssm = _glu(y_lb, wts["w_glu"], b_glu, out_norm_ssm)

    h = _out_proj(mixed_attn, mixed_ssm, wts["w_out"], x2)
    hn = _rmsnorm(h, norm_ffn, MXU_DTYPE)
    pq = _matmul(hn, wts["w_query"])
    i1, i2, gate = _peer_topk(pq, sub_keys)
    to_rows = lambda a: a.transpose(2, 0, 1).reshape(n, PEER_HEADS * PEER_TOPK)
    y_out = _peer(hn, wts["peer_u"], wts["peer_v"], to_rows(i1), to_rows(i2), to_rows(gate), h)
    new_kv = [a.reshape(nb, t, 2, N_KV_HEADS, HEAD_DIM) for a in (kv_cmp, kv_sel, kv_win)]
    return y_out.reshape(nb, t, d_model), new_kv, _state_from_blocks(h_last)


def kernel(x_prompt, x_sample, cache_kv_cmp, cache_kv_sel, cache_kv_win, state_ssm_re, state_ssm_im, page_table,
           rel_table, norm_mix, w_in, q_norm, k_norm, w_phi, b_phi, ssm_lam_re, ssm_lam_im, ssm_log_dt,
           ssm_b_re, ssm_b_im, ssm_c_re, ssm_c_im, ssm_d, w_glu, b_glu, out_norm_attn, out_norm_ssm, w_out,
           norm_ffn, peer_w_query, peer_sub_keys, peer_u, peer_v):
    depth = w_in.shape[0]
    bp, t_p, d_model = x_prompt.shape
    db, t_s, _ = x_sample.shape
    past_len = page_table.shape[1] * PAGE_SIZE
    kv_row = 2 * KV_WIDTH
    n_groups = ssm_lam_re.shape[1]
    assert cache_kv_win.shape[2] == WINDOW and t_p % KEY_TILE == 0 and t_s <= SAMPLE_ROWS

    yp, ys = x_prompt, jnp.pad(x_sample, ((0, 0), (0, SAMPLE_ROWS - t_s), (0, 0)))
    new_p = [[] for _ in range(5)]
    new_s = [[] for _ in range(5)]
    for l in range(depth):
        wts = _prepare_weights(rel_table, w_in[l], q_norm[l], k_norm[l], w_phi[l], w_glu[l], w_out[l],
                               peer_w_query[l], peer_u[l], peer_v[l])
        ssm = (ssm_lam_re[l], ssm_lam_im[l], ssm_log_dt[l], ssm_b_re[l], ssm_b_im[l], ssm_c_re[l],
               ssm_c_im[l], ssm_d[l])
        shared = (wts, norm_mix[l], k_norm[l], b_phi[l], ssm, b_glu[l], out_norm_attn[l],
                  out_norm_ssm[l], norm_ffn[l], peer_sub_keys[l])

        zero_state = jnp.zeros((bp, n_groups, SSM_STATE), F32)
        yp, new_kv, h_last = _layer(yp, t_p, 0, None, None, None, None, (zero_state, zero_state), *shared)
        keep = min(WINDOW, t_p)
        new_p[0].append(new_kv[0])
        new_p[1].append(new_kv[1])
        new_p[2].append(new_kv[2][:, t_p - keep:])
        new_p[3].append(h_last[0])
        new_p[4].append(h_last[1])

        n_phys = cache_kv_cmp.shape[1]
        h0 = (state_ssm_re[l], state_ssm_im[l])
        page_rows = PAGE_SIZE * 2 * N_KV_HEADS
        ys, new_kv, h_last = _layer(ys, t_s, past_len, cache_kv_cmp[l].reshape(n_phys, page_rows, HEAD_DIM),
                                cache_kv_sel[l].reshape(n_phys, page_rows, HEAD_DIM),
                                cache_kv_win[l].reshape(db * WIN_TILES, page_rows, HEAD_DIM), page_table, h0,
                                *shared)
        new_s[0].append(new_kv[0][:, :t_s])
        new_s[1].append(new_kv[1][:, :t_s])
        win_all = jnp.concatenate([cache_kv_win[l], new_kv[2][:, :t_s]], axis=1)
        keep = min(WINDOW, past_len + t_s)
        new_s[2].append(win_all[:, win_all.shape[1] - keep:])
        new_s[3].append(h_last[0])
        new_s[4].append(h_last[1])
    outs_p = [jnp.stack(v) for v in new_p]
    outs_s = [jnp.stack(v) for v in new_s]
    return (yp, ys[:, :t_s], *outs_p, *outs_s)
```

```python
import functools
import math

import numpy as np
import jax
import jax.numpy as jnp
from jax import lax
from jax.experimental import pallas as pl
from jax.experimental.pallas import tpu as pltpu

F32 = jnp.float32
MXU_DTYPE = jnp.bfloat16

HEAD_DIM = 128
N_KV_HEADS = 4
GQA_REP = 4
N_HEADS = N_KV_HEADS * GQA_REP
ATTN_WIDTH = N_HEADS * HEAD_DIM
KV_WIDTH = N_KV_HEADS * HEAD_DIM
ATTN_SCALE = HEAD_DIM ** -0.5
CMP_BLOCK = 32
CMP_STRIDE = 16
CMP_R = CMP_BLOCK // CMP_STRIDE
SLC_BLOCK = 64
N_SELECT = 16
FORCE_BONUS = 1.0e4
WINDOW = 512
NUM_BUCKETS = 32
MAX_DISTANCE = 128
SSM_GROUP = 16
SSM_STATE = 64
PEER_HEADS = 8
PEER_KEYS = 128
PEER_TOPK = 16
PEER_HALF = 128
EPS = 1e-6
NEG_INF = -1e30
PAGE_SIZE = 128

LANES = 128
SUBLANES = 8
VMEM_LIMIT_BYTES = 52 * 1024 * 1024

ROW_TILE = 512
SEL_PAGES_PER_STEP = 32
KEY_TILE = 128
WIN_TILES = WINDOW // KEY_TILE
SSM_CHUNK = 16
SAMPLE_ROWS = 8


def _cparams(sem):
    return pltpu.CompilerParams(dimension_semantics=sem, vmem_limit_bytes=VMEM_LIMIT_BYTES)


def _row_tile(n, pref):
    t = min(n, pref)
    while n % t:
        t -= SUBLANES
    return t


def _dot(a, b):
    return jnp.dot(a, b, preferred_element_type=F32)


def _dot_nt(a, b):
    return lax.dot_general(a, b, (((1,), (1,)), ((), ())), preferred_element_type=F32)


def _gelu(x):
    c = math.sqrt(2.0 / math.pi)
    return 0.5 * x * (1.0 + jnp.tanh(c * (x + 0.044715 * (x * x * x))))


def _sigmoid(x):
    return 1.0 / (1.0 + jnp.exp(-x))


def _bucket_lower_bounds():
    n = np.arange(2 * MAX_DISTANCE)
    max_exact = NUM_BUCKETS // 2
    nf = np.maximum(n, 1).astype(np.float32)
    large = max_exact + (np.log(nf / np.float32(max_exact)) / np.float32(math.log(MAX_DISTANCE / max_exact))
                         * np.float32(NUM_BUCKETS - max_exact)).astype(np.int32)
    bucket = np.where(n < max_exact, n, np.minimum(large, NUM_BUCKETS - 1))
    assert np.all(np.diff(bucket) >= 0) and bucket[-1] == NUM_BUCKETS - 1
    return [int(np.argmax(bucket >= b)) for b in range(NUM_BUCKETS)]


BUCKET_LO = _bucket_lower_bounds()


def _bias_from_dist(dist, table_entry):
    out = table_entry(0)
    for b in range(1, NUM_BUCKETS):
        out = jnp.where(dist >= BUCKET_LO[b], table_entry(b), out)
    return out


def _rmsnorm_kernel(x_ref, g_ref, o_ref):
    x = x_ref[...]
    ms = jnp.mean(x * x, axis=-1, keepdims=True)
    o_ref[...] = (x * lax.rsqrt(ms + EPS) * g_ref[...]).astype(o_ref.dtype)


def _rmsnorm(x, g, out_dtype):
    n, d = x.shape
    tm = _row_tile(n, ROW_TILE)
    return pl.pallas_call(
        _rmsnorm_kernel,
        grid=(n // tm,),
        in_specs=[pl.BlockSpec((tm, d), lambda i: (i, 0)), pl.BlockSpec((1, d), lambda i: (0, 0))],
        out_specs=pl.BlockSpec((tm, d), lambda i: (i, 0)),
        out_shape=jax.ShapeDtypeStruct((n, d), out_dtype),
        compiler_params=_cparams(("parallel",)),
    )(x, g.reshape(1, d))


def _matmul_kernel(*refs, epilogue):
    a_ref, b_ref = refs[0], refs[1]
    o_ref = refs[-1]
    acc = _dot(a_ref[...], b_ref[...])
    if epilogue == "none":
        o_ref[...] = acc
    elif epilogue == "sigmoid":
        o_ref[...] = _sigmoid(acc)
    elif epilogue == "laneblocks":
        for c in range(acc.shape[1] // LANES):
            o_ref[c] = acc[:, c * LANES:(c + 1) * LANES]
    else:
        raise ValueError(epilogue)


def _matmul(a, b, epilogue="none", extras=(), tm_pref=512, tn_pref=1024):
    m, k = a.shape
    _, n = b.shape
    tm = _row_tile(m, tm_pref)
    tn = min(n, tn_pref)
    while n % tn:
        tn -= LANES
    in_specs = [pl.BlockSpec((tm, k), lambda i, j: (i, 0)), pl.BlockSpec((k, tn), lambda i, j: (0, j))]
    for e in extras:
        if e.shape[0] == 1:
            in_specs.append(pl.BlockSpec((1, tn), lambda i, j: (0, j)))
        else:
            in_specs.append(pl.BlockSpec((tm, tn), lambda i, j: (i, j)))
    if epilogue == "laneblocks":
        out_spec = pl.BlockSpec((tn // LANES, tm, LANES), lambda i, j: (j, i, 0))
        out_shape = jax.ShapeDtypeStruct((n // LANES, m, LANES), F32)
    else:
        out_spec = pl.BlockSpec((tm, tn), lambda i, j: (i, j))
        out_shape = jax.ShapeDtypeStruct((m, n), F32)
    return pl.pallas_call(
        functools.partial(_matmul_kernel, epilogue=epilogue),
        grid=(m // tm, n // tn),
        in_specs=in_specs,
        out_specs=out_spec,
        out_shape=out_shape,
        compiler_params=_cparams(("parallel", "parallel")),
    )(a, b, *extras)


def _in_proj_kernel(a_ref, b_ref, mode_ref, gain_ref, o_ref, *kv_refs, first_kv_tile):
    j = pl.program_id(1)
    acc = _dot(a_ref[...], b_ref[...])
    tm = acc.shape[0]
    blocks = []
    for c in range(acc.shape[1] // HEAD_DIM):
        sl = slice(c * HEAD_DIM, (c + 1) * HEAD_DIM)
        blk = acc[:, sl]
        ms = jnp.mean(blk * blk, axis=-1, keepdims=True)
        nrm = blk * lax.rsqrt(ms + EPS) * gain_ref[:, sl]
        blocks.append(jnp.where(mode_ref[:, sl] > 0.0, nrm, blk))
        o_ref[:, sl] = blocks[-1]
    for b, kv_ref in enumerate(kv_refs):
        @pl.when(j == first_kv_tile + b)
        def _(kv_ref=kv_ref):
            for c, blk in enumerate(blocks):
                kv_ref[pl.ds(c, tm, stride=len(blocks)), :] = blk


def _in_proj(a, b, mode, gain, tm_pref=512):
    m, k = a.shape
    n = b.shape[1]
    tn = 2 * KV_WIDTH
    assert ATTN_WIDTH % tn == 0 and n == ATTN_WIDTH + 3 * tn
    tm = _row_tile(m, tm_pref)
    heads = tn // HEAD_DIM
    vec = pl.BlockSpec((1, tn), lambda i, j: (0, j))
    kv_spec = pl.BlockSpec((tm * heads, HEAD_DIM), lambda i, j: (i, 0))
    kv_shape = jax.ShapeDtypeStruct((m * heads, HEAD_DIM), F32)
    return pl.pallas_call(
        functools.partial(_in_proj_kernel, first_kv_tile=ATTN_WIDTH // tn),
        grid=(m // tm, n // tn),
        in_specs=[pl.BlockSpec((tm, k), lambda i, j: (i, 0)), pl.BlockSpec((k, tn), lambda i, j: (0, j)), vec, vec],
        out_specs=[pl.BlockSpec((tm, tn), lambda i, j: (i, j)), kv_spec, kv_spec, kv_spec],
        out_shape=[jax.ShapeDtypeStruct((m, n), F32), kv_shape, kv_shape, kv_shape],
        compiler_params=_cparams(("parallel", "arbitrary")),
    )(a, b, mode, gain)


def _out_proj_kernel(a1_ref, a2_ref, b1_ref, b2_ref, res_ref, o_ref):
    o_ref[...] = res_ref[...] + _dot(a1_ref[...], b1_ref[...]) + _dot(a2_ref[...], b2_ref[...])


def _out_proj(a1, a2, w, res, tm_pref=512, tn_pref=1024):
    m, k1 = a1.shape
    k2 = a2.shape[1]
    assert k1 == k2 and w.shape[0] == k1 + k2
    n = w.shape[1]
    tm = _row_tile(m, tm_pref)
    tn = min(n, tn_pref)
    return pl.pallas_call(
        _out_proj_kernel,
        grid=(m // tm, n // tn),
        in_specs=[pl.BlockSpec((tm, k1), lambda i, j: (i, 0)), pl.BlockSpec((tm, k2), lambda i, j: (i, 0)),
                  pl.BlockSpec((k1, tn), lambda i, j: (0, j)), pl.BlockSpec((k2, tn), lambda i, j: (1, j)),
                  pl.BlockSpec((tm, tn), lambda i, j: (i, j))],
        out_specs=pl.BlockSpec((tm, tn), lambda i, j: (i, j)),
        out_shape=jax.ShapeDtypeStruct((m, n), F32),
        compiler_params=_cparams(("parallel", "parallel")),
    )(a1, a2, w, w, res)


CMP_PAGES_PER_STEP = 16


def _compress_kernel(pt_ref, *refs):
    del pt_ref
    pages = refs[:CMP_PAGES_PER_STEP]
    w_ref = refs[CMP_PAGES_PER_STEP]
    o_ref = refs[CMP_PAGES_PER_STEP + 1]
    x_scr = refs[CMP_PAGES_PER_STEP + 2]
    chunks_per_page = PAGE_SIZE // CMP_STRIDE
    rows_per_head = CMP_PAGES_PER_STEP * chunks_per_page
    heads_per_row = 2 * N_KV_HEADS
    for c in range(2):
        for g in range(N_KV_HEADS):
            for i in range(CMP_PAGES_PER_STEP):
                r0 = g * rows_per_head + i * chunks_per_page
                for s in range(CMP_STRIDE):
                    x_scr[r0:r0 + chunks_per_page, s * HEAD_DIM:(s + 1) * HEAD_DIM] = pages[i][
                        0, pl.ds(s * heads_per_row + c * N_KV_HEADS + g, chunks_per_page,
                                 stride=CMP_STRIDE * heads_per_row), :]
        res = _dot(x_scr[...].astype(MXU_DTYPE), w_ref[c])
        for g in range(N_KV_HEADS):
            o_ref[0, c * N_KV_HEADS + g] = res[g * rows_per_head:(g + 1) * rows_per_head]


def _compress(rows3, page_table, w_cmp):
    nb, n_pages = page_table.shape
    assert n_pages % CMP_PAGES_PER_STEP == 0
    n_steps = n_pages // CMP_PAGES_PER_STEP
    chunks_per_step = CMP_PAGES_PER_STEP * PAGE_SIZE // CMP_STRIDE
    n_chunks = n_pages * PAGE_SIZE // CMP_STRIDE

    def page_spec(i):
        return pl.BlockSpec((1, PAGE_SIZE * 2 * N_KV_HEADS, HEAD_DIM),
                            lambda b, j, pt: (pt[b, j * CMP_PAGES_PER_STEP + i], 0, 0))

    grid_spec = pltpu.PrefetchScalarGridSpec(
        num_scalar_prefetch=1,
        grid=(nb, n_steps),
        in_specs=[page_spec(i) for i in range(CMP_PAGES_PER_STEP)]
        + [pl.BlockSpec((2, CMP_STRIDE * HEAD_DIM, CMP_R * HEAD_DIM), lambda b, j, pt: (0, 0, 0))],
        out_specs=pl.BlockSpec((1, 2 * N_KV_HEADS, chunks_per_step, CMP_R * HEAD_DIM),
                               lambda b, j, pt: (b, 0, j, 0)),
        scratch_shapes=[pltpu.VMEM((N_KV_HEADS * chunks_per_step, CMP_STRIDE * HEAD_DIM), F32)],
    )
    return pl.pallas_call(
        _compress_kernel,
        grid_spec=grid_spec,
        out_shape=jax.ShapeDtypeStruct((nb, 2 * N_KV_HEADS, n_chunks, CMP_R * HEAD_DIM), F32),
        compiler_params=_cparams(("parallel", "arbitrary")),
    )(page_table, *([rows3] * CMP_PAGES_PER_STEP), w_cmp)


def _cmp_attn_kernel(tab_ref, q_ref, kp_ref, vp_ref, bphi_ref, kn_ref, m_ref, o_ref, sel_ref, bias_scr,
                     *, tq, pos0, n_cmp, n_slc):
    g = pl.program_id(0)
    i = pl.program_id(1)
    n_chunks = kp_ref.shape[2]
    kp = kp_ref[0, 0]
    vp = vp_ref[0, 0]
    k_c = bphi_ref[0:1, :] + kp[:, :HEAD_DIM] + pltpu.roll(kp[:, HEAD_DIM:], n_chunks - 1, 0)
    v_c = bphi_ref[1:2, :] + vp[:, :HEAD_DIM] + pltpu.roll(vp[:, HEAD_DIM:], n_chunks - 1, 0)
    ms = jnp.mean(k_c * k_c, axis=-1, keepdims=True)
    k_c = k_c * lax.rsqrt(ms + EPS) * kn_ref[...]

    q = jnp.concatenate([q_ref[0, :, r * HEAD_DIM:(r + 1) * HEAD_DIM] for r in range(GQA_REP)], axis=0)
    logits = _dot_nt(q.astype(MXU_DTYPE), k_c.astype(MXU_DTYPE)) * ATTN_SCALE
    row = lax.broadcasted_iota(jnp.int32, (tq, n_chunks), 0)
    col = lax.broadcasted_iota(jnp.int32, (tq, n_chunks), 1)
    dist = pos0 + i * tq + row - (col * CMP_STRIDE + (CMP_BLOCK - 1))

    @pl.when(pl.program_id(2) == 0)
    def _():
        bias_scr[...] = jnp.concatenate(
            [_bias_from_dist(dist, lambda b, r=r: tab_ref[g * GQA_REP + r, b]) for r in range(GQA_REP)], axis=0)

    bias = bias_scr[...]
    valid = (dist >= 0) & (col < n_cmp)
    valid = jnp.concatenate([valid] * GQA_REP, axis=0)
    logits = jnp.where(valid, logits + bias, NEG_INF)
    mx = jnp.max(logits, axis=-1, keepdims=True)
    e = jnp.exp(logits - mx)
    any_valid = (jnp.max(jnp.where(valid, 1.0, 0.0), axis=-1, keepdims=True))
    p = e / jnp.sum(e, axis=-1, keepdims=True) * any_valid
    out = _dot(p.astype(MXU_DTYPE), v_c.astype(MXU_DTYPE))
    for r in range(GQA_REP):
        o_ref[0, :, r * HEAD_DIM:(r + 1) * HEAD_DIM] = out[r * tq:(r + 1) * tq]

    psum = p[0:tq]
    for r in range(1, GQA_REP):
        psum = psum + p[r * tq:(r + 1) * tq]
    hi = psum.astype(MXU_DTYPE)
    lo = (psum - hi.astype(F32)).astype(MXU_DTYPE)
    imp = _dot(hi, m_ref[...]) + _dot(lo, m_ref[...])
    n_pad = imp.shape[1]
    j = lax.broadcasted_iota(jnp.int32, (tq, n_pad), 1)
    qp = pos0 + i * tq + lax.broadcasted_iota(jnp.int32, (tq, n_pad), 0)
    cur = jnp.right_shift(qp, int(math.log2(SLC_BLOCK)))
    ok = j * SLC_BLOCK <= qp
    forced = (j == 0) | (j == cur) | (j == cur - 1)
    score = jnp.where(ok, imp + jnp.where(forced, FORCE_BONUS, 0.0), NEG_INF)
    rank = jnp.zeros((tq, n_pad), F32)
    for s in range(n_slc):
        cs = score[:, s:s + 1]
        beats = (cs > score) | ((cs == score) & (j > s))
        rank = rank + jnp.where(beats, 1.0, 0.0)
    selected = (rank < float(min(N_SELECT, n_slc))) & (score > 0.5 * NEG_INF)
    sel_ref[0, 0] = jnp.where(selected, 1.0, 0.0)


def _cmp_attention(tab, p3, pk, bphi, k_norm0, m_mat, *, pos0, n_cmp, n_slc, tq):
    nb, t, _ = p3.shape
    n_chunks = pk.shape[2]
    n_pad = m_mat.shape[1]
    nq = t // tq
    kern = functools.partial(_cmp_attn_kernel, tq=tq, pos0=pos0, n_cmp=n_cmp, n_slc=n_slc)
    return pl.pallas_call(
        kern,
        grid=(N_KV_HEADS, nq, nb),
        in_specs=[
            pl.BlockSpec(memory_space=pltpu.SMEM),
            pl.BlockSpec((1, tq, GQA_REP * HEAD_DIM), lambda g, i, b: (b, i, g)),
            pl.BlockSpec((1, 1, n_chunks, CMP_R * HEAD_DIM), lambda g, i, b: (b, g, 0, 0)),
            pl.BlockSpec((1, 1, n_chunks, CMP_R * HEAD_DIM), lambda g, i, b: (b, N_KV_HEADS + g, 0, 0)),
            pl.BlockSpec((2, HEAD_DIM), lambda g, i, b: (0, 0)),
            pl.BlockSpec((1, HEAD_DIM), lambda g, i, b: (0, 0)),
            pl.BlockSpec((n_chunks, n_pad), lambda g, i, b: (0, 0)),
        ],
        out_specs=[
            pl.BlockSpec((1, tq, GQA_REP * HEAD_DIM), lambda g, i, b: (b, i, g)),
            pl.BlockSpec((1, 1, tq, n_pad), lambda g, i, b: (b, g, i, 0)),
        ],
        out_shape=[
            jax.ShapeDtypeStruct((nb, t, ATTN_WIDTH), F32),
            jax.ShapeDtypeStruct((nb, N_KV_HEADS, t, n_pad), F32),
        ],
        scratch_shapes=[pltpu.VMEM((GQA_REP * tq, n_chunks), F32)],
        compiler_params=_cparams(("parallel", "parallel", "arbitrary")),
    )(tab, p3, pk, pk, bphi, k_norm0, m_mat)


def _online_softmax_step(s, v, m_scr, l_scr, acc_scr):
    m_old = m_scr[...]
    m_new = jnp.maximum(m_old, jnp.max(s, axis=-1, keepdims=True))
    alpha = jnp.exp(m_old - m_new)
    p = jnp.exp(s - m_new)
    l_scr[...] = alpha * l_scr[...] + jnp.sum(p, axis=-1, keepdims=True)
    acc_scr[...] = alpha * acc_scr[...] + _dot(p.astype(MXU_DTYPE), v)
    m_scr[...] = m_new


ATTN_TILES_PER_ITER = 4


def _attn_prompt_kernel(*refs, mode, n_kt):
    if mode == "sel":
        q_ref, k_ref, v_ref, mb_ref, sel_ref, e_ref, o_ref, kb_scr, vt_scr, m_scr, l_scr, acc_scr = refs
    else:
        q_ref, k_ref, v_ref, mb_ref, o_ref, kb_scr, vt_scr, m_scr, l_scr, acc_scr = refs
    tq = KEY_TILE
    qt = pl.program_id(2)

    @pl.when(qt == 0)
    def _():
        for kt in range(n_kt):
            rows = slice(kt * KEY_TILE, (kt + 1) * KEY_TILE)
            kb_scr[kt] = k_ref[0, rows, :].astype(MXU_DTYPE)
            vt_scr[kt] = v_ref[0, rows, :].T.astype(MXU_DTYPE)

    q = jnp.concatenate([q_ref[0, :, r * HEAD_DIM:(r + 1) * HEAD_DIM] for r in range(GQA_REP)], axis=0)
    q = (q * ATTN_SCALE).astype(MXU_DTYPE)
    m_scr[...] = jnp.full(m_scr.shape, NEG_INF, F32)
    l_scr[...] = jnp.zeros(l_scr.shape, F32)
    acc_scr[...] = jnp.zeros(acc_scr.shape, F32)
    if mode == "sel":
        sel = sel_ref[0, 0].astype(MXU_DTYPE)

    n_kinds = mb_ref.shape[0]

    def logits(kt):
        d = qt - kt
        if mode == "sel":
            kind = jnp.where(d < 0, n_kinds - 1, jnp.minimum(d, 2))
        else:
            kind = jnp.where(kt < 0, n_kinds - 1, jnp.where(d == WIN_TILES, 3, jnp.minimum(d, 2)))
        kt = jnp.clip(kt, 0, n_kt - 1)
        s = _dot_nt(kb_scr[kt], q) + mb_ref[kind, 0]
        if mode == "sel":
            chosen = _dot_nt(e_ref[kt], sel)
            s = s + jnp.concatenate([(1.0 - chosen) * NEG_INF] * GQA_REP, axis=1)
        return kt, s

    def update(key_tiles):
        tiles = [logits(kt) for kt in key_tiles]
        m_old = m_scr[...]
        m_new = m_old
        for _, s in tiles:
            m_new = jnp.maximum(m_new, jnp.max(s, axis=0, keepdims=True))
        alpha = jnp.exp(m_old - m_new)
        l_new = alpha * l_scr[...]
        acc = alpha * acc_scr[...]
        for kt, s in tiles:
            p = jnp.exp(s - m_new)
            l_new = l_new + jnp.sum(p, axis=0, keepdims=True)
            acc = acc + _dot(vt_scr[kt], p.astype(MXU_DTYPE))
        l_scr[...] = l_new
        acc_scr[...] = acc
        m_scr[...] = m_new

    if mode == "sel":
        def body(it, carry):
            update([it * ATTN_TILES_PER_ITER + u for u in range(ATTN_TILES_PER_ITER)])
            return carry

        lax.fori_loop(0, (qt + ATTN_TILES_PER_ITER) // ATTN_TILES_PER_ITER, body, 0)
    else:
        update([qt - WIN_TILES + u for u in range(WIN_TILES + 1)])
    out = acc_scr[...] / l_scr[...]
    for r in range(GQA_REP):
        o_ref[0, :, r * HEAD_DIM:(r + 1) * HEAD_DIM] = out[:, r * tq:(r + 1) * tq].T


def _prompt_bias_tiles(tab, mode):
    ii = jnp.arange(KEY_TILE, dtype=jnp.int32)
    entry = lambda b: tab[:, b][:, None, None]

    def tile(offset):
        dist = (offset * KEY_TILE + ii[:, None] - ii[None, :])[None]
        return dist, jnp.broadcast_to(_bias_from_dist(dist, entry), (N_HEADS, KEY_TILE, KEY_TILE))

    d0, b0 = tile(0)
    tiles = [jnp.where(d0 >= 0, b0, NEG_INF), tile(1)[1], tile(2)[1]]
    if mode == "win":
        d4, b4 = tile(WIN_TILES)
        tiles.append(jnp.where(d4 <= WINDOW, b4, NEG_INF))
    tiles.append(jnp.full_like(b0, NEG_INF))
    mb = jnp.stack(tiles).reshape(len(tiles), N_KV_HEADS, GQA_REP, KEY_TILE, KEY_TILE)
    return mb.transpose(0, 1, 4, 2, 3).reshape(len(tiles), N_KV_HEADS, KEY_TILE, GQA_REP * KEY_TILE)


def _attn_prompt(p3, tab, mode, k_col, v_col, sel=None, e3=None):
    nb, t, _ = p3.shape
    tq = KEY_TILE
    nq = t // tq
    mb = _prompt_bias_tiles(tab, mode)
    cols = GQA_REP * tq
    in_specs = [
        pl.BlockSpec((1, tq, GQA_REP * HEAD_DIM), lambda b, g, i: (b, i, g)),
        pl.BlockSpec((1, t, HEAD_DIM), lambda b, g, i: (b, 0, k_col + g)),
        pl.BlockSpec((1, t, HEAD_DIM), lambda b, g, i: (b, 0, v_col + g)),
        pl.BlockSpec((mb.shape[0], 1, KEY_TILE, cols), lambda b, g, i: (0, g, 0, 0)),
    ]
    args = [p3, p3, p3, mb]
    if mode == "sel":
        n_pad = sel.shape[-1]
        in_specs += [
            pl.BlockSpec((1, 1, tq, n_pad), lambda b, g, i: (b, g, i, 0)),
            pl.BlockSpec((nq, KEY_TILE, n_pad), lambda b, g, i: (0, 0, 0)),
        ]
        args += [sel, e3]
    return pl.pallas_call(
        functools.partial(_attn_prompt_kernel, mode=mode, n_kt=nq),
        grid=(nb, N_KV_HEADS, nq),
        in_specs=in_specs,
        out_specs=pl.BlockSpec((1, tq, GQA_REP * HEAD_DIM), lambda b, g, i: (b, i, g)),
        out_shape=jax.ShapeDtypeStruct((nb, t, ATTN_WIDTH), F32),
        scratch_shapes=[pltpu.VMEM((nq, KEY_TILE, HEAD_DIM), MXU_DTYPE),
                        pltpu.VMEM((nq, HEAD_DIM, KEY_TILE), MXU_DTYPE),
                        pltpu.VMEM((1, cols), F32), pltpu.VMEM((1, cols), F32),
                        pltpu.VMEM((HEAD_DIM, cols), F32)],
        compiler_params=_cparams(("parallel", "parallel", "arbitrary")),
    )(*args)


def _attn_sample_kernel(*refs, mode, paged, pages, n_steps, tq, pos0, base_pos, near_keys):
    it = iter(refs)
    if paged:
        next(it)
    q_ref = next(it)
    page_refs = [next(it) for _ in range(pages)]
    new_ref, rowtab_ref = next(it), next(it)
    if mode == "sel":
        sel_ref, e_ref = next(it), next(it)
    o_ref, m_scr, l_scr, acc_scr = next(it), next(it), next(it), next(it)
    kk = pl.program_id(1)
    rows = N_HEADS * tq
    heads_per_row = 2 * N_KV_HEADS

    @pl.when(kk == 0)
    def _():
        m_scr[...] = jnp.full(m_scr.shape, NEG_INF, F32)
        l_scr[...] = jnp.zeros(l_scr.shape, F32)
        acc_scr[...] = jnp.zeros(acc_scr.shape, F32)

    def queries():
        zero = jnp.zeros((tq, HEAD_DIM), F32)
        blocks = []
        for h in range(N_HEADS):
            qh = q_ref[0, :, h * HEAD_DIM:(h + 1) * HEAD_DIM] * ATTN_SCALE
            blocks.append(jnp.concatenate([qh if g == h // GQA_REP else zero for g in range(N_KV_HEADS)], axis=1))
        return jnp.concatenate(blocks, axis=0).astype(MXU_DTYPE)

    def step(k_all, v_all, key_pos0, chosen_cols, near_keys):
        n_keys = k_all.shape[0]
        s = _dot_nt(queries(), k_all)
        row = lax.broadcasted_iota(jnp.int32, (rows, n_keys), 0) & (tq - 1)
        col = lax.broadcasted_iota(jnp.int32, (rows, n_keys), 1)
        dist = pos0 + row - (key_pos0 + col)
        bias = rowtab_ref[:, NUM_BUCKETS - 1:NUM_BUCKETS]
        if near_keys:
            near = _bias_from_dist(dist[:, n_keys - near_keys:], lambda b: rowtab_ref[:, b:b + 1])
            if near_keys < n_keys:
                near = jnp.concatenate([jnp.broadcast_to(bias, (rows, n_keys - near_keys)), near], axis=1)
            bias = near
        if mode == "sel":
            sel = sel_ref[0].reshape(N_KV_HEADS * tq, sel_ref.shape[-1]).astype(MXU_DTYPE)
            chosen = _dot(sel, chosen_cols)
            chosen = jnp.concatenate(
                [chosen[g * tq:(g + 1) * tq] for g in range(N_KV_HEADS) for _ in range(GQA_REP)], axis=0)
            mask = (dist >= 0) & (chosen > 0.5)
        else:
            mask = (dist >= 0) & (dist <= WINDOW)
        s = jnp.where(mask, s + bias, NEG_INF)
        _online_softmax_step(s, v_all, m_scr, l_scr, acc_scr)

    def cached(kv):
        return jnp.concatenate([
            jnp.concatenate([pg[0, pl.ds(kv * N_KV_HEADS + g, PAGE_SIZE, stride=heads_per_row), :]
                             for g in range(N_KV_HEADS)], axis=1)
            for pg in page_refs], axis=0).astype(MXU_DTYPE)

    keys_per_step = pages * PAGE_SIZE
    key_pos0 = base_pos + kk * keys_per_step
    far = pos0 - (key_pos0 + keys_per_step - 1) >= MAX_DISTANCE
    is_cache = kk < n_steps
    chosen_cols = e_ref[...] if mode == "sel" else None

    @pl.when(is_cache & far)
    def _():
        step(cached(0), cached(1), key_pos0, chosen_cols, 0)

    @pl.when(is_cache & jnp.logical_not(far))
    def _():
        step(cached(0), cached(1), key_pos0, chosen_cols, near_keys)

    @pl.when(kk == n_steps)
    def _():
        pad = jnp.zeros((KEY_TILE - tq, N_KV_HEADS * HEAD_DIM), F32)
        k_new = jnp.concatenate([new_ref[0, :, :KV_WIDTH], pad], axis=0).astype(MXU_DTYPE)
        v_new = jnp.concatenate([new_ref[0, :, KV_WIDTH:], pad], axis=0).astype(MXU_DTYPE)
        cols = e_ref[:, :KEY_TILE] if mode == "sel" else None
        step(k_new, v_new, pos0, cols, KEY_TILE)
        out = acc_scr[...] / l_scr[...]
        for h in range(N_HEADS):
            g = h // GQA_REP
            o_ref[0, :, h * HEAD_DIM:(h + 1) * HEAD_DIM] = out[h * tq:(h + 1) * tq, g * HEAD_DIM:(g + 1) * HEAD_DIM]


def _attn_sample(p3, cache_rows, page_table, tab, mode, new_col, pos0, base_pos, sel=None, e_mat=None):
    nb, tq, _ = p3.shape
    paged = page_table is not None
    pages = SEL_PAGES_PER_STEP if paged else WIN_TILES
    n_tiles = page_table.shape[1] if paged else WIN_TILES
    assert n_tiles % pages == 0
    n_steps = n_tiles // pages
    rows = N_HEADS * tq
    rowtab = jnp.repeat(tab, tq, axis=0)

    def page_spec(i):
        def index(b, kk, *pt):
            page = jnp.minimum(kk, n_steps - 1) * pages + i
            return (pt[0][b, page] if paged else b * n_tiles + page, 0, 0)
        return pl.BlockSpec((1, PAGE_SIZE * 2 * N_KV_HEADS, HEAD_DIM), index)

    def fixed(shape, fn):
        return pl.BlockSpec(shape, lambda b, kk, *_: fn(b, kk))

    in_specs = [fixed((1, tq, ATTN_WIDTH), lambda b, kk: (b, 0, 0))]
    in_specs += [page_spec(i) for i in range(pages)]
    in_specs += [fixed((1, tq, 2 * KV_WIDTH), lambda b, kk: (b, 0, new_col)),
                 fixed((rows, NUM_BUCKETS), lambda b, kk: (0, 0))]
    args = [p3] + [cache_rows] * pages + [p3, rowtab]
    if mode == "sel":
        n_pad = sel.shape[-1]
        in_specs += [fixed((1, N_KV_HEADS, tq, n_pad), lambda b, kk: (b, 0, 0, 0)),
                     fixed((n_pad, pages * PAGE_SIZE), lambda b, kk: (0, kk))]
        args += [sel, e_mat]
    keys_per_step = pages * PAGE_SIZE
    near_keys = PAGE_SIZE if base_pos + n_tiles * PAGE_SIZE == pos0 else keys_per_step
    kern = functools.partial(_attn_sample_kernel, mode=mode, paged=paged, pages=pages, n_steps=n_steps, tq=tq,
                             pos0=pos0, base_pos=base_pos, near_keys=near_keys)
    out_spec = fixed((1, tq, ATTN_WIDTH), lambda b, kk: (b, 0, 0))
    out_shape = jax.ShapeDtypeStruct((nb, tq, ATTN_WIDTH), F32)
    scratch = [pltpu.VMEM((rows, 1), F32), pltpu.VMEM((rows, 1), F32),
               pltpu.VMEM((rows, N_KV_HEADS * HEAD_DIM), F32)]
    cp = _cparams(("parallel", "arbitrary"))
    grid = (nb, n_steps + 1)
    if paged:
        grid_spec = pltpu.PrefetchScalarGridSpec(num_scalar_prefetch=1, grid=grid, in_specs=in_specs,
                                                 out_specs=out_spec, scratch_shapes=scratch)
        return pl.pallas_call(kern, grid_spec=grid_spec, out_shape=out_shape, compiler_params=cp)(
            page_table, *args)
    return pl.pallas_call(kern, grid=grid, in_specs=in_specs, out_specs=out_spec, out_shape=out_shape,
                          scratch_shapes=scratch, compiler_params=cp)(*args)


def _combine_kernel(oc_ref, os_ref, ow_ref, gate_ref, g_ref, o_ref):
    gates = gate_ref[...]
    parts = []
    for h in range(N_HEADS):
        sl = slice(h * HEAD_DIM, (h + 1) * HEAD_DIM)
        parts.append(gates[:, h:h + 1] * oc_ref[:, sl]
                     + gates[:, N_HEADS + h:N_HEADS + h + 1] * os_ref[:, sl]
                     + gates[:, 2 * N_HEADS + h:2 * N_HEADS + h + 1] * ow_ref[:, sl])
    o = jnp.concatenate(parts, axis=1)
    ms = jnp.mean(o * o, axis=-1, keepdims=True)
    o_ref[...] = (o * lax.rsqrt(ms + EPS) * g_ref[...]).astype(o_ref.dtype)


def _combine(o_cmp, o_sel, o_win, gates, gain):
    n, w = o_cmp.shape
    tm = _row_tile(n, ROW_TILE)
    row = pl.BlockSpec((tm, w), lambda i: (i, 0))
    return pl.pallas_call(
        _combine_kernel,
        grid=(n // tm,),
        in_specs=[row, row, row, pl.BlockSpec((tm, LANES), lambda i: (i, 0)),
                  pl.BlockSpec((1, w), lambda i: (0, 0))],
        out_specs=row,
        out_shape=jax.ShapeDtypeStruct((n, w), MXU_DTYPE),
        compiler_params=_cparams(("parallel",)),
    )(o_cmp, o_sel, o_win, gates, gain.reshape(1, w))


SSM_LANE_GROUPS = LANES // SSM_GROUP
SSM_BLOCK_STATE = SSM_LANE_GROUPS * 2 * SSM_STATE


def _ssm_kernel(u_ref, kt_ref, bs_ref, cs_ref, d_ref, la_ref, lb_ref, h0_ref, y_ref, hout_ref,
                us_scr, s_scr, hs_scr, *, n_seq, n_chunk):
    L = SSM_CHUNK
    rows = n_seq * n_chunk
    st = SSM_BLOCK_STATE
    group_shift = int(math.log2(SSM_GROUP))
    state_shift = int(math.log2(2 * SSM_STATE))
    in_mask = (jnp.right_shift(lax.broadcasted_iota(jnp.int32, (LANES, st), 0), group_shift)
               == jnp.right_shift(lax.broadcasted_iota(jnp.int32, (LANES, st), 1), state_shift))
    out_mask = (jnp.right_shift(lax.broadcasted_iota(jnp.int32, (st, LANES), 0), state_shift)
                == jnp.right_shift(lax.broadcasted_iota(jnp.int32, (st, LANES), 1), group_shift))
    zero = jnp.zeros((), MXU_DTYPE)

    def pair(s):
        return jnp.concatenate([us_scr[s], us_scr[s + 1]], axis=1)

    for s in range(L):
        us_scr[s] = u_ref[0, pl.ds(s, rows, stride=L), :].astype(MXU_DTYPE)
    s_acc = jnp.zeros((rows, st), F32)
    for s in range(0, L, 2):
        b_full = jnp.concatenate(
            [jnp.where(in_mask, jnp.concatenate([bs_ref[0, s + k]] * SSM_LANE_GROUPS, axis=1), zero)
             for k in range(2)], axis=0)
        s_acc = s_acc + _dot(pair(s), b_full)
    s_scr[...] = s_acc

    la = la_ref[0]
    lb = lb_ref[0]
    is_re = (lax.broadcasted_iota(jnp.int32, (1, st), 1) & SSM_STATE) == 0

    def swap(h):
        return jnp.where(is_re, pltpu.roll(h, st - SSM_STATE, 1), pltpu.roll(h, SSM_STATE, 1))

    def body(c, hs):
        new = []
        for b in range(n_seq):
            row = b * n_chunk + c
            hs_scr[pl.ds(row, 1), :] = hs[b]
            new.append(la * hs[b] + lb * swap(hs[b]) + s_scr[pl.ds(row, 1), :])
        return tuple(new)

    hs = lax.fori_loop(0, n_chunk, body, tuple(h0_ref[0, b:b + 1, :] for b in range(n_seq)))
    for b in range(n_seq):
        hout_ref[0, b:b + 1, :] = hs[b]

    h_start = hs_scr[...].astype(MXU_DTYPE)
    no_k = jnp.zeros((LANES, LANES), MXU_DTYPE)

    def k_tile(tau):
        return kt_ref[0, tau] if tau >= 0 else no_k

    for t in range(0, L, 2):
        c_full = jnp.concatenate(
            [jnp.where(out_mask, jnp.concatenate([cs_ref[0, t + k]] * SSM_LANE_GROUPS, axis=0), zero)
             for k in range(2)], axis=1)
        acc = _dot(h_start, c_full)
        for s in range(0, t + 1, 2):
            from_s = jnp.concatenate([k_tile(t - s), k_tile(t + 1 - s)], axis=1)
            from_s1 = jnp.concatenate([k_tile(t - s - 1), k_tile(t - s)], axis=1)
            acc = acc + _dot(pair(s), jnp.concatenate([from_s, from_s1], axis=0))
        for k in range(2):
            y_ref[0, pl.ds(t + k, rows, stride=L), :] = (
                acc[:, k * LANES:(k + 1) * LANES] + d_ref[0] * u_ref[0, pl.ds(t + k, rows, stride=L), :])


def _ssm(u_lb, kt, bs, cs, dvec, la, lb, h0, *, n_seq, n_chunk):
    nblk, n, w = u_lb.shape
    L = SSM_CHUNK
    rows = n_seq * n_chunk
    assert n == rows * L and w == LANES
    st = SSM_BLOCK_STATE

    def blk(shape):
        return pl.BlockSpec((1,) + shape, lambda i: (i,) + (0,) * len(shape))

    return pl.pallas_call(
        functools.partial(_ssm_kernel, n_seq=n_seq, n_chunk=n_chunk),
        grid=(nblk,),
        in_specs=[blk((n, w)), blk((L, w, w)), blk((L, w, w)), blk((L, w, w)), blk((1, w)), blk((1, st)),
                  blk((1, st)), blk((n_seq, st))],
        out_specs=[blk((n, w)), blk((n_seq, st))],
        out_shape=[jax.ShapeDtypeStruct((nblk, n, w), F32), jax.ShapeDtypeStruct((nblk, n_seq, st), F32)],
        scratch_shapes=[pltpu.VMEM((L, rows, w), MXU_DTYPE), pltpu.VMEM((rows, st), F32),
                        pltpu.VMEM((rows, st), F32)],
        compiler_params=_cparams(("parallel",)),
    )(u_lb, kt, bs, cs, dvec, la, lb, h0)


def _ssm_matrices(lam_re, lam_im, log_dt, b_re, b_im, c_re, c_im, d_skip, l_eff):
    g = lam_re.shape[0]
    L = SSM_CHUNK
    lg = SSM_LANE_GROUPS
    nblk = g // lg
    lam = lax.complex(lam_re.astype(F32), lam_im.astype(F32))
    dt = jnp.exp(log_dt.astype(F32))[:, None]
    lam_bar = jnp.exp(lam * dt)
    b_bar = ((lam_bar - 1.0) / lam)[..., None] * lax.complex(b_re.astype(F32), b_im.astype(F32))
    c_c = lax.complex(c_re.astype(F32), c_im.astype(F32))
    pw = [jnp.ones_like(lam_bar)]
    for _ in range(L):
        pw.append(pw[-1] * lam_bar)
    pw = jnp.stack(pw)
    pw_b = pw.reshape(L + 1, nblk, lg, SSM_STATE)
    bb = b_bar.reshape(nblk, lg, SSM_STATE, SSM_GROUP).transpose(0, 1, 3, 2)
    chan_group = np.arange(LANES) // SSM_GROUP
    to_chan = jnp.asarray(chan_group[None, :] == np.arange(lg)[:, None], F32)
    out_lane = jnp.asarray(np.arange(LANES)[None, :] % SSM_GROUP == np.arange(SSM_GROUP)[:, None], F32)
    same_group = jnp.asarray(chan_group[:, None] == chan_group[None, :], F32)
    kern = jnp.einsum("gop,kgp,gpi->gkio", c_c, pw[:L], b_bar).real
    kern = kern.reshape(nblk, lg, L, SSM_GROUP, SSM_GROUP).transpose(0, 2, 1, 3, 4)
    kern = kern.reshape(nblk, L, LANES, SSM_GROUP)
    kt = jnp.einsum("ltro,oc->ltrc", kern, out_lane, precision=lax.Precision.HIGHEST) * same_group
    exps = np.clip(l_eff - 1 - np.arange(L), 0, L)
    bx = pw_b[exps].transpose(1, 0, 2, 3)[:, :, :, None, :] * bb[:, None]
    bx = jnp.where(jnp.asarray(np.arange(L) < l_eff)[None, :, None, None, None], bx, 0.0)
    bs = jnp.concatenate([bx.real, bx.imag], axis=-1).reshape(nblk, L, LANES, 2 * SSM_STATE)
    c_t = c_c.reshape(nblk, LANES, SSM_STATE).transpose(0, 2, 1)
    pw_t = pw_b[1:L + 1].transpose(1, 0, 3, 2)
    expand = lambda a: jnp.einsum("ltpg,gc->ltpc", a, to_chan, precision=lax.Precision.HIGHEST)
    pr, pi = expand(pw_t.real), expand(pw_t.imag)
    cr, ci = c_t.real[:, None], c_t.imag[:, None]
    cs = jnp.concatenate([cr * pr - ci * pi, -(cr * pi + ci * pr)], axis=2)
    lam_l = pw[l_eff].reshape(nblk, lg, SSM_STATE)
    la = jnp.stack([lam_l.real, lam_l.real], axis=2).reshape(nblk, 1, SSM_BLOCK_STATE)
    lb = jnp.stack([-lam_l.imag, lam_l.imag], axis=2).reshape(nblk, 1, SSM_BLOCK_STATE)
    dvec = d_skip.astype(F32).reshape(nblk, 1, LANES)
    return kt.astype(MXU_DTYPE), bs.astype(MXU_DTYPE), cs.astype(MXU_DTYPE), dvec, la, lb


def _state_to_blocks(re, im):
    n_seq, g, p = re.shape
    h = jnp.stack([re, im], axis=2).astype(F32).reshape(n_seq, g // SSM_LANE_GROUPS, SSM_BLOCK_STATE)
    return h.transpose(1, 0, 2)


def _state_from_blocks(h):
    nblk, n_seq, _ = h.shape
    h = h.transpose(1, 0, 2).reshape(n_seq, nblk * SSM_LANE_GROUPS, 2, SSM_STATE)
    return h[:, :, 0], h[:, :, 1]


def _glu_kernel(y_ref, w_ref, b_ref, g_ref, o_ref):
    gl = _gelu(jnp.concatenate([y_ref[k] for k in range(y_ref.shape[0])], axis=1))
    z = _dot(gl.astype(MXU_DTYPE), w_ref[...]) + b_ref[...]
    o = gl * _sigmoid(z)
    ms = jnp.mean(o * o, axis=-1, keepdims=True)
    o_ref[...] = (o * lax.rsqrt(ms + EPS) * g_ref[...]).astype(o_ref.dtype)


def _glu(y_lb, w, b, gain):
    nblk, n, _ = y_lb.shape
    d = nblk * LANES
    tm = _row_tile(n, ROW_TILE)
    vec = pl.BlockSpec((1, d), lambda i: (0, 0))
    return pl.pallas_call(
        _glu_kernel,
        grid=(n // tm,),
        in_specs=[pl.BlockSpec((nblk, tm, LANES), lambda i: (0, i, 0)), pl.BlockSpec((d, d), lambda i: (0, 0)),
                  vec, vec],
        out_specs=pl.BlockSpec((tm, d), lambda i: (i, 0)),
        out_shape=jax.ShapeDtypeStruct((n, d), MXU_DTYPE),
        compiler_params=_cparams(("parallel",)),
    )(y_lb, w, b.reshape(1, d), gain.reshape(1, d))


def _top_rows(x, k):
    n = x.shape[0]
    idx_iota = lax.broadcasted_iota(jnp.int32, x.shape, 0)
    vals, idxs = [], []
    cur = x
    for _ in range(k):
        m = jnp.max(cur, axis=0, keepdims=True)
        ix = jnp.min(jnp.where(cur == m, idx_iota, n), axis=0, keepdims=True)
        vals.append(m)
        idxs.append(ix)
        cur = jnp.where(idx_iota == ix, -jnp.inf, cur)
    return jnp.concatenate(vals, axis=0), jnp.concatenate(idxs, axis=0)


def _pick_rows(table, sel):
    out = jnp.zeros(sel.shape, table.dtype)
    for a in range(table.shape[0]):
        out = out + jnp.where(sel == a, table[a:a + 1, :], 0)
    return out


def _peer_topk_kernel(q_ref, keys_ref, i1_ref, i2_ref, gate_ref):
    q = q_ref[...].astype(MXU_DTYPE)
    s1 = _dot_nt(keys_ref[0, 0].astype(MXU_DTYPE), q[:, :PEER_HALF])
    s2 = _dot_nt(keys_ref[0, 1].astype(MXU_DTYPE), q[:, PEER_HALF:])
    v1, x1 = _top_rows(s1, PEER_TOPK)
    v2, x2 = _top_rows(s2, PEER_TOPK)
    t = v1.shape[1]
    pairs = [(a, b) for a in range(PEER_TOPK) for b in range(PEER_TOPK) if (a + 1) * (b + 1) <= PEER_TOPK]
    n_pad = -len(pairs) % SUBLANES
    cand = jnp.concatenate([v1[a:a + 1] + v2[b:b + 1] for a, b in pairs]
                           + [jnp.full((n_pad, t), -jnp.inf, F32)], axis=0)
    top, pos = _top_rows(cand, PEER_TOPK)
    i1_ref[0] = _pick_rows(jnp.concatenate([x1[a:a + 1] for a, _ in pairs], axis=0), pos)
    i2_ref[0] = _pick_rows(jnp.concatenate([x2[b:b + 1] for _, b in pairs], axis=0), pos)
    e = jnp.exp(top - jnp.max(top, axis=0, keepdims=True))
    gate_ref[0] = e / jnp.sum(e, axis=0, keepdims=True)


def _peer_topk(q, sub_keys):
    n = q.shape[0]
    tt = _row_tile(n, 512)
    out = pl.BlockSpec((1, PEER_TOPK, tt), lambda i, h: (h, 0, i))
    shp = (PEER_HEADS, PEER_TOPK, n)
    return pl.pallas_call(
        _peer_topk_kernel,
        grid=(n // tt, PEER_HEADS),
        in_specs=[pl.BlockSpec((tt, 2 * PEER_HALF), lambda i, h: (i, h)),
                  pl.BlockSpec((1, 2, PEER_KEYS, PEER_HALF), lambda i, h: (h, 0, 0, 0))],
        out_specs=[out, out, out],
        out_shape=[jax.ShapeDtypeStruct(shp, jnp.int32), jax.ShapeDtypeStruct(shp, jnp.int32),
                   jax.ShapeDtypeStruct(shp, F32)],
        compiler_params=_cparams(("parallel", "parallel")),
    )(q, sub_keys)


PEER_EXPERT_BLOCK = 4 * PEER_KEYS


PEER_GRID_HALF = PEER_KEYS // 2
HIGH_HALF = 0xFFFF0000
PEER_BUILD_UNROLL = 128


def _peer_kernel(x_ref, u_ref, v_ref, i1_ref, i2_ref, gate_ref, h_ref, o_ref, g_scr, w_scr, *, tm, n_blocks):
    j = pl.program_id(1)
    n_entries = PEER_HEADS * PEER_TOPK
    words = PEER_GRID_HALF // 2
    blocks_per_half = n_blocks // 2
    unroll = math.gcd(tm, PEER_BUILD_UNROLL)

    def build(base):
        r = lax.broadcasted_iota(jnp.int32, (PEER_GRID_HALF, n_entries), 0)
        i1_of_row = base + jnp.where(r < words, 2 * r, 2 * (r - words) + 1)
        i2_of_row = lax.broadcasted_iota(jnp.int32, (PEER_KEYS, n_entries), 0)

        def body(nb, carry):
            for k in range(unroll):
                n = nb * unroll + k
                a = jnp.where(i1_of_row == i1_ref[pl.ds(n, 1), :], gate_ref[pl.ds(n, 1), :], 0.0)
                b = jnp.where(i2_of_row == i2_ref[pl.ds(n, 1), :], 1.0, 0.0)
                grid = _dot_nt(a.astype(MXU_DTYPE), b.astype(MXU_DTYPE))
                bits = lax.bitcast_convert_type(grid.astype(jnp.bfloat16).astype(F32), jnp.uint32)
                g_scr[pl.ds(pl.multiple_of(n * words, words), words), :] = (
                    jnp.right_shift(bits[:words], jnp.uint32(16)) | (bits[words:] & jnp.uint32(HIGH_HALF)))
            return carry

        lax.fori_loop(0, tm // unroll, body, 0)

    def weights():
        act = _dot_nt(x_ref[...], u_ref[...])
        pairs = PEER_EXPERT_BLOCK // (2 * PEER_KEYS)
        m0 = (j - jnp.where(j >= blocks_per_half, blocks_per_half, 0)) * pairs
        pieces = []
        for q in range(pairs):
            word = g_scr[pl.ds(m0 + q, tm, stride=words), :]
            pieces.append(lax.bitcast_convert_type(jnp.left_shift(word, jnp.uint32(16)), F32))
            pieces.append(lax.bitcast_convert_type(word & jnp.uint32(HIGH_HALF), F32))
        g = jnp.concatenate(pieces, axis=1)
        w_scr[j % 2] = (g * _gelu(act)).astype(MXU_DTYPE)

    def values():
        o_ref[...] += _dot(w_scr[(j + 1) % 2], v_ref[...])

    @pl.when(j == 0)
    def _():
        o_ref[...] = h_ref[...]
        build(0)
        weights()

    @pl.when(j == blocks_per_half)
    def _():
        build(PEER_GRID_HALF)

    @pl.when((j > 0) & (j < n_blocks))
    def _():
        values()
        weights()

    @pl.when(j == n_blocks)
    def _():
        values()


def _peer(xn, u, v, i1, i2, gate, h):
    n, d = xn.shape
    n_blocks = u.shape[0] // PEER_EXPERT_BLOCK
    assert n_blocks * PEER_EXPERT_BLOCK == PEER_KEYS * PEER_KEYS and n_blocks % 2 == 0
    tm = _row_tile(n, 512)
    n_entries = PEER_HEADS * PEER_TOPK
    once = pl.Buffered(1)
    row_in = pl.BlockSpec((tm, d), lambda i, j: (i, 0), pipeline_mode=once)
    ent = pl.BlockSpec((tm, n_entries), lambda i, j: (i, 0))
    return pl.pallas_call(
        functools.partial(_peer_kernel, tm=tm, n_blocks=n_blocks),
        grid=(n // tm, n_blocks + 1),
        in_specs=[pl.BlockSpec((tm, d), lambda i, j: (i, 0)),
                  pl.BlockSpec((PEER_EXPERT_BLOCK, d), lambda i, j: (jnp.minimum(j, n_blocks - 1), 0)),
                  pl.BlockSpec((PEER_EXPERT_BLOCK, d), lambda i, j: (jnp.maximum(j - 1, 0), 0)),
                  ent, ent, ent, row_in],
        out_specs=pl.BlockSpec((tm, d), lambda i, j: (i, 0), pipeline_mode=once),
        out_shape=jax.ShapeDtypeStruct((n, d), F32),
        scratch_shapes=[pltpu.VMEM((tm * PEER_GRID_HALF // 2, PEER_KEYS), jnp.uint32),
                        pltpu.VMEM((2, tm, PEER_EXPERT_BLOCK), MXU_DTYPE)],
        compiler_params=_cparams(("parallel", "arbitrary")),
    )(xn, u, v, i1, i2, gate, h)


def _prepare_weights(rel_table, w_in, q_norm, k_norm, w_phi, w_glu, w_out, peer_w_query, peer_u, peer_v):
    off_gate = ATTN_WIDTH + 6 * KV_WIDTH
    off_ssm = off_gate + 3 * N_HEADS
    w_main = w_in[:, :off_gate].astype(MXU_DTYPE)
    w_ssm = w_in[:, off_ssm:].astype(MXU_DTYPE)
    w_gate = jnp.pad(w_in[:, off_gate:off_ssm], ((0, 0), (0, LANES - 3 * N_HEADS))).astype(MXU_DTYPE)
    n_main = w_main.shape[1]
    ones = jnp.ones((KV_WIDTH,), F32)
    zeros = jnp.zeros((KV_WIDTH,), F32)
    gain = jnp.concatenate([
        jnp.tile(q_norm.astype(F32), N_HEADS), ones, ones,
        jnp.tile(k_norm[1].astype(F32), N_KV_HEADS), ones,
        jnp.tile(k_norm[2].astype(F32), N_KV_HEADS), ones]).reshape(1, n_main)
    mode = jnp.concatenate([
        jnp.ones((ATTN_WIDTH,), F32), zeros, zeros, ones, zeros, ones, zeros]).reshape(1, n_main)
    w_cmp = w_phi.reshape(2, CMP_R, CMP_STRIDE, HEAD_DIM, HEAD_DIM).transpose(0, 2, 3, 1, 4)
    w_cmp = w_cmp.reshape(2, CMP_STRIDE * HEAD_DIM, CMP_R * HEAD_DIM).astype(MXU_DTYPE)
    return dict(
        w_main=w_main, w_ssm=w_ssm, w_gate=w_gate, gain=gain, mode=mode, w_cmp=w_cmp,
        tab=rel_table.astype(F32).T, w_glu=w_glu.astype(MXU_DTYPE), w_out=w_out.astype(MXU_DTYPE),
        w_query=peer_w_query.astype(MXU_DTYPE), peer_u=peer_u.astype(MXU_DTYPE),
        peer_v=peer_v.astype(MXU_DTYPE))


def _overlap_matrix(n_chunks, n_cmp, n_slc, n_pad):
    c0 = np.arange(n_chunks)[:, None] * CMP_STRIDE
    s0 = np.arange(n_pad)[None, :] * SLC_BLOCK
    overlap = np.clip(np.minimum(c0 + CMP_BLOCK, s0 + SLC_BLOCK) - np.maximum(c0, s0), 0, None) / CMP_BLOCK
    overlap = overlap * (np.arange(n_chunks)[:, None] < n_cmp) * (np.arange(n_pad)[None, :] < n_slc)
    return jnp.asarray(overlap, dtype=MXU_DTYPE)


def _expansion_matrix(n_pad, n_keys, n_cols):
    key = np.arange(n_cols)[None, :]
    e = ((key // SLC_BLOCK) == np.arange(n_pad)[:, None]) & (key < n_keys)
    return jnp.asarray(e, dtype=MXU_DTYPE)


def _pad_lanes(n):
    return -(-n // LANES) * LANES


def _layer(x3, t_real, pos0, cache_cmp3, cache_sel3, cache_win3, page_table, h0, wts, norm_mix,
           k_norm, b_phi, ssm, b_glu, out_norm_attn, out_norm_ssm, norm_ffn, sub_keys):
    nb, t, d_model = x3.shape
    n = nb * t
    sample = page_table is not None
    x2 = x3.reshape(n, d_model)
    xn = _rmsnorm(x2, norm_mix, MXU_DTYPE)
    p, kv_cmp, kv_sel, kv_win = _in_proj(xn, wts["w_main"], wts["mode"], wts["gain"])
    gates = _matmul(xn, wts["w_gate"], "sigmoid")
    n_main = p.shape[1]
    p3 = p.reshape(nb, t, n_main)
    col = lambda off: off // HEAD_DIM
    off_sel, off_win = ATTN_WIDTH + 2 * KV_WIDTH, ATTN_WIDTH + 4 * KV_WIDTH

    if sample:
        assert t_real < CMP_STRIDE and pos0 % PAGE_SIZE == 0
        n_pages = page_table.shape[1]
        pk = _compress(cache_cmp3, page_table, wts["w_cmp"])
        total = pos0 + t_real
    else:
        n_pages = t // PAGE_SIZE
        pt = jnp.arange(nb * n_pages, dtype=jnp.int32).reshape(nb, n_pages)
        pk = _compress(kv_cmp.reshape(nb * n_pages, PAGE_SIZE * 2 * N_KV_HEADS, HEAD_DIM), pt, wts["w_cmp"])
        total = t
    n_chunks = pk.shape[2]
    n_cmp = total // CMP_STRIDE - CMP_R + 1
    n_slc = -(-total // SLC_BLOCK)
    n_pad = _pad_lanes(n_slc)
    m_mat = _overlap_matrix(n_chunks, n_cmp, n_slc, n_pad)
    tq_cmp = _row_tile(t, 256)
    tab = wts["tab"]
    o_cmp, sel = _cmp_attention(tab, p3, pk, b_phi.astype(F32), k_norm[0].astype(F32).reshape(1, HEAD_DIM),
                                m_mat, pos0=pos0, n_cmp=n_cmp, n_slc=n_slc, tq=tq_cmp)

    if sample:
        kv_block = 2 * KV_WIDTH
        n_cols = (n_pages + SEL_PAGES_PER_STEP) * KEY_TILE
        e_mat = _expansion_matrix(n_pad, (n_pages + 1) * KEY_TILE, n_cols)
        o_sel = _attn_sample(p3, cache_sel3, page_table, tab, "sel", off_sel // kv_block, pos0, 0, sel, e_mat)
        o_win = _attn_sample(p3, cache_win3, None, tab, "win", off_win // kv_block, pos0, pos0 - WINDOW)
    else:
        e3 = _expansion_matrix(n_pad, t, t).T.reshape(t // KEY_TILE, KEY_TILE, n_pad)
        o_sel = _attn_prompt(p3, tab, "sel", col(off_sel), col(off_sel + KV_WIDTH), sel, e3)
        o_win = _attn_prompt(p3, tab, "win", col(off_win), col(off_win + KV_WIDTH))
    mixed_attn = _combine(o_cmp.reshape(n, ATTN_WIDTH), o_sel.reshape(n, ATTN_WIDTH),
                          o_win.reshape(n, ATTN_WIDTH), gates, out_norm_attn)

    u_lb = _matmul(xn, wts["w_ssm"], "laneblocks")
    nblk = u_lb.shape[0]
    if t % SSM_CHUNK:
        assert t < SSM_CHUNK and t_real <= t
        u_lb = jnp.pad(u_lb.reshape(nblk, nb, t, LANES), ((0, 0), (0, 0), (0, SSM_CHUNK - t), (0, 0)))
        u_lb = u_lb.reshape(nblk, nb * SSM_CHUNK, LANES)
        l_eff, n_chunk = t_real, 1
    else:
        l_eff, n_chunk = SSM_CHUNK, t // SSM_CHUNK
    mats = _ssm_matrices(*ssm, l_eff)
    y_lb, h_last = _ssm(u_lb, *mats, _state_to_blocks(*h0), n_seq=nb, n_chunk=n_chunk)
    if t % SSM_CHUNK:
        y_lb = y_lb.reshape(nblk, nb, SSM_CHUNK, LANES)[:, :, :t].reshape(nblk, n, LANES)
    mixed_ssm = _glu(y_lb, wts["w_glu"], b_glu, out_norm_ssm)

    h = _out_proj(mixed_attn, mixed_ssm, wts["w_out"], x2)
    hn = _rmsnorm(h, norm_ffn, MXU_DTYPE)
    pq = _matmul(hn, wts["w_query"])
    i1, i2, gate = _peer_topk(pq, sub_keys)
    to_rows = lambda a: a.transpose(2, 0, 1).reshape(n, PEER_HEADS * PEER_TOPK)
    y_out = _peer(hn, wts["peer_u"], wts["peer_v"], to_rows(i1), to_rows(i2), to_rows(gate), h)
    new_kv = [a.reshape(nb, t, 2, N_KV_HEADS, HEAD_DIM) for a in (kv_cmp, kv_sel, kv_win)]
    return y_out.reshape(nb, t, d_model), new_kv, _state_from_blocks(h_last)


def kernel(x_prompt, x_sample, cache_kv_cmp, cache_kv_sel, cache_kv_win, state_ssm_re, state_ssm_im, page_table,
           rel_table, norm_mix, w_in, q_norm, k_norm, w_phi, b_phi, ssm_lam_re, ssm_lam_im, ssm_log_dt,
           ssm_b_re, ssm_b_im, ssm_c_re, ssm_c_im, ssm_d, w_glu, b_glu, out_norm_attn, out_norm_ssm, w_out,
           norm_ffn, peer_w_query, peer_sub_keys, peer_u, peer_v):
    depth = w_in.shape[0]
    bp, t_p, d_model = x_prompt.shape
    db, t_s, _ = x_sample.shape
    past_len = page_table.shape[1] * PAGE_SIZE
    kv_row = 2 * KV_WIDTH
    n_groups = ssm_lam_re.shape[1]
    assert cache_kv_win.shape[2] == WINDOW and t_p % KEY_TILE == 0 and t_s <= SAMPLE_ROWS

    yp, ys = x_prompt, jnp.pad(x_sample, ((0, 0), (0, SAMPLE_ROWS - t_s), (0, 0)))
    new_p = [[] for _ in range(5)]
    new_s = [[] for _ in range(5)]
    for l in range(depth):
        wts = _prepare_weights(rel_table, w_in[l], q_norm[l], k_norm[l], w_phi[l], w_glu[l], w_out[l],
                               peer_w_query[l], peer_u[l], peer_v[l])
        ssm = (ssm_lam_re[l], ssm_lam_im[l], ssm_log_dt[l], ssm_b_re[l], ssm_b_im[l], ssm_c_re[l],
               ssm_c_im[l], ssm_d[l])
        shared = (wts, norm_mix[l], k_norm[l], b_phi[l], ssm, b_glu[l], out_norm_attn[l],
                  out_norm_ssm[l], norm_ffn[l], peer_sub_keys[l])

        zero_state = jnp.zeros((bp, n_groups, SSM_STATE), F32)
        yp, new_kv, h_last = _layer(yp, t_p, 0, None, None, None, None, (zero_state, zero_state), *shared)
        keep = min(WINDOW, t_p)
        new_p[0].append(new_kv[0])
        new_p[1].append(new_kv[1])
        new_p[2].append(new_kv[2][:, t_p - keep:])
        new_p[3].append(h_last[0])
        new_p[4].append(h_last[1])

        n_phys = cache_kv_cmp.shape[1]
        h0 = (state_ssm_re[l], state_ssm_im[l])
        page_rows = PAGE_SIZE * 2 * N_KV_HEADS
        ys, new_kv, h_last = _layer(ys, t_s, past_len, cache_kv_cmp[l].reshape(n_phys, page_rows, HEAD_DIM),
                                cache_kv_sel[l].reshape(n_phys, page_rows, HEAD_DIM),
                                cache_kv_win[l].reshape(db * WIN_TILES, page_rows, HEAD_DIM), page_table, h0,
                                *shared)
        new_s[0].append(new_kv[0][:, :t_s])
        new_s[1].append(new_kv[1][:, :t_s])
        win_all = jnp.concatenate([cache_kv_win[l], new_kv[2][:, :t_s]], axis=1)
        keep = min(WINDOW, past_len + t_s)
        new_s[2].append(win_all[:, win_all.shape[1] - keep:])
        new_s[3].append(h_last[0])
        new_s[4].append(h_last[1])
    outs_p = [jnp.stack(v) for v in new_p]
    outs_s = [jnp.stack(v) for v in new_s]
    return (yp, ys[:, :t_s], *outs_p, *outs_s)
```
